```python
import jax, jax.numpy as jnp
from jax import lax
import numpy as np

D_MODEL = 1024
BATCH = 2
SEQ = 16384
DEPTH = 2
DEC_BATCH = 32
DEC_SEQ = 2048
PAST_LEN = 128

N_META = 16
GRID_W = 64
NORM_EPS = 1e-6

A_HEADS = 4
A_DK = 128
A_DV = 128
A_CONV = 5
A_CHUNK = 64
A_WIDTH = A_HEADS * A_DV
A_COLS = 4 * A_WIDTH + 4 * A_HEADS

B_HEADS = 8
B_HD = 64
B_WIDTH = B_HEADS * B_HD
B_W_LORA = 32
B_A_LORA = 32
B_G_LORA = 96
B_GN_EPS = 64e-5
B_COLS = 3 * B_WIDTH + B_W_LORA + B_A_LORA + B_G_LORA

E_COLS = A_COLS + B_COLS
MIX_WIDTH = A_WIDTH + B_WIDTH

C_HEADS = 8
C_KV_HEADS = 2
C_GROUP = C_HEADS // C_KV_HEADS
C_HD = 128
C_QBLOCK = 128
C_QKV = (C_HEADS + 2 * C_KV_HEADS) * C_HD
ROPE_THETA = 10000.0

D_FF = 2816
FFN_CONV = 3

N_EVEN = (DEPTH + 1) // 2
N_ODD = DEPTH // 2

kernel_name = 'hybrid_bidir_encoder_gdn_rwkv7_gqa'


def rmsnorm(x, w, eps=NORM_EPS):
    xf = x.astype(jnp.float32)
    y = xf * lax.rsqrt(jnp.mean(xf * xf, axis=-1, keepdims=True) + eps)
    return (y * w.astype(jnp.float32)).astype(x.dtype)


def l2norm(x, eps=1e-6):
    xf = x.astype(jnp.float32)
    return xf * lax.rsqrt(jnp.sum(xf * xf, axis=-1, keepdims=True) + eps)


def centred_dwconv(x, w):
    K = w.shape[0]
    half = K // 2
    T = x.shape[1]
    xp = jnp.pad(x, ((0, 0), (half, half), (0, 0)))
    out = xp[:, 0:T] * w[0]
    for j in range(1, K):
        out = out + xp[:, j:j + T] * w[j]
    return out


def gated_delta_chunked(q, k, v, g, beta):
    Bn, H, T, dk = k.shape
    dv = v.shape[-1]
    C = A_CHUNK
    n = T // C
    q = q * (dk ** -0.5)
    rs = lambda t: t.reshape((Bn, H, n, C) + t.shape[3:])
    q, k, v, g, beta = rs(q), rs(k), rs(v), rs(g), rs(beta)
    g = jnp.cumsum(g, axis=-1)
    tril = jnp.tril(jnp.ones((C, C), bool))
    strict = jnp.tril(jnp.ones((C, C), bool), -1)
    diff = g[..., :, None] - g[..., None, :]
    decay = jnp.where(tril, jnp.exp(jnp.where(tril, diff, 0.0)), 0.0)
    kb = k * beta[..., None]
    L = jnp.where(strict, jnp.einsum('bhncd,bhnsd->bhncs', kb, k) * decay, 0.0)
    M = L + jnp.eye(C, dtype=L.dtype)
    u = lax.linalg.triangular_solve(M, v * beta[..., None], left_side=True, lower=True, unit_diagonal=True)
    w = lax.linalg.triangular_solve(M, kb * jnp.exp(g)[..., None], left_side=True, lower=True, unit_diagonal=True)
    attn = jnp.where(tril, jnp.einsum('bhncd,bhnsd->bhncs', q, k) * decay, 0.0)
    qg = q * jnp.exp(g)[..., None]
    kg = k * jnp.exp(g[..., -1:] - g)[..., None]
    glast = jnp.exp(g[..., -1])

    def step(S, xs):
        u_i, w_i, attn_i, qg_i, kg_i, gl_i = xs
        v_new = u_i - jnp.einsum('bhcd,bhde->bhce', w_i, S)
        o = jnp.einsum('bhcd,bhde->bhce', qg_i, S) + jnp.einsum('bhcs,bhse->bhce', attn_i, v_new)
        S = S * gl_i[..., None, None] + jnp.einsum('bhcd,bhce->bhde', kg_i, v_new)
        return S, o

    mv = lambda t: jnp.moveaxis(t, 2, 0)
    S0 = jnp.zeros((Bn, H, dk, dv), k.dtype)
    _, o = lax.scan(step, S0, (mv(u), mv(w), mv(attn), mv(qg), mv(kg), mv(glast)))
    return jnp.moveaxis(o, 0, 2).reshape(Bn, H, T, dv)


def mixer_a(p, conv_w, a_log, dt_bias, norm_w):
    Bn, T, _ = p.shape
    W, H = A_WIDTH, A_HEADS
    qkv = jax.nn.silu(centred_dwconv(p[..., :3 * W], conv_w)).astype(jnp.float32)
    z = p[..., 3 * W:4 * W].astype(jnp.float32).reshape(Bn, T, H, A_DV)
    gates = p[..., 4 * W:].astype(jnp.float32)
    to_heads = lambda t, d: t.reshape(Bn, T, H, d).transpose(0, 2, 1, 3)
    q = l2norm(to_heads(qkv[..., :W], A_DK))
    k = l2norm(to_heads(qkv[..., W:2 * W], A_DK))
    v = to_heads(qkv[..., 2 * W:], A_DV)
    alpha = gates[..., :2 * H].reshape(Bn, T, 2, H)
    beta = jax.nn.sigmoid(gates[..., 2 * H:].reshape(Bn, T, 2, H)).transpose(2, 0, 3, 1)
    g = (-jnp.exp(a_log.astype(jnp.float32)) * jax.nn.softplus(alpha + dt_bias)).transpose(2, 0, 3, 1)
    pad = (-N_META) % A_CHUNK

    def pad_t(t):
        widths = [(0, 0)] * t.ndim
        widths[2] = (pad, 0)
        return jnp.pad(t, widths)

    flip = lambda t: jnp.flip(t, axis=2)
    q, k, v = pad_t(q), pad_t(k), pad_t(v)
    gf, gb, bf, bb = pad_t(g[0]), pad_t(g[1]), pad_t(beta[0]), pad_t(beta[1])
    o_f = gated_delta_chunked(q, k, v, gf, bf)
    o_b = flip(gated_delta_chunked(flip(q), flip(k), flip(v), flip(gb), flip(bb)))
    o = (o_f + o_b)[:, :, pad:].transpose(0, 2, 1, 3)
    o = rmsnorm(o, norm_w) * jax.nn.silu(z)
    return o.reshape(Bn, T, W).astype(p.dtype)


def rwkv7_scan(r, w, k, v, a, b):
    Bn, T, H, D = r.shape

    def step(S, xs):
        r_t, w_t, k_t, v_t, a_t, b_t = xs
        sa = jnp.einsum('bhvk,bhk->bhv', S, a_t)
        S = S * w_t[:, :, None, :] + sa[..., None] * b_t[:, :, None, :] + v_t[..., None] * k_t[:, :, None, :]
        return S, jnp.einsum('bhvk,bhk->bhv', S, r_t)

    mv = lambda t: jnp.moveaxis(t, 1, 0)
    S0 = jnp.zeros((Bn, H, D, D), r.dtype)
    _, y = lax.scan(step, S0, (mv(r), mv(w), mv(k), mv(v), mv(a), mv(b)))
    return jnp.moveaxis(y, 0, 1)


def mixer_b(p, shift_mu, w0, w2, a0, a2, g2, k_k, k_a, r_k, ln_w, ln_b):
    out_dtype = p.dtype
    Bn, T, _ = p.shape
    W, H, D = B_WIDTH, B_HEADS, B_HD
    zero = jnp.zeros_like(p[:, :1])
    p_prev = jnp.concatenate([zero, p[:, :-1]], axis=1)
    p_next = jnp.concatenate([p[:, 1:], zero], axis=1)
    p = (p + shift_mu[0] * (p_prev - p) + shift_mu[1] * (p_next - p)).astype(jnp.float32)
    r, k, v = p[..., :W], p[..., W:2 * W], p[..., 2 * W:3 * W]
    o = 3 * W
    xw = p[..., o:o + B_W_LORA]
    xa = p[..., o + B_W_LORA:o + B_W_LORA + B_A_LORA]
    xg = p[..., o + B_W_LORA + B_A_LORA:]
    wlog = -jax.nn.softplus(-(w0[:, None, None, :] + jnp.einsum('btl,zlc->zbtc', jnp.tanh(xw), w2))) - 0.5
    decay = jnp.exp(-jnp.exp(wlog)).reshape(2, Bn, T, H, D)
    a = jax.nn.sigmoid(a0 + xa @ a2)
    g = jax.nn.sigmoid(xg) @ g2
    heads = lambda t: t.reshape(Bn, T, H, D)
    kk = l2norm(heads(k * k_k))
    k = k * (1.0 + (a - 1.0) * k_a)
    r_h, k_h, v_h, a_h = heads(r), heads(k), heads(v), heads(a)
    b_vec = kk * a_h
    flip = lambda t: jnp.flip(t, axis=1)
    y_f = rwkv7_scan(r_h, decay[0], k_h, v_h, -kk, b_vec)
    y_b = flip(rwkv7_scan(flip(r_h), flip(decay[1]), flip(k_h), flip(v_h), flip(-kk), flip(b_vec)))
    y = y_f + y_b
    yc = y - jnp.mean(y, axis=-1, keepdims=True)
    y = yc * lax.rsqrt(jnp.mean(yc * yc, axis=-1, keepdims=True) + B_GN_EPS) * ln_w.reshape(H, D) + ln_b.reshape(H, D)
    y = y + jnp.sum(r_h * k_h * r_k, axis=-1, keepdims=True) * v_h
    return (y.reshape(Bn, T, W) * g).astype(out_dtype)


def axial_angles(n_tokens):
    rows = n_tokens // GRID_W
    row = jnp.concatenate([-jnp.ones((N_META,), jnp.float32), jnp.repeat(jnp.arange(rows, dtype=jnp.float32), GRID_W)])
    col = jnp.concatenate([jnp.arange(N_META, dtype=jnp.float32), jnp.tile(jnp.arange(GRID_W, dtype=jnp.float32), rows)])
    nf = C_HD // 4
    inv = ROPE_THETA ** (-jnp.arange(nf, dtype=jnp.float32) / nf)
    ang = jnp.stack([row, col], axis=-1)[:, :, None] * inv
    return jnp.cos(ang), jnp.sin(ang)


def axial_rotary(x, cos, sin):
    xs = x.reshape(x.shape[:-1] + (2, 2, C_HD // 4))
    x1, x2 = xs[..., 0, :], xs[..., 1, :]
    out = jnp.stack([x1 * cos - x2 * sin, x2 * cos + x1 * sin], axis=-2)
    return out.reshape(x.shape)


def mixer_c(p, q_norm, k_norm):
    Bn, T, _ = p.shape
    n = T - N_META
    qw, kw = C_HEADS * C_HD, C_KV_HEADS * C_HD
    q = rmsnorm(p[..., :qw].reshape(Bn, T, C_KV_HEADS, C_GROUP, C_HD), q_norm).astype(jnp.float32)
    k = rmsnorm(p[..., qw:qw + kw].reshape(Bn, T, C_KV_HEADS, C_HD), k_norm).astype(jnp.float32)
    v = p[..., qw + kw:].reshape(Bn, T, C_KV_HEADS, C_HD).astype(jnp.float32)
    cos, sin = axial_angles(n)
    q = axial_rotary(q, cos[:, None, None], sin[:, None, None]) * (C_HD ** -0.5)
    k = axial_rotary(k, cos[:, None], sin[:, None])

    def attend(qb):
        s = jnp.einsum('bqhgd,bkhd->bhgqk', qb, k)
        pr = jax.nn.softmax(s, axis=-1)
        return jnp.einsum('bhgqk,bkhd->bqhgd', pr, v)

    o_meta = attend(q[:, :N_META])
    nb = n // C_QBLOCK
    q_blocks = q[:, N_META:].reshape(Bn, nb, C_QBLOCK, C_KV_HEADS, C_GROUP, C_HD).transpose(1, 0, 2, 3, 4, 5)
    o_real = lax.map(attend, q_blocks).transpose(1, 0, 2, 3, 4, 5).reshape(Bn, n, C_KV_HEADS, C_GROUP, C_HD)
    o = jnp.concatenate([o_meta, o_real], axis=1)
    return o.reshape(Bn, T, C_HEADS * C_HD).astype(p.dtype)


def conv_ffn(x, w_in, conv_w, conv_b, w_out):
    h = x @ w_in
    u, gate = h[..., :D_FF], h[..., D_FF:]
    gate = centred_dwconv(gate, conv_w) + conv_b
    return (jax.nn.gelu(gate, approximate=True) * u) @ w_out


def setup_inputs(seed: int = 0) -> dict:
    key = jax.random.key(seed)
    keys = iter(jax.random.split(key, 48))

    def nrm(shape, scale=1.0):
        return scale * jax.random.normal(next(keys), shape, jnp.float32)

    def gain(shape, base=1.0):
        return base + 0.02 * nrm(shape)

    def unif(shape, lo, hi):
        return jax.random.uniform(next(keys), shape, jnp.float32, lo, hi)

    dt = jnp.exp(unif((N_EVEN, 2, A_HEADS), float(np.log(1e-3)), float(np.log(1e-1))))
    w0_base = -6.0 + 5.0 * jnp.linspace(0.0, 1.0, B_WIDTH, dtype=jnp.float32) ** 1.5
    return {
        'x_prompt': nrm((BATCH, SEQ, D_MODEL)),
        'x_sample': nrm((DEC_BATCH, DEC_SEQ, D_MODEL)),
        'meta': nrm((N_META, D_MODEL)),
        'mix_pre_norm': gain((DEPTH, D_MODEL)),
        'mix_post_norm': gain((DEPTH, D_MODEL)),
        'ffn_pre_norm': gain((DEPTH, D_MODEL)),
        'ffn_post_norm': gain((DEPTH, D_MODEL)),
        'e_w_in': nrm((N_EVEN, D_MODEL, E_COLS), D_MODEL ** -0.5),
        'a_conv_w': nrm((N_EVEN, A_CONV, 3 * A_WIDTH), A_CONV ** -0.5),
        'a_log': jnp.log(unif((N_EVEN, 2, A_HEADS), 1.0, 16.0)),
        'a_dt_bias': dt + jnp.log(-jnp.expm1(-dt)),
        'a_out_norm': gain((N_EVEN, A_DV)),
        'b_shift': unif((N_EVEN, 2, B_COLS), 0.0, 0.5),
        'b_w0': w0_base + 0.1 * nrm((N_EVEN, 2, B_WIDTH)),
        'b_w2': nrm((N_EVEN, 2, B_W_LORA, B_WIDTH), 0.1),
        'b_a0': nrm((N_EVEN, B_WIDTH), 0.1),
        'b_a2': nrm((N_EVEN, B_A_LORA, B_WIDTH), 0.1),
        'b_g2': nrm((N_EVEN, B_G_LORA, B_WIDTH), B_G_LORA ** -0.5),
        'b_k_k': gain((N_EVEN, B_WIDTH), 0.85),
        'b_k_a': gain((N_EVEN, B_WIDTH)),
        'b_r_k': nrm((N_EVEN, B_HEADS, B_HD), 0.1),
        'b_ln_w': gain((N_EVEN, B_WIDTH)),
        'b_ln_b': nrm((N_EVEN, B_WIDTH), 0.02),
        'e_w_out': nrm((N_EVEN, MIX_WIDTH, D_MODEL), MIX_WIDTH ** -0.5),
        'o_w_qkv': nrm((N_ODD, D_MODEL, C_QKV), D_MODEL ** -0.5),
        'o_q_norm': gain((N_ODD, C_HD)),
        'o_k_norm': gain((N_ODD, C_HD)),
        'o_w_out': nrm((N_ODD, C_HEADS * C_HD, D_MODEL), (C_HEADS * C_HD) ** -0.5),
        'f_w_in': nrm((DEPTH, D_MODEL, 2 * D_FF), D_MODEL ** -0.5),
        'f_conv_w': nrm((DEPTH, FFN_CONV, D_FF), FFN_CONV ** -0.5),
        'f_conv_b': nrm((DEPTH, D_FF), 0.02),
        'f_w_out': nrm((DEPTH, D_FF, D_MODEL), D_FF ** -0.5),
    }


def reference(x_prompt, x_sample, meta, mix_pre_norm, mix_post_norm, ffn_pre_norm, ffn_post_norm,
              e_w_in, a_conv_w, a_log, a_dt_bias, a_out_norm,
              b_shift, b_w0, b_w2, b_a0, b_a2, b_g2, b_k_k, b_k_a, b_r_k, b_ln_w, b_ln_b,
              e_w_out, o_w_qkv, o_q_norm, o_k_norm, o_w_out,
              f_w_in, f_conv_w, f_conv_b, f_w_out):
    def trunk(x):
        Bn = x.shape[0]
        h = jnp.concatenate([jnp.broadcast_to(meta.astype(x.dtype)[None], (Bn, N_META, D_MODEL)), x], axis=1)
        for i in range(DEPTH):
            j = i // 2
            hn = rmsnorm(h, mix_pre_norm[i])
            if i % 2 == 0:
                proj = hn @ e_w_in[j]
                o_a = mixer_a(proj[..., :A_COLS], a_conv_w[j], a_log[j], a_dt_bias[j], a_out_norm[j])
                o_b = mixer_b(proj[..., A_COLS:], b_shift[j], b_w0[j], b_w2[j], b_a0[j], b_a2[j], b_g2[j],
                              b_k_k[j], b_k_a[j], b_r_k[j], b_ln_w[j], b_ln_b[j])
                mix = jnp.concatenate([o_a, o_b], axis=-1) @ e_w_out[j]
            else:
                mix = mixer_c(hn @ o_w_qkv[j], o_q_norm[j], o_k_norm[j]) @ o_w_out[j]
            h = h + rmsnorm(mix, mix_post_norm[i]).astype(h.dtype)
            ff = conv_ffn(rmsnorm(h, ffn_pre_norm[i]), f_w_in[i], f_conv_w[i], f_conv_b[i], f_w_out[i])
            h = h + rmsnorm(ff, ffn_post_norm[i]).astype(h.dtype)
        return h[:, N_META:]

    y_prompt = trunk(x_prompt)
    y_sample = trunk(x_sample)
    return (y_prompt, y_sample)
```

```python
import functools
import itertools
from typing import NamedTuple

import numpy as np
import jax
import jax.numpy as jnp
from jax import lax
from jax.experimental import pallas as pl
from jax.experimental.pallas import tpu as pltpu

F32 = jnp.float32
BF16 = jnp.bfloat16

D_MODEL = 1024
DEPTH = 2
N_META = 16
GRID_W = 64
NORM_EPS = 1e-6

A_HEADS = 4
A_DK = 128
A_DV = 128
A_CONV = 5
A_WIDTH = A_HEADS * A_DV

B_HEADS = 8
B_HD = 64
B_WIDTH = B_HEADS * B_HD
B_W_LORA = 32
B_A_LORA = 32
B_G_LORA = 96
B_GN_EPS = 64e-5
B_LORA_PAD = 256

C_HEADS = 8
C_KV_HEADS = 2
C_GROUP = C_HEADS // C_KV_HEADS
C_HD = 128
ROPE_THETA = 10000.0

D_FF = 2816
FFN_CONV = 3

LANE = 128
CHUNK = 64
SEQ_BLK = 128
HALO = 16
FF_CHUNK = 256
VMEM_LIMIT = 56 * 1024 * 1024


class _Group(NamedTuple):
    start: int
    nb: int
    n: int
    tp: int
    fv: int


class _Geom(NamedTuple):
    groups: tuple
    rows: int
    tm: int


def _choose_geom(shapes):
    tm = 512 if min(n for _, n in shapes) >= 512 else 128
    best = None
    for pads in itertools.product((128, 256, 384, 512), repeat=len(shapes)):
        start, ok, groups = 0, True, []
        for (nb, n), p in zip(shapes, pads):
            ok = ok and start % tm == 0
            groups.append(_Group(start, nb, n, n + p, p - N_META))
            start += nb * (n + p)
        if ok and start % tm == 0 and (best is None or start < best.rows):
            best = _Geom(tuple(groups), start, tm)
    assert best is not None
    return best


def _row_valid(geom, base, tm):
    start = jnp.int32(geom.groups[0].start)
    tp = jnp.int32(geom.groups[0].tp)
    fv = jnp.int32(geom.groups[0].fv)
    for g in geom.groups[1:]:
        assert tm <= g.tp
        inside = base >= g.start
        start = jnp.where(inside, g.start, start)
        tp = jnp.where(inside, g.tp, tp)
        fv = jnp.where(inside, g.fv, fv)
    pos = lax.rem(base - start, tp) + lax.broadcasted_iota(jnp.int32, (tm, 1), 0)
    pos = jnp.where(pos >= tp, pos - tp, pos)
    return pos >= fv


def _rms(x, w, eps=NORM_EPS):
    return x * lax.rsqrt(jnp.mean(x * x, axis=-1, keepdims=True) + eps) * w


def _sigmoid(x):
    return 1.0 / (1.0 + jnp.exp(-x))


def _softplus(x):
    return jnp.maximum(x, 0.0) + jnp.log1p(jnp.exp(-jnp.abs(x)))


def _dot(a, b):
    return jnp.dot(a.astype(BF16), b.astype(BF16), preferred_element_type=F32)


def _dot_nt(a, b):
    return lax.dot_general(a.astype(BF16), b.astype(BF16), (((1,), (1,)), ((), ())),
                           preferred_element_type=F32)


def _seg_sum(x, ones_ref, terms):
    acc = None
    rem = x
    for t in range(terms):
        piece = rem.astype(BF16)
        part = jnp.dot(piece, ones_ref[...], preferred_element_type=F32)
        acc = part if acc is None else acc + part
        if t + 1 < terms:
            rem = rem - piece.astype(F32)
    return acc


def _chunk_cumsum(x, reverse):
    tm = x.shape[0]
    pos = lax.broadcasted_iota(jnp.int32, (tm, 1), 0) & (CHUNK - 1)
    s = 1
    while s < CHUNK:
        if reverse:
            x = x + jnp.where(pos < CHUNK - s, pltpu.roll(x, tm - s, axis=0), 0.0)
        else:
            x = x + jnp.where(pos >= s, pltpu.roll(x, s, axis=0), 0.0)
        s *= 2
    return x


def _neumann_inverse(n_mat, eye):
    prod = eye + n_mat
    power = n_mat
    s = 2
    while s < CHUNK:
        power = _dot(power, power)
        prod = prod + _dot(prod, power)
        s *= 2
    return prod


def _halo_specs(tm, width, rows):
    per = tm // HALO
    last = rows // HALO - 1
    return [
        pl.BlockSpec((tm, width), lambda i: (i, 0)),
        pl.BlockSpec((HALO, width), lambda i: (jnp.maximum(i * per - 1, 0), 0)),
        pl.BlockSpec((HALO, width), lambda i: (jnp.minimum((i + 1) * per, last), 0)),
    ]


def _const_spec(shape):
    return pl.BlockSpec(shape, lambda *_: (0,) * len(shape))


def _params(sem=("parallel",)):
    return pltpu.CompilerParams(dimension_semantics=sem, vmem_limit_bytes=VMEM_LIMIT)


def _proj_kernel(x_ref, nw_ref, w_ref, *rest, widths):
    out_refs, xn_ref = rest[:len(widths)], rest[len(widths)]
    xn_ref[...] = _rms(x_ref[...], nw_ref[...]).astype(BF16)
    c0 = 0
    for o_ref, wd in zip(out_refs, widths):
        for s in range(0, wd, 512):
            e = min(s + 512, wd)
            o_ref[:, s:e] = jnp.dot(xn_ref[...], w_ref[:, c0 + s:c0 + e], preferred_element_type=F32)
        c0 += wd


def _norm_proj(h, norm_w, w_bf16, widths, geom, name):
    tm, rows = geom.tm, geom.rows
    return pl.pallas_call(
        functools.partial(_proj_kernel, widths=widths),
        out_shape=[jax.ShapeDtypeStruct((rows, wd), F32) for wd in widths],
        grid=(rows // tm,),
        in_specs=[pl.BlockSpec((tm, D_MODEL), lambda i: (i, 0)),
                  _const_spec((1, D_MODEL)),
                  _const_spec(w_bf16.shape)],
        out_specs=[pl.BlockSpec((tm, wd), lambda i: (i, 0)) for wd in widths],
        scratch_shapes=[pltpu.VMEM((tm, D_MODEL), BF16)],
        compiler_params=_params(),
        name=name,
    )(h, norm_w.reshape(1, D_MODEL), w_bf16)


def _a_prep_kernel(x_ref, xp_ref, xn_ref, g_ref, cw_ref, alog_ref, dtb_ref,
                   q_ref, k_ref, v_ref, go_ref, xw_ref, *, geom):
    tm = geom.tm
    i = pl.program_id(0)
    valid = _row_valid(geom, i * tm, tm)
    xw_ref[0:HALO] = jnp.where(i > 0, xp_ref[...], 0.0)
    xw_ref[HALO:HALO + tm] = x_ref[...]
    xw_ref[HALO + tm:2 * HALO + tm] = jnp.where(i < pl.num_programs(0) - 1, xn_ref[...], 0.0)
    half = A_CONV // 2
    for c, o_ref in enumerate((q_ref, k_ref, v_ref)):
        cs = slice(c * A_WIDTH, (c + 1) * A_WIDTH)
        acc = xw_ref[HALO - half:HALO - half + tm, cs] * cw_ref[0:1, cs]
        for j in range(1, A_CONV):
            acc = acc + xw_ref[HALO - half + j:HALO - half + j + tm, cs] * cw_ref[j:j + 1, cs]
        y = jnp.where(valid, acc * _sigmoid(acc), 0.0)
        for hd in range(A_HEADS):
            hs = slice(hd * A_DK, (hd + 1) * A_DK)
            yh = y[:, hs]
            if c < 2:
                yh = yh * lax.rsqrt(jnp.sum(yh * yh, axis=-1, keepdims=True) + 1e-6)
                if c == 0:
                    yh = yh * (A_DK ** -0.5)
            o_ref[:, hs] = yh
    al = g_ref[...]
    lane = lax.broadcasted_iota(jnp.int32, (1, LANE), 1)
    gval = jnp.where(valid, -jnp.exp(alog_ref[...]) * _softplus(al + dtb_ref[...]), 0.0)
    beta = jnp.where(valid, _sigmoid(al), 0.0)
    cum_f = _chunk_cumsum(gval, False)
    cum_b = _chunk_cumsum(gval, True)
    go_ref[...] = jnp.where(lane < A_HEADS, cum_f,
                            jnp.where(lane < 2 * A_HEADS, cum_b,
                                      jnp.where(lane < 4 * A_HEADS, beta, 0.0)))


def _a_prep(pa, ga, conv_w, a_log, dt_bias, geom):
    tm, rows = geom.tm, geom.rows
    qkv_w = 3 * A_WIDTH
    pad = lambda v: jnp.pad(v.reshape(1, -1).astype(F32), ((0, 0), (0, LANE - v.size)))
    out = jax.ShapeDtypeStruct((rows, A_WIDTH), F32)
    return pl.pallas_call(
        functools.partial(_a_prep_kernel, geom=geom),
        out_shape=[out, out, out, jax.ShapeDtypeStruct((rows, LANE), F32)],
        grid=(rows // tm,),
        in_specs=_halo_specs(tm, qkv_w, rows) + [
            pl.BlockSpec((tm, LANE), lambda i: (i, 0)),
            _const_spec((A_CONV, qkv_w)), _const_spec((1, LANE)), _const_spec((1, LANE))],
        out_specs=[pl.BlockSpec((tm, A_WIDTH), lambda i: (i, 0))] * 3 + [pl.BlockSpec((tm, LANE), lambda i: (i, 0))],
        scratch_shapes=[pltpu.VMEM((tm + 2 * HALO, qkv_w), F32)],
        compiler_params=_params(),
        name="a_prep",
    )(pa, pa, pa, ga, conv_w, pad(a_log), pad(dt_bias))


def _a_chunk(q_ref, k_ref, v_ref, g_ref, gt_ref, o_ref, s_ref, jj, reverse, consts):
    tri, strict, eye = consts
    rows = slice(jj * CHUNK, (jj + 1) * CHUNK)
    edge = 0 if reverse else CHUNK - 1
    for hd in range(A_HEADS):
        hs = slice(hd * A_DK, (hd + 1) * A_DK)
        col = hd + (A_HEADS if reverse else 0)
        q, k, v = q_ref[rows, hs], k_ref[rows, hs], v_ref[rows, hs]
        g_col = g_ref[rows, col:col + 1]
        b_col = g_ref[rows, 2 * A_HEADS + col:2 * A_HEADS + col + 1]
        g_row = gt_ref[jj, col:col + 1, :]
        decay = jnp.where(tri, jnp.exp(jnp.where(tri, g_col - g_row, 0.0)), 0.0)
        kb = k * b_col
        kq = _dot_nt(jnp.concatenate([kb, q], axis=0), k)
        l_mat = jnp.where(strict, kq[:CHUNK] * decay, 0.0)
        attn = jnp.where(tri, kq[CHUNK:] * decay, 0.0)
        t_inv = _neumann_inverse(-l_mat, eye)
        e_g = jnp.exp(g_col)
        uw = _dot(t_inv, jnp.concatenate([v * b_col, kb * e_g], axis=1))
        u, w = uw[:, :A_DV], uw[:, A_DV:]
        g_last = g_col[edge:edge + 1]
        kg = k * jnp.exp(g_last - g_col)
        state = s_ref[hd]
        v_new = u - _dot(w, state)
        o_ref[rows, hs] = _dot(jnp.concatenate([q * e_g, attn], axis=1),
                               jnp.concatenate([state, v_new], axis=0))
        s_ref[hd] = state * jnp.exp(g_last) + _dot(kg.T, v_new)


def _a_scan_kernel(fi_ref, bi_ref, fr_ref,
                   qf, kf, vf, gf, gtf, qb, kb, vb, gb, gtb, of, ob, sf, sb):
    del fi_ref, bi_ref
    @pl.when(fr_ref[pl.program_id(0)] == 1)
    def _():
        sf[...] = jnp.zeros_like(sf)
        sb[...] = jnp.zeros_like(sb)
    r = lax.broadcasted_iota(jnp.int32, (CHUNK, CHUNK), 0)
    c = lax.broadcasted_iota(jnp.int32, (CHUNK, CHUNK), 1)
    eye = jnp.where(r == c, 1.0, 0.0)
    fwd = (r >= c, r > c, eye)
    bwd = (r <= c, r < c, eye)
    nc = SEQ_BLK // CHUNK
    for j in range(nc):
        _a_chunk(qf, kf, vf, gf, gtf, of, sf, j, False, fwd)
        _a_chunk(qb, kb, vb, gb, gtb, ob, sb, nc - 1 - j, True, bwd)


def _scan_tables(geom):
    fi, bi, fr = [], [], []
    for g in geom.groups:
        nblk = g.tp // SEQ_BLK
        for b in range(g.nb):
            base = (g.start + b * g.tp) // SEQ_BLK
            for i in range(nblk):
                fi.append(base + i)
                bi.append(base + nblk - 1 - i)
                fr.append(1 if i == 0 else 0)
    mk = lambda v: jnp.asarray(np.asarray(v, np.int32))
    return mk(fi), mk(bi), mk(fr)


def _a_scan(qn, kn, vv, gts, geom, tables):
    rows = geom.rows
    nc = SEQ_BLK // CHUNK
    gts_t = gts[:, :4 * A_HEADS].reshape(rows // CHUNK, CHUNK, 4 * A_HEADS).transpose(0, 2, 1)
    wide_f = pl.BlockSpec((SEQ_BLK, A_WIDTH), lambda s, fi, bi, fr: (fi[s], 0))
    wide_b = pl.BlockSpec((SEQ_BLK, A_WIDTH), lambda s, fi, bi, fr: (bi[s], 0))
    gate_f = pl.BlockSpec((SEQ_BLK, LANE), lambda s, fi, bi, fr: (fi[s], 0))
    gate_b = pl.BlockSpec((SEQ_BLK, LANE), lambda s, fi, bi, fr: (bi[s], 0))
    gt_f = pl.BlockSpec((nc, 4 * A_HEADS, CHUNK), lambda s, fi, bi, fr: (fi[s], 0, 0))
    gt_b = pl.BlockSpec((nc, 4 * A_HEADS, CHUNK), lambda s, fi, bi, fr: (bi[s], 0, 0))
    out = jax.ShapeDtypeStruct((rows, A_WIDTH), F32)
    return pl.pallas_call(
        _a_scan_kernel,
        out_shape=[out, out],
        grid_spec=pltpu.PrefetchScalarGridSpec(
            num_scalar_prefetch=3,
            grid=(tables[0].shape[0],),
            in_specs=[wide_f, wide_f, wide_f, gate_f, gt_f, wide_b, wide_b, wide_b, gate_b, gt_b],
            out_specs=[wide_f, wide_b],
            scratch_shapes=[pltpu.VMEM((A_HEADS, A_DK, A_DV), F32)] * 2),
        compiler_params=_params(("arbitrary",)),
        name="a_scan",
    )(*tables, qn, kn, vv, gts, gts_t, qn, kn, vv, gts, gts_t)


def _b_prep_kernel(x_ref, xp_ref, xn_ref, mu_ref, w0_ref, w2_ref, a0_ref, a2_ref, g2_ref,
                   kk_ref, ka_ref, rk_ref, ones_ref,
                   r_out, k_out, v_out, kkn_out, eta_out, cf_out, cb_out, bonus_out, gate_out,
                   xw_ref, *, geom):
    tm = geom.tm
    i = pl.program_id(0)
    xw_ref[0:HALO] = jnp.where(i > 0, xp_ref[...], 0.0)
    xw_ref[HALO:HALO + tm] = x_ref[...]
    xw_ref[HALO + tm:2 * HALO + tm] = jnp.where(i < pl.num_programs(0) - 1, xn_ref[...], 0.0)

    def shifted(cs):
        cur = xw_ref[HALO:HALO + tm, cs]
        prev = xw_ref[HALO - 1:HALO - 1 + tm, cs]
        nxt = xw_ref[HALO + 1:HALO + 1 + tm, cs]
        return cur + mu_ref[0:1, cs] * (prev - cur) + mu_ref[1:2, cs] * (nxt - cur)

    w = B_WIDTH
    r = shifted(slice(0, w))
    k = shifted(slice(w, 2 * w))
    v = shifted(slice(2 * w, 3 * w))
    lo = shifted(slice(3 * w, 3 * w + B_LORA_PAD))
    wl = _dot(jnp.tanh(lo), w2_ref[...]) + w0_ref[...]
    log_decay = -jnp.exp(-_softplus(-wl) - 0.5)
    cf_out[...] = _chunk_cumsum(log_decay[:, :w], False)
    cb_out[...] = _chunk_cumsum(log_decay[:, w:], True)
    eta = _sigmoid(a0_ref[...] + _dot(lo, a2_ref[...]))
    gate_out[...] = _dot(_sigmoid(lo), g2_ref[...])
    kx = k * kk_ref[...]
    kkn_out[...] = kx * lax.rsqrt(_seg_sum(kx * kx, ones_ref, 2) + 1e-6)
    k = k * (1.0 + (eta - 1.0) * ka_ref[...])
    bonus_out[...] = _seg_sum(r * k * rk_ref[...], ones_ref, 2) * v
    r_out[...] = r
    k_out[...] = k
    v_out[...] = v
    eta_out[...] = eta


def _b_prep(pb, mu, w0, w2, a0, a2, g2, k_k, k_a, r_k, ones, geom):
    tm, rows = geom.tm, geom.rows
    wd = 3 * B_WIDTH + B_LORA_PAD
    out = jax.ShapeDtypeStruct((rows, B_WIDTH), F32)
    row = lambda v: v.reshape(1, -1).astype(F32)
    return pl.pallas_call(
        functools.partial(_b_prep_kernel, geom=geom),
        out_shape=[out] * 9,
        grid=(rows // tm,),
        in_specs=_halo_specs(tm, wd, rows) + [
            _const_spec((2, wd)), _const_spec((1, 2 * B_WIDTH)), _const_spec((B_LORA_PAD, 2 * B_WIDTH)),
            _const_spec((1, B_WIDTH)), _const_spec((B_LORA_PAD, B_WIDTH)), _const_spec((B_LORA_PAD, B_WIDTH)),
            _const_spec((1, B_WIDTH)), _const_spec((1, B_WIDTH)), _const_spec((1, B_WIDTH)),
            _const_spec((B_WIDTH, B_WIDTH))],
        out_specs=[pl.BlockSpec((tm, B_WIDTH), lambda i: (i, 0))] * 9,
        scratch_shapes=[pltpu.VMEM((tm + 2 * HALO, wd), F32)],
        compiler_params=_params(),
        name="b_prep",
    )(pb, pb, pb, mu, w0, w2, row(a0), a2, g2, row(k_k), row(k_a), row(r_k), ones)


def _b_chunk(refs, y_ref, p_ref, jj, reverse, consts):
    strict, incl, eye, blockdiag, lane_lo, row_pos = consts
    r_ref, k_ref, v_ref, kk_ref, eta_ref, c_ref = refs
    rows = slice(jj * CHUNK, (jj + 1) * CHUNK)
    pair_w = 2 * B_HD

    def expand(x):
        return jnp.concatenate([jnp.where(lane_lo, x, 0.0), jnp.where(lane_lo, 0.0, x)], axis=0)

    for pr in range(B_HEADS // 2):
        ls = slice(pr * pair_w, (pr + 1) * pair_w)
        r, k, v, kk, eta = r_ref[rows, ls], k_ref[rows, ls], v_ref[rows, ls], kk_ref[rows, ls], eta_ref[rows, ls]
        c_in = c_ref[rows, ls]
        if reverse:
            c_ex = jnp.where(row_pos < CHUNK - 1, pltpu.roll(c_in, CHUNK - 1, axis=0), 0.0)
            c_tot = c_in[0:1]
        else:
            c_ex = jnp.where(row_pos >= 1, pltpu.roll(c_in, 1, axis=0), 0.0)
            c_tot = c_in[CHUNK - 1:CHUNK]
        b = kk * eta
        inv_w = jnp.exp(-c_in)
        rest_w = jnp.exp(c_tot - c_in)
        a_e = expand(-kk * jnp.exp(c_ex))
        r_e = expand(r * jnp.exp(c_in))
        b_e = expand(b * inv_w)
        k_e = expand(k * inv_w)
        v_e = expand(v)
        m1 = _dot_nt(jnp.concatenate([a_e, r_e], axis=0), jnp.concatenate([b_e, k_e], axis=0))
        n2 = 2 * CHUNK
        n_ab = jnp.where(strict, m1[:n2, :n2], 0.0)
        a_ak = jnp.where(strict, m1[:n2, n2:], 0.0)
        a_rb = jnp.where(incl, m1[n2:, :n2], 0.0)
        a_rk = jnp.where(incl, m1[n2:, n2:], 0.0)
        t_inv = _neumann_inverse(n_ab, eye)
        x = _dot(t_inv, jnp.concatenate([_dot(a_ak, v_e), a_e], axis=1))
        state = p_ref[pr]
        u_e = x[:, :pair_w] + _dot(x[:, pair_w:], state)
        y_e = _dot(jnp.concatenate([r_e, a_rb, a_rk], axis=1), jnp.concatenate([state, u_e, v_e], axis=0))
        y_ref[rows, ls] = y_e[:CHUNK] + y_e[CHUNK:]
        u_c = u_e[:CHUNK] + u_e[CHUNK:]
        kd_t = jnp.concatenate([b * rest_w, k * rest_w], axis=0).T
        upd = _dot(kd_t, jnp.concatenate([u_c, v], axis=0))
        w_col = jnp.sum(jnp.where(eye > 0.0, jnp.exp(c_tot), 0.0), axis=1, keepdims=True)
        p_ref[pr] = state * w_col + jnp.where(blockdiag, upd, 0.0)


def _b_scan_kernel(fi_ref, bi_ref, fr_ref,
                   rf, kf, vf, kkf, ef, cf, rb, kb, vb, kkb, eb, cb, yf, yb, pf, pb):
    del fi_ref, bi_ref
    @pl.when(fr_ref[pl.program_id(0)] == 1)
    def _():
        pf[...] = jnp.zeros_like(pf)
        pb[...] = jnp.zeros_like(pb)
    n2 = 2 * CHUNK
    r = lax.broadcasted_iota(jnp.int32, (n2, n2), 0)
    c = lax.broadcasted_iota(jnp.int32, (n2, n2), 1)
    same = (r >= CHUNK) == (c >= CHUNK)
    rt, ct = r & (CHUNK - 1), c & (CHUNK - 1)
    eye = jnp.where(r == c, 1.0, 0.0)
    lane_lo = lax.broadcasted_iota(jnp.int32, (1, n2), 1) < B_HD
    row_pos = lax.broadcasted_iota(jnp.int32, (CHUNK, 1), 0)
    fwd = (same & (rt > ct), same & (rt >= ct), eye, same, lane_lo, row_pos)
    bwd = (same & (rt < ct), same & (rt <= ct), eye, same, lane_lo, row_pos)
    nc = SEQ_BLK // CHUNK
    for j in range(nc):
        _b_chunk((rf, kf, vf, kkf, ef, cf), yf, pf, j, False, fwd)
        _b_chunk((rb, kb, vb, kkb, eb, cb), yb, pb, nc - 1 - j, True, bwd)


def _b_scan(r, k, v, kkn, eta, cf, cb, geom, tables):
    wide_f = pl.BlockSpec((SEQ_BLK, B_WIDTH), lambda s, fi, bi, fr: (fi[s], 0))
    wide_b = pl.BlockSpec((SEQ_BLK, B_WIDTH), lambda s, fi, bi, fr: (bi[s], 0))
    out = jax.ShapeDtypeStruct((geom.rows, B_WIDTH), F32)
    return pl.pallas_call(
        _b_scan_kernel,
        out_shape=[out, out],
        grid_spec=pltpu.PrefetchScalarGridSpec(
            num_scalar_prefetch=3,
            grid=(tables[0].shape[0],),
            in_specs=[wide_f] * 6 + [wide_b] * 6,
            out_specs=[wide_f, wide_b],
            scratch_shapes=[pltpu.VMEM((B_HEADS // 2, 2 * B_HD, 2 * B_HD), F32)] * 2),
        compiler_params=_params(("arbitrary",)),
        name="b_scan",
    )(*tables, r, k, v, kkn, eta, cf, r, k, v, kkn, eta, cb)


def _finish_residual(h, mix, post_w, valid):
    return jnp.where(valid, h + _rms(mix, post_w), 0.0)


def _mixout_even_kernel(h_ref, oaf_ref, oab_ref, z_ref, ybf_ref, ybb_ref, bonus_ref, gate_ref,
                        w_ref, an_ref, lnw_ref, lnb_ref, post_ref, ones_ref, o_ref, cat_ref, *, geom):
    tm = geom.tm
    valid = _row_valid(geom, pl.program_id(0) * tm, tm)
    for hd in range(A_HEADS):
        hs = slice(hd * A_DV, (hd + 1) * A_DV)
        o = oaf_ref[:, hs] + oab_ref[:, hs]
        z = z_ref[:, hs]
        cat_ref[:, hs] = (_rms(o, an_ref[...]) * (z * _sigmoid(z))).astype(BF16)
    y = ybf_ref[...] + ybb_ref[...]
    yc = y - _seg_sum(y, ones_ref, 3) * (1.0 / B_HD)
    var = _seg_sum(yc * yc, ones_ref, 2) * (1.0 / B_HD)
    yn = yc * lax.rsqrt(var + B_GN_EPS) * lnw_ref[...] + lnb_ref[...]
    cat_ref[:, A_WIDTH:] = ((yn + bonus_ref[...]) * gate_ref[...]).astype(BF16)
    mix = jnp.dot(cat_ref[...], w_ref[...], preferred_element_type=F32)
    o_ref[...] = _finish_residual(h_ref[...], mix, post_ref[...], valid)


def _mixout_even(h, oaf, oab, pa, ybf, ybb, bonus, gate, w_out, a_norm, ln_w, ln_b, post_w, ones, geom):
    tm, rows = geom.tm, geom.rows
    row = lambda v: v.reshape(1, -1).astype(F32)
    a_blk = pl.BlockSpec((tm, A_WIDTH), lambda i: (i, 0))
    b_blk = pl.BlockSpec((tm, B_WIDTH), lambda i: (i, 0))
    return pl.pallas_call(
        functools.partial(_mixout_even_kernel, geom=geom),
        out_shape=jax.ShapeDtypeStruct((rows, D_MODEL), F32),
        grid=(rows // tm,),
        in_specs=[pl.BlockSpec((tm, D_MODEL), lambda i: (i, 0)), a_blk, a_blk,
                  pl.BlockSpec((tm, A_WIDTH), lambda i: (i, 3)),
                  b_blk, b_blk, b_blk, b_blk,
                  _const_spec((A_WIDTH + B_WIDTH, D_MODEL)), _const_spec((1, A_DV)),
                  _const_spec((1, B_WIDTH)), _const_spec((1, B_WIDTH)), _const_spec((1, D_MODEL)),
                  _const_spec((B_WIDTH, B_WIDTH))],
        out_specs=pl.BlockSpec((tm, D_MODEL), lambda i: (i, 0)),
        scratch_shapes=[pltpu.VMEM((tm, A_WIDTH + B_WIDTH), BF16)],
        compiler_params=_params(),
        name="mixout_even",
    )(h, oaf, oab, pa, ybf, ybb, bonus, gate, w_out, row(a_norm), row(ln_w), row(ln_b), row(post_w), ones)


def _mixout_odd_kernel(h_ref, o_ref_in, w_ref, post_ref, o_ref, *, geom):
    tm = geom.tm
    valid = _row_valid(geom, pl.program_id(0) * tm, tm)
    mix = jnp.dot(o_ref_in[...].astype(BF16), w_ref[...], preferred_element_type=F32)
    o_ref[...] = _finish_residual(h_ref[...], mix, post_ref[...], valid)


def _mixout_odd(h, o, w_out, post_w, geom):
    tm, rows = geom.tm, geom.rows
    wd = C_HEADS * C_HD
    return pl.pallas_call(
        functools.partial(_mixout_odd_kernel, geom=geom),
        out_shape=jax.ShapeDtypeStruct((rows, D_MODEL), F32),
        grid=(rows // tm,),
        in_specs=[pl.BlockSpec((tm, D_MODEL), lambda i: (i, 0)), pl.BlockSpec((tm, wd), lambda i: (i, 0)),
                  _const_spec((wd, D_MODEL)), _const_spec((1, D_MODEL))],
        out_specs=pl.BlockSpec((tm, D_MODEL), lambda i: (i, 0)),
        compiler_params=_params(),
        name="mixout_odd",
    )(h, o, w_out, post_w.reshape(1, D_MODEL))


def _c_prep_kernel(x_ref, cos_ref, sin_ref, qn_ref, kn_ref, q_out, k_out, v_out):
    cos, sin = cos_ref[...], sin_ref[...]
    lane = lax.broadcasted_iota(jnp.int32, (1, C_HD), 1)
    first_half = (lane & (C_HD // 2 - 1)) < C_HD // 4

    def rope(xh, w, scale):
        xh = _rms(xh, w)
        partner = jnp.where(first_half, pltpu.roll(xh, C_HD - C_HD // 4, axis=1), pltpu.roll(xh, C_HD // 4, axis=1))
        return (xh * cos + partner * sin) * scale

    nq = C_HEADS * C_HD
    nk = C_KV_HEADS * C_HD
    for hd in range(C_HEADS):
        hs = slice(hd * C_HD, (hd + 1) * C_HD)
        q_out[:, hs] = rope(x_ref[:, hs], qn_ref[...], C_HD ** -0.5).astype(BF16)
    for hd in range(C_KV_HEADS):
        hs = slice(hd * C_HD, (hd + 1) * C_HD)
        k_out[:, hs] = rope(x_ref[:, nq + hd * C_HD:nq + (hd + 1) * C_HD], kn_ref[...], 1.0).astype(BF16)
    v_out[...] = x_ref[:, nq + nk:].astype(BF16)


def _c_prep(qkv, cos_t, sin_t, q_norm, k_norm, geom):
    tm, rows = geom.tm, geom.rows
    nq, nk = C_HEADS * C_HD, C_KV_HEADS * C_HD
    return pl.pallas_call(
        _c_prep_kernel,
        out_shape=[jax.ShapeDtypeStruct((rows, nq), BF16), jax.ShapeDtypeStruct((rows, nk), BF16),
                   jax.ShapeDtypeStruct((rows, nk), BF16)],
        grid=(rows // tm,),
        in_specs=[pl.BlockSpec((tm, nq + 2 * nk), lambda i: (i, 0)),
                  pl.BlockSpec((tm, C_HD), lambda i: (i, 0)), pl.BlockSpec((tm, C_HD), lambda i: (i, 0)),
                  _const_spec((1, C_HD)), _const_spec((1, C_HD))],
        out_specs=[pl.BlockSpec((tm, nq), lambda i: (i, 0)), pl.BlockSpec((tm, nk), lambda i: (i, 0)),
                   pl.BlockSpec((tm, nk), lambda i: (i, 0))],
        compiler_params=_params(),
        name="c_prep",
    )(qkv, cos_t, sin_t, q_norm.reshape(1, C_HD), k_norm.reshape(1, C_HD))


def _rope_tables(geom):
    nf = C_HD // 4
    inv = ROPE_THETA ** (-jnp.arange(nf, dtype=F32) / nf)
    cos_parts, sin_parts = [], []
    for g in geom.groups:
        t = jnp.arange(g.tp, dtype=jnp.int32) - (g.fv + N_META)
        is_meta = t < 0
        row = jnp.where(is_meta, -1, t // GRID_W).astype(F32)
        col = jnp.where(is_meta, t + N_META, t % GRID_W).astype(F32)
        ang_r, ang_c = row[:, None] * inv, col[:, None] * inv
        cos = jnp.concatenate([jnp.cos(ang_r)] * 2 + [jnp.cos(ang_c)] * 2, axis=1)
        sin = jnp.concatenate([-jnp.sin(ang_r), jnp.sin(ang_r), -jnp.sin(ang_c), jnp.sin(ang_c)], axis=1)
        cos_parts.append(jnp.tile(cos, (g.nb, 1)))
        sin_parts.append(jnp.tile(sin, (g.nb, 1)))
    return jnp.concatenate(cos_parts, axis=0), jnp.concatenate(sin_parts, axis=0)


def _attn_kernel(q_ref, k_ref, v_ref, o_ref, m_ref, l_ref, acc_ref, *, tq, tk, tp, fv):
    q = jnp.concatenate([q_ref[0, :, g * C_HD:(g + 1) * C_HD] for g in range(C_GROUP)], axis=0)
    m_ref[...] = jnp.full_like(m_ref, -1e30)
    l_ref[...] = jnp.zeros_like(l_ref)
    acc_ref[...] = jnp.zeros_like(acc_ref)

    def step(j, masked):
        ks = pl.ds(j * tk if isinstance(j, int) else pl.multiple_of(j * tk, tk), tk)
        s = lax.dot_general(q, k_ref[0, ks, :], (((1,), (1,)), ((), ())), preferred_element_type=F32)
        if masked:
            key = j * tk + lax.broadcasted_iota(jnp.int32, (1, tk), 1)
            s = jnp.where(key >= fv, s, -1e30)
        m_old = m_ref[...]
        m_new = jnp.maximum(m_old, jnp.max(s, axis=-1, keepdims=True))
        p = jnp.exp(s - m_new)
        alpha = jnp.exp(m_old - m_new)
        l_ref[...] = alpha * l_ref[...] + jnp.sum(p, axis=-1, keepdims=True)
        acc_ref[...] = alpha * acc_ref[...] + jnp.dot(p.astype(BF16), v_ref[0, ks, :], preferred_element_type=F32)
        m_ref[...] = m_new

    j0 = fv // tk
    step(j0, fv % tk != 0)
    if tp // tk > j0 + 1:
        def body(j, carry):
            step(j, False)
            return carry
        lax.fori_loop(j0 + 1, tp // tk, body, 0)
    out = acc_ref[...] / l_ref[...]
    for g in range(C_GROUP):
        o_ref[0, :, g * C_HD:(g + 1) * C_HD] = out[g * tq:(g + 1) * tq]


def _attention(qn, kn, vn, geom):
    outs = []
    for g in geom.groups:
        sl = slice(g.start, g.start + g.nb * g.tp)
        q3 = qn[sl].reshape(g.nb, g.tp, C_HEADS * C_HD)
        k3 = kn[sl].reshape(g.nb, g.tp, C_KV_HEADS * C_HD)
        v3 = vn[sl].reshape(g.nb, g.tp, C_KV_HEADS * C_HD)
        tq = SEQ_BLK
        tk = g.tp if g.tp <= 4096 else max(t for t in (512, 256, 128) if g.tp % t == 0)
        gw = C_GROUP * C_HD
        o3 = pl.pallas_call(
            functools.partial(_attn_kernel, tq=tq, tk=tk, tp=g.tp, fv=g.fv),
            out_shape=jax.ShapeDtypeStruct((g.nb, g.tp, C_HEADS * C_HD), F32),
            grid=(g.nb, C_KV_HEADS, g.tp // tq),
            in_specs=[pl.BlockSpec((1, tq, gw), lambda b, h, i: (b, i, h)),
                      pl.BlockSpec((1, g.tp, C_HD), lambda b, h, i: (b, 0, h)),
                      pl.BlockSpec((1, g.tp, C_HD), lambda b, h, i: (b, 0, h))],
            out_specs=pl.BlockSpec((1, tq, gw), lambda b, h, i: (b, i, h)),
            scratch_shapes=[pltpu.VMEM((C_GROUP * tq, 1), F32), pltpu.VMEM((C_GROUP * tq, 1), F32),
                            pltpu.VMEM((C_GROUP * tq, C_HD), F32)],
            compiler_params=_params(("parallel", "parallel", "parallel")),
            name="attention",
        )(q3, k3, v3)
        outs.append(o3.reshape(g.nb * g.tp, C_HEADS * C_HD))
    return jnp.concatenate(outs, axis=0)


def _ffn_kernel(x_ref, xp_ref, xn_ref, pre_ref, wu_ref, wg_ref, cw_ref, cb_ref, wo_ref, post_ref,
                o_ref, xs_ref, g_ref, acc_ref, *, tm):
    i = pl.program_id(0)
    xs_ref[0:HALO] = jnp.where(i > 0, _rms(xp_ref[...], pre_ref[...]), 0.0).astype(BF16)
    xs_ref[HALO:HALO + tm] = _rms(x_ref[...], pre_ref[...]).astype(BF16)
    xs_ref[HALO + tm:2 * HALO + tm] = jnp.where(i < pl.num_programs(0) - 1,
                                                _rms(xn_ref[...], pre_ref[...]), 0.0).astype(BF16)
    half = FFN_CONV // 2
    for c in range(D_FF // FF_CHUNK):
        cs = slice(c * FF_CHUNK, (c + 1) * FF_CHUNK)
        u = jnp.dot(xs_ref[HALO:HALO + tm], wu_ref[:, cs], preferred_element_type=F32)
        g_ref[...] = jnp.dot(xs_ref[...], wg_ref[:, cs], preferred_element_type=F32)
        gate = cb_ref[0:1, cs]
        for j in range(FFN_CONV):
            gate = gate + g_ref[HALO - half + j:HALO - half + j + tm] * cw_ref[j:j + 1, cs]
        act = 0.5 * gate * (1.0 + jnp.tanh(0.7978845608028654 * (gate + 0.044715 * gate * gate * gate)))
        part = jnp.dot((act * u).astype(BF16), wo_ref[cs, :], preferred_element_type=F32)
        if c == 0:
            acc_ref[...] = part
        else:
            acc_ref[...] += part
    o_ref[...] = x_ref[...] + _rms(acc_ref[...], post_ref[...])


def _conv_ffn(h, pre_w, w_in, conv_w, conv_b, w_out, post_w, geom):
    tm, rows = geom.tm, geom.rows
    assert D_FF % FF_CHUNK == 0
    wu = w_in[:, :D_FF].astype(BF16)
    wg = w_in[:, D_FF:].astype(BF16)
    return pl.pallas_call(
        functools.partial(_ffn_kernel, tm=tm),
        out_shape=jax.ShapeDtypeStruct((rows, D_MODEL), F32),
        grid=(rows // tm,),
        in_specs=_halo_specs(tm, D_MODEL, rows) + [
            _const_spec((1, D_MODEL)), _const_spec((D_MODEL, D_FF)), _const_spec((D_MODEL, D_FF)),
            _const_spec((FFN_CONV, D_FF)), _const_spec((1, D_FF)), _const_spec((D_FF, D_MODEL)),
            _const_spec((1, D_MODEL))],
        out_specs=pl.BlockSpec((tm, D_MODEL), lambda i: (i, 0)),
        scratch_shapes=[pltpu.VMEM((tm + 2 * HALO, D_MODEL), BF16), pltpu.VMEM((tm + 2 * HALO, FF_CHUNK), F32),
                        pltpu.VMEM((tm, D_MODEL), F32)],
        compiler_params=_params(),
        name="conv_ffn",
    )(h, h, h, pre_w.reshape(1, D_MODEL), wu, wg, conv_w, conv_b.reshape(1, D_FF),
      w_out.astype(BF16), post_w.reshape(1, D_MODEL))


def _even_layer(h, j, geom, tables, ones, p):
    aw, bw = A_WIDTH, B_WIDTH
    a_cols = 4 * aw + 4 * A_HEADS
    w_in = p["e_w_in"][j]
    lora = B_W_LORA + B_A_LORA + B_G_LORA
    zeros = lambda n: jnp.zeros((D_MODEL, n), F32)
    w_cat = jnp.concatenate([
        w_in[:, a_cols:a_cols + 3 * bw + lora], zeros(B_LORA_PAD - lora),
        w_in[:, 4 * aw:a_cols], zeros(LANE - 4 * A_HEADS),
        w_in[:, :4 * aw]], axis=1).astype(BF16)
    pb, ga, pa = _norm_proj(h, p["mix_pre_norm"][2 * j], w_cat, (3 * bw + B_LORA_PAD, LANE, 4 * aw), geom, "proj_even")

    qn, kn, vv, gts = _a_prep(pa, ga, p["a_conv_w"][j], p["a_log"][j], p["a_dt_bias"][j], geom)
    oaf, oab = _a_scan(qn, kn, vv, gts, geom, tables)

    mu = jnp.pad(p["b_shift"][j], ((0, 0), (0, B_LORA_PAD - lora)))
    lora_rows = lambda m, off: jnp.pad(m, ((off, B_LORA_PAD - off - m.shape[0]), (0, 0)))
    w2 = lora_rows(jnp.concatenate([p["b_w2"][j, 0], p["b_w2"][j, 1]], axis=1), 0)
    a2 = lora_rows(p["b_a2"][j], B_W_LORA)
    g2 = lora_rows(p["b_g2"][j], B_W_LORA + B_A_LORA)
    r, k, v, kkn, eta, cf, cb, bonus, gate = _b_prep(
        pb, mu, p["b_w0"][j].reshape(1, 2 * bw), w2, p["b_a0"][j], a2, g2,
        p["b_k_k"][j], p["b_k_a"][j], p["b_r_k"][j], ones, geom)
    ybf, ybb = _b_scan(r, k, v, kkn, eta, cf, cb, geom, tables)

    return _mixout_even(h, oaf, oab, pa, ybf, ybb, bonus, gate, p["e_w_out"][j].astype(BF16),
                        p["a_out_norm"][j], p["b_ln_w"][j], p["b_ln_b"][j], p["mix_post_norm"][2 * j], ones, geom)


def _odd_layer(h, j, geom, rope, p):
    (qkv,) = _norm_proj(h, p["mix_pre_norm"][2 * j + 1], p["o_w_qkv"][j].astype(BF16),
                        ((C_HEADS + 2 * C_KV_HEADS) * C_HD,), geom, "proj_odd")
    qn, kn, vn = _c_prep(qkv, rope[0], rope[1], p["o_q_norm"][j], p["o_k_norm"][j], geom)
    o = _attention(qn, kn, vn, geom)
    return _mixout_odd(h, o, p["o_w_out"][j].astype(BF16), p["mix_post_norm"][2 * j + 1], geom)


def kernel(x_prompt, x_sample, meta, mix_pre_norm, mix_post_norm, ffn_pre_norm, ffn_post_norm, e_w_in, a_conv_w, a_log, a_dt_bias, a_out_norm, b_shift, b_w0, b_w2, b_a0, b_a2, b_g2, b_k_k, b_k_a, b_r_k, b_ln_w, b_ln_b, e_w_out, o_w_qkv, o_q_norm, o_k_norm, o_w_out, f_w_in, f_conv_w, f_conv_b, f_w_out):
    p = dict(mix_pre_norm=mix_pre_norm, mix_post_norm=mix_post_norm, e_w_in=e_w_in, a_conv_w=a_conv_w,
             a_log=a_log, a_dt_bias=a_dt_bias, a_out_norm=a_out_norm, b_shift=b_shift, b_w0=b_w0, b_w2=b_w2,
             b_a0=b_a0, b_a2=b_a2, b_g2=b_g2, b_k_k=b_k_k, b_k_a=b_k_a, b_r_k=b_r_k, b_ln_w=b_ln_w,
             b_ln_b=b_ln_b, e_w_out=e_w_out, o_w_qkv=o_w_qkv, o_q_norm=o_q_norm, o_k_norm=o_k_norm,
             o_w_out=o_w_out)
    xs = (x_prompt, x_sample)
    geom = _choose_geom([(x.shape[0], x.shape[1]) for x in xs])
    tables = _scan_tables(geom)
    rope = _rope_tables(geom)
    seg = np.arange(B_WIDTH) // B_HD
    ones = jnp.asarray(seg[:, None] == seg[None, :], BF16)

    parts = []
    for x, g in zip(xs, geom.groups):
        lead = jnp.concatenate([jnp.zeros((g.fv, D_MODEL), x.dtype), meta.astype(x.dtype)], axis=0)
        seq = jnp.concatenate([jnp.broadcast_to(lead[None], (g.nb,) + lead.shape), x], axis=1)
        parts.append(seq.reshape(g.nb * g.tp, D_MODEL))
    h = jnp.concatenate(parts, axis=0)

    for i in range(DEPTH):
        h = _even_layer(h, i // 2, geom, tables, ones, p) if i % 2 == 0 else _odd_layer(h, i // 2, geom, rope, p)
        h = _conv_ffn(h, ffn_pre_norm[i], f_w_in[i], f_conv_w[i], f_conv_b[i], f_w_out[i], ffn_post_norm[i], geom)

    outs = []
    for g in geom.groups:
        seq = h[g.start:g.start + g.nb * g.tp].reshape(g.nb, g.tp, D_MODEL)
        outs.append(seq[:, g.tp - g.n:])
    return tuple(outs)
```

```python
import functools
import itertools
from typing import NamedTuple

import numpy as np
import jax
import jax.numpy as jnp
from jax import lax
from jax.experimental import pallas as pl
from jax.experimental.pallas import tpu as pltpu

F32 = jnp.float32
BF16 = jnp.bfloat16

D_MODEL = 1024
DEPTH = 2
N_META = 16
GRID_W = 64
NORM_EPS = 1e-6

A_HEADS = 4
A_DK = 128
A_DV = 128
A_CONV = 5
A_WIDTH = A_HEADS * A_DV

B_HEADS = 8
B_HD = 64
B_WIDTH = B_HEADS * B_HD
B_W_LORA = 32
B_A_LORA = 32
B_G_LORA = 96
B_GN_EPS = 64e-5
B_LORA_PAD = 256

C_HEADS = 8
C_KV_HEADS = 2
C_GROUP = C_HEADS // C_KV_HEADS
C_HD = 128
ROPE_THETA = 10000.0

D_FF = 2816
FFN_CONV = 3

LANE = 128
CHUNK = 64
SEQ_BLK = 128
HALO = 16
FF_CHUNK = 256
ATTN_MAX_TK = 2304
LOG2E = 1.4426950408889634
VMEM_LIMIT = 56 * 1024 * 1024


class _Group(NamedTuple):
    start: int
    nb: int
    n: int
    tp: int
    fv: int


class _Geom(NamedTuple):
    groups: tuple
    rows: int
    tm: int


def _choose_geom(shapes):
    tm = 512 if min(n for _, n in shapes) >= 512 else 128
    best = None
    for pads in itertools.product((128, 256, 384, 512), repeat=len(shapes)):
        start, ok, groups = 0, True, []
        for (nb, n), p in zip(shapes, pads):
            ok = ok and start % tm == 0
            groups.append(_Group(start, nb, n, n + p, p - N_META))
            start += nb * (n + p)
        if ok and start % tm == 0 and (best is None or start < best.rows):
            best = _Geom(tuple(groups), start, tm)
    assert best is not None
    return best


def _row_valid(geom, base, tm):
    start = jnp.int32(geom.groups[0].start)
    tp = jnp.int32(geom.groups[0].tp)
    fv = jnp.int32(geom.groups[0].fv)
    for g in geom.groups[1:]:
        assert tm <= g.tp
        inside = base >= g.start
        start = jnp.where(inside, g.start, start)
        tp = jnp.where(inside, g.tp, tp)
        fv = jnp.where(inside, g.fv, fv)
    pos = lax.rem(base - start, tp) + lax.broadcasted_iota(jnp.int32, (tm, 1), 0)
    pos = jnp.where(pos >= tp, pos - tp, pos)
    return pos >= fv


def _rms(x, w, eps=NORM_EPS):
    return x * lax.rsqrt(jnp.mean(x * x, axis=-1, keepdims=True) + eps) * w


def _sigmoid(x):
    return 1.0 / (1.0 + jnp.exp(-x))


def _softplus(x):
    return jnp.maximum(x, 0.0) + jnp.log1p(jnp.exp(-jnp.abs(x)))


def _dot(a, b):
    return jnp.dot(a.astype(BF16), b.astype(BF16), preferred_element_type=F32)


def _dot_nt(a, b):
    return lax.dot_general(a.astype(BF16), b.astype(BF16), (((1,), (1,)), ((), ())),
                           preferred_element_type=F32)


def _seg_sum(x, ones_ref, terms):
    acc = None
    rem = x
    for t in range(terms):
        piece = rem.astype(BF16)
        part = jnp.dot(piece, ones_ref[...], preferred_element_type=F32)
        acc = part if acc is None else acc + part
        if t + 1 < terms:
            rem = rem - piece.astype(F32)
    return acc


def _chunk_cumsum(x, reverse):
    tm = x.shape[0]
    pos = lax.broadcasted_iota(jnp.int32, (tm, 1), 0) & (CHUNK - 1)
    s = 1
    while s < CHUNK:
        if reverse:
            x = x + jnp.where(pos < CHUNK - s, pltpu.roll(x, tm - s, axis=0), 0.0)
        else:
            x = x + jnp.where(pos >= s, pltpu.roll(x, s, axis=0), 0.0)
        s *= 2
    return x


def _neumann_inverses(n_mats, eye):
    prods = [eye + n for n in n_mats]
    powers = list(n_mats)
    s = 2
    while s < CHUNK:
        powers = [_dot(pw, pw) for pw in powers]
        prods = [pr + _dot(pr, pw) for pr, pw in zip(prods, powers)]
        s *= 2
    return prods


def _halo_specs(tm, width, rows):
    per = tm // HALO
    last = rows // HALO - 1
    return [
        pl.BlockSpec((tm, width), lambda i: (i, 0)),
        pl.BlockSpec((HALO, width), lambda i: (jnp.maximum(i * per - 1, 0), 0)),
        pl.BlockSpec((HALO, width), lambda i: (jnp.minimum((i + 1) * per, last), 0)),
    ]


def _const_spec(shape):
    return pl.BlockSpec(shape, lambda *_: (0,) * len(shape))


def _params(sem=("parallel",)):
    return pltpu.CompilerParams(dimension_semantics=sem, vmem_limit_bytes=VMEM_LIMIT)


def _proj_kernel(x_ref, nw_ref, w_ref, *rest, widths):
    out_refs, xn_ref = rest[:len(widths)], rest[len(widths)]
    xn_ref[...] = _rms(x_ref[...], nw_ref[...]).astype(BF16)
    c0 = 0
    for o_ref, wd in zip(out_refs, widths):
        for s in range(0, wd, 512):
            e = min(s + 512, wd)
            o_ref[:, s:e] = jnp.dot(xn_ref[...], w_ref[:, c0 + s:c0 + e], preferred_element_type=F32)
        c0 += wd


def _norm_proj(h, norm_w, w_bf16, widths, geom, name):
    tm, rows = geom.tm, geom.rows
    return pl.pallas_call(
        functools.partial(_proj_kernel, widths=widths),
        out_shape=[jax.ShapeDtypeStruct((rows, wd), F32) for wd in widths],
        grid=(rows // tm,),
        in_specs=[pl.BlockSpec((tm, D_MODEL), lambda i: (i, 0)),
                  _const_spec((1, D_MODEL)),
                  _const_spec(w_bf16.shape)],
        out_specs=[pl.BlockSpec((tm, wd), lambda i: (i, 0)) for wd in widths],
        scratch_shapes=[pltpu.VMEM((tm, D_MODEL), BF16)],
        compiler_params=_params(),
        name=name,
    )(h, norm_w.reshape(1, D_MODEL), w_bf16)


def _a_prep_kernel(x_ref, xp_ref, xn_ref, g_ref, cw_ref, alog_ref, dtb_ref,
                   q_ref, k_ref, v_ref, go_ref, xw_ref, *, geom):
    tm = geom.tm
    i = pl.program_id(0)
    valid = _row_valid(geom, i * tm, tm)
    xw_ref[0:HALO] = jnp.where(i > 0, xp_ref[...], 0.0)
    xw_ref[HALO:HALO + tm] = x_ref[...]
    xw_ref[HALO + tm:2 * HALO + tm] = jnp.where(i < pl.num_programs(0) - 1, xn_ref[...], 0.0)
    half = A_CONV // 2
    for c, o_ref in enumerate((q_ref, k_ref, v_ref)):
        cs = slice(c * A_WIDTH, (c + 1) * A_WIDTH)
        acc = xw_ref[HALO - half:HALO - half + tm, cs] * cw_ref[0:1, cs]
        for j in range(1, A_CONV):
            acc = acc + xw_ref[HALO - half + j:HALO - half + j + tm, cs] * cw_ref[j:j + 1, cs]
        y = jnp.where(valid, acc * _sigmoid(acc), 0.0)
        for hd in range(A_HEADS):
            hs = slice(hd * A_DK, (hd + 1) * A_DK)
            yh = y[:, hs]
            if c < 2:
                yh = yh * lax.rsqrt(jnp.sum(yh * yh, axis=-1, keepdims=True) + 1e-6)
                if c == 0:
                    yh = yh * (A_DK ** -0.5)
            o_ref[:, hs] = yh
    al = g_ref[...]
    lane = lax.broadcasted_iota(jnp.int32, (1, LANE), 1)
    gval = jnp.where(valid, -jnp.exp(alog_ref[...]) * _softplus(al + dtb_ref[...]), 0.0)
    beta = jnp.where(valid, _sigmoid(al), 0.0)
    cum_f = _chunk_cumsum(gval, False)
    cum_b = _chunk_cumsum(gval, True)
    go_ref[...] = jnp.where(lane < A_HEADS, cum_f,
                            jnp.where(lane < 2 * A_HEADS, cum_b,
                                      jnp.where(lane < 4 * A_HEADS, beta, 0.0)))


def _a_prep(pa, ga, conv_w, a_log, dt_bias, geom):
    tm, rows = geom.tm, geom.rows
    qkv_w = 3 * A_WIDTH
    pad = lambda v: jnp.pad(v.reshape(1, -1).astype(F32), ((0, 0), (0, LANE - v.size)))
    out = jax.ShapeDtypeStruct((rows, A_WIDTH), F32)
    return pl.pallas_call(
        functools.partial(_a_prep_kernel, geom=geom),
        out_shape=[out, out, out, jax.ShapeDtypeStruct((rows, LANE), F32)],
        grid=(rows // tm,),
        in_specs=_halo_specs(tm, qkv_w, rows) + [
            pl.BlockSpec((tm, LANE), lambda i: (i, 0)),
            _const_spec((A_CONV, qkv_w)), _const_spec((1, LANE)), _const_spec((1, LANE))],
        out_specs=[pl.BlockSpec((tm, A_WIDTH), lambda i: (i, 0))] * 3 + [pl.BlockSpec((tm, LANE), lambda i: (i, 0))],
        scratch_shapes=[pltpu.VMEM((tm + 2 * HALO, qkv_w), F32)],
        compiler_params=_params(),
        name="a_prep",
    )(pa, pa, pa, ga, conv_w, pad(a_log), pad(dt_bias))


def _a_scan_kernel(fi_ref, bi_ref, fr_ref,
                   qf, kf, vf, gf, gtf, qb, kb, vb, gb, gtb, of, ob, sf, sb):
    del fi_ref, bi_ref
    @pl.when(fr_ref[pl.program_id(0)] == 1)
    def _():
        sf[...] = jnp.zeros_like(sf)
        sb[...] = jnp.zeros_like(sb)
    r = lax.broadcasted_iota(jnp.int32, (CHUNK, CHUNK), 0)
    c = lax.broadcasted_iota(jnp.int32, (CHUNK, CHUNK), 1)
    eye = jnp.where(r == c, 1.0, 0.0)
    nc = SEQ_BLK // CHUNK
    dirs = ((qf, kf, vf, gf, gtf, of, sf, False, r >= c, r > c),
            (qb, kb, vb, gb, gtb, ob, sb, True, r <= c, r < c))

    probs = []
    for d, (q_ref, k_ref, v_ref, g_ref, gt_ref, _, _, reverse, tri, strict) in enumerate(dirs):
        edge = 0 if reverse else CHUNK - 1
        for step in range(nc):
            jj = nc - 1 - step if reverse else step
            rows = slice(jj * CHUNK, (jj + 1) * CHUNK)
            for hd in range(A_HEADS):
                hs = slice(hd * A_DK, (hd + 1) * A_DK)
                col = hd + (A_HEADS if reverse else 0)
                q, k, v = q_ref[rows, hs], k_ref[rows, hs], v_ref[rows, hs]
                g_col = g_ref[rows, col:col + 1]
                b_col = g_ref[rows, 2 * A_HEADS + col:2 * A_HEADS + col + 1]
                g_row = gt_ref[jj, col:col + 1, :]
                decay = jnp.where(tri, jnp.exp(jnp.where(tri, g_col - g_row, 0.0)), 0.0)
                kb_ = k * b_col
                e_g = jnp.exp(g_col)
                g_last = g_col[edge:edge + 1]
                probs.append(dict(d=d, step=step, hd=hd, rows=rows, hs=hs, tri=tri, strict=strict, decay=decay,
                                  k=k, kbq=jnp.concatenate([kb_, q], axis=0),
                                  rhs=jnp.concatenate([v * b_col, kb_ * e_g], axis=1),
                                  qg=q * e_g, kg_t=(k * jnp.exp(g_last - g_col)).T, g_end=jnp.exp(g_last)))
    for p in probs:
        kq = _dot_nt(p["kbq"], p["k"])
        p["l"] = jnp.where(p["strict"], kq[:CHUNK] * p["decay"], 0.0)
        p["attn"] = jnp.where(p["tri"], kq[CHUNK:] * p["decay"], 0.0)
    t_invs = _neumann_inverses([-p["l"] for p in probs], eye)
    for p, t_inv in zip(probs, t_invs):
        p["uw"] = _dot(t_inv, p["rhs"])

    states = {(d, hd): dirs[d][6][hd] for d in range(2) for hd in range(A_HEADS)}
    for step in range(nc):
        cur = [p for p in probs if p["step"] == step]
        v_news = [p["uw"][:, :A_DV] - _dot(p["uw"][:, A_DV:], states[p["d"], p["hd"]]) for p in cur]
        for p, v_new in zip(cur, v_news):
            state = states[p["d"], p["hd"]]
            dirs[p["d"]][5][p["rows"], p["hs"]] = _dot(jnp.concatenate([p["qg"], p["attn"]], axis=1),
                                                      jnp.concatenate([state, v_new], axis=0))
        for p, v_new in zip(cur, v_news):
            key = (p["d"], p["hd"])
            states[key] = states[key] * p["g_end"] + _dot(p["kg_t"], v_new)
    for (d, hd), state in states.items():
        dirs[d][6][hd] = state


def _scan_tables(geom):
    fi, bi, fr = [], [], []
    for g in geom.groups:
        nblk = g.tp // SEQ_BLK
        for b in range(g.nb):
            base = (g.start + b * g.tp) // SEQ_BLK
            for i in range(nblk):
                fi.append(base + i)
                bi.append(base + nblk - 1 - i)
                fr.append(1 if i == 0 else 0)
    mk = lambda v: jnp.asarray(np.asarray(v, np.int32))
    return mk(fi), mk(bi), mk(fr)


def _a_scan(qn, kn, vv, gts, geom, tables):
    rows = geom.rows
    nc = SEQ_BLK // CHUNK
    gts_t = gts[:, :4 * A_HEADS].reshape(rows // CHUNK, CHUNK, 4 * A_HEADS).transpose(0, 2, 1)
    wide_f = pl.BlockSpec((SEQ_BLK, A_WIDTH), lambda s, fi, bi, fr: (fi[s], 0))
    wide_b = pl.BlockSpec((SEQ_BLK, A_WIDTH), lambda s, fi, bi, fr: (bi[s], 0))
    gate_f = pl.BlockSpec((SEQ_BLK, LANE), lambda s, fi, bi, fr: (fi[s], 0))
    gate_b = pl.BlockSpec((SEQ_BLK, LANE), lambda s, fi, bi, fr: (bi[s], 0))
    gt_f = pl.BlockSpec((nc, 4 * A_HEADS, CHUNK), lambda s, fi, bi, fr: (fi[s], 0, 0))
    gt_b = pl.BlockSpec((nc, 4 * A_HEADS, CHUNK), lambda s, fi, bi, fr: (bi[s], 0, 0))
    out = jax.ShapeDtypeStruct((rows, A_WIDTH), F32)
    return pl.pallas_call(
        _a_scan_kernel,
        out_shape=[out, out],
        grid_spec=pltpu.PrefetchScalarGridSpec(
            num_scalar_prefetch=3,
            grid=(tables[0].shape[0],),
            in_specs=[wide_f, wide_f, wide_f, gate_f, gt_f, wide_b, wide_b, wide_b, gate_b, gt_b],
            out_specs=[wide_f, wide_b],
            scratch_shapes=[pltpu.VMEM((A_HEADS, A_DK, A_DV), F32)] * 2),
        compiler_params=_params(("arbitrary",)),
        name="a_scan",
    )(*tables, qn, kn, vv, gts, gts_t, qn, kn, vv, gts, gts_t)


def _b_prep_kernel(x_ref, xp_ref, xn_ref, mu_ref, w0_ref, w2_ref, a0_ref, a2_ref, g2_ref,
                   kk_ref, ka_ref, rk_ref, ones_ref,
                   r_out, k_out, v_out, kkn_out, eta_out, cf_out, cb_out, bonus_out, gate_out,
                   xw_ref, *, geom):
    tm = geom.tm
    i = pl.program_id(0)
    xw_ref[0:HALO] = jnp.where(i > 0, xp_ref[...], 0.0)
    xw_ref[HALO:HALO + tm] = x_ref[...]
    xw_ref[HALO + tm:2 * HALO + tm] = jnp.where(i < pl.num_programs(0) - 1, xn_ref[...], 0.0)

    def shifted(cs):
        cur = xw_ref[HALO:HALO + tm, cs]
        prev = xw_ref[HALO - 1:HALO - 1 + tm, cs]
        nxt = xw_ref[HALO + 1:HALO + 1 + tm, cs]
        return cur + mu_ref[0:1, cs] * (prev - cur) + mu_ref[1:2, cs] * (nxt - cur)

    w = B_WIDTH
    r = shifted(slice(0, w))
    k = shifted(slice(w, 2 * w))
    v = shifted(slice(2 * w, 3 * w))
    lo = shifted(slice(3 * w, 3 * w + B_LORA_PAD))
    wl = _dot(jnp.tanh(lo), w2_ref[...]) + w0_ref[...]
    log_decay = -jnp.exp(-_softplus(-wl) - 0.5)
    cf_out[...] = _chunk_cumsum(log_decay[:, :w], False)
    cb_out[...] = _chunk_cumsum(log_decay[:, w:], True)
    eta = _sigmoid(a0_ref[...] + _dot(lo, a2_ref[...]))
    gate_out[...] = _dot(_sigmoid(lo), g2_ref[...])
    kx = k * kk_ref[...]
    kkn_out[...] = kx * lax.rsqrt(_seg_sum(kx * kx, ones_ref, 2) + 1e-6)
    k = k * (1.0 + (eta - 1.0) * ka_ref[...])
    bonus_out[...] = _seg_sum(r * k * rk_ref[...], ones_ref, 2) * v
    r_out[...] = r
    k_out[...] = k
    v_out[...] = v
    eta_out[...] = eta


def _b_prep(pb, mu, w0, w2, a0, a2, g2, k_k, k_a, r_k, ones, geom):
    tm, rows = geom.tm, geom.rows
    wd = 3 * B_WIDTH + B_LORA_PAD
    out = jax.ShapeDtypeStruct((rows, B_WIDTH), F32)
    row = lambda v: v.reshape(1, -1).astype(F32)
    return pl.pallas_call(
        functools.partial(_b_prep_kernel, geom=geom),
        out_shape=[out] * 9,
        grid=(rows // tm,),
        in_specs=_halo_specs(tm, wd, rows) + [
            _const_spec((2, wd)), _const_spec((1, 2 * B_WIDTH)), _const_spec((B_LORA_PAD, 2 * B_WIDTH)),
            _const_spec((1, B_WIDTH)), _const_spec((B_LORA_PAD, B_WIDTH)), _const_spec((B_LORA_PAD, B_WIDTH)),
            _const_spec((1, B_WIDTH)), _const_spec((1, B_WIDTH)), _const_spec((1, B_WIDTH)),
            _const_spec((B_WIDTH, B_WIDTH))],
        out_specs=[pl.BlockSpec((tm, B_WIDTH), lambda i: (i, 0))] * 9,
        scratch_shapes=[pltpu.VMEM((tm + 2 * HALO, wd), F32)],
        compiler_params=_params(),
        name="b_prep",
    )(pb, pb, pb, mu, w0, w2, row(a0), a2, g2, row(k_k), row(k_a), row(r_k), ones)


def _b_scan_kernel(fi_ref, bi_ref, fr_ref,
                   rf, kf, vf, kkf, ef, cf, rb, kb, vb, kkb, eb, cb, yf, yb, pf, pb):
    del fi_ref, bi_ref
    @pl.when(fr_ref[pl.program_id(0)] == 1)
    def _():
        pf[...] = jnp.zeros_like(pf)
        pb[...] = jnp.zeros_like(pb)
    n2 = 2 * CHUNK
    r = lax.broadcasted_iota(jnp.int32, (n2, n2), 0)
    c = lax.broadcasted_iota(jnp.int32, (n2, n2), 1)
    same = (r >= CHUNK) == (c >= CHUNK)
    rt, ct = r & (CHUNK - 1), c & (CHUNK - 1)
    eye = jnp.where(r == c, 1.0, 0.0)
    lane_lo = lax.broadcasted_iota(jnp.int32, (1, n2), 1) < B_HD
    row_pos = lax.broadcasted_iota(jnp.int32, (CHUNK, 1), 0)
    nc = SEQ_BLK // CHUNK
    pair_w = 2 * B_HD
    n_pairs = B_HEADS // 2
    dirs = (((rf, kf, vf, kkf, ef, cf), yf, pf, False, same & (rt > ct), same & (rt >= ct)),
            ((rb, kb, vb, kkb, eb, cb), yb, pb, True, same & (rt < ct), same & (rt <= ct)))

    def expand(x):
        return jnp.concatenate([jnp.where(lane_lo, x, 0.0), jnp.where(lane_lo, 0.0, x)], axis=0)

    probs = []
    for d, (refs, _, _, reverse, strict, incl) in enumerate(dirs):
        r_ref, k_ref, v_ref, kk_ref, eta_ref, c_ref = refs
        for step in range(nc):
            jj = nc - 1 - step if reverse else step
            rows = slice(jj * CHUNK, (jj + 1) * CHUNK)
            for pr in range(n_pairs):
                ls = slice(pr * pair_w, (pr + 1) * pair_w)
                r, k, v, kk, eta = (ref[rows, ls] for ref in (r_ref, k_ref, v_ref, kk_ref, eta_ref))
                c_in = c_ref[rows, ls]
                if reverse:
                    c_ex = jnp.where(row_pos < CHUNK - 1, pltpu.roll(c_in, CHUNK - 1, axis=0), 0.0)
                    c_tot = c_in[0:1]
                else:
                    c_ex = jnp.where(row_pos >= 1, pltpu.roll(c_in, 1, axis=0), 0.0)
                    c_tot = c_in[CHUNK - 1:CHUNK]
                b = kk * eta
                inv_w = jnp.exp(-c_in)
                rest_w = jnp.exp(c_tot - c_in)
                a_e = expand(-kk * jnp.exp(c_ex))
                r_e = expand(r * jnp.exp(c_in))
                probs.append(dict(
                    d=d, step=step, pr=pr, rows=rows, ls=ls, strict=strict, incl=incl, v=v,
                    a_e=a_e, r_e=r_e, v_e=expand(v),
                    lhs=jnp.concatenate([a_e, r_e], axis=0),
                    rhs=jnp.concatenate([expand(b * inv_w), expand(k * inv_w)], axis=0),
                    kd_t=jnp.concatenate([b * rest_w, k * rest_w], axis=0).T,
                    w_col=jnp.sum(jnp.where(eye > 0.0, jnp.exp(c_tot), 0.0), axis=1, keepdims=True)))
    for p in probs:
        m1 = _dot_nt(p["lhs"], p["rhs"])
        p["n_ab"] = jnp.where(p["strict"], m1[:n2, :n2], 0.0)
        p["a_ak"] = jnp.where(p["strict"], m1[:n2, n2:], 0.0)
        p["a_r"] = jnp.concatenate([jnp.where(p["incl"], m1[n2:, :n2], 0.0),
                                    jnp.where(p["incl"], m1[n2:, n2:], 0.0)], axis=1)
    t_invs = _neumann_inverses([p["n_ab"] for p in probs], eye)
    for p in probs:
        p["akv"] = _dot(p["a_ak"], p["v_e"])
    for p, t_inv in zip(probs, t_invs):
        p["x"] = _dot(t_inv, jnp.concatenate([p["akv"], p["a_e"]], axis=1))

    states = {(d, pr): dirs[d][2][pr] for d in range(2) for pr in range(n_pairs)}
    for step in range(nc):
        cur = [p for p in probs if p["step"] == step]
        u_es = [p["x"][:, :pair_w] + _dot(p["x"][:, pair_w:], states[p["d"], p["pr"]]) for p in cur]
        for p, u_e in zip(cur, u_es):
            y_e = _dot(jnp.concatenate([p["r_e"], p["a_r"]], axis=1),
                       jnp.concatenate([states[p["d"], p["pr"]], u_e, p["v_e"]], axis=0))
            dirs[p["d"]][1][p["rows"], p["ls"]] = y_e[:CHUNK] + y_e[CHUNK:]
        for p, u_e in zip(cur, u_es):
            key = (p["d"], p["pr"])
            upd = _dot(p["kd_t"], jnp.concatenate([u_e[:CHUNK] + u_e[CHUNK:], p["v"]], axis=0))
            states[key] = states[key] * p["w_col"] + jnp.where(same, upd, 0.0)
    for (d, pr), state in states.items():
        dirs[d][2][pr] = state


def _b_scan(r, k, v, kkn, eta, cf, cb, geom, tables):
    wide_f = pl.BlockSpec((SEQ_BLK, B_WIDTH), lambda s, fi, bi, fr: (fi[s], 0))
    wide_b = pl.BlockSpec((SEQ_BLK, B_WIDTH), lambda s, fi, bi, fr: (bi[s], 0))
    out = jax.ShapeDtypeStruct((geom.rows, B_WIDTH), F32)
    return pl.pallas_call(
        _b_scan_kernel,
        out_shape=[out, out],
        grid_spec=pltpu.PrefetchScalarGridSpec(
            num_scalar_prefetch=3,
            grid=(tables[0].shape[0],),
            in_specs=[wide_f] * 6 + [wide_b] * 6,
            out_specs=[wide_f, wide_b],
            scratch_shapes=[pltpu.VMEM((B_HEADS // 2, 2 * B_HD, 2 * B_HD), F32)] * 2),
        compiler_params=_params(("arbitrary",)),
        name="b_scan",
    )(*tables, r, k, v, kkn, eta, cf, r, k, v, kkn, eta, cb)


def _finish_residual(h, mix, post_w, valid):
    return jnp.where(valid, h + _rms(mix, post_w), 0.0)


def _mixout_even_kernel(h_ref, oaf_ref, oab_ref, z_ref, ybf_ref, ybb_ref, bonus_ref, gate_ref,
                        w_ref, an_ref, lnw_ref, lnb_ref, post_ref, ones_ref, o_ref, cat_ref, *, geom):
    tm = geom.tm
    valid = _row_valid(geom, pl.program_id(0) * tm, tm)
    for hd in range(A_HEADS):
        hs = slice(hd * A_DV, (hd + 1) * A_DV)
        o = oaf_ref[:, hs] + oab_ref[:, hs]
        z = z_ref[:, hs]
        cat_ref[:, hs] = (_rms(o, an_ref[...]) * (z * _sigmoid(z))).astype(BF16)
    y = ybf_ref[...] + ybb_ref[...]
    yc = y - _seg_sum(y, ones_ref, 3) * (1.0 / B_HD)
    var = _seg_sum(yc * yc, ones_ref, 2) * (1.0 / B_HD)
    yn = yc * lax.rsqrt(var + B_GN_EPS) * lnw_ref[...] + lnb_ref[...]
    cat_ref[:, A_WIDTH:] = ((yn + bonus_ref[...]) * gate_ref[...]).astype(BF16)
    mix = jnp.dot(cat_ref[...], w_ref[...], preferred_element_type=F32)
    o_ref[...] = _finish_residual(h_ref[...], mix, post_ref[...], valid)


def _mixout_even(h, oaf, oab, pa, ybf, ybb, bonus, gate, w_out, a_norm, ln_w, ln_b, post_w, ones, geom):
    tm, rows = geom.tm, geom.rows
    row = lambda v: v.reshape(1, -1).astype(F32)
    a_blk = pl.BlockSpec((tm, A_WIDTH), lambda i: (i, 0))
    b_blk = pl.BlockSpec((tm, B_WIDTH), lambda i: (i, 0))
    return pl.pallas_call(
        functools.partial(_mixout_even_kernel, geom=geom),
        out_shape=jax.ShapeDtypeStruct((rows, D_MODEL), F32),
        grid=(rows // tm,),
        in_specs=[pl.BlockSpec((tm, D_MODEL), lambda i: (i, 0)), a_blk, a_blk,
                  pl.BlockSpec((tm, A_WIDTH), lambda i: (i, 3)),
                  b_blk, b_blk, b_blk, b_blk,
                  _const_spec((A_WIDTH + B_WIDTH, D_MODEL)), _const_spec((1, A_DV)),
                  _const_spec((1, B_WIDTH)), _const_spec((1, B_WIDTH)), _const_spec((1, D_MODEL)),
                  _const_spec((B_WIDTH, B_WIDTH))],
        out_specs=pl.BlockSpec((tm, D_MODEL), lambda i: (i, 0)),
        scratch_shapes=[pltpu.VMEM((tm, A_WIDTH + B_WIDTH), BF16)],
        compiler_params=_params(),
        name="mixout_even",
    )(h, oaf, oab, pa, ybf, ybb, bonus, gate, w_out, row(a_norm), row(ln_w), row(ln_b), row(post_w), ones)


def _mixout_odd_kernel(h_ref, o_ref_in, w_ref, post_ref, o_ref, *, geom):
    tm = geom.tm
    valid = _row_valid(geom, pl.program_id(0) * tm, tm)
    mix = jnp.dot(o_ref_in[...].astype(BF16), w_ref[...], preferred_element_type=F32)
    o_ref[...] = _finish_residual(h_ref[...], mix, post_ref[...], valid)


def _mixout_odd(h, o, w_out, post_w, geom):
    tm, rows = geom.tm, geom.rows
    wd = C_HEADS * C_HD
    return pl.pallas_call(
        functools.partial(_mixout_odd_kernel, geom=geom),
        out_shape=jax.ShapeDtypeStruct((rows, D_MODEL), F32),
        grid=(rows // tm,),
        in_specs=[pl.BlockSpec((tm, D_MODEL), lambda i: (i, 0)), pl.BlockSpec((tm, wd), lambda i: (i, 0)),
                  _const_spec((wd, D_MODEL)), _const_spec((1, D_MODEL))],
        out_specs=pl.BlockSpec((tm, D_MODEL), lambda i: (i, 0)),
        compiler_params=_params(),
        name="mixout_odd",
    )(h, o, w_out, post_w.reshape(1, D_MODEL))


def _c_prep_kernel(x_ref, cos_ref, sin_ref, qn_ref, kn_ref, q_out, k_out, v_out):
    cos, sin = cos_ref[...], sin_ref[...]
    lane = lax.broadcasted_iota(jnp.int32, (1, C_HD), 1)
    first_half = (lane & (C_HD // 2 - 1)) < C_HD // 4

    def rope(xh, w, scale):
        xh = _rms(xh, w)
        partner = jnp.where(first_half, pltpu.roll(xh, C_HD - C_HD // 4, axis=1), pltpu.roll(xh, C_HD // 4, axis=1))
        return (xh * cos + partner * sin) * scale

    nq = C_HEADS * C_HD
    nk = C_KV_HEADS * C_HD
    for hd in range(C_HEADS):
        hs = slice(hd * C_HD, (hd + 1) * C_HD)
        q_out[:, hs] = rope(x_ref[:, hs], qn_ref[...], C_HD ** -0.5 * LOG2E).astype(BF16)
    for hd in range(C_KV_HEADS):
        hs = slice(hd * C_HD, (hd + 1) * C_HD)
        k_out[:, hs] = rope(x_ref[:, nq + hd * C_HD:nq + (hd + 1) * C_HD], kn_ref[...], 1.0).astype(BF16)
        v_out[:, 2 * hd * C_HD:(2 * hd + 1) * C_HD] = x_ref[:, nq + nk + hd * C_HD:nq + nk + (hd + 1) * C_HD].astype(BF16)
        v_out[:, (2 * hd + 1) * C_HD:(2 * hd + 2) * C_HD] = jnp.ones((x_ref.shape[0], C_HD), BF16)


def _c_prep(qkv, cos_t, sin_t, q_norm, k_norm, geom):
    tm, rows = geom.tm, geom.rows
    nq, nk = C_HEADS * C_HD, C_KV_HEADS * C_HD
    return pl.pallas_call(
        _c_prep_kernel,
        out_shape=[jax.ShapeDtypeStruct((rows, nq), BF16), jax.ShapeDtypeStruct((rows, nk), BF16),
                   jax.ShapeDtypeStruct((rows, 2 * nk), BF16)],
        grid=(rows // tm,),
        in_specs=[pl.BlockSpec((tm, nq + 2 * nk), lambda i: (i, 0)),
                  pl.BlockSpec((tm, C_HD), lambda i: (i, 0)), pl.BlockSpec((tm, C_HD), lambda i: (i, 0)),
                  _const_spec((1, C_HD)), _const_spec((1, C_HD))],
        out_specs=[pl.BlockSpec((tm, nq), lambda i: (i, 0)), pl.BlockSpec((tm, nk), lambda i: (i, 0)),
                   pl.BlockSpec((tm, 2 * nk), lambda i: (i, 0))],
        compiler_params=_params(),
        name="c_prep",
    )(qkv, cos_t, sin_t, q_norm.reshape(1, C_HD), k_norm.reshape(1, C_HD))


def _rope_tables(geom):
    nf = C_HD // 4
    inv = ROPE_THETA ** (-jnp.arange(nf, dtype=F32) / nf)
    cos_parts, sin_parts = [], []
    for g in geom.groups:
        t = jnp.arange(g.tp, dtype=jnp.int32) - (g.fv + N_META)
        is_meta = t < 0
        row = jnp.where(is_meta, -1, t // GRID_W).astype(F32)
        col = jnp.where(is_meta, t + N_META, t % GRID_W).astype(F32)
        ang_r, ang_c = row[:, None] * inv, col[:, None] * inv
        cos = jnp.concatenate([jnp.cos(ang_r)] * 2 + [jnp.cos(ang_c)] * 2, axis=1)
        sin = jnp.concatenate([-jnp.sin(ang_r), jnp.sin(ang_r), -jnp.sin(ang_c), jnp.sin(ang_c)], axis=1)
        cos_parts.append(jnp.tile(cos, (g.nb, 1)))
        sin_parts.append(jnp.tile(sin, (g.nb, 1)))
    return jnp.concatenate(cos_parts, axis=0), jnp.concatenate(sin_parts, axis=0)


def _attn_kernel(q_ref, k_ref, v_ref, o_ref, s_ref, rm_ref, m_ref, acc_ref, *, tq, tk, tp, fv):
    q = jnp.concatenate([q_ref[0, :, g * C_HD:(g + 1) * C_HD] for g in range(C_GROUP)], axis=0)
    m_ref[...] = jnp.full_like(m_ref, -1e30)
    acc_ref[...] = jnp.zeros_like(acc_ref)
    j0 = fv // tk
    n = tp // tk - j0

    def keys(t):
        start = (j0 + t) * tk
        return pl.ds(start if isinstance(t, int) else pl.multiple_of(start, tk), tk)

    def produce(t, slot):
        s = lax.dot_general(q, k_ref[0, keys(t), :], (((1,), (1,)), ((), ())), preferred_element_type=F32)
        if isinstance(t, int) and t == 0 and fv % tk != 0:
            key = j0 * tk + lax.broadcasted_iota(jnp.int32, (1, tk), 1)
            s = jnp.where(key >= fv, s, -1e30)
        s_ref[slot] = s
        rm_ref[slot] = jnp.max(s, axis=-1, keepdims=True)

    def consume(t, slot):
        m_old = m_ref[...]
        m_new = jnp.maximum(m_old, rm_ref[slot])
        p = jnp.exp2(s_ref[slot] - m_new).astype(BF16)
        acc_ref[...] = jnp.exp2(m_old - m_new) * acc_ref[...] + jnp.dot(p, v_ref[0, keys(t), :],
                                                                        preferred_element_type=F32)
        m_ref[...] = m_new

    produce(0, 0)
    def body(i, carry):
        t = 2 * i
        produce(t + 1, 1)
        consume(t, 0)
        produce(t + 2, 0)
        consume(t + 1, 1)
        return carry
    pairs = (n - 1) // 2
    if pairs > 0:
        lax.fori_loop(0, pairs, body, 0)
    if (n - 1) % 2 == 1:
        produce(n - 1, 1)
        consume(n - 2, 0)
        consume(n - 1, 1)
    else:
        consume(n - 1, 0)
    acc = acc_ref[...]
    out = acc[:, :C_HD] / acc[:, C_HD:]
    for g in range(C_GROUP):
        o_ref[0, :, g * C_HD:(g + 1) * C_HD] = out[g * tq:(g + 1) * tq]


def _attention(qn, kn, vn, geom):
    outs = []
    for g in geom.groups:
        sl = slice(g.start, g.start + g.nb * g.tp)
        q3 = qn[sl].reshape(g.nb, g.tp, C_HEADS * C_HD)
        k3 = kn[sl].reshape(g.nb, g.tp, C_KV_HEADS * C_HD)
        v3 = vn[sl].reshape(g.nb, g.tp, 2 * C_KV_HEADS * C_HD)
        tq = SEQ_BLK
        tk = max(t for t in range(LANE, ATTN_MAX_TK + 1, LANE) if g.tp % t == 0)
        gw = C_GROUP * C_HD
        o3 = pl.pallas_call(
            functools.partial(_attn_kernel, tq=tq, tk=tk, tp=g.tp, fv=g.fv),
            out_shape=jax.ShapeDtypeStruct((g.nb, g.tp, C_HEADS * C_HD), F32),
            grid=(g.nb, C_KV_HEADS, g.tp // tq),
            in_specs=[pl.BlockSpec((1, tq, gw), lambda b, h, i: (b, i, h)),
                      pl.BlockSpec((1, g.tp, C_HD), lambda b, h, i: (b, 0, h)),
                      pl.BlockSpec((1, g.tp, 2 * C_HD), lambda b, h, i: (b, 0, h))],
            out_specs=pl.BlockSpec((1, tq, gw), lambda b, h, i: (b, i, h)),
            scratch_shapes=[pltpu.VMEM((2, C_GROUP * tq, tk), F32), pltpu.VMEM((2, C_GROUP * tq, 1), F32),
                            pltpu.VMEM((C_GROUP * tq, 1), F32), pltpu.VMEM((C_GROUP * tq, 2 * C_HD), F32)],
            compiler_params=_params(("parallel", "parallel", "parallel")),
            name="attention",
        )(q3, k3, v3)
        outs.append(o3.reshape(g.nb * g.tp, C_HEADS * C_HD))
    return jnp.concatenate(outs, axis=0)


def _ffn_kernel(x_ref, xp_ref, xn_ref, pre_ref, wu_ref, wg_ref, cw_ref, cb_ref, wo_ref, post_ref,
                o_ref, xs_ref, g_ref, acc_ref, *, tm):
    i = pl.program_id(0)
    xs_ref[0:HALO] = jnp.where(i > 0, _rms(xp_ref[...], pre_ref[...]), 0.0).astype(BF16)
    xs_ref[HALO:HALO + tm] = _rms(x_ref[...], pre_ref[...]).astype(BF16)
    xs_ref[HALO + tm:2 * HALO + tm] = jnp.where(i < pl.num_programs(0) - 1,
                                                _rms(xn_ref[...], pre_ref[...]), 0.0).astype(BF16)
    half = FFN_CONV // 2
    for c in range(D_FF // FF_CHUNK):
        cs = slice(c * FF_CHUNK, (c + 1) * FF_CHUNK)
        u = jnp.dot(xs_ref[HALO:HALO + tm], wu_ref[:, cs], preferred_element_type=F32)
        g_ref[...] = jnp.dot(xs_ref[...], wg_ref[:, cs], preferred_element_type=F32)
        gate = cb_ref[0:1, cs]
        for j in range(FFN_CONV):
            gate = gate + g_ref[HALO - half + j:HALO - half + j + tm] * cw_ref[j:j + 1, cs]
        act = 0.5 * gate * (1.0 + jnp.tanh(0.7978845608028654 * (gate + 0.044715 * gate * gate * gate)))
        part = jnp.dot((act * u).astype(BF16), wo_ref[cs, :], preferred_element_type=F32)
        if c == 0:
            acc_ref[...] = part
        else:
            acc_ref[...] += part
    o_ref[...] = x_ref[...] + _rms(acc_ref[...], post_ref[...])


def _conv_ffn(h, pre_w, w_in, conv_w, conv_b, w_out, post_w, geom):
    tm, rows = geom.tm, geom.rows
    assert D_FF % FF_CHUNK == 0
    wu = w_in[:, :D_FF].astype(BF16)
    wg = w_in[:, D_FF:].astype(BF16)
    return pl.pallas_call(
        functools.partial(_ffn_kernel, tm=tm),
        out_shape=jax.ShapeDtypeStruct((rows, D_MODEL), F32),
        grid=(rows // tm,),
        in_specs=_halo_specs(tm, D_MODEL, rows) + [
            _const_spec((1, D_MODEL)), _const_spec((D_MODEL, D_FF)), _const_spec((D_MODEL, D_FF)),
            _const_spec((FFN_CONV, D_FF)), _const_spec((1, D_FF)), _const_spec((D_FF, D_MODEL)),
            _const_spec((1, D_MODEL))],
        out_specs=pl.BlockSpec((tm, D_MODEL), lambda i: (i, 0)),
        scratch_shapes=[pltpu.VMEM((tm + 2 * HALO, D_MODEL), BF16), pltpu.VMEM((tm + 2 * HALO, FF_CHUNK), F32),
                        pltpu.VMEM((tm, D_MODEL), F32)],
        compiler_params=_params(),
        name="conv_ffn",
    )(h, h, h, pre_w.reshape(1, D_MODEL), wu, wg, conv_w, conv_b.reshape(1, D_FF),
      w_out.astype(BF16), post_w.reshape(1, D_MODEL))


def _even_layer(h, j, geom, tables, ones, p):
    aw, bw = A_WIDTH, B_WIDTH
    a_cols = 4 * aw + 4 * A_HEADS
    w_in = p["e_w_in"][j]
    lora = B_W_LORA + B_A_LORA + B_G_LORA
    zeros = lambda n: jnp.zeros((D_MODEL, n), F32)
    w_cat = jnp.concatenate([
        w_in[:, a_cols:a_cols + 3 * bw + lora], zeros(B_LORA_PAD - lora),
        w_in[:, 4 * aw:a_cols], zeros(LANE - 4 * A_HEADS),
        w_in[:, :4 * aw]], axis=1).astype(BF16)
    pb, ga, pa = _norm_proj(h, p["mix_pre_norm"][2 * j], w_cat, (3 * bw + B_LORA_PAD, LANE, 4 * aw), geom, "proj_even")

    qn, kn, vv, gts = _a_prep(pa, ga, p["a_conv_w"][j], p["a_log"][j], p["a_dt_bias"][j], geom)
    oaf, oab = _a_scan(qn, kn, vv, gts, geom, tables)

    mu = jnp.pad(p["b_shift"][j], ((0, 0), (0, B_LORA_PAD - lora)))
    lora_rows = lambda m, off: jnp.pad(m, ((off, B_LORA_PAD - off - m.shape[0]), (0, 0)))
    w2 = lora_rows(jnp.concatenate([p["b_w2"][j, 0], p["b_w2"][j, 1]], axis=1), 0)
    a2 = lora_rows(p["b_a2"][j], B_W_LORA)
    g2 = lora_rows(p["b_g2"][j], B_W_LORA + B_A_LORA)
    r, k, v, kkn, eta, cf, cb, bonus, gate = _b_prep(
        pb, mu, p["b_w0"][j].reshape(1, 2 * bw), w2, p["b_a0"][j], a2, g2,
        p["b_k_k"][j], p["b_k_a"][j], p["b_r_k"][j], ones, geom)
    ybf, ybb = _b_scan(r, k, v, kkn, eta, cf, cb, geom, tables)

    return _mixout_even(h, oaf, oab, pa, ybf, ybb, bonus, gate, p["e_w_out"][j].astype(BF16),
                        p["a_out_norm"][j], p["b_ln_w"][j], p["b_ln_b"][j], p["mix_post_norm"][2 * j], ones, geom)


def _odd_layer(h, j, geom, rope, p):
    (qkv,) = _norm_proj(h, p["mix_pre_norm"][2 * j + 1], p["o_w_qkv"][j].astype(BF16),
                        ((C_HEADS + 2 * C_KV_HEADS) * C_HD,), geom, "proj_odd")
    qn, kn, vn = _c_prep(qkv, rope[0], rope[1], p["o_q_norm"][j], p["o_k_norm"][j], geom)
    o = _attention(qn, kn, vn, geom)
    return _mixout_odd(h, o, p["o_w_out"][j].astype(BF16), p["mix_post_norm"][2 * j + 1], geom)


def kernel(x_prompt, x_sample, meta, mix_pre_norm, mix_post_norm, ffn_pre_norm, ffn_post_norm, e_w_in, a_conv_w, a_log, a_dt_bias, a_out_norm, b_shift, b_w0, b_w2, b_a0, b_a2, b_g2, b_k_k, b_k_a, b_r_k, b_ln_w, b_ln_b, e_w_out, o_w_qkv, o_q_norm, o_k_norm, o_w_out, f_w_in, f_conv_w, f_conv_b, f_w_out):
    p = dict(mix_pre_norm=mix_pre_norm, mix_post_norm=mix_post_norm, e_w_in=e_w_in, a_conv_w=a_conv_w,
             a_log=a_log, a_dt_bias=a_dt_bias, a_out_norm=a_out_norm, b_shift=b_shift, b_w0=b_w0, b_w2=b_w2,
             b_a0=b_a0, b_a2=b_a2, b_g2=b_g2, b_k_k=b_k_k, b_k_a=b_k_a, b_r_k=b_r_k, b_ln_w=b_ln_w,
             b_ln_b=b_ln_b, e_w_out=e_w_out, o_w_qkv=o_w_qkv, o_q_norm=o_q_norm, o_k_norm=o_k_norm,
             o_w_out=o_w_out)
    xs = (x_prompt, x_sample)
    geom = _choose_geom([(x.shape[0], x.shape[1]) for x in xs])
    tables = _scan_tables(geom)
    rope = _rope_tables(geom)
    seg = np.arange(B_WIDTH) // B_HD
    ones = jnp.asarray(seg[:, None] == seg[None, :], BF16)

    parts = []
    for x, g in zip(xs, geom.groups):
        lead = jnp.concatenate([jnp.zeros((g.fv, D_MODEL), x.dtype), meta.astype(x.dtype)], axis=0)
        seq = jnp.concatenate([jnp.broadcast_to(lead[None], (g.nb,) + lead.shape), x], axis=1)
        parts.append(seq.reshape(g.nb * g.tp, D_MODEL))
    h = jnp.concatenate(parts, axis=0)

    for i in range(DEPTH):
        h = _even_layer(h, i // 2, geom, tables, ones, p) if i % 2 == 0 else _odd_layer(h, i // 2, geom, rope, p)
        h = _conv_ffn(h, ffn_pre_norm[i], f_w_in[i], f_conv_w[i], f_conv_b[i], f_w_out[i], ffn_post_norm[i], geom)

    outs = []
    for g in geom.groups:
        seq = h[g.start:g.start + g.nb * g.tp].reshape(g.nb, g.tp, D_MODEL)
        outs.append(seq[:, g.tp - g.n:])
    return tuple(outs)
```

```python
import functools
import itertools
from typing import NamedTuple

import numpy as np
import jax
import jax.numpy as jnp
from jax import lax
from jax.experimental import pallas as pl
from jax.experimental.pallas import tpu as pltpu

F32 = jnp.float32
BF16 = jnp.bfloat16

D_MODEL = 1024
DEPTH = 2
N_META = 16
GRID_W = 64
NORM_EPS = 1e-6

A_HEADS = 4
A_DK = 128
A_DV = 128
A_CONV = 5
A_WIDTH = A_HEADS * A_DV

B_HEADS = 8
B_HD = 64
B_WIDTH = B_HEADS * B_HD
B_W_LORA = 32
B_A_LORA = 32
B_G_LORA = 96
B_GN_EPS = 64e-5
B_LORA_PAD = 256

C_HEADS = 8
C_KV_HEADS = 2
C_GROUP = C_HEADS // C_KV_HEADS
C_HD = 128
ROPE_THETA = 10000.0

D_FF = 2816
FFN_CONV = 3

LANE = 128
CHUNK = 64
SEQ_BLK = 128
HALO = 16
FF_CHUNK = 256
ATTN_MAX_TK = 1664
ATTN_MAX_SUB = 17
LOG2E = 1.4426950408889634
GELU_C0 = 0.7978845608028654
VMEM_LIMIT = 56 * 1024 * 1024


class _Group(NamedTuple):
    start: int
    nb: int
    n: int
    tp: int
    fv: int


class _Geom(NamedTuple):
    groups: tuple
    rows: int
    tm: int


def _choose_geom(shapes):
    tm = 512 if min(n for _, n in shapes) >= 512 else 128
    best = None
    for pads in itertools.product((128, 256, 384, 512), repeat=len(shapes)):
        start, ok, groups = 0, True, []
        for (nb, n), p in zip(shapes, pads):
            ok = ok and start % tm == 0
            groups.append(_Group(start, nb, n, n + p, p - N_META))
            start += nb * (n + p)
        if ok and start % tm == 0 and (best is None or start < best.rows):
            best = _Geom(tuple(groups), start, tm)
    assert best is not None
    return best


def _row_valid(geom, base, tm):
    start = jnp.int32(geom.groups[0].start)
    tp = jnp.int32(geom.groups[0].tp)
    fv = jnp.int32(geom.groups[0].fv)
    for g in geom.groups[1:]:
        assert tm <= g.tp
        inside = base >= g.start
        start = jnp.where(inside, g.start, start)
        tp = jnp.where(inside, g.tp, tp)
        fv = jnp.where(inside, g.fv, fv)
    pos = lax.rem(base - start, tp) + lax.broadcasted_iota(jnp.int32, (tm, 1), 0)
    pos = jnp.where(pos >= tp, pos - tp, pos)
    return pos >= fv


def _rms(x, w, eps=NORM_EPS):
    return x * lax.rsqrt(jnp.mean(x * x, axis=-1, keepdims=True) + eps) * w


def _sigmoid(x):
    return 1.0 / (1.0 + jnp.exp(-x))


def _softplus(x):
    return jnp.maximum(x, 0.0) + jnp.log1p(jnp.exp(-jnp.abs(x)))


def _dot(a, b):
    return jnp.dot(a.astype(BF16), b.astype(BF16), preferred_element_type=F32)


def _dot_nt(a, b):
    return lax.dot_general(a.astype(BF16), b.astype(BF16), (((1,), (1,)), ((), ())),
                           preferred_element_type=F32)


def _seg_sum(x, ones_ref, terms):
    acc = None
    rem = x
    for t in range(terms):
        piece = rem.astype(BF16)
        part = jnp.dot(piece, ones_ref[...], preferred_element_type=F32)
        acc = part if acc is None else acc + part
        if t + 1 < terms:
            rem = rem - piece.astype(F32)
    return acc


def _chunk_cumsum(x, reverse):
    tm = x.shape[0]
    pos = lax.broadcasted_iota(jnp.int32, (tm, 1), 0) & (CHUNK - 1)
    s = 1
    while s < CHUNK:
        if reverse:
            x = x + jnp.where(pos < CHUNK - s, pltpu.roll(x, tm - s, axis=0), 0.0)
        else:
            x = x + jnp.where(pos >= s, pltpu.roll(x, s, axis=0), 0.0)
        s *= 2
    return x


def _neumann_inverses(n_mats, eye):
    prods = [eye + n for n in n_mats]
    powers = list(n_mats)
    s = 2
    while s < CHUNK:
        powers = [_dot(pw, pw) for pw in powers]
        prods = [pr + _dot(pr, pw) for pr, pw in zip(prods, powers)]
        s *= 2
    return prods


def _fill_window(xw_ref, x_ref, xp_ref, xn_ref, tm):
    i = pl.program_id(0)
    xw_ref[0:HALO] = jnp.where(i > 0, xp_ref[...].astype(F32), 0.0)
    xw_ref[HALO:HALO + tm] = x_ref[...].astype(F32)
    xw_ref[HALO + tm:2 * HALO + tm] = jnp.where(i < pl.num_programs(0) - 1, xn_ref[...].astype(F32), 0.0)


def _halo_specs(tm, width, rows):
    per = tm // HALO
    last = rows // HALO - 1
    return [
        pl.BlockSpec((tm, width), lambda i: (i, 0)),
        pl.BlockSpec((HALO, width), lambda i: (jnp.maximum(i * per - 1, 0), 0)),
        pl.BlockSpec((HALO, width), lambda i: (jnp.minimum((i + 1) * per, last), 0)),
    ]


def _const_spec(shape):
    return pl.BlockSpec(shape, lambda *_: (0,) * len(shape))


def _params(sem=("parallel",)):
    return pltpu.CompilerParams(dimension_semantics=sem, vmem_limit_bytes=VMEM_LIMIT)


def _proj_kernel(x_ref, nw_ref, w_ref, *rest, widths):
    out_refs, xn_ref = rest[:len(widths)], rest[len(widths)]
    xn_ref[...] = _rms(x_ref[...], nw_ref[...]).astype(BF16)
    c0 = 0
    for o_ref, wd in zip(out_refs, widths):
        for s in range(0, wd, 512):
            e = min(s + 512, wd)
            o_ref[:, s:e] = jnp.dot(xn_ref[...], w_ref[:, c0 + s:c0 + e],
                                    preferred_element_type=F32).astype(o_ref.dtype)
        c0 += wd


def _norm_proj(h, norm_w, w_bf16, widths, dtypes, geom, name):
    tm, rows = geom.tm, geom.rows
    return pl.pallas_call(
        functools.partial(_proj_kernel, widths=widths),
        out_shape=[jax.ShapeDtypeStruct((rows, wd), dt) for wd, dt in zip(widths, dtypes)],
        grid=(rows // tm,),
        in_specs=[pl.BlockSpec((tm, D_MODEL), lambda i: (i, 0)),
                  _const_spec((1, D_MODEL)),
                  _const_spec(w_bf16.shape)],
        out_specs=[pl.BlockSpec((tm, wd), lambda i: (i, 0)) for wd in widths],
        scratch_shapes=[pltpu.VMEM((tm, D_MODEL), BF16)],
        compiler_params=_params(),
        name=name,
    )(h, norm_w.reshape(1, D_MODEL), w_bf16)


def _a_prep_kernel(x_ref, xp_ref, xn_ref, g_ref, cw_ref, alog_ref, dtb_ref,
                   q_ref, k_ref, v_ref, go_ref, xw_ref, *, geom):
    tm = geom.tm
    i = pl.program_id(0)
    valid = _row_valid(geom, i * tm, tm)
    _fill_window(xw_ref, x_ref, xp_ref, xn_ref, tm)
    half = A_CONV // 2
    for c, o_ref in enumerate((q_ref, k_ref, v_ref)):
        cs = slice(c * A_WIDTH, (c + 1) * A_WIDTH)
        acc = xw_ref[HALO - half:HALO - half + tm, cs] * cw_ref[0:1, cs]
        for j in range(1, A_CONV):
            acc = acc + xw_ref[HALO - half + j:HALO - half + j + tm, cs] * cw_ref[j:j + 1, cs]
        y = jnp.where(valid, acc * _sigmoid(acc), 0.0)
        for hd in range(A_HEADS):
            hs = slice(hd * A_DK, (hd + 1) * A_DK)
            yh = y[:, hs]
            if c < 2:
                yh = yh * lax.rsqrt(jnp.sum(yh * yh, axis=-1, keepdims=True) + 1e-6)
                if c == 0:
                    yh = yh * (A_DK ** -0.5)
            o_ref[:, hs] = yh.astype(o_ref.dtype)
    al = g_ref[...]
    lane = lax.broadcasted_iota(jnp.int32, (1, LANE), 1)
    gval = jnp.where(valid, -jnp.exp(alog_ref[...]) * _softplus(al + dtb_ref[...]), 0.0)
    beta = jnp.where(valid, _sigmoid(al), 0.0)
    cum_f = _chunk_cumsum(gval, False)
    cum_b = _chunk_cumsum(gval, True)
    go_ref[...] = jnp.where(lane < A_HEADS, cum_f,
                            jnp.where(lane < 2 * A_HEADS, cum_b,
                                      jnp.where(lane < 4 * A_HEADS, beta, 0.0)))


def _a_prep(pa, ga, conv_w, a_log, dt_bias, geom):
    tm, rows = geom.tm, geom.rows
    qkv_w = 3 * A_WIDTH
    pad = lambda v: jnp.pad(v.reshape(1, -1).astype(F32), ((0, 0), (0, LANE - v.size)))
    out = jax.ShapeDtypeStruct((rows, A_WIDTH), BF16)
    return pl.pallas_call(
        functools.partial(_a_prep_kernel, geom=geom),
        out_shape=[out, out, out, jax.ShapeDtypeStruct((rows, LANE), F32)],
        grid=(rows // tm,),
        in_specs=_halo_specs(tm, qkv_w, rows) + [
            pl.BlockSpec((tm, LANE), lambda i: (i, 0)),
            _const_spec((A_CONV, qkv_w)), _const_spec((1, LANE)), _const_spec((1, LANE))],
        out_specs=[pl.BlockSpec((tm, A_WIDTH), lambda i: (i, 0))] * 3 + [pl.BlockSpec((tm, LANE), lambda i: (i, 0))],
        scratch_shapes=[pltpu.VMEM((tm + 2 * HALO, qkv_w), F32)],
        compiler_params=_params(),
        name="a_prep",
    )(pa, pa, pa, ga, conv_w, pad(a_log), pad(dt_bias))


def _a_scan_kernel(fi_ref, bi_ref, fr_ref,
                   qf, kf, vf, gf, gtf, qb, kb, vb, gb, gtb, of, ob, sf, sb):
    del fi_ref, bi_ref
    @pl.when(fr_ref[pl.program_id(0)] == 1)
    def _():
        sf[...] = jnp.zeros_like(sf)
        sb[...] = jnp.zeros_like(sb)
    r = lax.broadcasted_iota(jnp.int32, (CHUNK, CHUNK), 0)
    c = lax.broadcasted_iota(jnp.int32, (CHUNK, CHUNK), 1)
    eye = jnp.where(r == c, 1.0, 0.0)
    nc = SEQ_BLK // CHUNK
    dirs = ((qf, kf, vf, gf, gtf, of, sf, False, r >= c, r > c),
            (qb, kb, vb, gb, gtb, ob, sb, True, r <= c, r < c))

    probs = []
    for d, (q_ref, k_ref, v_ref, g_ref, gt_ref, _, _, reverse, tri, strict) in enumerate(dirs):
        edge = 0 if reverse else CHUNK - 1
        for step in range(nc):
            jj = nc - 1 - step if reverse else step
            rows = slice(jj * CHUNK, (jj + 1) * CHUNK)
            for hd in range(A_HEADS):
                hs = slice(hd * A_DK, (hd + 1) * A_DK)
                col = hd + (A_HEADS if reverse else 0)
                q, k, v = (ref[rows, hs].astype(F32) for ref in (q_ref, k_ref, v_ref))
                g_col = g_ref[rows, col:col + 1]
                b_col = g_ref[rows, 2 * A_HEADS + col:2 * A_HEADS + col + 1]
                g_row = gt_ref[jj, col:col + 1, :]
                decay = jnp.where(tri, jnp.exp(jnp.where(tri, g_col - g_row, 0.0)), 0.0)
                kb_ = k * b_col
                e_g = jnp.exp(g_col)
                g_last = g_col[edge:edge + 1]
                probs.append(dict(d=d, step=step, hd=hd, rows=rows, hs=hs, tri=tri, strict=strict, decay=decay,
                                  k=k, kbq=jnp.concatenate([kb_, q], axis=0),
                                  rhs=jnp.concatenate([v * b_col, kb_ * e_g], axis=1),
                                  qg=q * e_g, kg_t=(k * jnp.exp(g_last - g_col)).T, g_end=jnp.exp(g_last)))
    for p in probs:
        kq = _dot_nt(p["kbq"], p["k"])
        p["l"] = jnp.where(p["strict"], kq[:CHUNK] * p["decay"], 0.0)
        p["attn"] = jnp.where(p["tri"], kq[CHUNK:] * p["decay"], 0.0)
    t_invs = _neumann_inverses([-p["l"] for p in probs], eye)
    for p, t_inv in zip(probs, t_invs):
        p["uw"] = _dot(t_inv, p["rhs"])

    states = {(d, hd): dirs[d][6][hd] for d in range(2) for hd in range(A_HEADS)}
    for step in range(nc):
        cur = [p for p in probs if p["step"] == step]
        v_news = [p["uw"][:, :A_DV] - _dot(p["uw"][:, A_DV:], states[p["d"], p["hd"]]) for p in cur]
        for p, v_new in zip(cur, v_news):
            state = states[p["d"], p["hd"]]
            dirs[p["d"]][5][p["rows"], p["hs"]] = _dot(jnp.concatenate([p["qg"], p["attn"]], axis=1),
                                                      jnp.concatenate([state, v_new], axis=0))
        for p, v_new in zip(cur, v_news):
            key = (p["d"], p["hd"])
            states[key] = states[key] * p["g_end"] + _dot(p["kg_t"], v_new)
    for (d, hd), state in states.items():
        dirs[d][6][hd] = state


def _scan_tables(geom):
    fi, bi, fr = [], [], []
    for g in geom.groups:
        nblk = g.tp // SEQ_BLK
        for b in range(g.nb):
            base = (g.start + b * g.tp) // SEQ_BLK
            for i in range(nblk):
                fi.append(base + i)
                bi.append(base + nblk - 1 - i)
                fr.append(1 if i == 0 else 0)
    mk = lambda v: jnp.asarray(np.asarray(v, np.int32))
    return mk(fi), mk(bi), mk(fr)


def _a_scan(qn, kn, vv, gts, geom, tables):
    rows = geom.rows
    nc = SEQ_BLK // CHUNK
    gts_t = gts[:, :4 * A_HEADS].reshape(rows // CHUNK, CHUNK, 4 * A_HEADS).transpose(0, 2, 1)
    wide_f = pl.BlockSpec((SEQ_BLK, A_WIDTH), lambda s, fi, bi, fr: (fi[s], 0))
    wide_b = pl.BlockSpec((SEQ_BLK, A_WIDTH), lambda s, fi, bi, fr: (bi[s], 0))
    gate_f = pl.BlockSpec((SEQ_BLK, LANE), lambda s, fi, bi, fr: (fi[s], 0))
    gate_b = pl.BlockSpec((SEQ_BLK, LANE), lambda s, fi, bi, fr: (bi[s], 0))
    gt_f = pl.BlockSpec((nc, 4 * A_HEADS, CHUNK), lambda s, fi, bi, fr: (fi[s], 0, 0))
    gt_b = pl.BlockSpec((nc, 4 * A_HEADS, CHUNK), lambda s, fi, bi, fr: (bi[s], 0, 0))
    out = jax.ShapeDtypeStruct((rows, A_WIDTH), F32)
    return pl.pallas_call(
        _a_scan_kernel,
        out_shape=[out, out],
        grid_spec=pltpu.PrefetchScalarGridSpec(
            num_scalar_prefetch=3,
            grid=(tables[0].shape[0],),
            in_specs=[wide_f, wide_f, wide_f, gate_f, gt_f, wide_b, wide_b, wide_b, gate_b, gt_b],
            out_specs=[wide_f, wide_b],
            scratch_shapes=[pltpu.VMEM((A_HEADS, A_DK, A_DV), F32)] * 2),
        compiler_params=_params(("arbitrary",)),
        name="a_scan",
    )(*tables, qn, kn, vv, gts, gts_t, qn, kn, vv, gts, gts_t)


def _b_prep_kernel(x_ref, xp_ref, xn_ref, l_ref, lp_ref, ln_ref, mu_ref, w0_ref, w2_ref, a0_ref, a2_ref, g2_ref,
                   kk_ref, ka_ref, rk_ref, ones_ref,
                   r_out, k_out, v_out, kkn_out, eta_out, cf_out, cb_out, bonus_out, gate_out,
                   xw_ref, lw_ref, *, geom):
    tm = geom.tm
    _fill_window(xw_ref, x_ref, xp_ref, xn_ref, tm)
    _fill_window(lw_ref, l_ref, lp_ref, ln_ref, tm)

    def shifted(win_ref, cs, mu_off):
        cur = win_ref[HALO:HALO + tm, cs]
        prev = win_ref[HALO - 1:HALO - 1 + tm, cs]
        nxt = win_ref[HALO + 1:HALO + 1 + tm, cs]
        ms = slice(mu_off + cs.start, mu_off + cs.stop)
        return cur + mu_ref[0:1, ms] * (prev - cur) + mu_ref[1:2, ms] * (nxt - cur)

    w = B_WIDTH
    r = shifted(xw_ref, slice(0, w), 0)
    k = shifted(xw_ref, slice(w, 2 * w), 0)
    v = shifted(xw_ref, slice(2 * w, 3 * w), 0)
    lo = shifted(lw_ref, slice(0, B_LORA_PAD), 3 * w)
    wl = _dot(jnp.tanh(lo), w2_ref[...]) + w0_ref[...]
    log_decay = -jnp.exp(-_softplus(-wl) - 0.5)
    cf_out[...] = _chunk_cumsum(log_decay[:, :w], False)
    cb_out[...] = _chunk_cumsum(log_decay[:, w:], True)
    eta = _sigmoid(a0_ref[...] + _dot(lo, a2_ref[...]))
    gate_out[...] = _dot(_sigmoid(lo), g2_ref[...])
    kx = k * kk_ref[...]
    kkn_out[...] = (kx * lax.rsqrt(_seg_sum(kx * kx, ones_ref, 2) + 1e-6)).astype(kkn_out.dtype)
    k = k * (1.0 + (eta - 1.0) * ka_ref[...])
    bonus_out[...] = _seg_sum(r * k * rk_ref[...], ones_ref, 2) * v
    r_out[...] = r.astype(r_out.dtype)
    k_out[...] = k.astype(k_out.dtype)
    v_out[...] = v.astype(v_out.dtype)
    eta_out[...] = eta.astype(eta_out.dtype)


def _b_prep(pb, lo, mu, w0, w2, a0, a2, g2, k_k, k_a, r_k, ones, geom):
    tm, rows = geom.tm, geom.rows
    wd = 3 * B_WIDTH
    narrow = jax.ShapeDtypeStruct((rows, B_WIDTH), BF16)
    wide = jax.ShapeDtypeStruct((rows, B_WIDTH), F32)
    row = lambda v: v.reshape(1, -1).astype(F32)
    return pl.pallas_call(
        functools.partial(_b_prep_kernel, geom=geom),
        out_shape=[narrow] * 5 + [wide] * 4,
        grid=(rows // tm,),
        in_specs=_halo_specs(tm, wd, rows) + _halo_specs(tm, B_LORA_PAD, rows) + [
            _const_spec((2, wd + B_LORA_PAD)), _const_spec((1, 2 * B_WIDTH)), _const_spec((B_LORA_PAD, 2 * B_WIDTH)),
            _const_spec((1, B_WIDTH)), _const_spec((B_LORA_PAD, B_WIDTH)), _const_spec((B_LORA_PAD, B_WIDTH)),
            _const_spec((1, B_WIDTH)), _const_spec((1, B_WIDTH)), _const_spec((1, B_WIDTH)),
            _const_spec((B_WIDTH, B_WIDTH))],
        out_specs=[pl.BlockSpec((tm, B_WIDTH), lambda i: (i, 0))] * 9,
        scratch_shapes=[pltpu.VMEM((tm + 2 * HALO, wd), F32), pltpu.VMEM((tm + 2 * HALO, B_LORA_PAD), F32)],
        compiler_params=_params(),
        name="b_prep",
    )(pb, pb, pb, lo, lo, lo, mu, w0, w2, row(a0), a2, g2, row(k_k), row(k_a), row(r_k), ones)


def _b_scan_kernel(fi_ref, bi_ref, fr_ref,
                   rf, kf, vf, kkf, ef, cf, rb, kb, vb, kkb, eb, cb, yf, yb, pf, pb):
    del fi_ref, bi_ref
    @pl.when(fr_ref[pl.program_id(0)] == 1)
    def _():
        pf[...] = jnp.zeros_like(pf)
        pb[...] = jnp.zeros_like(pb)
    n2 = 2 * CHUNK
    r = lax.broadcasted_iota(jnp.int32, (n2, n2), 0)
    c = lax.broadcasted_iota(jnp.int32, (n2, n2), 1)
    same = (r >= CHUNK) == (c >= CHUNK)
    rt, ct = r & (CHUNK - 1), c & (CHUNK - 1)
    eye = jnp.where(r == c, 1.0, 0.0)
    lane_lo = lax.broadcasted_iota(jnp.int32, (1, n2), 1) < B_HD
    row_pos = lax.broadcasted_iota(jnp.int32, (CHUNK, 1), 0)
    nc = SEQ_BLK // CHUNK
    pair_w = 2 * B_HD
    n_pairs = B_HEADS // 2
    dirs = (((rf, kf, vf, kkf, ef, cf), yf, pf, False, same & (rt > ct), same & (rt >= ct)),
            ((rb, kb, vb, kkb, eb, cb), yb, pb, True, same & (rt < ct), same & (rt <= ct)))

    def expand(x):
        return jnp.concatenate([jnp.where(lane_lo, x, 0.0), jnp.where(lane_lo, 0.0, x)], axis=0)

    probs = []
    for d, (refs, _, _, reverse, strict, incl) in enumerate(dirs):
        r_ref, k_ref, v_ref, kk_ref, eta_ref, c_ref = refs
        for step in range(nc):
            jj = nc - 1 - step if reverse else step
            rows = slice(jj * CHUNK, (jj + 1) * CHUNK)
            for pr in range(n_pairs):
                ls = slice(pr * pair_w, (pr + 1) * pair_w)
                r, k, v, kk, eta = (ref[rows, ls].astype(F32) for ref in (r_ref, k_ref, v_ref, kk_ref, eta_ref))
                c_in = c_ref[rows, ls]
                if reverse:
                    c_ex = jnp.where(row_pos < CHUNK - 1, pltpu.roll(c_in, CHUNK - 1, axis=0), 0.0)
                    c_tot = c_in[0:1]
                else:
                    c_ex = jnp.where(row_pos >= 1, pltpu.roll(c_in, 1, axis=0), 0.0)
                    c_tot = c_in[CHUNK - 1:CHUNK]
                b = kk * eta
                inv_w = jnp.exp(-c_in)
                rest_w = jnp.exp(c_tot - c_in)
                a_e = expand(-kk * jnp.exp(c_ex))
                r_e = expand(r * jnp.exp(c_in))
                probs.append(dict(
                    d=d, step=step, pr=pr, rows=rows, ls=ls, strict=strict, incl=incl, v=v,
                    a_e=a_e, r_e=r_e, v_e=expand(v),
                    lhs=jnp.concatenate([a_e, r_e], axis=0),
                    rhs=jnp.concatenate([expand(b * inv_w), expand(k * inv_w)], axis=0),
                    kd_t=jnp.concatenate([b * rest_w, k * rest_w], axis=0).T,
                    w_col=jnp.sum(jnp.where(eye > 0.0, jnp.exp(c_tot), 0.0), axis=1, keepdims=True)))
    for p in probs:
        m1 = _dot_nt(p["lhs"], p["rhs"])
        p["n_ab"] = jnp.where(p["strict"], m1[:n2, :n2], 0.0)
        p["a_ak"] = jnp.where(p["strict"], m1[:n2, n2:], 0.0)
        p["a_r"] = jnp.concatenate([jnp.where(p["incl"], m1[n2:, :n2], 0.0),
                                    jnp.where(p["incl"], m1[n2:, n2:], 0.0)], axis=1)
    t_invs = _neumann_inverses([p["n_ab"] for p in probs], eye)
    for p in probs:
        p["akv"] = _dot(p["a_ak"], p["v_e"])
    for p, t_inv in zip(probs, t_invs):
        p["x"] = _dot(t_inv, jnp.concatenate([p["akv"], p["a_e"]], axis=1))

    states = {(d, pr): dirs[d][2][pr] for d in range(2) for pr in range(n_pairs)}
    for step in range(nc):
        cur = [p for p in probs if p["step"] == step]
        u_es = [p["x"][:, :pair_w] + _dot(p["x"][:, pair_w:], states[p["d"], p["pr"]]) for p in cur]
        for p, u_e in zip(cur, u_es):
            y_e = _dot(jnp.concatenate([p["r_e"], p["a_r"]], axis=1),
                       jnp.concatenate([states[p["d"], p["pr"]], u_e, p["v_e"]], axis=0))
            dirs[p["d"]][1][p["rows"], p["ls"]] = y_e[:CHUNK] + y_e[CHUNK:]
        for p, u_e in zip(cur, u_es):
            key = (p["d"], p["pr"])
            upd = _dot(p["kd_t"], jnp.concatenate([u_e[:CHUNK] + u_e[CHUNK:], p["v"]], axis=0))
            states[key] = states[key] * p["w_col"] + jnp.where(same, upd, 0.0)
    for (d, pr), state in states.items():
        dirs[d][2][pr] = state


def _b_scan(r, k, v, kkn, eta, cf, cb, geom, tables):
    wide_f = pl.BlockSpec((SEQ_BLK, B_WIDTH), lambda s, fi, bi, fr: (fi[s], 0))
    wide_b = pl.BlockSpec((SEQ_BLK, B_WIDTH), lambda s, fi, bi, fr: (bi[s], 0))
    out = jax.ShapeDtypeStruct((geom.rows, B_WIDTH), F32)
    return pl.pallas_call(
        _b_scan_kernel,
        out_shape=[out, out],
        grid_spec=pltpu.PrefetchScalarGridSpec(
            num_scalar_prefetch=3,
            grid=(tables[0].shape[0],),
            in_specs=[wide_f] * 6 + [wide_b] * 6,
            out_specs=[wide_f, wide_b],
            scratch_shapes=[pltpu.VMEM((B_HEADS // 2, 2 * B_HD, 2 * B_HD), F32)] * 2),
        compiler_params=_params(("arbitrary",)),
        name="b_scan",
    )(*tables, r, k, v, kkn, eta, cf, r, k, v, kkn, eta, cb)


def _finish_residual(h, mix, post_w, valid):
    return jnp.where(valid, h + _rms(mix, post_w), 0.0)


def _mixout_even_kernel(h_ref, oaf_ref, oab_ref, z_ref, ybf_ref, ybb_ref, bonus_ref, gate_ref,
                        w_ref, an_ref, lnw_ref, lnb_ref, post_ref, ones_ref, o_ref, cat_ref, *, geom):
    tm = geom.tm
    valid = _row_valid(geom, pl.program_id(0) * tm, tm)
    for hd in range(A_HEADS):
        hs = slice(hd * A_DV, (hd + 1) * A_DV)
        o = oaf_ref[:, hs] + oab_ref[:, hs]
        z = z_ref[:, hs].astype(F32)
        cat_ref[:, hs] = (_rms(o, an_ref[...]) * (z * _sigmoid(z))).astype(BF16)
    y = ybf_ref[...] + ybb_ref[...]
    yc = y - _seg_sum(y, ones_ref, 3) * (1.0 / B_HD)
    var = _seg_sum(yc * yc, ones_ref, 2) * (1.0 / B_HD)
    yn = yc * lax.rsqrt(var + B_GN_EPS) * lnw_ref[...] + lnb_ref[...]
    cat_ref[:, A_WIDTH:] = ((yn + bonus_ref[...]) * gate_ref[...]).astype(BF16)
    mix = jnp.dot(cat_ref[...], w_ref[...], preferred_element_type=F32)
    o_ref[...] = _finish_residual(h_ref[...], mix, post_ref[...], valid)


def _mixout_even(h, oaf, oab, pa, ybf, ybb, bonus, gate, w_out, a_norm, ln_w, ln_b, post_w, ones, geom):
    tm, rows = geom.tm, geom.rows
    row = lambda v: v.reshape(1, -1).astype(F32)
    a_blk = pl.BlockSpec((tm, A_WIDTH), lambda i: (i, 0))
    b_blk = pl.BlockSpec((tm, B_WIDTH), lambda i: (i, 0))
    return pl.pallas_call(
        functools.partial(_mixout_even_kernel, geom=geom),
        out_shape=jax.ShapeDtypeStruct((rows, D_MODEL), F32),
        grid=(rows // tm,),
        in_specs=[pl.BlockSpec((tm, D_MODEL), lambda i: (i, 0)), a_blk, a_blk,
                  pl.BlockSpec((tm, A_WIDTH), lambda i: (i, 3)),
                  b_blk, b_blk, b_blk, b_blk,
                  _const_spec((A_WIDTH + B_WIDTH, D_MODEL)), _const_spec((1, A_DV)),
                  _const_spec((1, B_WIDTH)), _const_spec((1, B_WIDTH)), _const_spec((1, D_MODEL)),
                  _const_spec((B_WIDTH, B_WIDTH))],
        out_specs=pl.BlockSpec((tm, D_MODEL), lambda i: (i, 0)),
        scratch_shapes=[pltpu.VMEM((tm, A_WIDTH + B_WIDTH), BF16)],
        compiler_params=_params(),
        name="mixout_even",
    )(h, oaf, oab, pa, ybf, ybb, bonus, gate, w_out, row(a_norm), row(ln_w), row(ln_b), row(post_w), ones)


def _mixout_odd_kernel(h_ref, o_ref_in, w_ref, post_ref, o_ref, *, geom):
    tm = geom.tm
    valid = _row_valid(geom, pl.program_id(0) * tm, tm)
    mix = jnp.dot(o_ref_in[...].astype(BF16), w_ref[...], preferred_element_type=F32)
    o_ref[...] = _finish_residual(h_ref[...], mix, post_ref[...], valid)


def _mixout_odd(h, o, w_out, post_w, geom):
    tm, rows = geom.tm, geom.rows
    wd = C_HEADS * C_HD
    return pl.pallas_call(
        functools.partial(_mixout_odd_kernel, geom=geom),
        out_shape=jax.ShapeDtypeStruct((rows, D_MODEL), F32),
        grid=(rows // tm,),
        in_specs=[pl.BlockSpec((tm, D_MODEL), lambda i: (i, 0)), pl.BlockSpec((tm, wd), lambda i: (i, 0)),
                  _const_spec((wd, D_MODEL)), _const_spec((1, D_MODEL))],
        out_specs=pl.BlockSpec((tm, D_MODEL), lambda i: (i, 0)),
        compiler_params=_params(),
        name="mixout_odd",
    )(h, o, w_out, post_w.reshape(1, D_MODEL))


def _qkv_kernel(x_ref, nw_ref, w_ref, cos_ref, sin_ref, qn_ref, kn_ref, q_out, k_out, v_out, xn_ref):
    xn_ref[...] = _rms(x_ref[...], nw_ref[...]).astype(BF16)
    cos, sin = cos_ref[...], sin_ref[...]
    lane = lax.broadcasted_iota(jnp.int32, (1, C_HD), 1)
    first_half = (lane & (C_HD // 2 - 1)) < C_HD // 4

    def rope(xh, w, scale):
        xh = _rms(xh, w)
        partner = jnp.where(first_half, pltpu.roll(xh, C_HD - C_HD // 4, axis=1), pltpu.roll(xh, C_HD // 4, axis=1))
        return (xh * cos + partner * sin) * scale

    nq = C_HEADS * C_HD
    nk = C_KV_HEADS * C_HD
    per = 512 // C_HD
    for c in range(nq // 512):
        y = jnp.dot(xn_ref[...], w_ref[:, c * 512:(c + 1) * 512], preferred_element_type=F32)
        for hd in range(per):
            hs = slice((c * per + hd) * C_HD, (c * per + hd + 1) * C_HD)
            q_out[:, hs] = rope(y[:, hd * C_HD:(hd + 1) * C_HD], qn_ref[...], C_HD ** -0.5 * LOG2E).astype(BF16)
    y = jnp.dot(xn_ref[...], w_ref[:, nq:], preferred_element_type=F32)
    for hd in range(C_KV_HEADS):
        hs = slice(hd * C_HD, (hd + 1) * C_HD)
        k_out[:, hs] = rope(y[:, hs], kn_ref[...], 1.0).astype(BF16)
        v_out[:, 2 * hd * C_HD:(2 * hd + 1) * C_HD] = y[:, nk + hd * C_HD:nk + (hd + 1) * C_HD].astype(BF16)
        v_out[:, (2 * hd + 1) * C_HD:(2 * hd + 2) * C_HD] = jnp.ones((x_ref.shape[0], C_HD), BF16)


def _qkv_proj(h, norm_w, w_bf16, cos_t, sin_t, q_norm, k_norm, geom):
    tm, rows = geom.tm, geom.rows
    nq, nk = C_HEADS * C_HD, C_KV_HEADS * C_HD
    return pl.pallas_call(
        _qkv_kernel,
        out_shape=[jax.ShapeDtypeStruct((rows, nq), BF16), jax.ShapeDtypeStruct((rows, nk), BF16),
                   jax.ShapeDtypeStruct((rows, 2 * nk), BF16)],
        grid=(rows // tm,),
        in_specs=[pl.BlockSpec((tm, D_MODEL), lambda i: (i, 0)), _const_spec((1, D_MODEL)),
                  _const_spec((D_MODEL, nq + 2 * nk)),
                  pl.BlockSpec((tm, C_HD), lambda i: (i, 0)), pl.BlockSpec((tm, C_HD), lambda i: (i, 0)),
                  _const_spec((1, C_HD)), _const_spec((1, C_HD))],
        out_specs=[pl.BlockSpec((tm, nq), lambda i: (i, 0)), pl.BlockSpec((tm, nk), lambda i: (i, 0)),
                   pl.BlockSpec((tm, 2 * nk), lambda i: (i, 0))],
        scratch_shapes=[pltpu.VMEM((tm, D_MODEL), BF16)],
        compiler_params=_params(),
        name="qkv_proj",
    )(h, norm_w.reshape(1, D_MODEL), w_bf16, cos_t, sin_t, q_norm.reshape(1, C_HD), k_norm.reshape(1, C_HD))


def _rope_tables(geom):
    nf = C_HD // 4
    inv = ROPE_THETA ** (-jnp.arange(nf, dtype=F32) / nf)
    cos_parts, sin_parts = [], []
    for g in geom.groups:
        t = jnp.arange(g.tp, dtype=jnp.int32) - (g.fv + N_META)
        is_meta = t < 0
        row = jnp.where(is_meta, -1, t // GRID_W).astype(F32)
        col = jnp.where(is_meta, t + N_META, t % GRID_W).astype(F32)
        ang_r, ang_c = row[:, None] * inv, col[:, None] * inv
        cos = jnp.concatenate([jnp.cos(ang_r)] * 2 + [jnp.cos(ang_c)] * 2, axis=1)
        sin = jnp.concatenate([-jnp.sin(ang_r), jnp.sin(ang_r), -jnp.sin(ang_c), jnp.sin(ang_c)], axis=1)
        cos_parts.append(jnp.tile(cos, (g.nb, 1)))
        sin_parts.append(jnp.tile(sin, (g.nb, 1)))
    return jnp.concatenate(cos_parts, axis=0), jnp.concatenate(sin_parts, axis=0)


def _attn_kernel(q_ref, k_ref, v_ref, o_ref, s_ref, rm_ref, m_ref, acc_ref, *, nsub, bounds, fv):
    tq = SEQ_BLK
    nc = len(bounds)
    assert nc % 2 == 0

    def q_rows(qi):
        return pl.ds(pl.multiple_of(qi * tq, tq), tq)

    def produce(qi, c, slot):
        start, size = bounds[c]
        q = jnp.concatenate([q_ref[0, q_rows(qi), g * C_HD:(g + 1) * C_HD] for g in range(C_GROUP)], axis=0)
        s = lax.dot_general(q, k_ref[0, start:start + size, :], (((1,), (1,)), ((), ())),
                            preferred_element_type=F32)
        if start < fv:
            key = start + lax.broadcasted_iota(jnp.int32, (1, size), 1)
            s = jnp.where(key >= fv, s, -1e30)
        s_ref[slot, :, :size] = s
        rm_ref[slot] = jnp.max(s, axis=-1, keepdims=True)

    def consume(c, slot):
        start, size = bounds[c]
        m_old = m_ref[...]
        m_new = jnp.maximum(m_old, rm_ref[slot])
        p = jnp.exp2(s_ref[slot, :, :size] - m_new).astype(BF16)
        acc_ref[...] = jnp.exp2(m_old - m_new) * acc_ref[...] + jnp.dot(p, v_ref[0, start:start + size, :],
                                                                        preferred_element_type=F32)
        m_ref[...] = m_new

    produce(0, 0, 0)

    def body(qi, carry):
        m_ref[...] = jnp.full_like(m_ref, -1e30)
        acc_ref[...] = jnp.zeros_like(acc_ref)
        for c in range(nc):
            if c + 1 < nc:
                produce(qi, c + 1, (c + 1) % 2)
            else:
                produce(jnp.minimum(qi + 1, nsub - 1), 0, 0)
            consume(c, c % 2)
        acc = acc_ref[...]
        out = acc[:, :C_HD] / acc[:, C_HD:]
        for g in range(C_GROUP):
            o_ref[0, q_rows(qi), g * C_HD:(g + 1) * C_HD] = out[g * tq:(g + 1) * tq]
        return carry

    lax.fori_loop(0, nsub, body, 0)


def _key_chunks(tp, fv):
    lo = fv // LANE
    blocks = tp // LANE - lo
    per = ATTN_MAX_TK // LANE
    nc = -(-blocks // per)
    nc += nc % 2
    sizes = [blocks // nc + (1 if i < blocks % nc else 0) for i in range(nc)]
    assert min(sizes) > 0
    starts = np.cumsum([lo] + sizes[:-1])
    return tuple((int(s) * LANE, int(z) * LANE) for s, z in zip(starts, sizes))


def _attention(qn, kn, vn, geom):
    outs = []
    for g in geom.groups:
        sl = slice(g.start, g.start + g.nb * g.tp)
        q3 = qn[sl].reshape(g.nb, g.tp, C_HEADS * C_HD)
        k3 = kn[sl].reshape(g.nb, g.tp, C_KV_HEADS * C_HD)
        v3 = vn[sl].reshape(g.nb, g.tp, 2 * C_KV_HEADS * C_HD)
        bounds = _key_chunks(g.tp, g.fv)
        nblk = g.tp // SEQ_BLK
        nsub = max(d for d in range(1, ATTN_MAX_SUB + 1) if nblk % d == 0)
        tq, gw, rows = nsub * SEQ_BLK, C_GROUP * C_HD, C_GROUP * SEQ_BLK
        o3 = pl.pallas_call(
            functools.partial(_attn_kernel, nsub=nsub, bounds=bounds, fv=g.fv),
            out_shape=jax.ShapeDtypeStruct((g.nb, g.tp, C_HEADS * C_HD), F32),
            grid=(g.nb, C_KV_HEADS, g.tp // tq),
            in_specs=[pl.BlockSpec((1, tq, gw), lambda b, h, i: (b, i, h)),
                      pl.BlockSpec((1, g.tp, C_HD), lambda b, h, i: (b, 0, h)),
                      pl.BlockSpec((1, g.tp, 2 * C_HD), lambda b, h, i: (b, 0, h))],
            out_specs=pl.BlockSpec((1, tq, gw), lambda b, h, i: (b, i, h)),
            scratch_shapes=[pltpu.VMEM((2, rows, max(z for _, z in bounds)), F32), pltpu.VMEM((2, rows, 1), F32),
                            pltpu.VMEM((rows, 1), F32), pltpu.VMEM((rows, 2 * C_HD), F32)],
            compiler_params=_params(("parallel", "parallel", "parallel")),
            name="attention",
        )(q3, k3, v3)
        outs.append(o3.reshape(g.nb * g.tp, C_HEADS * C_HD))
    return jnp.concatenate(outs, axis=0)


def _ffn_kernel(x_ref, xp_ref, xn_ref, pre_ref, wu_ref, wg_ref, cw_ref, cb_ref, wo_ref, post_ref,
                o_ref, xs_ref, g_ref, acc_ref, *, tm):
    i = pl.program_id(0)
    xs_ref[0:HALO] = jnp.where(i > 0, _rms(xp_ref[...], pre_ref[...]), 0.0).astype(BF16)
    xs_ref[HALO:HALO + tm] = _rms(x_ref[...], pre_ref[...]).astype(BF16)
    xs_ref[HALO + tm:2 * HALO + tm] = jnp.where(i < pl.num_programs(0) - 1,
                                                _rms(xn_ref[...], pre_ref[...]), 0.0).astype(BF16)
    half = FFN_CONV // 2
    for c in range(D_FF // FF_CHUNK):
        cs = slice(c * FF_CHUNK, (c + 1) * FF_CHUNK)
        u = jnp.dot(xs_ref[HALO:HALO + tm], wu_ref[:, cs], preferred_element_type=F32)
        g_ref[...] = jnp.dot(xs_ref[...], wg_ref[:, cs], preferred_element_type=F32)
        gate = cb_ref[0:1, cs]
        for j in range(FFN_CONV):
            gate = gate + g_ref[HALO - half + j:HALO - half + j + tm] * cw_ref[j:j + 1, cs]
        th = jnp.tanh(gate * (GELU_C0 + GELU_C0 * 0.044715 * (gate * gate)))
        hu = 0.5 * gate * u
        part = jnp.dot((hu + hu * th).astype(BF16), wo_ref[cs, :], preferred_element_type=F32)
        if c == 0:
            acc_ref[...] = part
        else:
            acc_ref[...] += part
    o_ref[...] = x_ref[...] + _rms(acc_ref[...], post_ref[...])


def _conv_ffn(h, pre_w, w_in, conv_w, conv_b, w_out, post_w, geom):
    tm, rows = geom.tm, geom.rows
    assert D_FF % FF_CHUNK == 0
    wu = w_in[:, :D_FF].astype(BF16)
    wg = w_in[:, D_FF:].astype(BF16)
    return pl.pallas_call(
        functools.partial(_ffn_kernel, tm=tm),
        out_shape=jax.ShapeDtypeStruct((rows, D_MODEL), F32),
        grid=(rows // tm,),
        in_specs=_halo_specs(tm, D_MODEL, rows) + [
            _const_spec((1, D_MODEL)), _const_spec((D_MODEL, D_FF)), _const_spec((D_MODEL, D_FF)),
            _const_spec((FFN_CONV, D_FF)), _const_spec((1, D_FF)), _const_spec((D_FF, D_MODEL)),
            _const_spec((1, D_MODEL))],
        out_specs=pl.BlockSpec((tm, D_MODEL), lambda i: (i, 0)),
        scratch_shapes=[pltpu.VMEM((tm + 2 * HALO, D_MODEL), BF16), pltpu.VMEM((tm + 2 * HALO, FF_CHUNK), F32),
                        pltpu.VMEM((tm, D_MODEL), F32)],
        compiler_params=_params(),
        name="conv_ffn",
    )(h, h, h, pre_w.reshape(1, D_MODEL), wu, wg, conv_w, conv_b.reshape(1, D_FF),
      w_out.astype(BF16), post_w.reshape(1, D_MODEL))


def _even_layer(h, j, geom, tables, ones, p):
    aw, bw = A_WIDTH, B_WIDTH
    a_cols = 4 * aw + 4 * A_HEADS
    w_in = p["e_w_in"][j]
    lora = B_W_LORA + B_A_LORA + B_G_LORA
    zeros = lambda n: jnp.zeros((D_MODEL, n), F32)
    w_cat = jnp.concatenate([
        w_in[:, a_cols:a_cols + 3 * bw + lora], zeros(B_LORA_PAD - lora),
        w_in[:, 4 * aw:a_cols], zeros(LANE - 4 * A_HEADS),
        w_in[:, :4 * aw]], axis=1).astype(BF16)
    pb, lo, ga, pa = _norm_proj(h, p["mix_pre_norm"][2 * j], w_cat, (3 * bw, B_LORA_PAD, LANE, 4 * aw),
                                (BF16, F32, F32, BF16), geom, "proj_even")

    qn, kn, vv, gts = _a_prep(pa, ga, p["a_conv_w"][j], p["a_log"][j], p["a_dt_bias"][j], geom)
    oaf, oab = _a_scan(qn, kn, vv, gts, geom, tables)

    mu = jnp.pad(p["b_shift"][j], ((0, 0), (0, B_LORA_PAD - lora)))
    lora_rows = lambda m, off: jnp.pad(m, ((off, B_LORA_PAD - off - m.shape[0]), (0, 0)))
    w2 = lora_rows(jnp.concatenate([p["b_w2"][j, 0], p["b_w2"][j, 1]], axis=1), 0)
    a2 = lora_rows(p["b_a2"][j], B_W_LORA)
    g2 = lora_rows(p["b_g2"][j], B_W_LORA + B_A_LORA)
    r, k, v, kkn, eta, cf, cb, bonus, gate = _b_prep(
        pb, lo, mu, p["b_w0"][j].reshape(1, 2 * bw), w2, p["b_a0"][j], a2, g2,
        p["b_k_k"][j], p["b_k_a"][j], p["b_r_k"][j], ones, geom)
    ybf, ybb = _b_scan(r, k, v, kkn, eta, cf, cb, geom, tables)

    return _mixout_even(h, oaf, oab, pa, ybf, ybb, bonus, gate, p["e_w_out"][j].astype(BF16),
                        p["a_out_norm"][j], p["b_ln_w"][j], p["b_ln_b"][j], p["mix_post_norm"][2 * j], ones, geom)


def _odd_layer(h, j, geom, rope, p):
    qn, kn, vn = _qkv_proj(h, p["mix_pre_norm"][2 * j + 1], p["o_w_qkv"][j].astype(BF16), rope[0], rope[1],
                           p["o_q_norm"][j], p["o_k_norm"][j], geom)
    o = _attention(qn, kn, vn, geom)
    return _mixout_odd(h, o, p["o_w_out"][j].astype(BF16), p["mix_post_norm"][2 * j + 1], geom)


def kernel(x_prompt, x_sample, meta, mix_pre_norm, mix_post_norm, ffn_pre_norm, ffn_post_norm, e_w_in, a_conv_w, a_log, a_dt_bias, a_out_norm, b_shift, b_w0, b_w2, b_a0, b_a2, b_g2, b_k_k, b_k_a, b_r_k, b_ln_w, b_ln_b, e_w_out, o_w_qkv, o_q_norm, o_k_norm, o_w_out, f_w_in, f_conv_w, f_conv_b, f_w_out):
    p = dict(mix_pre_norm=mix_pre_norm, mix_post_norm=mix_post_norm, e_w_in=e_w_in, a_conv_w=a_conv_w,
             a_log=a_log, a_dt_bias=a_dt_bias, a_out_norm=a_out_norm, b_shift=b_shift, b_w0=b_w0, b_w2=b_w2,
             b_a0=b_a0, b_a2=b_a2, b_g2=b_g2, b_k_k=b_k_k, b_k_a=b_k_a, b_r_k=b_r_k, b_ln_w=b_ln_w,
             b_ln_b=b_ln_b, e_w_out=e_w_out, o_w_qkv=o_w_qkv, o_q_norm=o_q_norm, o_k_norm=o_k_norm,
             o_w_out=o_w_out)
    xs = (x_prompt, x_sample)
    geom = _choose_geom([(x.shape[0], x.shape[1]) for x in xs])
    tables = _scan_tables(geom)
    rope = _rope_tables(geom)
    seg = np.arange(B_WIDTH) // B_HD
    ones = jnp.asarray(seg[:, None] == seg[None, :], BF16)

    parts = []
    for x, g in zip(xs, geom.groups):
        lead = jnp.concatenate([jnp.zeros((g.fv, D_MODEL), x.dtype), meta.astype(x.dtype)], axis=0)
        seq = jnp.concatenate([jnp.broadcast_to(lead[None], (g.nb,) + lead.shape), x], axis=1)
        parts.append(seq.reshape(g.nb * g.tp, D_MODEL))
    h = jnp.concatenate(parts, axis=0)

    for i in range(DEPTH):
        h = _even_layer(h, i // 2, geom, tables, ones, p) if i % 2 == 0 else _odd_layer(h, i // 2, geom, rope, p)
        h = _conv_ffn(h, ffn_pre_norm[i], f_w_in[i], f_conv_w[i], f_conv_b[i], f_w_out[i], ffn_post_norm[i], geom)

    outs = []
    for g in geom.groups:
        seq = h[g.start:g.start + g.nb * g.tp].reshape(g.nb, g.tp, D_MODEL)
        outs.append(seq[:, g.tp - g.n:])
    return tuple(outs)
```

```python
import functools
import itertools
from typing import NamedTuple

import numpy as np
import jax
import jax.numpy as jnp
from jax import lax
from jax.experimental import pallas as pl
from jax.experimental.pallas import tpu as pltpu

F32 = jnp.float32
BF16 = jnp.bfloat16

D_MODEL = 1024
DEPTH = 2
N_META = 16
GRID_W = 64
NORM_EPS = 1e-6

A_HEADS = 4
A_DK = 128
A_DV = 128
A_CONV = 5
A_WIDTH = A_HEADS * A_DV

B_HEADS = 8
B_HD = 64
B_WIDTH = B_HEADS * B_HD
B_W_LORA = 32
B_A_LORA = 32
B_G_LORA = 96
B_GN_EPS = 64e-5
B_LORA_PAD = 256

C_HEADS = 8
C_KV_HEADS = 2
C_GROUP = C_HEADS // C_KV_HEADS
C_HD = 128
ROPE_THETA = 10000.0

D_FF = 2816
FFN_CONV = 3

LANE = 128
CHUNK = 64
SEQ_BLK = 128
HALO = 16
FF_CHUNK = 256
FF_GROUP = 4
ATTN_MAX_TK = 1664
ATTN_MAX_SUB = 17
LOG2E = 1.4426950408889634
GELU_C0 = 0.7978845608028654
VMEM_LIMIT = 56 * 1024 * 1024


class _Group(NamedTuple):
    start: int
    nb: int
    n: int
    tp: int
    fv: int


class _Geom(NamedTuple):
    groups: tuple
    rows: int
    tm: int


def _choose_geom(shapes):
    tm = 512 if min(n for _, n in shapes) >= 512 else 128
    best = None
    for pads in itertools.product((128, 256, 384, 512), repeat=len(shapes)):
        start, ok, groups = 0, True, []
        for (nb, n), p in zip(shapes, pads):
            ok = ok and start % tm == 0
            groups.append(_Group(start, nb, n, n + p, p - N_META))
            start += nb * (n + p)
        if ok and start % tm == 0 and (best is None or start < best.rows):
            best = _Geom(tuple(groups), start, tm)
    assert best is not None
    return best


def _row_valid(geom, base, tm):
    start = jnp.int32(geom.groups[0].start)
    tp = jnp.int32(geom.groups[0].tp)
    fv = jnp.int32(geom.groups[0].fv)
    for g in geom.groups[1:]:
        assert tm <= g.tp
        inside = base >= g.start
        start = jnp.where(inside, g.start, start)
        tp = jnp.where(inside, g.tp, tp)
        fv = jnp.where(inside, g.fv, fv)
    pos = lax.rem(base - start, tp) + lax.broadcasted_iota(jnp.int32, (tm, 1), 0)
    pos = jnp.where(pos >= tp, pos - tp, pos)
    return pos >= fv


def _rms(x, w, eps=NORM_EPS):
    return x * lax.rsqrt(jnp.mean(x * x, axis=-1, keepdims=True) + eps) * w


def _sigmoid(x):
    return 1.0 / (1.0 + jnp.exp(-x))


def _softplus(x):
    return jnp.maximum(x, 0.0) + jnp.log1p(jnp.exp(-jnp.abs(x)))


def _dot(a, b):
    return jnp.dot(a.astype(BF16), b.astype(BF16), preferred_element_type=F32)


def _dot_nt(a, b):
    return lax.dot_general(a.astype(BF16), b.astype(BF16), (((1,), (1,)), ((), ())),
                           preferred_element_type=F32)


def _seg_sum(x, ones_ref, terms):
    acc = None
    rem = x
    for t in range(terms):
        piece = rem.astype(BF16)
        part = jnp.dot(piece, ones_ref[...], preferred_element_type=F32)
        acc = part if acc is None else acc + part
        if t + 1 < terms:
            rem = rem - piece.astype(F32)
    return acc


def _chunk_cumsum(x, reverse):
    tm = x.shape[0]
    pos = lax.broadcasted_iota(jnp.int32, (tm, 1), 0) & (CHUNK - 1)
    s = 1
    while s < CHUNK:
        if reverse:
            x = x + jnp.where(pos < CHUNK - s, pltpu.roll(x, tm - s, axis=0), 0.0)
        else:
            x = x + jnp.where(pos >= s, pltpu.roll(x, s, axis=0), 0.0)
        s *= 2
    return x


def _neumann_inverses(n_mats, eye):
    n = eye.shape[0]
    prods = [eye + m for m in n_mats]
    powers = [_dot(m, m) for m in n_mats]
    s = 4
    while s < CHUNK:
        both = [_dot(jnp.concatenate([pr, pw], axis=0), pw) for pr, pw in zip(prods, powers)]
        prods = [pr + b[:n] for pr, b in zip(prods, both)]
        powers = [b[n:] for b in both]
        s *= 2
    return [pr + _dot(pr, pw) for pr, pw in zip(prods, powers)]


def _fill_window(xw_ref, x_ref, xp_ref, xn_ref, tm):
    i = pl.program_id(0)
    xw_ref[0:HALO] = jnp.where(i > 0, xp_ref[...].astype(F32), 0.0)
    xw_ref[HALO:HALO + tm] = x_ref[...].astype(F32)
    xw_ref[HALO + tm:2 * HALO + tm] = jnp.where(i < pl.num_programs(0) - 1, xn_ref[...].astype(F32), 0.0)


def _halo_specs(tm, width, rows):
    per = tm // HALO
    last = rows // HALO - 1
    return [
        pl.BlockSpec((tm, width), lambda i: (i, 0)),
        pl.BlockSpec((HALO, width), lambda i: (jnp.maximum(i * per - 1, 0), 0)),
        pl.BlockSpec((HALO, width), lambda i: (jnp.minimum((i + 1) * per, last), 0)),
    ]


def _const_spec(shape):
    return pl.BlockSpec(shape, lambda *_: (0,) * len(shape))


def _params(sem=("parallel",)):
    return pltpu.CompilerParams(dimension_semantics=sem, vmem_limit_bytes=VMEM_LIMIT)


def _proj_kernel(x_ref, nw_ref, w_ref, *rest, widths):
    out_refs, xn_ref = rest[:len(widths)], rest[len(widths)]
    xn_ref[...] = _rms(x_ref[...], nw_ref[...]).astype(BF16)
    c0 = 0
    for o_ref, wd in zip(out_refs, widths):
        for s in range(0, wd, 512):
            e = min(s + 512, wd)
            o_ref[:, s:e] = jnp.dot(xn_ref[...], w_ref[:, c0 + s:c0 + e],
                                    preferred_element_type=F32).astype(o_ref.dtype)
        c0 += wd


def _norm_proj(h, norm_w, w_bf16, widths, dtypes, geom, name):
    tm, rows = geom.tm, geom.rows
    return pl.pallas_call(
        functools.partial(_proj_kernel, widths=widths),
        out_shape=[jax.ShapeDtypeStruct((rows, wd), dt) for wd, dt in zip(widths, dtypes)],
        grid=(rows // tm,),
        in_specs=[pl.BlockSpec((tm, D_MODEL), lambda i: (i, 0)),
                  _const_spec((1, D_MODEL)),
                  _const_spec(w_bf16.shape)],
        out_specs=[pl.BlockSpec((tm, wd), lambda i: (i, 0)) for wd in widths],
        scratch_shapes=[pltpu.VMEM((tm, D_MODEL), BF16)],
        compiler_params=_params(),
        name=name,
    )(h, norm_w.reshape(1, D_MODEL), w_bf16)


def _a_prep_kernel(x_ref, xp_ref, xn_ref, g_ref, cw_ref, alog_ref, dtb_ref,
                   q_ref, k_ref, v_ref, go_ref, xw_ref, *, geom):
    tm = geom.tm
    i = pl.program_id(0)
    valid = _row_valid(geom, i * tm, tm)
    _fill_window(xw_ref, x_ref, xp_ref, xn_ref, tm)
    half = A_CONV // 2
    for c, o_ref in enumerate((q_ref, k_ref, v_ref)):
        cs = slice(c * A_WIDTH, (c + 1) * A_WIDTH)
        acc = xw_ref[HALO - half:HALO - half + tm, cs] * cw_ref[0:1, cs]
        for j in range(1, A_CONV):
            acc = acc + xw_ref[HALO - half + j:HALO - half + j + tm, cs] * cw_ref[j:j + 1, cs]
        y = jnp.where(valid, acc * _sigmoid(acc), 0.0)
        for hd in range(A_HEADS):
            hs = slice(hd * A_DK, (hd + 1) * A_DK)
            yh = y[:, hs]
            if c < 2:
                yh = yh * lax.rsqrt(jnp.sum(yh * yh, axis=-1, keepdims=True) + 1e-6)
                if c == 0:
                    yh = yh * (A_DK ** -0.5)
            o_ref[:, hs] = yh.astype(o_ref.dtype)
    al = g_ref[...]
    lane = lax.broadcasted_iota(jnp.int32, (1, LANE), 1)
    gval = jnp.where(valid, -jnp.exp(alog_ref[...]) * _softplus(al + dtb_ref[...]), 0.0)
    beta = jnp.where(valid, _sigmoid(al), 0.0)
    cum_f = _chunk_cumsum(gval, False)
    cum_b = _chunk_cumsum(gval, True)
    go_ref[...] = jnp.where(lane < A_HEADS, cum_f,
                            jnp.where(lane < 2 * A_HEADS, cum_b,
                                      jnp.where(lane < 4 * A_HEADS, beta, 0.0)))


def _a_prep(pa, ga, conv_w, a_log, dt_bias, geom):
    tm, rows = geom.tm, geom.rows
    qkv_w = 3 * A_WIDTH
    pad = lambda v: jnp.pad(v.reshape(1, -1).astype(F32), ((0, 0), (0, LANE - v.size)))
    out = jax.ShapeDtypeStruct((rows, A_WIDTH), BF16)
    return pl.pallas_call(
        functools.partial(_a_prep_kernel, geom=geom),
        out_shape=[out, out, out, jax.ShapeDtypeStruct((rows, LANE), F32)],
        grid=(rows // tm,),
        in_specs=_halo_specs(tm, qkv_w, rows) + [
            pl.BlockSpec((tm, LANE), lambda i: (i, 0)),
            _const_spec((A_CONV, qkv_w)), _const_spec((1, LANE)), _const_spec((1, LANE))],
        out_specs=[pl.BlockSpec((tm, A_WIDTH), lambda i: (i, 0))] * 3 + [pl.BlockSpec((tm, LANE), lambda i: (i, 0))],
        scratch_shapes=[pltpu.VMEM((tm + 2 * HALO, qkv_w), F32)],
        compiler_params=_params(),
        name="a_prep",
    )(pa, pa, pa, ga, conv_w, pad(a_log), pad(dt_bias))


def _a_scan_kernel(fi_ref, bi_ref, fr_ref,
                   qf, kf, vf, gf, gtf, qb, kb, vb, gb, gtb, of, ob, sf, sb):
    del fi_ref, bi_ref
    @pl.when(fr_ref[pl.program_id(0)] == 1)
    def _():
        sf[...] = jnp.zeros_like(sf)
        sb[...] = jnp.zeros_like(sb)
    r = lax.broadcasted_iota(jnp.int32, (CHUNK, CHUNK), 0)
    c = lax.broadcasted_iota(jnp.int32, (CHUNK, CHUNK), 1)
    eye = jnp.where(r == c, 1.0, 0.0)
    nc = SEQ_BLK // CHUNK
    dirs = ((qf, kf, vf, gf, gtf, of, sf, False, r >= c, r > c),
            (qb, kb, vb, gb, gtb, ob, sb, True, r <= c, r < c))

    probs = []
    for d, (q_ref, k_ref, v_ref, g_ref, gt_ref, _, _, reverse, tri, strict) in enumerate(dirs):
        edge = 0 if reverse else CHUNK - 1
        for step in range(nc):
            jj = nc - 1 - step if reverse else step
            rows = slice(jj * CHUNK, (jj + 1) * CHUNK)
            for hd in range(A_HEADS):
                hs = slice(hd * A_DK, (hd + 1) * A_DK)
                col = hd + (A_HEADS if reverse else 0)
                q, k, v = (ref[rows, hs].astype(F32) for ref in (q_ref, k_ref, v_ref))
                g_col = g_ref[rows, col:col + 1]
                b_col = g_ref[rows, 2 * A_HEADS + col:2 * A_HEADS + col + 1]
                g_row = gt_ref[jj, col:col + 1, :]
                decay = jnp.where(tri, jnp.exp(jnp.where(tri, g_col - g_row, 0.0)), 0.0)
                kb_ = k * b_col
                e_g = jnp.exp(g_col)
                g_last = g_col[edge:edge + 1]
                probs.append(dict(d=d, step=step, hd=hd, rows=rows, hs=hs, tri=tri, strict=strict, decay=decay,
                                  k=k, kbq=jnp.concatenate([kb_, q], axis=0),
                                  rhs=jnp.concatenate([v * b_col, kb_ * e_g], axis=1),
                                  qg=q * e_g, kg_t=(k * jnp.exp(g_last - g_col)).T, g_end=jnp.exp(g_last)))
    for p in probs:
        kq = _dot_nt(p["kbq"], p["k"])
        p["l"] = jnp.where(p["strict"], kq[:CHUNK] * p["decay"], 0.0)
        p["attn"] = jnp.where(p["tri"], kq[CHUNK:] * p["decay"], 0.0)
    t_invs = _neumann_inverses([-p["l"] for p in probs], eye)
    for p, t_inv in zip(probs, t_invs):
        p["uw"] = _dot(t_inv, p["rhs"])

    states = {(d, hd): dirs[d][6][hd] for d in range(2) for hd in range(A_HEADS)}
    for step in range(nc):
        cur = [p for p in probs if p["step"] == step]
        v_news = [p["uw"][:, :A_DV] - _dot(p["uw"][:, A_DV:], states[p["d"], p["hd"]]) for p in cur]
        for p, v_new in zip(cur, v_news):
            state = states[p["d"], p["hd"]]
            dirs[p["d"]][5][p["rows"], p["hs"]] = _dot(jnp.concatenate([p["qg"], p["attn"]], axis=1),
                                                      jnp.concatenate([state, v_new], axis=0))
        for p, v_new in zip(cur, v_news):
            key = (p["d"], p["hd"])
            states[key] = states[key] * p["g_end"] + _dot(p["kg_t"], v_new)
    for (d, hd), state in states.items():
        dirs[d][6][hd] = state


def _scan_tables(geom):
    fi, bi, fr = [], [], []
    for g in geom.groups:
        nblk = g.tp // SEQ_BLK
        for b in range(g.nb):
            base = (g.start + b * g.tp) // SEQ_BLK
            for i in range(nblk):
                fi.append(base + i)
                bi.append(base + nblk - 1 - i)
                fr.append(1 if i == 0 else 0)
    mk = lambda v: jnp.asarray(np.asarray(v, np.int32))
    return mk(fi), mk(bi), mk(fr)


def _a_scan(qn, kn, vv, gts, geom, tables):
    rows = geom.rows
    nc = SEQ_BLK // CHUNK
    gts_t = gts[:, :4 * A_HEADS].reshape(rows // CHUNK, CHUNK, 4 * A_HEADS).transpose(0, 2, 1)
    wide_f = pl.BlockSpec((SEQ_BLK, A_WIDTH), lambda s, fi, bi, fr: (fi[s], 0))
    wide_b = pl.BlockSpec((SEQ_BLK, A_WIDTH), lambda s, fi, bi, fr: (bi[s], 0))
    gate_f = pl.BlockSpec((SEQ_BLK, LANE), lambda s, fi, bi, fr: (fi[s], 0))
    gate_b = pl.BlockSpec((SEQ_BLK, LANE), lambda s, fi, bi, fr: (bi[s], 0))
    gt_f = pl.BlockSpec((nc, 4 * A_HEADS, CHUNK), lambda s, fi, bi, fr: (fi[s], 0, 0))
    gt_b = pl.BlockSpec((nc, 4 * A_HEADS, CHUNK), lambda s, fi, bi, fr: (bi[s], 0, 0))
    out = jax.ShapeDtypeStruct((rows, A_WIDTH), F32)
    return pl.pallas_call(
        _a_scan_kernel,
        out_shape=[out, out],
        grid_spec=pltpu.PrefetchScalarGridSpec(
            num_scalar_prefetch=3,
            grid=(tables[0].shape[0],),
            in_specs=[wide_f, wide_f, wide_f, gate_f, gt_f, wide_b, wide_b, wide_b, gate_b, gt_b],
            out_specs=[wide_f, wide_b],
            scratch_shapes=[pltpu.VMEM((A_HEADS, A_DK, A_DV), F32)] * 2),
        compiler_params=_params(("arbitrary",)),
        name="a_scan",
    )(*tables, qn, kn, vv, gts, gts_t, qn, kn, vv, gts, gts_t)


def _b_prep_kernel(x_ref, xp_ref, xn_ref, l_ref, lp_ref, ln_ref, mu_ref, w0_ref, w2_ref, a0_ref, a2_ref, g2_ref,
                   kk_ref, ka_ref, rk_ref, ones_ref,
                   r_out, k_out, v_out, kkn_out, eta_out, cf_out, cb_out, bonus_out, gate_out,
                   xw_ref, lw_ref, *, geom):
    tm = geom.tm
    _fill_window(xw_ref, x_ref, xp_ref, xn_ref, tm)
    _fill_window(lw_ref, l_ref, lp_ref, ln_ref, tm)

    def shifted(win_ref, cs, mu_off):
        cur = win_ref[HALO:HALO + tm, cs]
        prev = win_ref[HALO - 1:HALO - 1 + tm, cs]
        nxt = win_ref[HALO + 1:HALO + 1 + tm, cs]
        ms = slice(mu_off + cs.start, mu_off + cs.stop)
        return cur + mu_ref[0:1, ms] * (prev - cur) + mu_ref[1:2, ms] * (nxt - cur)

    w = B_WIDTH
    r = shifted(xw_ref, slice(0, w), 0)
    k = shifted(xw_ref, slice(w, 2 * w), 0)
    v = shifted(xw_ref, slice(2 * w, 3 * w), 0)
    lo = shifted(lw_ref, slice(0, B_LORA_PAD), 3 * w)
    wl = _dot(jnp.tanh(lo), w2_ref[...]) + w0_ref[...]
    log_decay = -jnp.exp(-_softplus(-wl) - 0.5)
    cf_out[...] = _chunk_cumsum(log_decay[:, :w], False)
    cb_out[...] = _chunk_cumsum(log_decay[:, w:], True)
    eta = _sigmoid(a0_ref[...] + _dot(lo, a2_ref[...]))
    gate_out[...] = _dot(_sigmoid(lo), g2_ref[...])
    kx = k * kk_ref[...]
    kkn_out[...] = (kx * lax.rsqrt(_seg_sum(kx * kx, ones_ref, 2) + 1e-6)).astype(kkn_out.dtype)
    k = k * (1.0 + (eta - 1.0) * ka_ref[...])
    bonus_out[...] = _seg_sum(r * k * rk_ref[...], ones_ref, 2) * v
    r_out[...] = r.astype(r_out.dtype)
    k_out[...] = k.astype(k_out.dtype)
    v_out[...] = v.astype(v_out.dtype)
    eta_out[...] = eta.astype(eta_out.dtype)


def _b_prep(pb, lo, mu, w0, w2, a0, a2, g2, k_k, k_a, r_k, ones, geom):
    tm, rows = geom.tm, geom.rows
    wd = 3 * B_WIDTH
    narrow = jax.ShapeDtypeStruct((rows, B_WIDTH), BF16)
    wide = jax.ShapeDtypeStruct((rows, B_WIDTH), F32)
    row = lambda v: v.reshape(1, -1).astype(F32)
    return pl.pallas_call(
        functools.partial(_b_prep_kernel, geom=geom),
        out_shape=[narrow] * 5 + [wide] * 4,
        grid=(rows // tm,),
        in_specs=_halo_specs(tm, wd, rows) + _halo_specs(tm, B_LORA_PAD, rows) + [
            _const_spec((2, wd + B_LORA_PAD)), _const_spec((1, 2 * B_WIDTH)), _const_spec((B_LORA_PAD, 2 * B_WIDTH)),
            _const_spec((1, B_WIDTH)), _const_spec((B_LORA_PAD, B_WIDTH)), _const_spec((B_LORA_PAD, B_WIDTH)),
            _const_spec((1, B_WIDTH)), _const_spec((1, B_WIDTH)), _const_spec((1, B_WIDTH)),
            _const_spec((B_WIDTH, B_WIDTH))],
        out_specs=[pl.BlockSpec((tm, B_WIDTH), lambda i: (i, 0))] * 9,
        scratch_shapes=[pltpu.VMEM((tm + 2 * HALO, wd), F32), pltpu.VMEM((tm + 2 * HALO, B_LORA_PAD), F32)],
        compiler_params=_params(),
        name="b_prep",
    )(pb, pb, pb, lo, lo, lo, mu, w0, w2, row(a0), a2, g2, row(k_k), row(k_a), row(r_k), ones)


def _b_scan_kernel(fi_ref, bi_ref, fr_ref,
                   rf, kf, vf, kkf, ef, cf, rb, kb, vb, kkb, eb, cb, yf, yb, pf, pb):
    del fi_ref, bi_ref
    @pl.when(fr_ref[pl.program_id(0)] == 1)
    def _():
        pf[...] = jnp.zeros_like(pf)
        pb[...] = jnp.zeros_like(pb)
    n2 = 2 * CHUNK
    r = lax.broadcasted_iota(jnp.int32, (n2, n2), 0)
    c = lax.broadcasted_iota(jnp.int32, (n2, n2), 1)
    same = (r >= CHUNK) == (c >= CHUNK)
    rt, ct = r & (CHUNK - 1), c & (CHUNK - 1)
    eye = jnp.where(r == c, 1.0, 0.0)
    lane_lo = lax.broadcasted_iota(jnp.int32, (1, n2), 1) < B_HD
    row_pos = lax.broadcasted_iota(jnp.int32, (CHUNK, 1), 0)
    nc = SEQ_BLK // CHUNK
    pair_w = 2 * B_HD
    n_pairs = B_HEADS // 2
    dirs = (((rf, kf, vf, kkf, ef, cf), yf, pf, False, same & (rt > ct), same & (rt >= ct)),
            ((rb, kb, vb, kkb, eb, cb), yb, pb, True, same & (rt < ct), same & (rt <= ct)))

    def expand(x):
        return jnp.concatenate([jnp.where(lane_lo, x, 0.0), jnp.where(lane_lo, 0.0, x)], axis=0)

    probs = []
    for d, (refs, _, _, reverse, strict, incl) in enumerate(dirs):
        r_ref, k_ref, v_ref, kk_ref, eta_ref, c_ref = refs
        for step in range(nc):
            jj = nc - 1 - step if reverse else step
            rows = slice(jj * CHUNK, (jj + 1) * CHUNK)
            for pr in range(n_pairs):
                ls = slice(pr * pair_w, (pr + 1) * pair_w)
                r, k, v, kk, eta = (ref[rows, ls].astype(F32) for ref in (r_ref, k_ref, v_ref, kk_ref, eta_ref))
                c_in = c_ref[rows, ls]
                if reverse:
                    c_ex = jnp.where(row_pos < CHUNK - 1, pltpu.roll(c_in, CHUNK - 1, axis=0), 0.0)
                    c_tot = c_in[0:1]
                else:
                    c_ex = jnp.where(row_pos >= 1, pltpu.roll(c_in, 1, axis=0), 0.0)
                    c_tot = c_in[CHUNK - 1:CHUNK]
                b = kk * eta
                inv_w = jnp.exp(-c_in)
                rest_w = jnp.exp(c_tot - c_in)
                a_e = expand(-kk * jnp.exp(c_ex))
                r_e = expand(r * jnp.exp(c_in))
                probs.append(dict(
                    d=d, step=step, pr=pr, rows=rows, ls=ls, strict=strict, incl=incl, v=v,
                    a_e=a_e, r_e=r_e, v_e=expand(v),
                    lhs=jnp.concatenate([a_e, r_e], axis=0),
                    rhs=jnp.concatenate([expand(b * inv_w), expand(k * inv_w)], axis=0),
                    kd_t=jnp.concatenate([b * rest_w, k * rest_w], axis=0).T,
                    w_col=jnp.sum(jnp.where(eye > 0.0, jnp.exp(c_tot), 0.0), axis=1, keepdims=True)))
    for p in probs:
        m1 = _dot_nt(p["lhs"], p["rhs"])
        p["n_ab"] = jnp.where(p["strict"], m1[:n2, :n2], 0.0)
        p["a_ak"] = jnp.where(p["strict"], m1[:n2, n2:], 0.0)
        p["a_r"] = jnp.concatenate([jnp.where(p["incl"], m1[n2:, :n2], 0.0),
                                    jnp.where(p["incl"], m1[n2:, n2:], 0.0)], axis=1)
    t_invs = _neumann_inverses([p["n_ab"] for p in probs], eye)
    for p in probs:
        p["akv"] = _dot(p["a_ak"], p["v_e"])
    for p, t_inv in zip(probs, t_invs):
        p["x"] = _dot(t_inv, jnp.concatenate([p["akv"], p["a_e"]], axis=1))

    states = {(d, pr): dirs[d][2][pr] for d in range(2) for pr in range(n_pairs)}
    for step in range(nc):
        cur = [p for p in probs if p["step"] == step]
        u_es = [p["x"][:, :pair_w] + _dot(p["x"][:, pair_w:], states[p["d"], p["pr"]]) for p in cur]
        for p, u_e in zip(cur, u_es):
            y_e = _dot(jnp.concatenate([p["r_e"], p["a_r"]], axis=1),
                       jnp.concatenate([states[p["d"], p["pr"]], u_e, p["v_e"]], axis=0))
            dirs[p["d"]][1][p["rows"], p["ls"]] = y_e[:CHUNK] + y_e[CHUNK:]
        for p, u_e in zip(cur, u_es):
            key = (p["d"], p["pr"])
            upd = _dot(p["kd_t"], jnp.concatenate([u_e[:CHUNK] + u_e[CHUNK:], p["v"]], axis=0))
            states[key] = states[key] * p["w_col"] + jnp.where(same, upd, 0.0)
    for (d, pr), state in states.items():
        dirs[d][2][pr] = state


def _b_scan(r, k, v, kkn, eta, cf, cb, geom, tables):
    wide_f = pl.BlockSpec((SEQ_BLK, B_WIDTH), lambda s, fi, bi, fr: (fi[s], 0))
    wide_b = pl.BlockSpec((SEQ_BLK, B_WIDTH), lambda s, fi, bi, fr: (bi[s], 0))
    out = jax.ShapeDtypeStruct((geom.rows, B_WIDTH), F32)
    return pl.pallas_call(
        _b_scan_kernel,
        out_shape=[out, out],
        grid_spec=pltpu.PrefetchScalarGridSpec(
            num_scalar_prefetch=3,
            grid=(tables[0].shape[0],),
            in_specs=[wide_f] * 6 + [wide_b] * 6,
            out_specs=[wide_f, wide_b],
            scratch_shapes=[pltpu.VMEM((B_HEADS // 2, 2 * B_HD, 2 * B_HD), F32)] * 2),
        compiler_params=_params(("arbitrary",)),
        name="b_scan",
    )(*tables, r, k, v, kkn, eta, cf, r, k, v, kkn, eta, cb)


def _finish_residual(h, mix, post_w, valid):
    return jnp.where(valid, h + _rms(mix, post_w), 0.0)


def _mixout_even_kernel(h_ref, oaf_ref, oab_ref, z_ref, ybf_ref, ybb_ref, bonus_ref, gate_ref,
                        w_ref, an_ref, lnw_ref, lnb_ref, post_ref, ones_ref, o_ref, cat_ref, *, geom):
    tm = geom.tm
    valid = _row_valid(geom, pl.program_id(0) * tm, tm)
    for hd in range(A_HEADS):
        hs = slice(hd * A_DV, (hd + 1) * A_DV)
        o = oaf_ref[:, hs] + oab_ref[:, hs]
        z = z_ref[:, hs].astype(F32)
        cat_ref[:, hs] = (_rms(o, an_ref[...]) * (z * _sigmoid(z))).astype(BF16)
    y = ybf_ref[...] + ybb_ref[...]
    yc = y - _seg_sum(y, ones_ref, 3) * (1.0 / B_HD)
    var = _seg_sum(yc * yc, ones_ref, 2) * (1.0 / B_HD)
    yn = yc * lax.rsqrt(var + B_GN_EPS) * lnw_ref[...] + lnb_ref[...]
    cat_ref[:, A_WIDTH:] = ((yn + bonus_ref[...]) * gate_ref[...]).astype(BF16)
    mix = jnp.dot(cat_ref[...], w_ref[...], preferred_element_type=F32)
    o_ref[...] = _finish_residual(h_ref[...], mix, post_ref[...], valid)


def _mixout_even(h, oaf, oab, pa, ybf, ybb, bonus, gate, w_out, a_norm, ln_w, ln_b, post_w, ones, geom):
    tm, rows = geom.tm, geom.rows
    row = lambda v: v.reshape(1, -1).astype(F32)
    a_blk = pl.BlockSpec((tm, A_WIDTH), lambda i: (i, 0))
    b_blk = pl.BlockSpec((tm, B_WIDTH), lambda i: (i, 0))
    return pl.pallas_call(
        functools.partial(_mixout_even_kernel, geom=geom),
        out_shape=jax.ShapeDtypeStruct((rows, D_MODEL), F32),
        grid=(rows // tm,),
        in_specs=[pl.BlockSpec((tm, D_MODEL), lambda i: (i, 0)), a_blk, a_blk,
                  pl.BlockSpec((tm, A_WIDTH), lambda i: (i, 3)),
                  b_blk, b_blk, b_blk, b_blk,
                  _const_spec((A_WIDTH + B_WIDTH, D_MODEL)), _const_spec((1, A_DV)),
                  _const_spec((1, B_WIDTH)), _const_spec((1, B_WIDTH)), _const_spec((1, D_MODEL)),
                  _const_spec((B_WIDTH, B_WIDTH))],
        out_specs=pl.BlockSpec((tm, D_MODEL), lambda i: (i, 0)),
        scratch_shapes=[pltpu.VMEM((tm, A_WIDTH + B_WIDTH), BF16)],
        compiler_params=_params(),
        name="mixout_even",
    )(h, oaf, oab, pa, ybf, ybb, bonus, gate, w_out, row(a_norm), row(ln_w), row(ln_b), row(post_w), ones)


def _mixout_odd_kernel(h_ref, o_ref_in, w_ref, post_ref, o_ref, *, geom):
    tm = geom.tm
    valid = _row_valid(geom, pl.program_id(0) * tm, tm)
    mix = jnp.dot(o_ref_in[...].astype(BF16), w_ref[...], preferred_element_type=F32)
    o_ref[...] = _finish_residual(h_ref[...], mix, post_ref[...], valid)


def _mixout_odd(h, o, w_out, post_w, geom):
    tm, rows = geom.tm, geom.rows
    wd = C_HEADS * C_HD
    return pl.pallas_call(
        functools.partial(_mixout_odd_kernel, geom=geom),
        out_shape=jax.ShapeDtypeStruct((rows, D_MODEL), F32),
        grid=(rows // tm,),
        in_specs=[pl.BlockSpec((tm, D_MODEL), lambda i: (i, 0)), pl.BlockSpec((tm, wd), lambda i: (i, 0)),
                  _const_spec((wd, D_MODEL)), _const_spec((1, D_MODEL))],
        out_specs=pl.BlockSpec((tm, D_MODEL), lambda i: (i, 0)),
        compiler_params=_params(),
        name="mixout_odd",
    )(h, o, w_out, post_w.reshape(1, D_MODEL))


def _qkv_kernel(x_ref, nw_ref, w_ref, cos_ref, sin_ref, qn_ref, kn_ref, q_out, k_out, v_out, xn_ref):
    xn_ref[...] = _rms(x_ref[...], nw_ref[...]).astype(BF16)
    cos, sin = cos_ref[...], sin_ref[...]
    lane = lax.broadcasted_iota(jnp.int32, (1, C_HD), 1)
    first_half = (lane & (C_HD // 2 - 1)) < C_HD // 4

    def rope(xh, w, scale):
        xh = _rms(xh, w)
        partner = jnp.where(first_half, pltpu.roll(xh, C_HD - C_HD // 4, axis=1), pltpu.roll(xh, C_HD // 4, axis=1))
        return (xh * cos + partner * sin) * scale

    nq = C_HEADS * C_HD
    nk = C_KV_HEADS * C_HD
    per = 512 // C_HD
    for c in range(nq // 512):
        y = jnp.dot(xn_ref[...], w_ref[:, c * 512:(c + 1) * 512], preferred_element_type=F32)
        for hd in range(per):
            hs = slice((c * per + hd) * C_HD, (c * per + hd + 1) * C_HD)
            q_out[:, hs] = rope(y[:, hd * C_HD:(hd + 1) * C_HD], qn_ref[...], C_HD ** -0.5 * LOG2E).astype(BF16)
    y = jnp.dot(xn_ref[...], w_ref[:, nq:], preferred_element_type=F32)
    for hd in range(C_KV_HEADS):
        hs = slice(hd * C_HD, (hd + 1) * C_HD)
        k_out[:, hs] = rope(y[:, hs], kn_ref[...], 1.0).astype(BF16)
        v_out[:, 2 * hd * C_HD:(2 * hd + 1) * C_HD] = y[:, nk + hd * C_HD:nk + (hd + 1) * C_HD].astype(BF16)
        v_out[:, (2 * hd + 1) * C_HD:(2 * hd + 2) * C_HD] = jnp.ones((x_ref.shape[0], C_HD), BF16)


def _qkv_proj(h, norm_w, w_bf16, cos_t, sin_t, q_norm, k_norm, geom):
    tm, rows = geom.tm, geom.rows
    nq, nk = C_HEADS * C_HD, C_KV_HEADS * C_HD
    return pl.pallas_call(
        _qkv_kernel,
        out_shape=[jax.ShapeDtypeStruct((rows, nq), BF16), jax.ShapeDtypeStruct((rows, nk), BF16),
                   jax.ShapeDtypeStruct((rows, 2 * nk), BF16)],
        grid=(rows // tm,),
        in_specs=[pl.BlockSpec((tm, D_MODEL), lambda i: (i, 0)), _const_spec((1, D_MODEL)),
                  _const_spec((D_MODEL, nq + 2 * nk)),
                  pl.BlockSpec((tm, C_HD), lambda i: (i, 0)), pl.BlockSpec((tm, C_HD), lambda i: (i, 0)),
                  _const_spec((1, C_HD)), _const_spec((1, C_HD))],
        out_specs=[pl.BlockSpec((tm, nq), lambda i: (i, 0)), pl.BlockSpec((tm, nk), lambda i: (i, 0)),
                   pl.BlockSpec((tm, 2 * nk), lambda i: (i, 0))],
        scratch_shapes=[pltpu.VMEM((tm, D_MODEL), BF16)],
        compiler_params=_params(),
        name="qkv_proj",
    )(h, norm_w.reshape(1, D_MODEL), w_bf16, cos_t, sin_t, q_norm.reshape(1, C_HD), k_norm.reshape(1, C_HD))


def _rope_tables(geom):
    nf = C_HD // 4
    inv = ROPE_THETA ** (-jnp.arange(nf, dtype=F32) / nf)
    cos_parts, sin_parts = [], []
    for g in geom.groups:
        t = jnp.arange(g.tp, dtype=jnp.int32) - (g.fv + N_META)
        is_meta = t < 0
        row = jnp.where(is_meta, -1, t // GRID_W).astype(F32)
        col = jnp.where(is_meta, t + N_META, t % GRID_W).astype(F32)
        ang_r, ang_c = row[:, None] * inv, col[:, None] * inv
        cos = jnp.concatenate([jnp.cos(ang_r)] * 2 + [jnp.cos(ang_c)] * 2, axis=1)
        sin = jnp.concatenate([-jnp.sin(ang_r), jnp.sin(ang_r), -jnp.sin(ang_c), jnp.sin(ang_c)], axis=1)
        cos_parts.append(jnp.tile(cos, (g.nb, 1)))
        sin_parts.append(jnp.tile(sin, (g.nb, 1)))
    return jnp.concatenate(cos_parts, axis=0), jnp.concatenate(sin_parts, axis=0)


def _attn_kernel(q_ref, k_ref, v_ref, o_ref, s_ref, rm_ref, m_ref, acc_ref, *, nsub, bounds, fv):
    tq = SEQ_BLK
    nc = len(bounds)
    assert nc % 2 == 0

    def q_rows(qi):
        return pl.ds(pl.multiple_of(qi * tq, tq), tq)

    def produce(qi, c, slot):
        start, size = bounds[c]
        q = jnp.concatenate([q_ref[0, q_rows(qi), g * C_HD:(g + 1) * C_HD] for g in range(C_GROUP)], axis=0)
        s = lax.dot_general(q, k_ref[0, start:start + size, :], (((1,), (1,)), ((), ())),
                            preferred_element_type=F32)
        if start < fv:
            key = start + lax.broadcasted_iota(jnp.int32, (1, size), 1)
            s = jnp.where(key >= fv, s, -1e30)
        s_ref[slot, :, :size] = s
        rm_ref[slot] = jnp.max(s, axis=-1, keepdims=True)

    def consume(c, slot):
        start, size = bounds[c]
        m_old = m_ref[...]
        m_new = jnp.maximum(m_old, rm_ref[slot])
        p = jnp.exp2(s_ref[slot, :, :size] - m_new).astype(BF16)
        acc_ref[...] = jnp.exp2(m_old - m_new) * acc_ref[...] + jnp.dot(p, v_ref[0, start:start + size, :],
                                                                        preferred_element_type=F32)
        m_ref[...] = m_new

    produce(0, 0, 0)

    def body(qi, carry):
        m_ref[...] = jnp.full_like(m_ref, -1e30)
        acc_ref[...] = jnp.zeros_like(acc_ref)
        for c in range(nc):
            if c + 1 < nc:
                produce(qi, c + 1, (c + 1) % 2)
            else:
                produce(jnp.minimum(qi + 1, nsub - 1), 0, 0)
            consume(c, c % 2)
        acc = acc_ref[...]
        out = acc[:, :C_HD] / acc[:, C_HD:]
        for g in range(C_GROUP):
            o_ref[0, q_rows(qi), g * C_HD:(g + 1) * C_HD] = out[g * tq:(g + 1) * tq]
        return carry

    lax.fori_loop(0, nsub, body, 0)


def _key_chunks(tp, fv):
    lo = fv // LANE
    blocks = tp // LANE - lo
    per = ATTN_MAX_TK // LANE
    nc = -(-blocks // per)
    nc += nc % 2
    sizes = [blocks // nc + (1 if i < blocks % nc else 0) for i in range(nc)]
    assert min(sizes) > 0
    starts = np.cumsum([lo] + sizes[:-1])
    return tuple((int(s) * LANE, int(z) * LANE) for s, z in zip(starts, sizes))


def _attention(qn, kn, vn, geom):
    outs = []
    for g in geom.groups:
        sl = slice(g.start, g.start + g.nb * g.tp)
        q3 = qn[sl].reshape(g.nb, g.tp, C_HEADS * C_HD)
        k3 = kn[sl].reshape(g.nb, g.tp, C_KV_HEADS * C_HD)
        v3 = vn[sl].reshape(g.nb, g.tp, 2 * C_KV_HEADS * C_HD)
        bounds = _key_chunks(g.tp, g.fv)
        nblk = g.tp // SEQ_BLK
        nsub = max(d for d in range(1, ATTN_MAX_SUB + 1) if nblk % d == 0)
        tq, gw, rows = nsub * SEQ_BLK, C_GROUP * C_HD, C_GROUP * SEQ_BLK
        o3 = pl.pallas_call(
            functools.partial(_attn_kernel, nsub=nsub, bounds=bounds, fv=g.fv),
            out_shape=jax.ShapeDtypeStruct((g.nb, g.tp, C_HEADS * C_HD), F32),
            grid=(g.nb, C_KV_HEADS, g.tp // tq),
            in_specs=[pl.BlockSpec((1, tq, gw), lambda b, h, i: (b, i, h)),
                      pl.BlockSpec((1, g.tp, C_HD), lambda b, h, i: (b, 0, h)),
                      pl.BlockSpec((1, g.tp, 2 * C_HD), lambda b, h, i: (b, 0, h))],
            out_specs=pl.BlockSpec((1, tq, gw), lambda b, h, i: (b, i, h)),
            scratch_shapes=[pltpu.VMEM((2, rows, max(z for _, z in bounds)), F32), pltpu.VMEM((2, rows, 1), F32),
                            pltpu.VMEM((rows, 1), F32), pltpu.VMEM((rows, 2 * C_HD), F32)],
            compiler_params=_params(("parallel", "parallel", "parallel")),
            name="attention",
        )(q3, k3, v3)
        outs.append(o3.reshape(g.nb * g.tp, C_HEADS * C_HD))
    return jnp.concatenate(outs, axis=0)


def _ffn_kernel(x_ref, xp_ref, xn_ref, pre_ref, wu_ref, wg_ref, cw_ref, cb_ref, wo_ref, post_ref,
                o_ref, xs_ref, g0_ref, g1_ref, act_ref, acc_ref, *, tm):
    i = pl.program_id(0)
    xs_ref[0:HALO] = jnp.where(i > 0, _rms(xp_ref[...], pre_ref[...]), 0.0).astype(BF16)
    xs_ref[HALO:HALO + tm] = _rms(x_ref[...], pre_ref[...]).astype(BF16)
    xs_ref[HALO + tm:2 * HALO + tm] = jnp.where(i < pl.num_programs(0) - 1,
                                                _rms(xn_ref[...], pre_ref[...]), 0.0).astype(BF16)
    half = FFN_CONV // 2
    n_chunks = D_FF // FF_CHUNK
    for c in range(n_chunks):
        cs = slice(c * FF_CHUNK, (c + 1) * FF_CHUNK)
        g_ref = (g0_ref, g1_ref)[c % 2]
        u = jnp.dot(xs_ref[HALO:HALO + tm], wu_ref[:, cs], preferred_element_type=F32)
        g_ref[...] = jnp.dot(xs_ref[...], wg_ref[:, cs], preferred_element_type=F32)
        gate = cb_ref[0:1, cs]
        for j in range(FFN_CONV):
            gate = gate + g_ref[HALO - half + j:HALO - half + j + tm] * cw_ref[j:j + 1, cs]
        th = jnp.tanh(gate * (GELU_C0 + GELU_C0 * 0.044715 * (gate * gate)))
        hu = 0.5 * gate * u
        act_ref[:, cs] = (hu + hu * th).astype(BF16)
        if c % FF_GROUP == FF_GROUP - 1 or c == n_chunks - 1:
            ks = slice((c // FF_GROUP) * FF_GROUP * FF_CHUNK, (c + 1) * FF_CHUNK)
            part = jnp.dot(act_ref[:, ks], wo_ref[ks, :], preferred_element_type=F32)
            if c < FF_GROUP:
                acc_ref[...] = part
            else:
                acc_ref[...] += part
    o_ref[...] = x_ref[...] + _rms(acc_ref[...], post_ref[...])


def _conv_ffn(h, pre_w, w_in, conv_w, conv_b, w_out, post_w, geom):
    tm, rows = geom.tm, geom.rows
    assert D_FF % FF_CHUNK == 0
    wu = w_in[:, :D_FF].astype(BF16)
    wg = w_in[:, D_FF:].astype(BF16)
    return pl.pallas_call(
        functools.partial(_ffn_kernel, tm=tm),
        out_shape=jax.ShapeDtypeStruct((rows, D_MODEL), F32),
        grid=(rows // tm,),
        in_specs=_halo_specs(tm, D_MODEL, rows) + [
            _const_spec((1, D_MODEL)), _const_spec((D_MODEL, D_FF)), _const_spec((D_MODEL, D_FF)),
            _const_spec((FFN_CONV, D_FF)), _const_spec((1, D_FF)), _const_spec((D_FF, D_MODEL)),
            _const_spec((1, D_MODEL))],
        out_specs=pl.BlockSpec((tm, D_MODEL), lambda i: (i, 0)),
        scratch_shapes=[pltpu.VMEM((tm + 2 * HALO, D_MODEL), BF16), pltpu.VMEM((tm + 2 * HALO, FF_CHUNK), F32),
                        pltpu.VMEM((tm + 2 * HALO, FF_CHUNK), F32), pltpu.VMEM((tm, D_FF), BF16),
                        pltpu.VMEM((tm, D_MODEL), F32)],
        compiler_params=_params(),
        name="conv_ffn",
    )(h, h, h, pre_w.reshape(1, D_MODEL), wu, wg, conv_w, conv_b.reshape(1, D_FF),
      w_out.astype(BF16), post_w.reshape(1, D_MODEL))


def _even_layer(h, j, geom, tables, ones, p):
    aw, bw = A_WIDTH, B_WIDTH
    a_cols = 4 * aw + 4 * A_HEADS
    w_in = p["e_w_in"][j]
    lora = B_W_LORA + B_A_LORA + B_G_LORA
    zeros = lambda n: jnp.zeros((D_MODEL, n), F32)
    w_cat = jnp.concatenate([
        w_in[:, a_cols:a_cols + 3 * bw + lora], zeros(B_LORA_PAD - lora),
        w_in[:, 4 * aw:a_cols], zeros(LANE - 4 * A_HEADS),
        w_in[:, :4 * aw]], axis=1).astype(BF16)
    pb, lo, ga, pa = _norm_proj(h, p["mix_pre_norm"][2 * j], w_cat, (3 * bw, B_LORA_PAD, LANE, 4 * aw),
                                (BF16, F32, F32, BF16), geom, "proj_even")

    qn, kn, vv, gts = _a_prep(pa, ga, p["a_conv_w"][j], p["a_log"][j], p["a_dt_bias"][j], geom)
    oaf, oab = _a_scan(qn, kn, vv, gts, geom, tables)

    mu = jnp.pad(p["b_shift"][j], ((0, 0), (0, B_LORA_PAD - lora)))
    lora_rows = lambda m, off: jnp.pad(m, ((off, B_LORA_PAD - off - m.shape[0]), (0, 0)))
    w2 = lora_rows(jnp.concatenate([p["b_w2"][j, 0], p["b_w2"][j, 1]], axis=1), 0)
    a2 = lora_rows(p["b_a2"][j], B_W_LORA)
    g2 = lora_rows(p["b_g2"][j], B_W_LORA + B_A_LORA)
    r, k, v, kkn, eta, cf, cb, bonus, gate = _b_prep(
        pb, lo, mu, p["b_w0"][j].reshape(1, 2 * bw), w2, p["b_a0"][j], a2, g2,
        p["b_k_k"][j], p["b_k_a"][j], p["b_r_k"][j], ones, geom)
    ybf, ybb = _b_scan(r, k, v, kkn, eta, cf, cb, geom, tables)

    return _mixout_even(h, oaf, oab, pa, ybf, ybb, bonus, gate, p["e_w_out"][j].astype(BF16),
                        p["a_out_norm"][j], p["b_ln_w"][j], p["b_ln_b"][j], p["mix_post_norm"][2 * j], ones, geom)


def _odd_layer(h, j, geom, rope, p):
    qn, kn, vn = _qkv_proj(h, p["mix_pre_norm"][2 * j + 1], p["o_w_qkv"][j].astype(BF16), rope[0], rope[1],
                           p["o_q_norm"][j], p["o_k_norm"][j], geom)
    o = _attention(qn, kn, vn, geom)
    return _mixout_odd(h, o, p["o_w_out"][j].astype(BF16), p["mix_post_norm"][2 * j + 1], geom)


def kernel(x_prompt, x_sample, meta, mix_pre_norm, mix_post_norm, ffn_pre_norm, ffn_post_norm, e_w_in, a_conv_w, a_log, a_dt_bias, a_out_norm, b_shift, b_w0, b_w2, b_a0, b_a2, b_g2, b_k_k, b_k_a, b_r_k, b_ln_w, b_ln_b, e_w_out, o_w_qkv, o_q_norm, o_k_norm, o_w_out, f_w_in, f_conv_w, f_conv_b, f_w_out):
    p = dict(mix_pre_norm=mix_pre_norm, mix_post_norm=mix_post_norm, e_w_in=e_w_in, a_conv_w=a_conv_w,
             a_log=a_log, a_dt_bias=a_dt_bias, a_out_norm=a_out_norm, b_shift=b_shift, b_w0=b_w0, b_w2=b_w2,
             b_a0=b_a0, b_a2=b_a2, b_g2=b_g2, b_k_k=b_k_k, b_k_a=b_k_a, b_r_k=b_r_k, b_ln_w=b_ln_w,
             b_ln_b=b_ln_b, e_w_out=e_w_out, o_w_qkv=o_w_qkv, o_q_norm=o_q_norm, o_k_norm=o_k_norm,
             o_w_out=o_w_out)
    xs = (x_prompt, x_sample)
    geom = _choose_geom([(x.shape[0], x.shape[1]) for x in xs])
    tables = _scan_tables(geom)
    rope = _rope_tables(geom)
    seg = np.arange(B_WIDTH) // B_HD
    ones = jnp.asarray(seg[:, None] == seg[None, :], BF16)

    parts = []
    for x, g in zip(xs, geom.groups):
        lead = jnp.concatenate([jnp.zeros((g.fv, D_MODEL), x.dtype), meta.astype(x.dtype)], axis=0)
        seq = jnp.concatenate([jnp.broadcast_to(lead[None], (g.nb,) + lead.shape), x], axis=1)
        parts.append(seq.reshape(g.nb * g.tp, D_MODEL))
    h = jnp.concatenate(parts, axis=0)

    for i in range(DEPTH):
        h = _even_layer(h, i // 2, geom, tables, ones, p) if i % 2 == 0 else _odd_layer(h, i // 2, geom, rope, p)
        h = _conv_ffn(h, ffn_pre_norm[i], f_w_in[i], f_conv_w[i], f_conv_b[i], f_w_out[i], ffn_post_norm[i], geom)

    outs = []
    for g in geom.groups:
        seq = h[g.start:g.start + g.nb * g.tp].reshape(g.nb, g.tp, D_MODEL)
        outs.append(seq[:, g.tp - g.n:])
    return tuple(outs)
```

```python
import functools
import itertools
from typing import NamedTuple

import numpy as np
import jax
import jax.numpy as jnp
from jax import lax
from jax.experimental import pallas as pl
from jax.experimental.pallas import tpu as pltpu

F32 = jnp.float32
BF16 = jnp.bfloat16

D_MODEL = 1024
DEPTH = 2
N_META = 16
GRID_W = 64
NORM_EPS = 1e-6

A_HEADS = 4
A_DK = 128
A_DV = 128
A_CONV = 5
A_WIDTH = A_HEADS * A_DV

B_HEADS = 8
B_HD = 64
B_WIDTH = B_HEADS * B_HD
B_W_LORA = 32
B_A_LORA = 32
B_G_LORA = 96
B_GN_EPS = 64e-5
B_LORA_PAD = 256

C_HEADS = 8
C_KV_HEADS = 2
C_GROUP = C_HEADS // C_KV_HEADS
C_HD = 128
ROPE_THETA = 10000.0

D_FF = 2816
FFN_CONV = 3

LANE = 128
CHUNK = 64
SEQ_BLK = 128
HALO = 16
FF_CHUNK = 256
FF_GROUP = 4
ATTN_MAX_TK = 1792
ATTN_MAX_SUB = 17
LOG2E = 1.4426950408889634
GELU_C0 = 0.7978845608028654
VMEM_LIMIT = 56 * 1024 * 1024


class _Group(NamedTuple):
    start: int
    nb: int
    n: int
    tp: int
    fv: int


class _Geom(NamedTuple):
    groups: tuple
    rows: int
    tm: int


def _choose_geom(shapes):
    tm = 512 if min(n for _, n in shapes) >= 512 else 128
    best = None
    for pads in itertools.product((128, 256, 384, 512), repeat=len(shapes)):
        start, ok, groups = 0, True, []
        for (nb, n), p in zip(shapes, pads):
            ok = ok and start % tm == 0
            groups.append(_Group(start, nb, n, n + p, p - N_META))
            start += nb * (n + p)
        if ok and start % tm == 0 and (best is None or start < best.rows):
            best = _Geom(tuple(groups), start, tm)
    assert best is not None
    return best


def _row_valid(geom, base, tm):
    start = jnp.int32(geom.groups[0].start)
    tp = jnp.int32(geom.groups[0].tp)
    fv = jnp.int32(geom.groups[0].fv)
    for g in geom.groups[1:]:
        assert tm <= g.tp
        inside = base >= g.start
        start = jnp.where(inside, g.start, start)
        tp = jnp.where(inside, g.tp, tp)
        fv = jnp.where(inside, g.fv, fv)
    pos = lax.rem(base - start, tp) + lax.broadcasted_iota(jnp.int32, (tm, 1), 0)
    pos = jnp.where(pos >= tp, pos - tp, pos)
    return pos >= fv


def _rms(x, w, eps=NORM_EPS):
    return x * lax.rsqrt(jnp.mean(x * x, axis=-1, keepdims=True) + eps) * w


def _sigmoid(x):
    return 1.0 / (1.0 + jnp.exp(-x))


def _softplus(x):
    return jnp.maximum(x, 0.0) + jnp.log1p(jnp.exp(-jnp.abs(x)))


def _dot(a, b):
    return jnp.dot(a.astype(BF16), b.astype(BF16), preferred_element_type=F32)


def _dot_nt(a, b):
    return lax.dot_general(a.astype(BF16), b.astype(BF16), (((1,), (1,)), ((), ())),
                           preferred_element_type=F32)


def _seg_sum(x, ones_ref, terms):
    acc = None
    rem = x
    for t in range(terms):
        piece = rem.astype(BF16)
        part = jnp.dot(piece, ones_ref[...], preferred_element_type=F32)
        acc = part if acc is None else acc + part
        if t + 1 < terms:
            rem = rem - piece.astype(F32)
    return acc


def _chunk_cumsum(x, reverse):
    tm = x.shape[0]
    pos = lax.broadcasted_iota(jnp.int32, (tm, 1), 0) & (CHUNK - 1)
    s = 1
    while s < CHUNK:
        if reverse:
            x = x + jnp.where(pos < CHUNK - s, pltpu.roll(x, tm - s, axis=0), 0.0)
        else:
            x = x + jnp.where(pos >= s, pltpu.roll(x, s, axis=0), 0.0)
        s *= 2
    return x


def _neumann_inverses(n_mats, eye):
    n = eye.shape[0]
    prods = [eye + m for m in n_mats]
    powers = [_dot(m, m) for m in n_mats]
    s = 4
    while s < CHUNK:
        both = [_dot(jnp.concatenate([pr, pw], axis=0), pw) for pr, pw in zip(prods, powers)]
        prods = [pr + b[:n] for pr, b in zip(prods, both)]
        powers = [b[n:] for b in both]
        s *= 2
    return [pr + _dot(pr, pw) for pr, pw in zip(prods, powers)]


def _fill_window(xw_ref, x_ref, xp_ref, xn_ref, tm):
    i = pl.program_id(0)
    xw_ref[0:HALO] = jnp.where(i > 0, xp_ref[...].astype(F32), 0.0)
    xw_ref[HALO:HALO + tm] = x_ref[...].astype(F32)
    xw_ref[HALO + tm:2 * HALO + tm] = jnp.where(i < pl.num_programs(0) - 1, xn_ref[...].astype(F32), 0.0)


def _halo_specs(tm, width, rows):
    per = tm // HALO
    last = rows // HALO - 1
    return [
        pl.BlockSpec((tm, width), lambda i: (i, 0)),
        pl.BlockSpec((HALO, width), lambda i: (jnp.maximum(i * per - 1, 0), 0)),
        pl.BlockSpec((HALO, width), lambda i: (jnp.minimum((i + 1) * per, last), 0)),
    ]


def _const_spec(shape):
    return pl.BlockSpec(shape, lambda *_: (0,) * len(shape))


def _params(sem=("parallel",)):
    return pltpu.CompilerParams(dimension_semantics=sem, vmem_limit_bytes=VMEM_LIMIT)


def _proj_kernel(x_ref, nw_ref, w_ref, *rest, widths):
    out_refs, xn_ref = rest[:len(widths)], rest[len(widths)]
    xn_ref[...] = _rms(x_ref[...], nw_ref[...]).astype(BF16)
    c0 = 0
    for o_ref, wd in zip(out_refs, widths):
        for s in range(0, wd, 512):
            e = min(s + 512, wd)
            o_ref[:, s:e] = jnp.dot(xn_ref[...], w_ref[:, c0 + s:c0 + e],
                                    preferred_element_type=F32).astype(o_ref.dtype)
        c0 += wd


def _norm_proj(h, norm_w, w_bf16, widths, dtypes, geom, name):
    tm, rows = geom.tm, geom.rows
    return pl.pallas_call(
        functools.partial(_proj_kernel, widths=widths),
        out_shape=[jax.ShapeDtypeStruct((rows, wd), dt) for wd, dt in zip(widths, dtypes)],
        grid=(rows // tm,),
        in_specs=[pl.BlockSpec((tm, D_MODEL), lambda i: (i, 0)),
                  _const_spec((1, D_MODEL)),
                  _const_spec(w_bf16.shape)],
        out_specs=[pl.BlockSpec((tm, wd), lambda i: (i, 0)) for wd in widths],
        scratch_shapes=[pltpu.VMEM((tm, D_MODEL), BF16)],
        compiler_params=_params(),
        name=name,
    )(h, norm_w.reshape(1, D_MODEL), w_bf16)


def _a_prep_kernel(x_ref, xp_ref, xn_ref, g_ref, cw_ref, alog_ref, dtb_ref,
                   q_ref, k_ref, v_ref, go_ref, xw_ref, *, geom):
    tm = geom.tm
    i = pl.program_id(0)
    valid = _row_valid(geom, i * tm, tm)
    _fill_window(xw_ref, x_ref, xp_ref, xn_ref, tm)
    half = A_CONV // 2
    for c, o_ref in enumerate((q_ref, k_ref, v_ref)):
        cs = slice(c * A_WIDTH, (c + 1) * A_WIDTH)
        acc = xw_ref[HALO - half:HALO - half + tm, cs] * cw_ref[0:1, cs]
        for j in range(1, A_CONV):
            acc = acc + xw_ref[HALO - half + j:HALO - half + j + tm, cs] * cw_ref[j:j + 1, cs]
        y = jnp.where(valid, acc * _sigmoid(acc), 0.0)
        for hd in range(A_HEADS):
            hs = slice(hd * A_DK, (hd + 1) * A_DK)
            yh = y[:, hs]
            if c < 2:
                yh = yh * lax.rsqrt(jnp.sum(yh * yh, axis=-1, keepdims=True) + 1e-6)
                if c == 0:
                    yh = yh * (A_DK ** -0.5)
            o_ref[:, hs] = yh.astype(o_ref.dtype)
    al = g_ref[...]
    lane = lax.broadcasted_iota(jnp.int32, (1, LANE), 1)
    gval = jnp.where(valid, -jnp.exp(alog_ref[...]) * _softplus(al + dtb_ref[...]), 0.0)
    beta = jnp.where(valid, _sigmoid(al), 0.0)
    cum_f = _chunk_cumsum(gval, False)
    cum_b = _chunk_cumsum(gval, True)
    go_ref[...] = jnp.where(lane < A_HEADS, cum_f,
                            jnp.where(lane < 2 * A_HEADS, cum_b,
                                      jnp.where(lane < 4 * A_HEADS, beta, 0.0)))


def _a_prep(pa, ga, conv_w, a_log, dt_bias, geom):
    tm, rows = geom.tm, geom.rows
    qkv_w = 3 * A_WIDTH
    pad = lambda v: jnp.pad(v.reshape(1, -1).astype(F32), ((0, 0), (0, LANE - v.size)))
    out = jax.ShapeDtypeStruct((rows, A_WIDTH), BF16)
    return pl.pallas_call(
        functools.partial(_a_prep_kernel, geom=geom),
        out_shape=[out, out, out, jax.ShapeDtypeStruct((rows, LANE), F32)],
        grid=(rows // tm,),
        in_specs=_halo_specs(tm, qkv_w, rows) + [
            pl.BlockSpec((tm, LANE), lambda i: (i, 0)),
            _const_spec((A_CONV, qkv_w)), _const_spec((1, LANE)), _const_spec((1, LANE))],
        out_specs=[pl.BlockSpec((tm, A_WIDTH), lambda i: (i, 0))] * 3 + [pl.BlockSpec((tm, LANE), lambda i: (i, 0))],
        scratch_shapes=[pltpu.VMEM((tm + 2 * HALO, qkv_w), F32)],
        compiler_params=_params(),
        name="a_prep",
    )(pa, pa, pa, ga, conv_w, pad(a_log), pad(dt_bias))


def _a_scan_kernel(fi_ref, bi_ref, fr_ref,
                   qf, kf, vf, gf, gtf, qb, kb, vb, gb, gtb, of, ob, sf, sb):
    del fi_ref, bi_ref
    @pl.when(fr_ref[pl.program_id(0)] == 1)
    def _():
        sf[...] = jnp.zeros_like(sf)
        sb[...] = jnp.zeros_like(sb)
    r = lax.broadcasted_iota(jnp.int32, (CHUNK, CHUNK), 0)
    c = lax.broadcasted_iota(jnp.int32, (CHUNK, CHUNK), 1)
    eye = jnp.where(r == c, 1.0, 0.0)
    nc = SEQ_BLK // CHUNK
    dirs = ((qf, kf, vf, gf, gtf, of, sf, False, r >= c, r > c),
            (qb, kb, vb, gb, gtb, ob, sb, True, r <= c, r < c))

    probs = []
    for d, (q_ref, k_ref, v_ref, g_ref, gt_ref, _, _, reverse, tri, strict) in enumerate(dirs):
        edge = 0 if reverse else CHUNK - 1
        for step in range(nc):
            jj = nc - 1 - step if reverse else step
            rows = slice(jj * CHUNK, (jj + 1) * CHUNK)
            for hd in range(A_HEADS):
                hs = slice(hd * A_DK, (hd + 1) * A_DK)
                col = hd + (A_HEADS if reverse else 0)
                q, k, v = (ref[rows, hs].astype(F32) for ref in (q_ref, k_ref, v_ref))
                g_col = g_ref[rows, col:col + 1]
                b_col = g_ref[rows, 2 * A_HEADS + col:2 * A_HEADS + col + 1]
                g_row = gt_ref[jj, col:col + 1, :]
                decay = jnp.where(tri, jnp.exp(jnp.where(tri, g_col - g_row, 0.0)), 0.0)
                kb_ = k * b_col
                e_g = jnp.exp(g_col)
                g_last = g_col[edge:edge + 1]
                probs.append(dict(d=d, step=step, hd=hd, rows=rows, hs=hs, tri=tri, strict=strict, decay=decay,
                                  k=k, kbq=jnp.concatenate([kb_, q], axis=0),
                                  rhs=jnp.concatenate([v * b_col, kb_ * e_g], axis=1),
                                  qg=q * e_g, kg_t=(k * jnp.exp(g_last - g_col)).T, g_end=jnp.exp(g_last)))
    for p in probs:
        kq = _dot_nt(p["kbq"], p["k"])
        p["l"] = jnp.where(p["strict"], kq[:CHUNK] * p["decay"], 0.0)
        p["attn"] = jnp.where(p["tri"], kq[CHUNK:] * p["decay"], 0.0)
    t_invs = _neumann_inverses([-p["l"] for p in probs], eye)
    for p, t_inv in zip(probs, t_invs):
        p["uw"] = _dot(t_inv, p["rhs"])

    states = {(d, hd): dirs[d][6][hd] for d in range(2) for hd in range(A_HEADS)}
    for step in range(nc):
        cur = [p for p in probs if p["step"] == step]
        v_news = [p["uw"][:, :A_DV] - _dot(p["uw"][:, A_DV:], states[p["d"], p["hd"]]) for p in cur]
        for p, v_new in zip(cur, v_news):
            state = states[p["d"], p["hd"]]
            dirs[p["d"]][5][p["rows"], p["hs"]] = _dot(jnp.concatenate([p["qg"], p["attn"]], axis=1),
                                                      jnp.concatenate([state, v_new], axis=0))
        for p, v_new in zip(cur, v_news):
            key = (p["d"], p["hd"])
            states[key] = states[key] * p["g_end"] + _dot(p["kg_t"], v_new)
    for (d, hd), state in states.items():
        dirs[d][6][hd] = state


def _scan_tables(geom):
    fi, bi, fr = [], [], []
    for g in geom.groups:
        nblk = g.tp // SEQ_BLK
        for b in range(g.nb):
            base = (g.start + b * g.tp) // SEQ_BLK
            for i in range(nblk):
                fi.append(base + i)
                bi.append(base + nblk - 1 - i)
                fr.append(1 if i == 0 else 0)
    mk = lambda v: jnp.asarray(np.asarray(v, np.int32))
    return mk(fi), mk(bi), mk(fr)


def _a_scan(qn, kn, vv, gts, geom, tables):
    rows = geom.rows
    nc = SEQ_BLK // CHUNK
    gts_t = gts[:, :4 * A_HEADS].reshape(rows // CHUNK, CHUNK, 4 * A_HEADS).transpose(0, 2, 1)
    wide_f = pl.BlockSpec((SEQ_BLK, A_WIDTH), lambda s, fi, bi, fr: (fi[s], 0))
    wide_b = pl.BlockSpec((SEQ_BLK, A_WIDTH), lambda s, fi, bi, fr: (bi[s], 0))
    gate_f = pl.BlockSpec((SEQ_BLK, LANE), lambda s, fi, bi, fr: (fi[s], 0))
    gate_b = pl.BlockSpec((SEQ_BLK, LANE), lambda s, fi, bi, fr: (bi[s], 0))
    gt_f = pl.BlockSpec((nc, 4 * A_HEADS, CHUNK), lambda s, fi, bi, fr: (fi[s], 0, 0))
    gt_b = pl.BlockSpec((nc, 4 * A_HEADS, CHUNK), lambda s, fi, bi, fr: (bi[s], 0, 0))
    out = jax.ShapeDtypeStruct((rows, A_WIDTH), F32)
    return pl.pallas_call(
        _a_scan_kernel,
        out_shape=[out, out],
        grid_spec=pltpu.PrefetchScalarGridSpec(
            num_scalar_prefetch=3,
            grid=(tables[0].shape[0],),
            in_specs=[wide_f, wide_f, wide_f, gate_f, gt_f, wide_b, wide_b, wide_b, gate_b, gt_b],
            out_specs=[wide_f, wide_b],
            scratch_shapes=[pltpu.VMEM((A_HEADS, A_DK, A_DV), F32)] * 2),
        compiler_params=_params(("arbitrary",)),
        name="a_scan",
    )(*tables, qn, kn, vv, gts, gts_t, qn, kn, vv, gts, gts_t)


def _b_prep_kernel(x_ref, xp_ref, xn_ref, l_ref, lp_ref, ln_ref, mu_ref, w0_ref, w2_ref, a0_ref, a2_ref, g2_ref,
                   kk_ref, ka_ref, rk_ref, ones_ref,
                   r_out, k_out, v_out, kkn_out, eta_out, cf_out, cb_out, bonus_out, gate_out,
                   xw_ref, lw_ref, *, geom):
    tm = geom.tm
    _fill_window(xw_ref, x_ref, xp_ref, xn_ref, tm)
    _fill_window(lw_ref, l_ref, lp_ref, ln_ref, tm)

    def shifted(win_ref, cs, mu_off):
        cur = win_ref[HALO:HALO + tm, cs]
        prev = win_ref[HALO - 1:HALO - 1 + tm, cs]
        nxt = win_ref[HALO + 1:HALO + 1 + tm, cs]
        ms = slice(mu_off + cs.start, mu_off + cs.stop)
        return cur + mu_ref[0:1, ms] * (prev - cur) + mu_ref[1:2, ms] * (nxt - cur)

    w = B_WIDTH
    r = shifted(xw_ref, slice(0, w), 0)
    k = shifted(xw_ref, slice(w, 2 * w), 0)
    v = shifted(xw_ref, slice(2 * w, 3 * w), 0)
    lo = shifted(lw_ref, slice(0, B_LORA_PAD), 3 * w)
    wl = _dot(jnp.tanh(lo), w2_ref[...]) + w0_ref[...]
    log_decay = -jnp.exp(-_softplus(-wl) - 0.5)
    cf_out[...] = _chunk_cumsum(log_decay[:, :w], False)
    cb_out[...] = _chunk_cumsum(log_decay[:, w:], True)
    eta = _sigmoid(a0_ref[...] + _dot(lo, a2_ref[...]))
    gate_out[...] = _dot(_sigmoid(lo), g2_ref[...])
    kx = k * kk_ref[...]
    kkn_out[...] = (kx * lax.rsqrt(_seg_sum(kx * kx, ones_ref, 2) + 1e-6)).astype(kkn_out.dtype)
    k = k * (1.0 + (eta - 1.0) * ka_ref[...])
    bonus_out[...] = _seg_sum(r * k * rk_ref[...], ones_ref, 2) * v
    r_out[...] = r.astype(r_out.dtype)
    k_out[...] = k.astype(k_out.dtype)
    v_out[...] = v.astype(v_out.dtype)
    eta_out[...] = eta.astype(eta_out.dtype)


def _b_prep(pb, lo, mu, w0, w2, a0, a2, g2, k_k, k_a, r_k, ones, geom):
    tm, rows = geom.tm, geom.rows
    wd = 3 * B_WIDTH
    narrow = jax.ShapeDtypeStruct((rows, B_WIDTH), BF16)
    wide = jax.ShapeDtypeStruct((rows, B_WIDTH), F32)
    row = lambda v: v.reshape(1, -1).astype(F32)
    return pl.pallas_call(
        functools.partial(_b_prep_kernel, geom=geom),
        out_shape=[narrow] * 5 + [wide] * 4,
        grid=(rows // tm,),
        in_specs=_halo_specs(tm, wd, rows) + _halo_specs(tm, B_LORA_PAD, rows) + [
            _const_spec((2, wd + B_LORA_PAD)), _const_spec((1, 2 * B_WIDTH)), _const_spec((B_LORA_PAD, 2 * B_WIDTH)),
            _const_spec((1, B_WIDTH)), _const_spec((B_LORA_PAD, B_WIDTH)), _const_spec((B_LORA_PAD, B_WIDTH)),
            _const_spec((1, B_WIDTH)), _const_spec((1, B_WIDTH)), _const_spec((1, B_WIDTH)),
            _const_spec((B_WIDTH, B_WIDTH))],
        out_specs=[pl.BlockSpec((tm, B_WIDTH), lambda i: (i, 0))] * 9,
        scratch_shapes=[pltpu.VMEM((tm + 2 * HALO, wd), F32), pltpu.VMEM((tm + 2 * HALO, B_LORA_PAD), F32)],
        compiler_params=_params(),
        name="b_prep",
    )(pb, pb, pb, lo, lo, lo, mu, w0, w2, row(a0), a2, g2, row(k_k), row(k_a), row(r_k), ones)


def _b_scan_kernel(fi_ref, bi_ref, fr_ref,
                   rf, kf, vf, kkf, ef, cf, rb, kb, vb, kkb, eb, cb, yf, yb, pf, pb):
    del fi_ref, bi_ref
    @pl.when(fr_ref[pl.program_id(0)] == 1)
    def _():
        pf[...] = jnp.zeros_like(pf)
        pb[...] = jnp.zeros_like(pb)
    n2 = 2 * CHUNK
    r = lax.broadcasted_iota(jnp.int32, (n2, n2), 0)
    c = lax.broadcasted_iota(jnp.int32, (n2, n2), 1)
    same = (r >= CHUNK) == (c >= CHUNK)
    rt, ct = r & (CHUNK - 1), c & (CHUNK - 1)
    eye = jnp.where(r == c, 1.0, 0.0)
    lane_lo = lax.broadcasted_iota(jnp.int32, (1, n2), 1) < B_HD
    row_pos = lax.broadcasted_iota(jnp.int32, (CHUNK, 1), 0)
    nc = SEQ_BLK // CHUNK
    pair_w = 2 * B_HD
    n_pairs = B_HEADS // 2
    dirs = (((rf, kf, vf, kkf, ef, cf), yf, pf, False, same & (rt > ct), same & (rt >= ct)),
            ((rb, kb, vb, kkb, eb, cb), yb, pb, True, same & (rt < ct), same & (rt <= ct)))

    def expand(x):
        return jnp.concatenate([jnp.where(lane_lo, x, 0.0), jnp.where(lane_lo, 0.0, x)], axis=0)

    probs = []
    for d, (refs, _, _, reverse, strict, incl) in enumerate(dirs):
        r_ref, k_ref, v_ref, kk_ref, eta_ref, c_ref = refs
        for step in range(nc):
            jj = nc - 1 - step if reverse else step
            rows = slice(jj * CHUNK, (jj + 1) * CHUNK)
            for pr in range(n_pairs):
                ls = slice(pr * pair_w, (pr + 1) * pair_w)
                r, k, v, kk, eta = (ref[rows, ls].astype(F32) for ref in (r_ref, k_ref, v_ref, kk_ref, eta_ref))
                c_in = c_ref[rows, ls]
                if reverse:
                    c_ex = jnp.where(row_pos < CHUNK - 1, pltpu.roll(c_in, CHUNK - 1, axis=0), 0.0)
                    c_tot = c_in[0:1]
                else:
                    c_ex = jnp.where(row_pos >= 1, pltpu.roll(c_in, 1, axis=0), 0.0)
                    c_tot = c_in[CHUNK - 1:CHUNK]
                b = kk * eta
                inv_w = jnp.exp(-c_in)
                rest_w = jnp.exp(c_tot - c_in)
                a_e = expand(-kk * jnp.exp(c_ex))
                r_e = expand(r * jnp.exp(c_in))
                probs.append(dict(
                    d=d, step=step, pr=pr, rows=rows, ls=ls, strict=strict, incl=incl, v=v,
                    a_e=a_e, r_e=r_e, v_e=expand(v),
                    lhs=jnp.concatenate([a_e, r_e], axis=0),
                    rhs=jnp.concatenate([expand(b * inv_w), expand(k * inv_w)], axis=0),
                    kd_t=jnp.concatenate([b * rest_w, k * rest_w], axis=0).T,
                    w_col=jnp.sum(jnp.where(eye > 0.0, jnp.exp(c_tot), 0.0), axis=1, keepdims=True)))
    for p in probs:
        m1 = _dot_nt(p["lhs"], p["rhs"])
        p["n_ab"] = jnp.where(p["strict"], m1[:n2, :n2], 0.0)
        p["a_ak"] = jnp.where(p["strict"], m1[:n2, n2:], 0.0)
        p["a_r"] = jnp.concatenate([jnp.where(p["incl"], m1[n2:, :n2], 0.0),
                                    jnp.where(p["incl"], m1[n2:, n2:], 0.0)], axis=1)
    t_invs = _neumann_inverses([p["n_ab"] for p in probs], eye)
    for p in probs:
        p["akv"] = _dot(p["a_ak"], p["v_e"])
    for p, t_inv in zip(probs, t_invs):
        p["x"] = _dot(t_inv, jnp.concatenate([p["akv"], p["a_e"]], axis=1))

    states = {(d, pr): dirs[d][2][pr] for d in range(2) for pr in range(n_pairs)}
    for step in range(nc):
        cur = [p for p in probs if p["step"] == step]
        u_es = [p["x"][:, :pair_w] + _dot(p["x"][:, pair_w:], states[p["d"], p["pr"]]) for p in cur]
        for p, u_e in zip(cur, u_es):
            y_e = _dot(jnp.concatenate([p["r_e"], p["a_r"]], axis=1),
                       jnp.concatenate([states[p["d"], p["pr"]], u_e, p["v_e"]], axis=0))
            dirs[p["d"]][1][p["rows"], p["ls"]] = y_e[:CHUNK] + y_e[CHUNK:]
        for p, u_e in zip(cur, u_es):
            key = (p["d"], p["pr"])
            upd = _dot(p["kd_t"], jnp.concatenate([u_e[:CHUNK] + u_e[CHUNK:], p["v"]], axis=0))
            states[key] = states[key] * p["w_col"] + jnp.where(same, upd, 0.0)
    for (d, pr), state in states.items():
        dirs[d][2][pr] = state


def _b_scan(r, k, v, kkn, eta, cf, cb, geom, tables):
    wide_f = pl.BlockSpec((SEQ_BLK, B_WIDTH), lambda s, fi, bi, fr: (fi[s], 0))
    wide_b = pl.BlockSpec((SEQ_BLK, B_WIDTH), lambda s, fi, bi, fr: (bi[s], 0))
    out = jax.ShapeDtypeStruct((geom.rows, B_WIDTH), F32)
    return pl.pallas_call(
        _b_scan_kernel,
        out_shape=[out, out],
        grid_spec=pltpu.PrefetchScalarGridSpec(
            num_scalar_prefetch=3,
            grid=(tables[0].shape[0],),
            in_specs=[wide_f] * 6 + [wide_b] * 6,
            out_specs=[wide_f, wide_b],
            scratch_shapes=[pltpu.VMEM((B_HEADS // 2, 2 * B_HD, 2 * B_HD), F32)] * 2),
        compiler_params=_params(("arbitrary",)),
        name="b_scan",
    )(*tables, r, k, v, kkn, eta, cf, r, k, v, kkn, eta, cb)


def _finish_residual(h, mix, post_w, valid):
    return jnp.where(valid, h + _rms(mix, post_w), 0.0)


def _mixout_even_kernel(h_ref, oaf_ref, oab_ref, z_ref, ybf_ref, ybb_ref, bonus_ref, gate_ref,
                        w_ref, an_ref, lnw_ref, lnb_ref, post_ref, ones_ref, o_ref, cat_ref, *, geom):
    tm = geom.tm
    valid = _row_valid(geom, pl.program_id(0) * tm, tm)
    for hd in range(A_HEADS):
        hs = slice(hd * A_DV, (hd + 1) * A_DV)
        o = oaf_ref[:, hs] + oab_ref[:, hs]
        z = z_ref[:, hs].astype(F32)
        cat_ref[:, hs] = (_rms(o, an_ref[...]) * (z * _sigmoid(z))).astype(BF16)
    y = ybf_ref[...] + ybb_ref[...]
    yc = y - _seg_sum(y, ones_ref, 3) * (1.0 / B_HD)
    var = _seg_sum(yc * yc, ones_ref, 2) * (1.0 / B_HD)
    yn = yc * lax.rsqrt(var + B_GN_EPS) * lnw_ref[...] + lnb_ref[...]
    cat_ref[:, A_WIDTH:] = ((yn + bonus_ref[...]) * gate_ref[...]).astype(BF16)
    mix = jnp.dot(cat_ref[...], w_ref[...], preferred_element_type=F32)
    o_ref[...] = _finish_residual(h_ref[...], mix, post_ref[...], valid)


def _mixout_even(h, oaf, oab, pa, ybf, ybb, bonus, gate, w_out, a_norm, ln_w, ln_b, post_w, ones, geom):
    tm, rows = geom.tm, geom.rows
    row = lambda v: v.reshape(1, -1).astype(F32)
    a_blk = pl.BlockSpec((tm, A_WIDTH), lambda i: (i, 0))
    b_blk = pl.BlockSpec((tm, B_WIDTH), lambda i: (i, 0))
    return pl.pallas_call(
        functools.partial(_mixout_even_kernel, geom=geom),
        out_shape=jax.ShapeDtypeStruct((rows, D_MODEL), F32),
        grid=(rows // tm,),
        in_specs=[pl.BlockSpec((tm, D_MODEL), lambda i: (i, 0)), a_blk, a_blk,
                  pl.BlockSpec((tm, A_WIDTH), lambda i: (i, 3)),
                  b_blk, b_blk, b_blk, b_blk,
                  _const_spec((A_WIDTH + B_WIDTH, D_MODEL)), _const_spec((1, A_DV)),
                  _const_spec((1, B_WIDTH)), _const_spec((1, B_WIDTH)), _const_spec((1, D_MODEL)),
                  _const_spec((B_WIDTH, B_WIDTH))],
        out_specs=pl.BlockSpec((tm, D_MODEL), lambda i: (i, 0)),
        scratch_shapes=[pltpu.VMEM((tm, A_WIDTH + B_WIDTH), BF16)],
        compiler_params=_params(),
        name="mixout_even",
    )(h, oaf, oab, pa, ybf, ybb, bonus, gate, w_out, row(a_norm), row(ln_w), row(ln_b), row(post_w), ones)


def _mixout_odd_kernel(h_ref, o_ref_in, w_ref, post_ref, o_ref, *, geom):
    tm = geom.tm
    valid = _row_valid(geom, pl.program_id(0) * tm, tm)
    mix = jnp.dot(o_ref_in[...].astype(BF16), w_ref[...], preferred_element_type=F32)
    o_ref[...] = _finish_residual(h_ref[...], mix, post_ref[...], valid)


def _mixout_odd(h, o, w_out, post_w, geom):
    tm, rows = geom.tm, geom.rows
    wd = C_HEADS * C_HD
    return pl.pallas_call(
        functools.partial(_mixout_odd_kernel, geom=geom),
        out_shape=jax.ShapeDtypeStruct((rows, D_MODEL), F32),
        grid=(rows // tm,),
        in_specs=[pl.BlockSpec((tm, D_MODEL), lambda i: (i, 0)), pl.BlockSpec((tm, wd), lambda i: (i, 0)),
                  _const_spec((wd, D_MODEL)), _const_spec((1, D_MODEL))],
        out_specs=pl.BlockSpec((tm, D_MODEL), lambda i: (i, 0)),
        compiler_params=_params(),
        name="mixout_odd",
    )(h, o, w_out, post_w.reshape(1, D_MODEL))


def _qkv_kernel(x_ref, nw_ref, w_ref, cos_ref, sin_ref, qn_ref, kn_ref, q_out, k_out, v_out, xn_ref):
    xn_ref[...] = _rms(x_ref[...], nw_ref[...]).astype(BF16)
    cos, sin = cos_ref[...], sin_ref[...]
    lane = lax.broadcasted_iota(jnp.int32, (1, C_HD), 1)
    first_half = (lane & (C_HD // 2 - 1)) < C_HD // 4

    def rope(xh, w, scale):
        xh = _rms(xh, w)
        partner = jnp.where(first_half, pltpu.roll(xh, C_HD - C_HD // 4, axis=1), pltpu.roll(xh, C_HD // 4, axis=1))
        return (xh * cos + partner * sin) * scale

    nq = C_HEADS * C_HD
    nk = C_KV_HEADS * C_HD
    per = 512 // C_HD
    for c in range(nq // 512):
        y = jnp.dot(xn_ref[...], w_ref[:, c * 512:(c + 1) * 512], preferred_element_type=F32)
        for hd in range(per):
            hs = slice((c * per + hd) * C_HD, (c * per + hd + 1) * C_HD)
            q_out[:, hs] = rope(y[:, hd * C_HD:(hd + 1) * C_HD], qn_ref[...], C_HD ** -0.5 * LOG2E).astype(BF16)
    y = jnp.dot(xn_ref[...], w_ref[:, nq:], preferred_element_type=F32)
    for hd in range(C_KV_HEADS):
        hs = slice(hd * C_HD, (hd + 1) * C_HD)
        k_out[:, hs] = rope(y[:, hs], kn_ref[...], 1.0).astype(BF16)
        v_out[:, 2 * hd * C_HD:(2 * hd + 1) * C_HD] = y[:, nk + hd * C_HD:nk + (hd + 1) * C_HD].astype(BF16)
        v_out[:, (2 * hd + 1) * C_HD:(2 * hd + 2) * C_HD] = jnp.ones((x_ref.shape[0], C_HD), BF16)


def _qkv_proj(h, norm_w, w_bf16, cos_t, sin_t, q_norm, k_norm, geom):
    tm, rows = geom.tm, geom.rows
    nq, nk = C_HEADS * C_HD, C_KV_HEADS * C_HD
    return pl.pallas_call(
        _qkv_kernel,
        out_shape=[jax.ShapeDtypeStruct((rows, nq), BF16), jax.ShapeDtypeStruct((rows, nk), BF16),
                   jax.ShapeDtypeStruct((rows, 2 * nk), BF16)],
        grid=(rows // tm,),
        in_specs=[pl.BlockSpec((tm, D_MODEL), lambda i: (i, 0)), _const_spec((1, D_MODEL)),
                  _const_spec((D_MODEL, nq + 2 * nk)),
                  pl.BlockSpec((tm, C_HD), lambda i: (i, 0)), pl.BlockSpec((tm, C_HD), lambda i: (i, 0)),
                  _const_spec((1, C_HD)), _const_spec((1, C_HD))],
        out_specs=[pl.BlockSpec((tm, nq), lambda i: (i, 0)), pl.BlockSpec((tm, nk), lambda i: (i, 0)),
                   pl.BlockSpec((tm, 2 * nk), lambda i: (i, 0))],
        scratch_shapes=[pltpu.VMEM((tm, D_MODEL), BF16)],
        compiler_params=_params(),
        name="qkv_proj",
    )(h, norm_w.reshape(1, D_MODEL), w_bf16, cos_t, sin_t, q_norm.reshape(1, C_HD), k_norm.reshape(1, C_HD))


def _rope_tables(geom):
    nf = C_HD // 4
    inv = ROPE_THETA ** (-jnp.arange(nf, dtype=F32) / nf)
    cos_parts, sin_parts = [], []
    for g in geom.groups:
        t = jnp.arange(g.tp, dtype=jnp.int32) - (g.fv + N_META)
        is_meta = t < 0
        row = jnp.where(is_meta, -1, t // GRID_W).astype(F32)
        col = jnp.where(is_meta, t + N_META, t % GRID_W).astype(F32)
        ang_r, ang_c = row[:, None] * inv, col[:, None] * inv
        cos = jnp.concatenate([jnp.cos(ang_r)] * 2 + [jnp.cos(ang_c)] * 2, axis=1)
        sin = jnp.concatenate([-jnp.sin(ang_r), jnp.sin(ang_r), -jnp.sin(ang_c), jnp.sin(ang_c)], axis=1)
        cos_parts.append(jnp.tile(cos, (g.nb, 1)))
        sin_parts.append(jnp.tile(sin, (g.nb, 1)))
    return jnp.concatenate(cos_parts, axis=0), jnp.concatenate(sin_parts, axis=0)


def _attn_kernel(*refs, nsub, bounds, fv):
    (q_ref, k_ref, v_ref), (o_ref, s_ref, rm_ref, m_ref, acc_ref) = refs[:3], refs[-5:]
    tq = SEQ_BLK
    nc = len(bounds)
    assert nc % 2 == 0

    def q_rows(qi):
        return pl.ds(pl.multiple_of(qi * tq, tq), tq)

    def produce(qi, c, slot):
        start, size = bounds[c]
        q = jnp.concatenate([q_ref[q_rows(qi), g * C_HD:(g + 1) * C_HD] for g in range(C_GROUP)], axis=0)
        s = lax.dot_general(q, k_ref[start:start + size, :], (((1,), (1,)), ((), ())),
                            preferred_element_type=F32)
        if start < fv:
            key = start + lax.broadcasted_iota(jnp.int32, (1, size), 1)
            s = jnp.where(key >= fv, s, -1e30)
        s_ref[slot, :, :size] = s
        rm_ref[slot] = jnp.max(s, axis=-1, keepdims=True)

    def consume(c, slot):
        start, size = bounds[c]
        m_old = m_ref[...]
        m_new = jnp.maximum(m_old, rm_ref[slot])
        p = jnp.exp2(s_ref[slot, :, :size] - m_new).astype(BF16)
        acc_ref[...] = jnp.exp2(m_old - m_new) * acc_ref[...] + jnp.dot(p, v_ref[start:start + size, :],
                                                                        preferred_element_type=F32)
        m_ref[...] = m_new

    produce(0, 0, 0)

    def body(qi, carry):
        m_ref[...] = jnp.full_like(m_ref, -1e30)
        acc_ref[...] = jnp.zeros_like(acc_ref)
        for c in range(nc):
            if c + 1 < nc:
                produce(qi, c + 1, (c + 1) % 2)
            else:
                produce(jnp.minimum(qi + 1, nsub - 1), 0, 0)
            consume(c, c % 2)
        acc = acc_ref[...]
        out = acc[:, :C_HD] / acc[:, C_HD:]
        for g in range(C_GROUP):
            o_ref[q_rows(qi), g * C_HD:(g + 1) * C_HD] = out[g * tq:(g + 1) * tq]
        return carry

    lax.fori_loop(0, nsub, body, 0)


def _key_chunks(tp, fv):
    lo = fv // LANE
    blocks = tp // LANE - lo
    per = ATTN_MAX_TK // LANE
    nc = -(-blocks // per)
    nc += nc % 2
    pairs, single = divmod(blocks, 2)
    sizes = [2 * (pairs // nc + (1 if i < pairs % nc else 0)) for i in range(nc)]
    sizes[-1] += single
    if min(sizes) == 0:
        sizes = [blocks // nc + (1 if i < blocks % nc else 0) for i in range(nc)]
    assert min(sizes) > 0 and max(sizes) <= per
    starts = np.cumsum([lo] + sizes[:-1])
    return tuple((int(s) * LANE, int(z) * LANE) for s, z in zip(starts, sizes))


def _attention(qn, kn, vn, geom):
    out = None
    for g in geom.groups:
        bounds = _key_chunks(g.tp, g.fv)
        nblk = g.tp // SEQ_BLK
        nsub = max(d for d in range(1, ATTN_MAX_SUB + 1) if nblk % d == 0)
        tq, gw, rows = nsub * SEQ_BLK, C_GROUP * C_HD, C_GROUP * SEQ_BLK
        al = lambda x: pl.multiple_of(x, LANE)
        q_spec = pl.BlockSpec((pl.Element(tq), pl.Element(gw)),
                              lambda b, h, i, g=g, tq=tq: (al(g.start + b * g.tp + i * tq), al(h * gw)))
        args = [qn, kn, vn]
        in_specs = [q_spec,
                    pl.BlockSpec((pl.Element(g.tp), pl.Element(C_HD)),
                                 lambda b, h, i, g=g: (al(g.start + b * g.tp), al(h * C_HD))),
                    pl.BlockSpec((pl.Element(g.tp), pl.Element(2 * C_HD)),
                                 lambda b, h, i, g=g: (al(g.start + b * g.tp), al(h * 2 * C_HD)))]
        if out is not None:
            args.append(out)
            in_specs.append(pl.BlockSpec(memory_space=pl.ANY))
        out = pl.pallas_call(
            functools.partial(_attn_kernel, nsub=nsub, bounds=bounds, fv=g.fv),
            out_shape=jax.ShapeDtypeStruct((geom.rows, C_HEADS * C_HD), F32),
            grid=(g.nb, C_KV_HEADS, g.tp // tq),
            in_specs=in_specs,
            out_specs=q_spec,
            scratch_shapes=[pltpu.VMEM((2, rows, max(z for _, z in bounds)), F32), pltpu.VMEM((2, rows, 1), F32),
                            pltpu.VMEM((rows, 1), F32), pltpu.VMEM((rows, 2 * C_HD), F32)],
            input_output_aliases={3: 0} if len(args) == 4 else {},
            compiler_params=_params(("parallel", "parallel", "parallel")),
            name="attention",
        )(*args)
    return out


def _ffn_kernel(x_ref, xp_ref, xn_ref, pre_ref, wu_ref, wg_ref, cw_ref, cb_ref, wo_ref, post_ref,
                o_ref, xs_ref, g0_ref, g1_ref, act_ref, acc_ref, *, tm):
    i = pl.program_id(0)
    xs_ref[0:HALO] = jnp.where(i > 0, _rms(xp_ref[...], pre_ref[...]), 0.0).astype(BF16)
    xs_ref[HALO:HALO + tm] = _rms(x_ref[...], pre_ref[...]).astype(BF16)
    xs_ref[HALO + tm:2 * HALO + tm] = jnp.where(i < pl.num_programs(0) - 1,
                                                _rms(xn_ref[...], pre_ref[...]), 0.0).astype(BF16)
    half = FFN_CONV // 2
    n_chunks = D_FF // FF_CHUNK
    for c in range(n_chunks):
        cs = slice(c * FF_CHUNK, (c + 1) * FF_CHUNK)
        g_ref = (g0_ref, g1_ref)[c % 2]
        u = jnp.dot(xs_ref[HALO:HALO + tm], wu_ref[:, cs], preferred_element_type=F32)
        g_ref[...] = jnp.dot(xs_ref[...], wg_ref[:, cs], preferred_element_type=F32)
        gate = cb_ref[0:1, cs]
        for j in range(FFN_CONV):
            gate = gate + g_ref[HALO - half + j:HALO - half + j + tm] * cw_ref[j:j + 1, cs]
        th = jnp.tanh(gate * (GELU_C0 + GELU_C0 * 0.044715 * (gate * gate)))
        hu = 0.5 * gate * u
        act_ref[:, cs] = (hu + hu * th).astype(BF16)
        if c % FF_GROUP == FF_GROUP - 1 or c == n_chunks - 1:
            ks = slice((c // FF_GROUP) * FF_GROUP * FF_CHUNK, (c + 1) * FF_CHUNK)
            part = jnp.dot(act_ref[:, ks], wo_ref[ks, :], preferred_element_type=F32)
            if c < FF_GROUP:
                acc_ref[...] = part
            else:
                acc_ref[...] += part
    o_ref[...] = x_ref[...] + _rms(acc_ref[...], post_ref[...])


def _conv_ffn(h, pre_w, w_in, conv_w, conv_b, w_out, post_w, geom):
    tm, rows = geom.tm, geom.rows
    assert D_FF % FF_CHUNK == 0
    wu = w_in[:, :D_FF].astype(BF16)
    wg = w_in[:, D_FF:].astype(BF16)
    return pl.pallas_call(
        functools.partial(_ffn_kernel, tm=tm),
        out_shape=jax.ShapeDtypeStruct((rows, D_MODEL), F32),
        grid=(rows // tm,),
        in_specs=_halo_specs(tm, D_MODEL, rows) + [
            _const_spec((1, D_MODEL)), _const_spec((D_MODEL, D_FF)), _const_spec((D_MODEL, D_FF)),
            _const_spec((FFN_CONV, D_FF)), _const_spec((1, D_FF)), _const_spec((D_FF, D_MODEL)),
            _const_spec((1, D_MODEL))],
        out_specs=pl.BlockSpec((tm, D_MODEL), lambda i: (i, 0)),
        scratch_shapes=[pltpu.VMEM((tm + 2 * HALO, D_MODEL), BF16), pltpu.VMEM((tm + 2 * HALO, FF_CHUNK), F32),
                        pltpu.VMEM((tm + 2 * HALO, FF_CHUNK), F32), pltpu.VMEM((tm, D_FF), BF16),
                        pltpu.VMEM((tm, D_MODEL), F32)],
        compiler_params=_params(),
        name="conv_ffn",
    )(h, h, h, pre_w.reshape(1, D_MODEL), wu, wg, conv_w, conv_b.reshape(1, D_FF),
      w_out.astype(BF16), post_w.reshape(1, D_MODEL))


def _even_layer(h, j, geom, tables, ones, p):
    aw, bw = A_WIDTH, B_WIDTH
    a_cols = 4 * aw + 4 * A_HEADS
    w_in = p["e_w_in"][j]
    lora = B_W_LORA + B_A_LORA + B_G_LORA
    zeros = lambda n: jnp.zeros((D_MODEL, n), F32)
    w_cat = jnp.concatenate([
        w_in[:, a_cols:a_cols + 3 * bw + lora], zeros(B_LORA_PAD - lora),
        w_in[:, 4 * aw:a_cols], zeros(LANE - 4 * A_HEADS),
        w_in[:, :4 * aw]], axis=1).astype(BF16)
    pb, lo, ga, pa = _norm_proj(h, p["mix_pre_norm"][2 * j], w_cat, (3 * bw, B_LORA_PAD, LANE, 4 * aw),
                                (BF16, F32, F32, BF16), geom, "proj_even")

    qn, kn, vv, gts = _a_prep(pa, ga, p["a_conv_w"][j], p["a_log"][j], p["a_dt_bias"][j], geom)
    oaf, oab = _a_scan(qn, kn, vv, gts, geom, tables)

    mu = jnp.pad(p["b_shift"][j], ((0, 0), (0, B_LORA_PAD - lora)))
    lora_rows = lambda m, off: jnp.pad(m, ((off, B_LORA_PAD - off - m.shape[0]), (0, 0)))
    w2 = lora_rows(jnp.concatenate([p["b_w2"][j, 0], p["b_w2"][j, 1]], axis=1), 0)
    a2 = lora_rows(p["b_a2"][j], B_W_LORA)
    g2 = lora_rows(p["b_g2"][j], B_W_LORA + B_A_LORA)
    r, k, v, kkn, eta, cf, cb, bonus, gate = _b_prep(
        pb, lo, mu, p["b_w0"][j].reshape(1, 2 * bw), w2, p["b_a0"][j], a2, g2,
        p["b_k_k"][j], p["b_k_a"][j], p["b_r_k"][j], ones, geom)
    ybf, ybb = _b_scan(r, k, v, kkn, eta, cf, cb, geom, tables)

    return _mixout_even(h, oaf, oab, pa, ybf, ybb, bonus, gate, p["e_w_out"][j].astype(BF16),
                        p["a_out_norm"][j], p["b_ln_w"][j], p["b_ln_b"][j], p["mix_post_norm"][2 * j], ones, geom)


def _odd_layer(h, j, geom, rope, p):
    qn, kn, vn = _qkv_proj(h, p["mix_pre_norm"][2 * j + 1], p["o_w_qkv"][j].astype(BF16), rope[0], rope[1],
                           p["o_q_norm"][j], p["o_k_norm"][j], geom)
    o = _attention(qn, kn, vn, geom)
    return _mixout_odd(h, o, p["o_w_out"][j].astype(BF16), p["mix_post_norm"][2 * j + 1], geom)


def kernel(x_prompt, x_sample, meta, mix_pre_norm, mix_post_norm, ffn_pre_norm, ffn_post_norm, e_w_in, a_conv_w, a_log, a_dt_bias, a_out_norm, b_shift, b_w0, b_w2, b_a0, b_a2, b_g2, b_k_k, b_k_a, b_r_k, b_ln_w, b_ln_b, e_w_out, o_w_qkv, o_q_norm, o_k_norm, o_w_out, f_w_in, f_conv_w, f_conv_b, f_w_out):
    p = dict(mix_pre_norm=mix_pre_norm, mix_post_norm=mix_post_norm, e_w_in=e_w_in, a_conv_w=a_conv_w,
             a_log=a_log, a_dt_bias=a_dt_bias, a_out_norm=a_out_norm, b_shift=b_shift, b_w0=b_w0, b_w2=b_w2,
             b_a0=b_a0, b_a2=b_a2, b_g2=b_g2, b_k_k=b_k_k, b_k_a=b_k_a, b_r_k=b_r_k, b_ln_w=b_ln_w,
             b_ln_b=b_ln_b, e_w_out=e_w_out, o_w_qkv=o_w_qkv, o_q_norm=o_q_norm, o_k_norm=o_k_norm,
             o_w_out=o_w_out)
    xs = (x_prompt, x_sample)
    geom = _choose_geom([(x.shape[0], x.shape[1]) for x in xs])
    tables = _scan_tables(geom)
    rope = _rope_tables(geom)
    seg = np.arange(B_WIDTH) // B_HD
    ones = jnp.asarray(seg[:, None] == seg[None, :], BF16)

    parts = []
    for x, g in zip(xs, geom.groups):
        lead = jnp.concatenate([jnp.zeros((g.fv, D_MODEL), x.dtype), meta.astype(x.dtype)], axis=0)
        seq = jnp.concatenate([jnp.broadcast_to(lead[None], (g.nb,) + lead.shape), x], axis=1)
        parts.append(seq.reshape(g.nb * g.tp, D_MODEL))
    h = jnp.concatenate(parts, axis=0)

    for i in range(DEPTH):
        h = _even_layer(h, i // 2, geom, tables, ones, p) if i % 2 == 0 else _odd_layer(h, i // 2, geom, rope, p)
        h = _conv_ffn(h, ffn_pre_norm[i], f_w_in[i], f_conv_w[i], f_conv_b[i], f_w_out[i], ffn_post_norm[i], geom)

    outs = []
    for g in geom.groups:
        seq = h[g.start:g.start + g.nb * g.tp].reshape(g.nb, g.tp, D_MODEL)
        outs.append(seq[:, g.tp - g.n:])
    return tuple(outs)
```

```python
import functools
import itertools
from typing import NamedTuple

import numpy as np
import jax
import jax.numpy as jnp
from jax import lax
from jax.experimental import pallas as pl
from jax.experimental.pallas import tpu as pltpu

F32 = jnp.float32
BF16 = jnp.bfloat16

D_MODEL = 1024
DEPTH = 2
N_META = 16
GRID_W = 64
NORM_EPS = 1e-6

A_HEADS = 4
A_DK = 128
A_DV = 128
A_CONV = 5
A_WIDTH = A_HEADS * A_DV

B_HEADS = 8
B_HD = 64
B_WIDTH = B_HEADS * B_HD
B_W_LORA = 32
B_A_LORA = 32
B_G_LORA = 96
B_GN_EPS = 64e-5
B_LORA_PAD = 256

C_HEADS = 8
C_KV_HEADS = 2
C_GROUP = C_HEADS // C_KV_HEADS
C_HD = 128
ROPE_THETA = 10000.0

D_FF = 2816
FFN_CONV = 3

LANE = 128
CHUNK = 64
SEQ_BLK = 128
HALO = 16
FF_CHUNK = 256
FF_GROUP = 4
ATTN_MAX_TK = 1792
ATTN_MAX_SUB = 17
LOG2E = 1.4426950408889634
GELU_C0 = 0.7978845608028654
VMEM_LIMIT = 56 * 1024 * 1024


class _Group(NamedTuple):
    start: int
    nb: int
    n: int
    tp: int
    fv: int


class _Geom(NamedTuple):
    groups: tuple
    rows: int
    tm: int


def _choose_geom(shapes):
    tm = 512 if min(n for _, n in shapes) >= 512 else 128
    best = None
    for pads in itertools.product((128, 256, 384, 512), repeat=len(shapes)):
        start, ok, groups = 0, True, []
        for (nb, n), p in zip(shapes, pads):
            ok = ok and start % tm == 0
            groups.append(_Group(start, nb, n, n + p, p - N_META))
            start += nb * (n + p)
        if ok and start % tm == 0 and (best is None or start < best.rows):
            best = _Geom(tuple(groups), start, tm)
    assert best is not None
    return best


def _row_valid(geom, base, tm):
    start = jnp.int32(geom.groups[0].start)
    tp = jnp.int32(geom.groups[0].tp)
    fv = jnp.int32(geom.groups[0].fv)
    for g in geom.groups[1:]:
        assert tm <= g.tp
        inside = base >= g.start
        start = jnp.where(inside, g.start, start)
        tp = jnp.where(inside, g.tp, tp)
        fv = jnp.where(inside, g.fv, fv)
    pos = lax.rem(base - start, tp) + lax.broadcasted_iota(jnp.int32, (tm, 1), 0)
    pos = jnp.where(pos >= tp, pos - tp, pos)
    return pos >= fv


def _rms(x, w, eps=NORM_EPS):
    return x * lax.rsqrt(jnp.mean(x * x, axis=-1, keepdims=True) + eps) * w


def _sigmoid(x):
    return 1.0 / (1.0 + jnp.exp(-x))


def _softplus(x):
    return jnp.maximum(x, 0.0) + jnp.log1p(jnp.exp(-jnp.abs(x)))


def _dot(a, b):
    return jnp.dot(a.astype(BF16), b.astype(BF16), preferred_element_type=F32)


def _dot_nt(a, b):
    return lax.dot_general(a.astype(BF16), b.astype(BF16), (((1,), (1,)), ((), ())),
                           preferred_element_type=F32)


def _seg_sum(x, ones_ref, terms):
    acc = None
    rem = x
    for t in range(terms):
        piece = rem.astype(BF16)
        part = jnp.dot(piece, ones_ref[...], preferred_element_type=F32)
        acc = part if acc is None else acc + part
        if t + 1 < terms:
            rem = rem - piece.astype(F32)
    return acc


def _chunk_cumsum(x, reverse):
    tm = x.shape[0]
    pos = lax.broadcasted_iota(jnp.int32, (tm, 1), 0) & (CHUNK - 1)
    s = 1
    while s < CHUNK:
        if reverse:
            x = x + jnp.where(pos < CHUNK - s, pltpu.roll(x, tm - s, axis=0), 0.0)
        else:
            x = x + jnp.where(pos >= s, pltpu.roll(x, s, axis=0), 0.0)
        s *= 2
    return x


def _neumann_inverses(n_mats, eye):
    n = eye.shape[0]
    prods = [eye + m for m in n_mats]
    powers = [_dot(m, m) for m in n_mats]
    s = 4
    while s < CHUNK:
        both = [_dot(jnp.concatenate([pr, pw], axis=0), pw) for pr, pw in zip(prods, powers)]
        prods = [pr + b[:n] for pr, b in zip(prods, both)]
        powers = [b[n:] for b in both]
        s *= 2
    return [pr + _dot(pr, pw) for pr, pw in zip(prods, powers)]


def _fill_window(xw_ref, x_ref, xp_ref, xn_ref, tm):
    i = pl.program_id(0)
    xw_ref[0:HALO] = jnp.where(i > 0, xp_ref[...].astype(F32), 0.0)
    xw_ref[HALO:HALO + tm] = x_ref[...].astype(F32)
    xw_ref[HALO + tm:2 * HALO + tm] = jnp.where(i < pl.num_programs(0) - 1, xn_ref[...].astype(F32), 0.0)


def _halo_specs(tm, width, rows):
    per = tm // HALO
    last = rows // HALO - 1
    return [
        pl.BlockSpec((tm, width), lambda i: (i, 0)),
        pl.BlockSpec((HALO, width), lambda i: (jnp.maximum(i * per - 1, 0), 0)),
        pl.BlockSpec((HALO, width), lambda i: (jnp.minimum((i + 1) * per, last), 0)),
    ]


def _const_spec(shape):
    return pl.BlockSpec(shape, lambda *_: (0,) * len(shape))


def _params(sem=("parallel",)):
    return pltpu.CompilerParams(dimension_semantics=sem, vmem_limit_bytes=VMEM_LIMIT)


def _proj_kernel(x_ref, nw_ref, w_ref, *rest, widths):
    out_refs, xn_ref = rest[:len(widths)], rest[len(widths)]
    xn_ref[...] = _rms(x_ref[...], nw_ref[...]).astype(BF16)
    c0 = 0
    for o_ref, wd in zip(out_refs, widths):
        for s in range(0, wd, 512):
            e = min(s + 512, wd)
            o_ref[:, s:e] = jnp.dot(xn_ref[...], w_ref[:, c0 + s:c0 + e],
                                    preferred_element_type=F32).astype(o_ref.dtype)
        c0 += wd


def _norm_proj(h, norm_w, w_bf16, widths, dtypes, geom, name):
    tm, rows = geom.tm, geom.rows
    return pl.pallas_call(
        functools.partial(_proj_kernel, widths=widths),
        out_shape=[jax.ShapeDtypeStruct((rows, wd), dt) for wd, dt in zip(widths, dtypes)],
        grid=(rows // tm,),
        in_specs=[pl.BlockSpec((tm, D_MODEL), lambda i: (i, 0)),
                  _const_spec((1, D_MODEL)),
                  _const_spec(w_bf16.shape)],
        out_specs=[pl.BlockSpec((tm, wd), lambda i: (i, 0)) for wd in widths],
        scratch_shapes=[pltpu.VMEM((tm, D_MODEL), BF16)],
        compiler_params=_params(),
        name=name,
    )(h, norm_w.reshape(1, D_MODEL), w_bf16)


def _a_prep_kernel(x_ref, xp_ref, xn_ref, g_ref, cw_ref, alog_ref, dtb_ref,
                   q_ref, k_ref, v_ref, go_ref, xw_ref, *, geom):
    tm = geom.tm
    i = pl.program_id(0)
    valid = _row_valid(geom, i * tm, tm)
    _fill_window(xw_ref, x_ref, xp_ref, xn_ref, tm)
    half = A_CONV // 2
    for c, o_ref in enumerate((q_ref, k_ref, v_ref)):
        cs = slice(c * A_WIDTH, (c + 1) * A_WIDTH)
        acc = xw_ref[HALO - half:HALO - half + tm, cs] * cw_ref[0:1, cs]
        for j in range(1, A_CONV):
            acc = acc + xw_ref[HALO - half + j:HALO - half + j + tm, cs] * cw_ref[j:j + 1, cs]
        y = jnp.where(valid, acc * _sigmoid(acc), 0.0)
        for hd in range(A_HEADS):
            hs = slice(hd * A_DK, (hd + 1) * A_DK)
            yh = y[:, hs]
            if c < 2:
                yh = yh * lax.rsqrt(jnp.sum(yh * yh, axis=-1, keepdims=True) + 1e-6)
                if c == 0:
                    yh = yh * (A_DK ** -0.5)
            o_ref[:, hs] = yh.astype(o_ref.dtype)
    al = g_ref[...]
    lane = lax.broadcasted_iota(jnp.int32, (1, LANE), 1)
    gval = jnp.where(valid, -jnp.exp(alog_ref[...]) * _softplus(al + dtb_ref[...]), 0.0)
    beta = jnp.where(valid, _sigmoid(al), 0.0)
    cum_f = _chunk_cumsum(gval, False)
    cum_b = _chunk_cumsum(gval, True)
    go_ref[...] = jnp.where(lane < A_HEADS, cum_f,
                            jnp.where(lane < 2 * A_HEADS, cum_b,
                                      jnp.where(lane < 4 * A_HEADS, beta, 0.0)))


def _a_prep(pa, ga, conv_w, a_log, dt_bias, geom):
    tm, rows = geom.tm, geom.rows
    qkv_w = 3 * A_WIDTH
    pad = lambda v: jnp.pad(v.reshape(1, -1).astype(F32), ((0, 0), (0, LANE - v.size)))
    out = jax.ShapeDtypeStruct((rows, A_WIDTH), BF16)
    return pl.pallas_call(
        functools.partial(_a_prep_kernel, geom=geom),
        out_shape=[out, out, out, jax.ShapeDtypeStruct((rows, LANE), F32)],
        grid=(rows // tm,),
        in_specs=_halo_specs(tm, qkv_w, rows) + [
            pl.BlockSpec((tm, LANE), lambda i: (i, 0)),
            _const_spec((A_CONV, qkv_w)), _const_spec((1, LANE)), _const_spec((1, LANE))],
        out_specs=[pl.BlockSpec((tm, A_WIDTH), lambda i: (i, 0))] * 3 + [pl.BlockSpec((tm, LANE), lambda i: (i, 0))],
        scratch_shapes=[pltpu.VMEM((tm + 2 * HALO, qkv_w), F32)],
        compiler_params=_params(),
        name="a_prep",
    )(pa, pa, pa, ga, conv_w, pad(a_log), pad(dt_bias))


def _a_scan_kernel(fi_ref, bi_ref, fr_ref,
                   qf, kf, vf, gf, gtf, qb, kb, vb, gb, gtb, of, ob, sf, sb):
    del fi_ref, bi_ref
    @pl.when(fr_ref[pl.program_id(0)] == 1)
    def _():
        sf[...] = jnp.zeros_like(sf)
        sb[...] = jnp.zeros_like(sb)
    r = lax.broadcasted_iota(jnp.int32, (CHUNK, CHUNK), 0)
    c = lax.broadcasted_iota(jnp.int32, (CHUNK, CHUNK), 1)
    eye = jnp.where(r == c, 1.0, 0.0)
    nc = SEQ_BLK // CHUNK
    dirs = ((qf, kf, vf, gf, gtf, of, sf, False, r >= c, r > c),
            (qb, kb, vb, gb, gtb, ob, sb, True, r <= c, r < c))

    probs = []
    for d, (q_ref, k_ref, v_ref, g_ref, gt_ref, _, _, reverse, tri, strict) in enumerate(dirs):
        edge = 0 if reverse else CHUNK - 1
        for step in range(nc):
            jj = nc - 1 - step if reverse else step
            rows = slice(jj * CHUNK, (jj + 1) * CHUNK)
            for hd in range(A_HEADS):
                hs = slice(hd * A_DK, (hd + 1) * A_DK)
                col = hd + (A_HEADS if reverse else 0)
                q, k, v = (ref[rows, hs].astype(F32) for ref in (q_ref, k_ref, v_ref))
                g_col = g_ref[rows, col:col + 1]
                b_col = g_ref[rows, 2 * A_HEADS + col:2 * A_HEADS + col + 1]
                g_row = gt_ref[jj, col:col + 1, :]
                decay = jnp.where(tri, jnp.exp(jnp.where(tri, g_col - g_row, 0.0)), 0.0)
                kb_ = k * b_col
                e_g = jnp.exp(g_col)
                g_last = g_col[edge:edge + 1]
                probs.append(dict(d=d, step=step, hd=hd, rows=rows, hs=hs, tri=tri, strict=strict, decay=decay,
                                  k=k, kbq=jnp.concatenate([kb_, q], axis=0),
                                  rhs=jnp.concatenate([v * b_col, kb_ * e_g], axis=1),
                                  qg=q * e_g, kg_t=(k * jnp.exp(g_last - g_col)).T, g_end=jnp.exp(g_last)))
    for p in probs:
        kq = _dot_nt(p["kbq"], p["k"])
        p["l"] = jnp.where(p["strict"], kq[:CHUNK] * p["decay"], 0.0)
        p["attn"] = jnp.where(p["tri"], kq[CHUNK:] * p["decay"], 0.0)
    t_invs = _neumann_inverses([-p["l"] for p in probs], eye)
    for p, t_inv in zip(probs, t_invs):
        p["uw"] = _dot(t_inv, p["rhs"])

    states = {(d, hd): dirs[d][6][hd] for d in range(2) for hd in range(A_HEADS)}
    for step in range(nc):
        cur = [p for p in probs if p["step"] == step]
        v_news = [p["uw"][:, :A_DV] - _dot(p["uw"][:, A_DV:], states[p["d"], p["hd"]]) for p in cur]
        for p, v_new in zip(cur, v_news):
            state = states[p["d"], p["hd"]]
            dirs[p["d"]][5][p["rows"], p["hs"]] = _dot(jnp.concatenate([p["qg"], p["attn"]], axis=1),
                                                      jnp.concatenate([state, v_new], axis=0))
        for p, v_new in zip(cur, v_news):
            key = (p["d"], p["hd"])
            states[key] = states[key] * p["g_end"] + _dot(p["kg_t"], v_new)
    for (d, hd), state in states.items():
        dirs[d][6][hd] = state


def _scan_tables(geom):
    fi, bi, fr = [], [], []
    for g in geom.groups:
        nblk = g.tp // SEQ_BLK
        for b in range(g.nb):
            base = (g.start + b * g.tp) // SEQ_BLK
            for i in range(nblk):
                fi.append(base + i)
                bi.append(base + nblk - 1 - i)
                fr.append(1 if i == 0 else 0)
    mk = lambda v: jnp.asarray(np.asarray(v, np.int32))
    return mk(fi), mk(bi), mk(fr)


def _a_scan(qn, kn, vv, gts, geom, tables):
    rows = geom.rows
    nc = SEQ_BLK // CHUNK
    gts_t = gts[:, :4 * A_HEADS].reshape(rows // CHUNK, CHUNK, 4 * A_HEADS).transpose(0, 2, 1)
    wide_f = pl.BlockSpec((SEQ_BLK, A_WIDTH), lambda s, fi, bi, fr: (fi[s], 0))
    wide_b = pl.BlockSpec((SEQ_BLK, A_WIDTH), lambda s, fi, bi, fr: (bi[s], 0))
    gate_f = pl.BlockSpec((SEQ_BLK, LANE), lambda s, fi, bi, fr: (fi[s], 0))
    gate_b = pl.BlockSpec((SEQ_BLK, LANE), lambda s, fi, bi, fr: (bi[s], 0))
    gt_f = pl.BlockSpec((nc, 4 * A_HEADS, CHUNK), lambda s, fi, bi, fr: (fi[s], 0, 0))
    gt_b = pl.BlockSpec((nc, 4 * A_HEADS, CHUNK), lambda s, fi, bi, fr: (bi[s], 0, 0))
    out = jax.ShapeDtypeStruct((rows, A_WIDTH), F32)
    return pl.pallas_call(
        _a_scan_kernel,
        out_shape=[out, out],
        grid_spec=pltpu.PrefetchScalarGridSpec(
            num_scalar_prefetch=3,
            grid=(tables[0].shape[0],),
            in_specs=[wide_f, wide_f, wide_f, gate_f, gt_f, wide_b, wide_b, wide_b, gate_b, gt_b],
            out_specs=[wide_f, wide_b],
            scratch_shapes=[pltpu.VMEM((A_HEADS, A_DK, A_DV), F32)] * 2),
        compiler_params=_params(("arbitrary",)),
        name="a_scan",
    )(*tables, qn, kn, vv, gts, gts_t, qn, kn, vv, gts, gts_t)


def _b_prep_kernel(x_ref, xp_ref, xn_ref, l_ref, lp_ref, ln_ref, mu_ref, w0_ref, w2_ref, a0_ref, a2_ref, g2_ref,
                   kk_ref, ka_ref, rk_ref, ones_ref,
                   r_out, k_out, v_out, kkn_out, eta_out, cf_out, cb_out, bonus_out, gate_out,
                   xw_ref, lw_ref, *, geom):
    tm = geom.tm
    _fill_window(xw_ref, x_ref, xp_ref, xn_ref, tm)
    _fill_window(lw_ref, l_ref, lp_ref, ln_ref, tm)

    def shifted(win_ref, cs, mu_off):
        cur = win_ref[HALO:HALO + tm, cs]
        prev = win_ref[HALO - 1:HALO - 1 + tm, cs]
        nxt = win_ref[HALO + 1:HALO + 1 + tm, cs]
        ms = slice(mu_off + cs.start, mu_off + cs.stop)
        return cur + mu_ref[0:1, ms] * (prev - cur) + mu_ref[1:2, ms] * (nxt - cur)

    w = B_WIDTH
    r = shifted(xw_ref, slice(0, w), 0)
    k = shifted(xw_ref, slice(w, 2 * w), 0)
    v = shifted(xw_ref, slice(2 * w, 3 * w), 0)
    lo = shifted(lw_ref, slice(0, B_LORA_PAD), 3 * w)
    wl = _dot(jnp.tanh(lo), w2_ref[...]) + w0_ref[...]
    log_decay = -float(np.exp(-0.5)) * _sigmoid(wl)
    cf_out[...] = _chunk_cumsum(log_decay[:, :w], False)
    cb_out[...] = _chunk_cumsum(log_decay[:, w:], True)
    eta = _sigmoid(a0_ref[...] + _dot(lo, a2_ref[...]))
    gate_out[...] = _dot(_sigmoid(lo), g2_ref[...])
    kx = k * kk_ref[...]
    kkn_out[...] = (kx * lax.rsqrt(_seg_sum(kx * kx, ones_ref, 2) + 1e-6)).astype(kkn_out.dtype)
    k = k * (1.0 + (eta - 1.0) * ka_ref[...])
    bonus_out[...] = _seg_sum(r * k * rk_ref[...], ones_ref, 2) * v
    r_out[...] = r.astype(r_out.dtype)
    k_out[...] = k.astype(k_out.dtype)
    v_out[...] = v.astype(v_out.dtype)
    eta_out[...] = eta.astype(eta_out.dtype)


def _b_prep(pb, lo, mu, w0, w2, a0, a2, g2, k_k, k_a, r_k, ones, geom):
    tm, rows = geom.tm, geom.rows
    wd = 3 * B_WIDTH
    narrow = jax.ShapeDtypeStruct((rows, B_WIDTH), BF16)
    wide = jax.ShapeDtypeStruct((rows, B_WIDTH), F32)
    row = lambda v: v.reshape(1, -1).astype(F32)
    return pl.pallas_call(
        functools.partial(_b_prep_kernel, geom=geom),
        out_shape=[narrow] * 5 + [wide] * 4,
        grid=(rows // tm,),
        in_specs=_halo_specs(tm, wd, rows) + _halo_specs(tm, B_LORA_PAD, rows) + [
            _const_spec((2, wd + B_LORA_PAD)), _const_spec((1, 2 * B_WIDTH)), _const_spec((B_LORA_PAD, 2 * B_WIDTH)),
            _const_spec((1, B_WIDTH)), _const_spec((B_LORA_PAD, B_WIDTH)), _const_spec((B_LORA_PAD, B_WIDTH)),
            _const_spec((1, B_WIDTH)), _const_spec((1, B_WIDTH)), _const_spec((1, B_WIDTH)),
            _const_spec((B_WIDTH, B_WIDTH))],
        out_specs=[pl.BlockSpec((tm, B_WIDTH), lambda i: (i, 0))] * 9,
        scratch_shapes=[pltpu.VMEM((tm + 2 * HALO, wd), F32), pltpu.VMEM((tm + 2 * HALO, B_LORA_PAD), F32)],
        compiler_params=_params(),
        name="b_prep",
    )(pb, pb, pb, lo, lo, lo, mu, w0, w2, row(a0), a2, g2, row(k_k), row(k_a), row(r_k), ones)


def _b_scan_kernel(fi_ref, bi_ref, fr_ref,
                   rf, kf, vf, kkf, ef, cf, rb, kb, vb, kkb, eb, cb, yf, yb, pf, pb):
    del fi_ref, bi_ref
    @pl.when(fr_ref[pl.program_id(0)] == 1)
    def _():
        pf[...] = jnp.zeros_like(pf)
        pb[...] = jnp.zeros_like(pb)
    n2 = 2 * CHUNK
    r = lax.broadcasted_iota(jnp.int32, (n2, n2), 0)
    c = lax.broadcasted_iota(jnp.int32, (n2, n2), 1)
    same = (r >= CHUNK) == (c >= CHUNK)
    rt, ct = r & (CHUNK - 1), c & (CHUNK - 1)
    eye = jnp.where(r == c, 1.0, 0.0)
    lane_lo = lax.broadcasted_iota(jnp.int32, (1, n2), 1) < B_HD
    row_pos = lax.broadcasted_iota(jnp.int32, (CHUNK, 1), 0)
    nc = SEQ_BLK // CHUNK
    pair_w = 2 * B_HD
    n_pairs = B_HEADS // 2
    dirs = (((rf, kf, vf, kkf, ef, cf), yf, pf, False, same & (rt > ct), same & (rt >= ct)),
            ((rb, kb, vb, kkb, eb, cb), yb, pb, True, same & (rt < ct), same & (rt <= ct)))

    def expand(x):
        return jnp.concatenate([jnp.where(lane_lo, x, 0.0), jnp.where(lane_lo, 0.0, x)], axis=0)

    probs = []
    for d, (refs, _, _, reverse, strict, incl) in enumerate(dirs):
        r_ref, k_ref, v_ref, kk_ref, eta_ref, c_ref = refs
        for step in range(nc):
            jj = nc - 1 - step if reverse else step
            rows = slice(jj * CHUNK, (jj + 1) * CHUNK)
            for pr in range(n_pairs):
                ls = slice(pr * pair_w, (pr + 1) * pair_w)
                r, k, v, kk, eta = (ref[rows, ls].astype(F32) for ref in (r_ref, k_ref, v_ref, kk_ref, eta_ref))
                c_in = c_ref[rows, ls]
                if reverse:
                    c_ex = jnp.where(row_pos < CHUNK - 1, pltpu.roll(c_in, CHUNK - 1, axis=0), 0.0)
                    c_tot = c_in[0:1]
                else:
                    c_ex = jnp.where(row_pos >= 1, pltpu.roll(c_in, 1, axis=0), 0.0)
                    c_tot = c_in[CHUNK - 1:CHUNK]
                b = kk * eta
                inv_w = jnp.exp(-c_in)
                rest_w = jnp.exp(c_tot - c_in)
                a_e = expand(-kk * jnp.exp(c_ex))
                r_e = expand(r * jnp.exp(c_in))
                probs.append(dict(
                    d=d, step=step, pr=pr, rows=rows, ls=ls, strict=strict, incl=incl, v=v,
                    a_e=a_e, r_e=r_e, v_e=expand(v),
                    lhs=jnp.concatenate([a_e, r_e], axis=0),
                    rhs=jnp.concatenate([expand(b * inv_w), expand(k * inv_w)], axis=0),
                    kd_t=jnp.concatenate([b * rest_w, k * rest_w], axis=0).T,
                    w_col=jnp.sum(jnp.where(eye > 0.0, jnp.exp(c_tot), 0.0), axis=1, keepdims=True)))
    for p in probs:
        m1 = _dot_nt(p["lhs"], p["rhs"])
        p["n_ab"] = jnp.where(p["strict"], m1[:n2, :n2], 0.0)
        p["a_ak"] = jnp.where(p["strict"], m1[:n2, n2:], 0.0)
        p["a_r"] = jnp.concatenate([jnp.where(p["incl"], m1[n2:, :n2], 0.0),
                                    jnp.where(p["incl"], m1[n2:, n2:], 0.0)], axis=1)
    t_invs = _neumann_inverses([p["n_ab"] for p in probs], eye)
    for p in probs:
        p["akv"] = _dot(p["a_ak"], p["v_e"])
    for p, t_inv in zip(probs, t_invs):
        p["x"] = _dot(t_inv, jnp.concatenate([p["akv"], p["a_e"]], axis=1))

    states = {(d, pr): dirs[d][2][pr] for d in range(2) for pr in range(n_pairs)}
    for step in range(nc):
        cur = [p for p in probs if p["step"] == step]
        u_es = [p["x"][:, :pair_w] + _dot(p["x"][:, pair_w:], states[p["d"], p["pr"]]) for p in cur]
        for p, u_e in zip(cur, u_es):
            y_e = _dot(jnp.concatenate([p["r_e"], p["a_r"]], axis=1),
                       jnp.concatenate([states[p["d"], p["pr"]], u_e, p["v_e"]], axis=0))
            dirs[p["d"]][1][p["rows"], p["ls"]] = y_e[:CHUNK] + y_e[CHUNK:]
        for p, u_e in zip(cur, u_es):
            key = (p["d"], p["pr"])
            upd = _dot(p["kd_t"], jnp.concatenate([u_e[:CHUNK] + u_e[CHUNK:], p["v"]], axis=0))
            states[key] = states[key] * p["w_col"] + jnp.where(same, upd, 0.0)
    for (d, pr), state in states.items():
        dirs[d][2][pr] = state


def _b_scan(r, k, v, kkn, eta, cf, cb, geom, tables):
    wide_f = pl.BlockSpec((SEQ_BLK, B_WIDTH), lambda s, fi, bi, fr: (fi[s], 0))
    wide_b = pl.BlockSpec((SEQ_BLK, B_WIDTH), lambda s, fi, bi, fr: (bi[s], 0))
    out = jax.ShapeDtypeStruct((geom.rows, B_WIDTH), F32)
    return pl.pallas_call(
        _b_scan_kernel,
        out_shape=[out, out],
        grid_spec=pltpu.PrefetchScalarGridSpec(
            num_scalar_prefetch=3,
            grid=(tables[0].shape[0],),
            in_specs=[wide_f] * 6 + [wide_b] * 6,
            out_specs=[wide_f, wide_b],
            scratch_shapes=[pltpu.VMEM((B_HEADS // 2, 2 * B_HD, 2 * B_HD), F32)] * 2),
        compiler_params=_params(("arbitrary",)),
        name="b_scan",
    )(*tables, r, k, v, kkn, eta, cf, r, k, v, kkn, eta, cb)


def _finish_residual(h, mix, post_w, valid):
    return jnp.where(valid, h + _rms(mix, post_w), 0.0)


def _mixout_even_kernel(h_ref, oaf_ref, oab_ref, z_ref, ybf_ref, ybb_ref, bonus_ref, gate_ref,
                        w_ref, an_ref, lnw_ref, lnb_ref, post_ref, ones_ref, o_ref, cat_ref, *, geom):
    tm = geom.tm
    valid = _row_valid(geom, pl.program_id(0) * tm, tm)
    for hd in range(A_HEADS):
        hs = slice(hd * A_DV, (hd + 1) * A_DV)
        o = oaf_ref[:, hs] + oab_ref[:, hs]
        z = z_ref[:, hs].astype(F32)
        cat_ref[:, hs] = (_rms(o, an_ref[...]) * (z * _sigmoid(z))).astype(BF16)
    y = ybf_ref[...] + ybb_ref[...]
    yc = y - _seg_sum(y, ones_ref, 3) * (1.0 / B_HD)
    var = _seg_sum(yc * yc, ones_ref, 2) * (1.0 / B_HD)
    yn = yc * lax.rsqrt(var + B_GN_EPS) * lnw_ref[...] + lnb_ref[...]
    cat_ref[:, A_WIDTH:] = ((yn + bonus_ref[...]) * gate_ref[...]).astype(BF16)
    mix = jnp.dot(cat_ref[...], w_ref[...], preferred_element_type=F32)
    o_ref[...] = _finish_residual(h_ref[...], mix, post_ref[...], valid)


def _mixout_even(h, oaf, oab, pa, ybf, ybb, bonus, gate, w_out, a_norm, ln_w, ln_b, post_w, ones, geom):
    tm, rows = geom.tm, geom.rows
    row = lambda v: v.reshape(1, -1).astype(F32)
    a_blk = pl.BlockSpec((tm, A_WIDTH), lambda i: (i, 0))
    b_blk = pl.BlockSpec((tm, B_WIDTH), lambda i: (i, 0))
    return pl.pallas_call(
        functools.partial(_mixout_even_kernel, geom=geom),
        out_shape=jax.ShapeDtypeStruct((rows, D_MODEL), F32),
        grid=(rows // tm,),
        in_specs=[pl.BlockSpec((tm, D_MODEL), lambda i: (i, 0)), a_blk, a_blk,
                  pl.BlockSpec((tm, A_WIDTH), lambda i: (i, 3)),
                  b_blk, b_blk, b_blk, b_blk,
                  _const_spec((A_WIDTH + B_WIDTH, D_MODEL)), _const_spec((1, A_DV)),
                  _const_spec((1, B_WIDTH)), _const_spec((1, B_WIDTH)), _const_spec((1, D_MODEL)),
                  _const_spec((B_WIDTH, B_WIDTH))],
        out_specs=pl.BlockSpec((tm, D_MODEL), lambda i: (i, 0)),
        scratch_shapes=[pltpu.VMEM((tm, A_WIDTH + B_WIDTH), BF16)],
        compiler_params=_params(),
        name="mixout_even",
    )(h, oaf, oab, pa, ybf, ybb, bonus, gate, w_out, row(a_norm), row(ln_w), row(ln_b), row(post_w), ones)


def _mixout_odd_kernel(h_ref, o_ref_in, w_ref, post_ref, o_ref, *, geom):
    tm = geom.tm
    valid = _row_valid(geom, pl.program_id(0) * tm, tm)
    mix = jnp.dot(o_ref_in[...].astype(BF16), w_ref[...], preferred_element_type=F32)
    o_ref[...] = _finish_residual(h_ref[...], mix, post_ref[...], valid)


def _mixout_odd(h, o, w_out, post_w, geom):
    tm, rows = geom.tm, geom.rows
    wd = C_HEADS * C_HD
    return pl.pallas_call(
        functools.partial(_mixout_odd_kernel, geom=geom),
        out_shape=jax.ShapeDtypeStruct((rows, D_MODEL), F32),
        grid=(rows // tm,),
        in_specs=[pl.BlockSpec((tm, D_MODEL), lambda i: (i, 0)), pl.BlockSpec((tm, wd), lambda i: (i, 0)),
                  _const_spec((wd, D_MODEL)), _const_spec((1, D_MODEL))],
        out_specs=pl.BlockSpec((tm, D_MODEL), lambda i: (i, 0)),
        compiler_params=_params(),
        name="mixout_odd",
    )(h, o, w_out, post_w.reshape(1, D_MODEL))


def _qkv_kernel(x_ref, nw_ref, w_ref, cos_ref, sin_ref, qn_ref, kn_ref, q_out, k_out, v_out, xn_ref):
    xn_ref[...] = _rms(x_ref[...], nw_ref[...]).astype(BF16)
    cos, sin = cos_ref[...], sin_ref[...]
    lane = lax.broadcasted_iota(jnp.int32, (1, C_HD), 1)
    first_half = (lane & (C_HD // 2 - 1)) < C_HD // 4

    def rope(xh, w, scale):
        xh = _rms(xh, w)
        partner = jnp.where(first_half, pltpu.roll(xh, C_HD - C_HD // 4, axis=1), pltpu.roll(xh, C_HD // 4, axis=1))
        return (xh * cos + partner * sin) * scale

    nq = C_HEADS * C_HD
    nk = C_KV_HEADS * C_HD
    per = 512 // C_HD
    for c in range(nq // 512):
        y = jnp.dot(xn_ref[...], w_ref[:, c * 512:(c + 1) * 512], preferred_element_type=F32)
        for hd in range(per):
            hs = slice((c * per + hd) * C_HD, (c * per + hd + 1) * C_HD)
            q_out[:, hs] = rope(y[:, hd * C_HD:(hd + 1) * C_HD], qn_ref[...], C_HD ** -0.5 * LOG2E).astype(BF16)
    y = jnp.dot(xn_ref[...], w_ref[:, nq:], preferred_element_type=F32)
    for hd in range(C_KV_HEADS):
        hs = slice(hd * C_HD, (hd + 1) * C_HD)
        k_out[:, hs] = rope(y[:, hs], kn_ref[...], 1.0).astype(BF16)
        v_out[:, 2 * hd * C_HD:(2 * hd + 1) * C_HD] = y[:, nk + hd * C_HD:nk + (hd + 1) * C_HD].astype(BF16)
        v_out[:, (2 * hd + 1) * C_HD:(2 * hd + 2) * C_HD] = jnp.ones((x_ref.shape[0], C_HD), BF16)


def _qkv_proj(h, norm_w, w_bf16, cos_t, sin_t, q_norm, k_norm, geom):
    tm, rows = geom.tm, geom.rows
    nq, nk = C_HEADS * C_HD, C_KV_HEADS * C_HD
    return pl.pallas_call(
        _qkv_kernel,
        out_shape=[jax.ShapeDtypeStruct((rows, nq), BF16), jax.ShapeDtypeStruct((rows, nk), BF16),
                   jax.ShapeDtypeStruct((rows, 2 * nk), BF16)],
        grid=(rows // tm,),
        in_specs=[pl.BlockSpec((tm, D_MODEL), lambda i: (i, 0)), _const_spec((1, D_MODEL)),
                  _const_spec((D_MODEL, nq + 2 * nk)),
                  pl.BlockSpec((tm, C_HD), lambda i: (i, 0)), pl.BlockSpec((tm, C_HD), lambda i: (i, 0)),
                  _const_spec((1, C_HD)), _const_spec((1, C_HD))],
        out_specs=[pl.BlockSpec((tm, nq), lambda i: (i, 0)), pl.BlockSpec((tm, nk), lambda i: (i, 0)),
                   pl.BlockSpec((tm, 2 * nk), lambda i: (i, 0))],
        scratch_shapes=[pltpu.VMEM((tm, D_MODEL), BF16)],
        compiler_params=_params(),
        name="qkv_proj",
    )(h, norm_w.reshape(1, D_MODEL), w_bf16, cos_t, sin_t, q_norm.reshape(1, C_HD), k_norm.reshape(1, C_HD))


def _rope_tables(geom):
    nf = C_HD // 4
    inv = ROPE_THETA ** (-jnp.arange(nf, dtype=F32) / nf)
    cos_parts, sin_parts = [], []
    for g in geom.groups:
        t = jnp.arange(g.tp, dtype=jnp.int32) - (g.fv + N_META)
        is_meta = t < 0
        row = jnp.where(is_meta, -1, t // GRID_W).astype(F32)
        col = jnp.where(is_meta, t + N_META, t % GRID_W).astype(F32)
        ang_r, ang_c = row[:, None] * inv, col[:, None] * inv
        cos = jnp.concatenate([jnp.cos(ang_r)] * 2 + [jnp.cos(ang_c)] * 2, axis=1)
        sin = jnp.concatenate([-jnp.sin(ang_r), jnp.sin(ang_r), -jnp.sin(ang_c), jnp.sin(ang_c)], axis=1)
        cos_parts.append(jnp.tile(cos, (g.nb, 1)))
        sin_parts.append(jnp.tile(sin, (g.nb, 1)))
    return jnp.concatenate(cos_parts, axis=0), jnp.concatenate(sin_parts, axis=0)


def _attn_kernel(*refs, nsub, bounds, fv):
    (q_ref, k_ref, v_ref), (o_ref, s_ref, rm_ref, m_ref, acc_ref) = refs[:3], refs[-5:]
    tq = SEQ_BLK
    nc = len(bounds)
    assert nc % 2 == 0

    def q_rows(qi):
        return pl.ds(pl.multiple_of(qi * tq, tq), tq)

    def produce(qi, c, slot):
        start, size = bounds[c]
        q = jnp.concatenate([q_ref[q_rows(qi), g * C_HD:(g + 1) * C_HD] for g in range(C_GROUP)], axis=0)
        s = lax.dot_general(q, k_ref[start:start + size, :], (((1,), (1,)), ((), ())),
                            preferred_element_type=F32)
        if start < fv:
            key = start + lax.broadcasted_iota(jnp.int32, (1, size), 1)
            s = jnp.where(key >= fv, s, -1e30)
        s_ref[slot, :, :size] = s
        rm_ref[slot] = jnp.max(s, axis=-1, keepdims=True)

    def consume(c, slot):
        start, size = bounds[c]
        m_old = m_ref[...]
        m_new = jnp.maximum(m_old, rm_ref[slot])
        p = jnp.exp2(s_ref[slot, :, :size] - m_new).astype(BF16)
        acc_ref[...] = jnp.exp2(m_old - m_new) * acc_ref[...] + jnp.dot(p, v_ref[start:start + size, :],
                                                                        preferred_element_type=F32)
        m_ref[...] = m_new

    produce(0, 0, 0)

    def body(qi, carry):
        m_ref[...] = jnp.full_like(m_ref, -1e30)
        acc_ref[...] = jnp.zeros_like(acc_ref)
        for c in range(nc):
            if c + 1 < nc:
                produce(qi, c + 1, (c + 1) % 2)
            else:
                produce(jnp.minimum(qi + 1, nsub - 1), 0, 0)
            consume(c, c % 2)
        acc = acc_ref[...]
        out = acc[:, :C_HD] / acc[:, C_HD:]
        for g in range(C_GROUP):
            o_ref[q_rows(qi), g * C_HD:(g + 1) * C_HD] = out[g * tq:(g + 1) * tq]
        return carry

    lax.fori_loop(0, nsub, body, 0)


def _key_chunks(tp, fv):
    lo = fv // LANE
    blocks = tp // LANE - lo
    per = ATTN_MAX_TK // LANE
    nc = -(-blocks // per)
    nc += nc % 2
    pairs, single = divmod(blocks, 2)
    sizes = [2 * (pairs // nc + (1 if i < pairs % nc else 0)) for i in range(nc)]
    sizes[-1] += single
    if min(sizes) == 0:
        sizes = [blocks // nc + (1 if i < blocks % nc else 0) for i in range(nc)]
    assert min(sizes) > 0 and max(sizes) <= per
    starts = np.cumsum([lo] + sizes[:-1])
    return tuple((int(s) * LANE, int(z) * LANE) for s, z in zip(starts, sizes))


def _attention(qn, kn, vn, geom):
    out = None
    for g in geom.groups:
        bounds = _key_chunks(g.tp, g.fv)
        nblk = g.tp // SEQ_BLK
        nsub = max(d for d in range(1, ATTN_MAX_SUB + 1) if nblk % d == 0)
        tq, gw, rows = nsub * SEQ_BLK, C_GROUP * C_HD, C_GROUP * SEQ_BLK
        al = lambda x: pl.multiple_of(x, LANE)
        q_spec = pl.BlockSpec((pl.Element(tq), pl.Element(gw)),
                              lambda b, h, i, g=g, tq=tq: (al(g.start + b * g.tp + i * tq), al(h * gw)))
        args = [qn, kn, vn]
        in_specs = [q_spec,
                    pl.BlockSpec((pl.Element(g.tp), pl.Element(C_HD)),
                                 lambda b, h, i, g=g: (al(g.start + b * g.tp), al(h * C_HD))),
                    pl.BlockSpec((pl.Element(g.tp), pl.Element(2 * C_HD)),
                                 lambda b, h, i, g=g: (al(g.start + b * g.tp), al(h * 2 * C_HD)))]
        if out is not None:
            args.append(out)
            in_specs.append(pl.BlockSpec(memory_space=pl.ANY))
        out = pl.pallas_call(
            functools.partial(_attn_kernel, nsub=nsub, bounds=bounds, fv=g.fv),
            out_shape=jax.ShapeDtypeStruct((geom.rows, C_HEADS * C_HD), F32),
            grid=(g.nb, C_KV_HEADS, g.tp // tq),
            in_specs=in_specs,
            out_specs=q_spec,
            scratch_shapes=[pltpu.VMEM((2, rows, max(z for _, z in bounds)), F32), pltpu.VMEM((2, rows, 1), F32),
                            pltpu.VMEM((rows, 1), F32), pltpu.VMEM((rows, 2 * C_HD), F32)],
            input_output_aliases={3: 0} if len(args) == 4 else {},
            compiler_params=_params(("parallel", "parallel", "parallel")),
            name="attention",
        )(*args)
    return out


def _ffn_kernel(x_ref, xp_ref, xn_ref, pre_ref, wu_ref, wg_ref, cw_ref, cb_ref, wo_ref, post_ref,
                o_ref, xs_ref, g0_ref, g1_ref, act_ref, acc_ref, *, tm):
    i = pl.program_id(0)
    xs_ref[0:HALO] = jnp.where(i > 0, _rms(xp_ref[...], pre_ref[...]), 0.0).astype(BF16)
    xs_ref[HALO:HALO + tm] = _rms(x_ref[...], pre_ref[...]).astype(BF16)
    xs_ref[HALO + tm:2 * HALO + tm] = jnp.where(i < pl.num_programs(0) - 1,
                                                _rms(xn_ref[...], pre_ref[...]), 0.0).astype(BF16)
    half = FFN_CONV // 2
    n_chunks = D_FF // FF_CHUNK
    for c in range(n_chunks):
        cs = slice(c * FF_CHUNK, (c + 1) * FF_CHUNK)
        g_ref = (g0_ref, g1_ref)[c % 2]
        u = jnp.dot(xs_ref[HALO:HALO + tm], wu_ref[:, cs], preferred_element_type=F32)
        g_ref[...] = jnp.dot(xs_ref[...], wg_ref[:, cs], preferred_element_type=F32)
        gate = cb_ref[0:1, cs]
        for j in range(FFN_CONV):
            gate = gate + g_ref[HALO - half + j:HALO - half + j + tm] * cw_ref[j:j + 1, cs]
        th = jnp.tanh(gate * (GELU_C0 + GELU_C0 * 0.044715 * (gate * gate)))
        hu = 0.5 * gate * u
        act_ref[:, cs] = (hu + hu * th).astype(BF16)
        if c % FF_GROUP == FF_GROUP - 1 or c == n_chunks - 1:
            ks = slice((c // FF_GROUP) * FF_GROUP * FF_CHUNK, (c + 1) * FF_CHUNK)
            part = jnp.dot(act_ref[:, ks], wo_ref[ks, :], preferred_element_type=F32)
            if c < FF_GROUP:
                acc_ref[...] = part
            else:
                acc_ref[...] += part
    o_ref[...] = x_ref[...] + _rms(acc_ref[...], post_ref[...])


def _conv_ffn(h, pre_w, w_in, conv_w, conv_b, w_out, post_w, geom):
    tm, rows = geom.tm, geom.rows
    assert D_FF % FF_CHUNK == 0
    wu = w_in[:, :D_FF].astype(BF16)
    wg = w_in[:, D_FF:].astype(BF16)
    return pl.pallas_call(
        functools.partial(_ffn_kernel, tm=tm),
        out_shape=jax.ShapeDtypeStruct((rows, D_MODEL), F32),
        grid=(rows // tm,),
        in_specs=_halo_specs(tm, D_MODEL, rows) + [
            _const_spec((1, D_MODEL)), _const_spec((D_MODEL, D_FF)), _const_spec((D_MODEL, D_FF)),
            _const_spec((FFN_CONV, D_FF)), _const_spec((1, D_FF)), _const_spec((D_FF, D_MODEL)),
            _const_spec((1, D_MODEL))],
        out_specs=pl.BlockSpec((tm, D_MODEL), lambda i: (i, 0)),
        scratch_shapes=[pltpu.VMEM((tm + 2 * HALO, D_MODEL), BF16), pltpu.VMEM((tm + 2 * HALO, FF_CHUNK), F32),
                        pltpu.VMEM((tm + 2 * HALO, FF_CHUNK), F32), pltpu.VMEM((tm, D_FF), BF16),
                        pltpu.VMEM((tm, D_MODEL), F32)],
        compiler_params=_params(),
        name="conv_ffn",
    )(h, h, h, pre_w.reshape(1, D_MODEL), wu, wg, conv_w, conv_b.reshape(1, D_FF),
      w_out.astype(BF16), post_w.reshape(1, D_MODEL))


def _lead_kernel(meta_ref, *rest):
    o_ref = rest[-1]
    fv = o_ref.shape[0] - N_META
    o_ref[0:fv] = jnp.zeros((fv, D_MODEL), o_ref.dtype)
    o_ref[fv:] = meta_ref[...].astype(o_ref.dtype)


def _rows_in_kernel(x_ref, _, o_ref):
    o_ref[...] = x_ref[0]


def _rows_out_kernel(h_ref, o_ref):
    o_ref[0] = h_ref[...]


def _copy_block(n):
    return max(b for b in (2048, 1024, 512, 256, 128) if n % b == 0)


def _pack_rows(xs, meta, geom):
    al = lambda v: pl.multiple_of(v, LANE)
    shape = jax.ShapeDtypeStruct((geom.rows, D_MODEL), xs[0].dtype)
    h = None
    for x, g in zip(xs, geom.groups):
        pad = g.tp - g.n
        args = [meta] if h is None else [meta, h]
        h = pl.pallas_call(
            _lead_kernel, out_shape=shape, grid=(g.nb,),
            in_specs=[_const_spec((N_META, D_MODEL))] + [pl.BlockSpec(memory_space=pl.ANY)] * (len(args) - 1),
            out_specs=pl.BlockSpec((pl.Element(pad), pl.Element(D_MODEL)),
                                   lambda b, g=g: (al(g.start + b * g.tp), 0)),
            input_output_aliases={1: 0} if len(args) == 2 else {},
            compiler_params=_params(), name="pack_lead",
        )(*args)
        blk = _copy_block(g.n)
        h = pl.pallas_call(
            _rows_in_kernel, out_shape=shape, grid=(g.nb, g.n // blk),
            in_specs=[pl.BlockSpec((1, blk, D_MODEL), lambda b, j: (b, j, 0)), pl.BlockSpec(memory_space=pl.ANY)],
            out_specs=pl.BlockSpec((pl.Element(blk), pl.Element(D_MODEL)),
                                   lambda b, j, g=g, pad=pad, blk=blk: (al(g.start + b * g.tp + pad + j * blk), 0)),
            input_output_aliases={1: 0},
            compiler_params=_params(("parallel", "parallel")), name="pack_rows",
        )(x, h)
    return h


def _unpack_rows(h, geom):
    al = lambda v: pl.multiple_of(v, LANE)
    outs = []
    for g in geom.groups:
        pad = g.tp - g.n
        blk = _copy_block(g.n)
        outs.append(pl.pallas_call(
            _rows_out_kernel, out_shape=jax.ShapeDtypeStruct((g.nb, g.n, D_MODEL), h.dtype),
            grid=(g.nb, g.n // blk),
            in_specs=[pl.BlockSpec((pl.Element(blk), pl.Element(D_MODEL)),
                                   lambda b, j, g=g, pad=pad, blk=blk: (al(g.start + b * g.tp + pad + j * blk), 0))],
            out_specs=pl.BlockSpec((1, blk, D_MODEL), lambda b, j: (b, j, 0)),
            compiler_params=_params(("parallel", "parallel")), name="unpack_rows",
        )(h))
    return tuple(outs)


def _even_layer(h, j, geom, tables, ones, p):
    aw, bw = A_WIDTH, B_WIDTH
    a_cols = 4 * aw + 4 * A_HEADS
    w_in = p["e_w_in"][j]
    lora = B_W_LORA + B_A_LORA + B_G_LORA
    zeros = lambda n: jnp.zeros((D_MODEL, n), F32)
    w_cat = jnp.concatenate([
        w_in[:, a_cols:a_cols + 3 * bw + lora], zeros(B_LORA_PAD - lora),
        w_in[:, 4 * aw:a_cols], zeros(LANE - 4 * A_HEADS),
        w_in[:, :4 * aw]], axis=1).astype(BF16)
    pb, lo, ga, pa = _norm_proj(h, p["mix_pre_norm"][2 * j], w_cat, (3 * bw, B_LORA_PAD, LANE, 4 * aw),
                                (BF16, F32, F32, BF16), geom, "proj_even")

    qn, kn, vv, gts = _a_prep(pa, ga, p["a_conv_w"][j], p["a_log"][j], p["a_dt_bias"][j], geom)
    oaf, oab = _a_scan(qn, kn, vv, gts, geom, tables)

    mu = jnp.pad(p["b_shift"][j], ((0, 0), (0, B_LORA_PAD - lora)))
    lora_rows = lambda m, off: jnp.pad(m, ((off, B_LORA_PAD - off - m.shape[0]), (0, 0)))
    w2 = lora_rows(jnp.concatenate([p["b_w2"][j, 0], p["b_w2"][j, 1]], axis=1), 0)
    a2 = lora_rows(p["b_a2"][j], B_W_LORA)
    g2 = lora_rows(p["b_g2"][j], B_W_LORA + B_A_LORA)
    r, k, v, kkn, eta, cf, cb, bonus, gate = _b_prep(
        pb, lo, mu, p["b_w0"][j].reshape(1, 2 * bw), w2, p["b_a0"][j], a2, g2,
        p["b_k_k"][j], p["b_k_a"][j], p["b_r_k"][j], ones, geom)
    ybf, ybb = _b_scan(r, k, v, kkn, eta, cf, cb, geom, tables)

    return _mixout_even(h, oaf, oab, pa, ybf, ybb, bonus, gate, p["e_w_out"][j].astype(BF16),
                        p["a_out_norm"][j], p["b_ln_w"][j], p["b_ln_b"][j], p["mix_post_norm"][2 * j], ones, geom)


def _odd_layer(h, j, geom, rope, p):
    qn, kn, vn = _qkv_proj(h, p["mix_pre_norm"][2 * j + 1], p["o_w_qkv"][j].astype(BF16), rope[0], rope[1],
                           p["o_q_norm"][j], p["o_k_norm"][j], geom)
    o = _attention(qn, kn, vn, geom)
    return _mixout_odd(h, o, p["o_w_out"][j].astype(BF16), p["mix_post_norm"][2 * j + 1], geom)


def kernel(x_prompt, x_sample, meta, mix_pre_norm, mix_post_norm, ffn_pre_norm, ffn_post_norm, e_w_in, a_conv_w, a_log, a_dt_bias, a_out_norm, b_shift, b_w0, b_w2, b_a0, b_a2, b_g2, b_k_k, b_k_a, b_r_k, b_ln_w, b_ln_b, e_w_out, o_w_qkv, o_q_norm, o_k_norm, o_w_out, f_w_in, f_conv_w, f_conv_b, f_w_out):
    p = dict(mix_pre_norm=mix_pre_norm, mix_post_norm=mix_post_norm, e_w_in=e_w_in, a_conv_w=a_conv_w,
             a_log=a_log, a_dt_bias=a_dt_bias, a_out_norm=a_out_norm, b_shift=b_shift, b_w0=b_w0, b_w2=b_w2,
             b_a0=b_a0, b_a2=b_a2, b_g2=b_g2, b_k_k=b_k_k, b_k_a=b_k_a, b_r_k=b_r_k, b_ln_w=b_ln_w,
             b_ln_b=b_ln_b, e_w_out=e_w_out, o_w_qkv=o_w_qkv, o_q_norm=o_q_norm, o_k_norm=o_k_norm,
             o_w_out=o_w_out)
    xs = (x_prompt, x_sample)
    geom = _choose_geom([(x.shape[0], x.shape[1]) for x in xs])
    tables = _scan_tables(geom)
    rope = _rope_tables(geom)
    seg = np.arange(B_WIDTH) // B_HD
    ones = jnp.asarray(seg[:, None] == seg[None, :], BF16)

    h = _pack_rows(xs, meta, geom)
    for i in range(DEPTH):
        h = _even_layer(h, i // 2, geom, tables, ones, p) if i % 2 == 0 else _odd_layer(h, i // 2, geom, rope, p)
        h = _conv_ffn(h, ffn_pre_norm[i], f_w_in[i], f_conv_w[i], f_conv_b[i], f_w_out[i], ffn_post_norm[i], geom)
    return _unpack_rows(h, geom)
```

```python
import functools
import itertools
from typing import NamedTuple

import numpy as np
import jax
import jax.numpy as jnp
from jax import lax
from jax.experimental import pallas as pl
from jax.experimental.pallas import tpu as pltpu

F32 = jnp.float32
BF16 = jnp.bfloat16

D_MODEL = 1024
DEPTH = 2
N_META = 16
GRID_W = 64
NORM_EPS = 1e-6

A_HEADS = 4
A_DK = 128
A_DV = 128
A_CONV = 5
A_WIDTH = A_HEADS * A_DV

B_HEADS = 8
B_HD = 64
B_WIDTH = B_HEADS * B_HD
B_W_LORA = 32
B_A_LORA = 32
B_G_LORA = 96
B_GN_EPS = 64e-5
B_LORA_PAD = 256

C_HEADS = 8
C_KV_HEADS = 2
C_GROUP = C_HEADS // C_KV_HEADS
C_HD = 128
ROPE_THETA = 10000.0

D_FF = 2816
FFN_CONV = 3

LANE = 128
CHUNK = 64
SEQ_BLK = 128
HALO = 16
FF_CHUNK = 256
FF_GROUP = 4
ATTN_MAX_TK = 1792
ATTN_MAX_SUB = 17
LOG2E = 1.4426950408889634
GELU_C0 = 0.7978845608028654
VMEM_LIMIT = 56 * 1024 * 1024


class _Group(NamedTuple):
    start: int
    nb: int
    n: int
    tp: int
    fv: int


class _Geom(NamedTuple):
    groups: tuple
    rows: int
    tm: int


def _choose_geom(shapes):
    tm = 512 if min(n for _, n in shapes) >= 512 else 128
    best = None
    for pads in itertools.product((128, 256, 384, 512), repeat=len(shapes)):
        start, ok, groups = 0, True, []
        for (nb, n), p in zip(shapes, pads):
            ok = ok and start % tm == 0
            groups.append(_Group(start, nb, n, n + p, p - N_META))
            start += nb * (n + p)
        if ok and start % tm == 0 and (best is None or start < best.rows):
            best = _Geom(tuple(groups), start, tm)
    assert best is not None
    return best


def _row_valid(geom, base, tm):
    start = jnp.int32(geom.groups[0].start)
    tp = jnp.int32(geom.groups[0].tp)
    fv = jnp.int32(geom.groups[0].fv)
    for g in geom.groups[1:]:
        assert tm <= g.tp
        inside = base >= g.start
        start = jnp.where(inside, g.start, start)
        tp = jnp.where(inside, g.tp, tp)
        fv = jnp.where(inside, g.fv, fv)
    pos = lax.rem(base - start, tp) + lax.broadcasted_iota(jnp.int32, (tm, 1), 0)
    pos = jnp.where(pos >= tp, pos - tp, pos)
    return pos >= fv


def _rms(x, w, eps=NORM_EPS):
    return x * lax.rsqrt(jnp.mean(x * x, axis=-1, keepdims=True) + eps) * w


def _sigmoid(x):
    return 1.0 / (1.0 + jnp.exp(-x))


def _softplus(x):
    return jnp.maximum(x, 0.0) + jnp.log1p(jnp.exp(-jnp.abs(x)))


def _dot(a, b):
    return jnp.dot(a.astype(BF16), b.astype(BF16), preferred_element_type=F32)


def _dot_nt(a, b):
    return lax.dot_general(a.astype(BF16), b.astype(BF16), (((1,), (1,)), ((), ())),
                           preferred_element_type=F32)


def _seg_sum(x, ones_ref, terms):
    acc = None
    rem = x
    for t in range(terms):
        piece = rem.astype(BF16)
        part = jnp.dot(piece, ones_ref[...], preferred_element_type=F32)
        acc = part if acc is None else acc + part
        if t + 1 < terms:
            rem = rem - piece.astype(F32)
    return acc


def _chunk_cumsum(x, reverse):
    tm = x.shape[0]
    pos = lax.broadcasted_iota(jnp.int32, (tm, 1), 0) & (CHUNK - 1)
    s = 1
    while s < CHUNK:
        if reverse:
            x = x + jnp.where(pos < CHUNK - s, pltpu.roll(x, tm - s, axis=0), 0.0)
        else:
            x = x + jnp.where(pos >= s, pltpu.roll(x, s, axis=0), 0.0)
        s *= 2
    return x


def _neumann_inverses(n_mats, eye):
    n = eye.shape[0]
    prods = [eye + m for m in n_mats]
    powers = [_dot(m, m) for m in n_mats]
    s = 4
    while s < CHUNK:
        both = [_dot(jnp.concatenate([pr, pw], axis=0), pw) for pr, pw in zip(prods, powers)]
        prods = [pr + b[:n] for pr, b in zip(prods, both)]
        powers = [b[n:] for b in both]
        s *= 2
    return [pr + _dot(pr, pw) for pr, pw in zip(prods, powers)]


def _halo_specs(tm, width, rows):
    per = tm // HALO
    last = rows // HALO - 1
    return [
        pl.BlockSpec((tm, width), lambda i: (i, 0)),
        pl.BlockSpec((HALO, width), lambda i: (jnp.maximum(i * per - 1, 0), 0)),
        pl.BlockSpec((HALO, width), lambda i: (jnp.minimum((i + 1) * per, last), 0)),
    ]


def _const_spec(shape):
    return pl.BlockSpec(shape, lambda *_: (0,) * len(shape))


def _params(sem=("parallel",)):
    return pltpu.CompilerParams(dimension_semantics=sem, vmem_limit_bytes=VMEM_LIMIT)


def _a_prep_body(xw_ref, al, cw_ref, alog_ref, dtb_ref, q_ref, k_ref, v_ref, go_ref, valid, tm):
    half = A_CONV // 2
    for c, o_ref in enumerate((q_ref, k_ref, v_ref)):
        cs = slice(c * A_WIDTH, (c + 1) * A_WIDTH)
        acc = xw_ref[HALO - half:HALO - half + tm, cs] * cw_ref[0:1, cs]
        for j in range(1, A_CONV):
            acc = acc + xw_ref[HALO - half + j:HALO - half + j + tm, cs] * cw_ref[j:j + 1, cs]
        y = jnp.where(valid, acc * _sigmoid(acc), 0.0)
        for hd in range(A_HEADS):
            hs = slice(hd * A_DK, (hd + 1) * A_DK)
            yh = y[:, hs]
            if c < 2:
                yh = yh * lax.rsqrt(jnp.sum(yh * yh, axis=-1, keepdims=True) + 1e-6)
                if c == 0:
                    yh = yh * (A_DK ** -0.5)
            o_ref[:, hs] = yh.astype(o_ref.dtype)
    lane = lax.broadcasted_iota(jnp.int32, (1, LANE), 1)
    gval = jnp.where(valid, -jnp.exp(alog_ref[...]) * _softplus(al + dtb_ref[...]), 0.0)
    beta = jnp.where(valid, _sigmoid(al), 0.0)
    cum_f = _chunk_cumsum(gval, False)
    cum_b = _chunk_cumsum(gval, True)
    go_ref[...] = jnp.where(lane < A_HEADS, cum_f,
                            jnp.where(lane < 2 * A_HEADS, cum_b,
                                      jnp.where(lane < 4 * A_HEADS, beta, 0.0)))


def _a_scan_kernel(fi_ref, bi_ref, fr_ref,
                   qf, kf, vf, gf, gtf, qb, kb, vb, gb, gtb, of, ob, sf, sb):
    del fi_ref, bi_ref
    @pl.when(fr_ref[pl.program_id(0)] == 1)
    def _():
        sf[...] = jnp.zeros_like(sf)
        sb[...] = jnp.zeros_like(sb)
    r = lax.broadcasted_iota(jnp.int32, (CHUNK, CHUNK), 0)
    c = lax.broadcasted_iota(jnp.int32, (CHUNK, CHUNK), 1)
    eye = jnp.where(r == c, 1.0, 0.0)
    nc = SEQ_BLK // CHUNK
    dirs = ((qf, kf, vf, gf, gtf, of, sf, False, r >= c, r > c),
            (qb, kb, vb, gb, gtb, ob, sb, True, r <= c, r < c))

    probs = []
    for d, (q_ref, k_ref, v_ref, g_ref, gt_ref, _, _, reverse, tri, strict) in enumerate(dirs):
        edge = 0 if reverse else CHUNK - 1
        for step in range(nc):
            jj = nc - 1 - step if reverse else step
            rows = slice(jj * CHUNK, (jj + 1) * CHUNK)
            for hd in range(A_HEADS):
                hs = slice(hd * A_DK, (hd + 1) * A_DK)
                col = hd + (A_HEADS if reverse else 0)
                q, k, v = (ref[rows, hs].astype(F32) for ref in (q_ref, k_ref, v_ref))
                g_col = g_ref[rows, col:col + 1]
                b_col = g_ref[rows, 2 * A_HEADS + col:2 * A_HEADS + col + 1]
                g_row = gt_ref[jj, col:col + 1, :]
                decay = jnp.where(tri, jnp.exp(jnp.where(tri, g_col - g_row, 0.0)), 0.0)
                kb_ = k * b_col
                e_g = jnp.exp(g_col)
                g_last = g_col[edge:edge + 1]
                probs.append(dict(d=d, step=step, hd=hd, rows=rows, hs=hs, tri=tri, strict=strict, decay=decay,
                                  k=k, kbq=jnp.concatenate([kb_, q], axis=0),
                                  rhs=jnp.concatenate([v * b_col, kb_ * e_g], axis=1),
                                  qg=q * e_g, kg_t=(k * jnp.exp(g_last - g_col)).T, g_end=jnp.exp(g_last)))
    for p in probs:
        kq = _dot_nt(p["kbq"], p["k"])
        p["l"] = jnp.where(p["strict"], kq[:CHUNK] * p["decay"], 0.0)
        p["attn"] = jnp.where(p["tri"], kq[CHUNK:] * p["decay"], 0.0)
    t_invs = _neumann_inverses([-p["l"] for p in probs], eye)
    for p, t_inv in zip(probs, t_invs):
        p["uw"] = _dot(t_inv, p["rhs"])

    states = {(d, hd): dirs[d][6][hd] for d in range(2) for hd in range(A_HEADS)}
    for step in range(nc):
        cur = [p for p in probs if p["step"] == step]
        v_news = [p["uw"][:, :A_DV] - _dot(p["uw"][:, A_DV:], states[p["d"], p["hd"]]) for p in cur]
        for p, v_new in zip(cur, v_news):
            state = states[p["d"], p["hd"]]
            dirs[p["d"]][5][p["rows"], p["hs"]] = _dot(jnp.concatenate([p["qg"], p["attn"]], axis=1),
                                                      jnp.concatenate([state, v_new], axis=0))
        for p, v_new in zip(cur, v_news):
            key = (p["d"], p["hd"])
            states[key] = states[key] * p["g_end"] + _dot(p["kg_t"], v_new)
    for (d, hd), state in states.items():
        dirs[d][6][hd] = state


def _scan_tables(geom):
    fi, bi, fr = [], [], []
    for g in geom.groups:
        nblk = g.tp // SEQ_BLK
        for b in range(g.nb):
            base = (g.start + b * g.tp) // SEQ_BLK
            for i in range(nblk):
                fi.append(base + i)
                bi.append(base + nblk - 1 - i)
                fr.append(1 if i == 0 else 0)
    mk = lambda v: jnp.asarray(np.asarray(v, np.int32))
    return mk(fi), mk(bi), mk(fr)


def _a_scan(qn, kn, vv, gts, geom, tables):
    rows = geom.rows
    nc = SEQ_BLK // CHUNK
    gts_t = gts[:, :4 * A_HEADS].reshape(rows // CHUNK, CHUNK, 4 * A_HEADS).transpose(0, 2, 1)
    wide_f = pl.BlockSpec((SEQ_BLK, A_WIDTH), lambda s, fi, bi, fr: (fi[s], 0))
    wide_b = pl.BlockSpec((SEQ_BLK, A_WIDTH), lambda s, fi, bi, fr: (bi[s], 0))
    gate_f = pl.BlockSpec((SEQ_BLK, LANE), lambda s, fi, bi, fr: (fi[s], 0))
    gate_b = pl.BlockSpec((SEQ_BLK, LANE), lambda s, fi, bi, fr: (bi[s], 0))
    gt_f = pl.BlockSpec((nc, 4 * A_HEADS, CHUNK), lambda s, fi, bi, fr: (fi[s], 0, 0))
    gt_b = pl.BlockSpec((nc, 4 * A_HEADS, CHUNK), lambda s, fi, bi, fr: (bi[s], 0, 0))
    out = jax.ShapeDtypeStruct((rows, A_WIDTH), F32)
    return pl.pallas_call(
        _a_scan_kernel,
        out_shape=[out, out],
        grid_spec=pltpu.PrefetchScalarGridSpec(
            num_scalar_prefetch=3,
            grid=(tables[0].shape[0],),
            in_specs=[wide_f, wide_f, wide_f, gate_f, gt_f, wide_b, wide_b, wide_b, gate_b, gt_b],
            out_specs=[wide_f, wide_b],
            scratch_shapes=[pltpu.VMEM((A_HEADS, A_DK, A_DV), F32)] * 2),
        compiler_params=_params(("arbitrary",)),
        name="a_scan",
    )(*tables, qn, kn, vv, gts, gts_t, qn, kn, vv, gts, gts_t)


def _b_prep_body(xw_ref, lw_ref, mu_ref, w0_ref, w2_ref, a0_ref, a2_ref, g2_ref, kk_ref, ka_ref, rk_ref, ones_ref,
                 r_out, k_out, v_out, kkn_out, eta_out, cf_out, cb_out, bonus_out, gate_out, tm):
    def shifted(win_ref, cs, mu_off):
        cur = win_ref[HALO:HALO + tm, cs]
        prev = win_ref[HALO - 1:HALO - 1 + tm, cs]
        nxt = win_ref[HALO + 1:HALO + 1 + tm, cs]
        ms = slice(mu_off + cs.start, mu_off + cs.stop)
        return cur + mu_ref[0:1, ms] * (prev - cur) + mu_ref[1:2, ms] * (nxt - cur)

    w = B_WIDTH
    r = shifted(xw_ref, slice(0, w), 0)
    k = shifted(xw_ref, slice(w, 2 * w), 0)
    v = shifted(xw_ref, slice(2 * w, 3 * w), 0)
    lo = shifted(lw_ref, slice(0, B_LORA_PAD), 3 * w)
    wl = _dot(jnp.tanh(lo), w2_ref[...]) + w0_ref[...]
    log_decay = -float(np.exp(-0.5)) * _sigmoid(wl)
    cf_out[...] = _chunk_cumsum(log_decay[:, :w], False)
    cb_out[...] = _chunk_cumsum(log_decay[:, w:], True)
    eta = _sigmoid(a0_ref[...] + _dot(lo, a2_ref[...]))
    gate_out[...] = _dot(_sigmoid(lo), g2_ref[...])
    kx = k * kk_ref[...]
    kkn_out[...] = (kx * lax.rsqrt(_seg_sum(kx * kx, ones_ref, 2) + 1e-6)).astype(kkn_out.dtype)
    k = k * (1.0 + (eta - 1.0) * ka_ref[...])
    bonus_out[...] = _seg_sum(r * k * rk_ref[...], ones_ref, 2) * v
    r_out[...] = r.astype(r_out.dtype)
    k_out[...] = k.astype(k_out.dtype)
    v_out[...] = v.astype(v_out.dtype)
    eta_out[...] = eta.astype(eta_out.dtype)


def _even_prep_kernel(h_ref, hp_ref, hn_ref, nw_ref, w_ref, cw_ref, alog_ref, dtb_ref,
                      mu_ref, w0_ref, w2_ref, a0_ref, a2_ref, g2_ref, kk_ref, ka_ref, rk_ref, ones_ref,
                      q_out, ka_out, va_out, go_out, z_out,
                      r_out, kb_out, vb_out, kkn_out, eta_out, cf_out, cb_out, bonus_out, gate_out,
                      xn_ref, aw_ref, bw_ref, lw_ref, *, geom):
    tm = geom.tm
    i = pl.program_id(0)
    valid = _row_valid(geom, i * tm, tm)
    xn_ref[0:HALO] = jnp.where(i > 0, _rms(hp_ref[...], nw_ref[...]), 0.0).astype(BF16)
    xn_ref[HALO:HALO + tm] = _rms(h_ref[...], nw_ref[...]).astype(BF16)
    xn_ref[HALO + tm:2 * HALO + tm] = jnp.where(i < pl.num_programs(0) - 1,
                                                _rms(hn_ref[...], nw_ref[...]), 0.0).astype(BF16)
    aw, bw = 3 * A_WIDTH, 3 * B_WIDTH
    c0 = 0
    for win_ref, wd in ((aw_ref, aw), (bw_ref, bw), (lw_ref, B_LORA_PAD)):
        for s in range(0, wd, 512):
            e = min(s + 512, wd)
            win_ref[:, s:e] = jnp.dot(xn_ref[...], w_ref[:, c0 + s:c0 + e], preferred_element_type=F32)
        c0 += wd
    centre = xn_ref[HALO:HALO + tm]
    gates = jnp.dot(centre, w_ref[:, c0:c0 + LANE], preferred_element_type=F32)
    z_out[...] = jnp.dot(centre, w_ref[:, c0 + LANE:], preferred_element_type=F32).astype(z_out.dtype)
    _a_prep_body(aw_ref, gates, cw_ref, alog_ref, dtb_ref, q_out, ka_out, va_out, go_out, valid, tm)
    _b_prep_body(bw_ref, lw_ref, mu_ref, w0_ref, w2_ref, a0_ref, a2_ref, g2_ref, kk_ref, ka_ref, rk_ref, ones_ref,
                 r_out, kb_out, vb_out, kkn_out, eta_out, cf_out, cb_out, bonus_out, gate_out, tm)


def _even_prep(h, norm_w, w_bf16, conv_w, a_log, dt_bias, mu, w0, w2, a0, a2, g2, k_k, k_a, r_k, ones, geom):
    tm, rows = geom.tm, geom.rows
    aw, bw = 3 * A_WIDTH, 3 * B_WIDTH
    row = lambda v: v.reshape(1, -1).astype(F32)
    pad = lambda v: jnp.pad(row(v), ((0, 0), (0, LANE - v.size)))
    narrow = lambda wd: jax.ShapeDtypeStruct((rows, wd), BF16)
    wide = lambda wd: jax.ShapeDtypeStruct((rows, wd), F32)
    blk = lambda wd: pl.BlockSpec((tm, wd), lambda i: (i, 0))
    win = tm + 2 * HALO
    return pl.pallas_call(
        functools.partial(_even_prep_kernel, geom=geom),
        out_shape=[narrow(A_WIDTH)] * 3 + [wide(LANE), narrow(A_WIDTH)] + [narrow(B_WIDTH)] * 5 + [wide(B_WIDTH)] * 4,
        grid=(rows // tm,),
        in_specs=_halo_specs(tm, D_MODEL, rows) + [
            _const_spec((1, D_MODEL)), _const_spec(w_bf16.shape),
            _const_spec((A_CONV, aw)), _const_spec((1, LANE)), _const_spec((1, LANE)),
            _const_spec((2, bw + B_LORA_PAD)), _const_spec((1, 2 * B_WIDTH)), _const_spec((B_LORA_PAD, 2 * B_WIDTH)),
            _const_spec((1, B_WIDTH)), _const_spec((B_LORA_PAD, B_WIDTH)), _const_spec((B_LORA_PAD, B_WIDTH)),
            _const_spec((1, B_WIDTH)), _const_spec((1, B_WIDTH)), _const_spec((1, B_WIDTH)),
            _const_spec((B_WIDTH, B_WIDTH))],
        out_specs=[blk(A_WIDTH)] * 3 + [blk(LANE), blk(A_WIDTH)] + [blk(B_WIDTH)] * 9,
        scratch_shapes=[pltpu.VMEM((win, D_MODEL), BF16), pltpu.VMEM((win, aw), F32), pltpu.VMEM((win, bw), F32),
                        pltpu.VMEM((win, B_LORA_PAD), F32)],
        compiler_params=_params(),
        name="even_prep",
    )(h, h, h, row(norm_w), w_bf16, conv_w, pad(a_log), pad(dt_bias), mu, w0, w2, row(a0), a2, g2,
      row(k_k), row(k_a), row(r_k), ones)


def _b_scan_kernel(fi_ref, bi_ref, fr_ref,
                   rf, kf, vf, kkf, ef, cf, rb, kb, vb, kkb, eb, cb, yf, yb, pf, pb):
    del fi_ref, bi_ref
    @pl.when(fr_ref[pl.program_id(0)] == 1)
    def _():
        pf[...] = jnp.zeros_like(pf)
        pb[...] = jnp.zeros_like(pb)
    n2 = 2 * CHUNK
    r = lax.broadcasted_iota(jnp.int32, (n2, n2), 0)
    c = lax.broadcasted_iota(jnp.int32, (n2, n2), 1)
    same = (r >= CHUNK) == (c >= CHUNK)
    rt, ct = r & (CHUNK - 1), c & (CHUNK - 1)
    eye = jnp.where(r == c, 1.0, 0.0)
    lane_lo = lax.broadcasted_iota(jnp.int32, (1, n2), 1) < B_HD
    row_pos = lax.broadcasted_iota(jnp.int32, (CHUNK, 1), 0)
    nc = SEQ_BLK // CHUNK
    pair_w = 2 * B_HD
    n_pairs = B_HEADS // 2
    dirs = (((rf, kf, vf, kkf, ef, cf), yf, pf, False, same & (rt > ct), same & (rt >= ct)),
            ((rb, kb, vb, kkb, eb, cb), yb, pb, True, same & (rt < ct), same & (rt <= ct)))

    def expand(x):
        return jnp.concatenate([jnp.where(lane_lo, x, 0.0), jnp.where(lane_lo, 0.0, x)], axis=0)

    probs = []
    for d, (refs, _, _, reverse, strict, incl) in enumerate(dirs):
        r_ref, k_ref, v_ref, kk_ref, eta_ref, c_ref = refs
        for step in range(nc):
            jj = nc - 1 - step if reverse else step
            rows = slice(jj * CHUNK, (jj + 1) * CHUNK)
            for pr in range(n_pairs):
                ls = slice(pr * pair_w, (pr + 1) * pair_w)
                r, k, v, kk, eta = (ref[rows, ls].astype(F32) for ref in (r_ref, k_ref, v_ref, kk_ref, eta_ref))
                c_in = c_ref[rows, ls]
                if reverse:
                    c_ex = jnp.where(row_pos < CHUNK - 1, pltpu.roll(c_in, CHUNK - 1, axis=0), 0.0)
                    c_tot = c_in[0:1]
                else:
                    c_ex = jnp.where(row_pos >= 1, pltpu.roll(c_in, 1, axis=0), 0.0)
                    c_tot = c_in[CHUNK - 1:CHUNK]
                b = kk * eta
                inv_w = jnp.exp(-c_in)
                rest_w = jnp.exp(c_tot - c_in)
                a_e = expand(-kk * jnp.exp(c_ex))
                r_e = expand(r * jnp.exp(c_in))
                probs.append(dict(
                    d=d, step=step, pr=pr, rows=rows, ls=ls, strict=strict, incl=incl, v=v,
                    a_e=a_e, r_e=r_e, v_e=expand(v),
                    lhs=jnp.concatenate([a_e, r_e], axis=0),
                    rhs=jnp.concatenate([expand(b * inv_w), expand(k * inv_w)], axis=0),
                    kd_t=jnp.concatenate([b * rest_w, k * rest_w], axis=0).T,
                    w_col=jnp.sum(jnp.where(eye > 0.0, jnp.exp(c_tot), 0.0), axis=1, keepdims=True)))
    for p in probs:
        m1 = _dot_nt(p["lhs"], p["rhs"])
        p["n_ab"] = jnp.where(p["strict"], m1[:n2, :n2], 0.0)
        p["a_ak"] = jnp.where(p["strict"], m1[:n2, n2:], 0.0)
        p["a_r"] = jnp.concatenate([jnp.where(p["incl"], m1[n2:, :n2], 0.0),
                                    jnp.where(p["incl"], m1[n2:, n2:], 0.0)], axis=1)
    t_invs = _neumann_inverses([p["n_ab"] for p in probs], eye)
    for p in probs:
        p["akv"] = _dot(p["a_ak"], p["v_e"])
    for p, t_inv in zip(probs, t_invs):
        p["x"] = _dot(t_inv, jnp.concatenate([p["akv"], p["a_e"]], axis=1))

    states = {(d, pr): dirs[d][2][pr] for d in range(2) for pr in range(n_pairs)}
    for step in range(nc):
        cur = [p for p in probs if p["step"] == step]
        u_es = [p["x"][:, :pair_w] + _dot(p["x"][:, pair_w:], states[p["d"], p["pr"]]) for p in cur]
        for p, u_e in zip(cur, u_es):
            y_e = _dot(jnp.concatenate([p["r_e"], p["a_r"]], axis=1),
                       jnp.concatenate([states[p["d"], p["pr"]], u_e, p["v_e"]], axis=0))
            dirs[p["d"]][1][p["rows"], p["ls"]] = y_e[:CHUNK] + y_e[CHUNK:]
        for p, u_e in zip(cur, u_es):
            key = (p["d"], p["pr"])
            upd = _dot(p["kd_t"], jnp.concatenate([u_e[:CHUNK] + u_e[CHUNK:], p["v"]], axis=0))
            states[key] = states[key] * p["w_col"] + jnp.where(same, upd, 0.0)
    for (d, pr), state in states.items():
        dirs[d][2][pr] = state


def _b_scan(r, k, v, kkn, eta, cf, cb, geom, tables):
    wide_f = pl.BlockSpec((SEQ_BLK, B_WIDTH), lambda s, fi, bi, fr: (fi[s], 0))
    wide_b = pl.BlockSpec((SEQ_BLK, B_WIDTH), lambda s, fi, bi, fr: (bi[s], 0))
    out = jax.ShapeDtypeStruct((geom.rows, B_WIDTH), F32)
    return pl.pallas_call(
        _b_scan_kernel,
        out_shape=[out, out],
        grid_spec=pltpu.PrefetchScalarGridSpec(
            num_scalar_prefetch=3,
            grid=(tables[0].shape[0],),
            in_specs=[wide_f] * 6 + [wide_b] * 6,
            out_specs=[wide_f, wide_b],
            scratch_shapes=[pltpu.VMEM((B_HEADS // 2, 2 * B_HD, 2 * B_HD), F32)] * 2),
        compiler_params=_params(("arbitrary",)),
        name="b_scan",
    )(*tables, r, k, v, kkn, eta, cf, r, k, v, kkn, eta, cb)


def _finish_residual(h, mix, post_w, valid):
    return jnp.where(valid, h + _rms(mix, post_w), 0.0)


def _mixout_even_kernel(h_ref, oaf_ref, oab_ref, z_ref, ybf_ref, ybb_ref, bonus_ref, gate_ref,
                        w_ref, an_ref, lnw_ref, lnb_ref, post_ref, ones_ref, o_ref, cat_ref, *, geom):
    tm = geom.tm
    valid = _row_valid(geom, pl.program_id(0) * tm, tm)
    for hd in range(A_HEADS):
        hs = slice(hd * A_DV, (hd + 1) * A_DV)
        o = oaf_ref[:, hs] + oab_ref[:, hs]
        z = z_ref[:, hs].astype(F32)
        cat_ref[:, hs] = (_rms(o, an_ref[...]) * (z * _sigmoid(z))).astype(BF16)
    y = ybf_ref[...] + ybb_ref[...]
    yc = y - _seg_sum(y, ones_ref, 3) * (1.0 / B_HD)
    var = _seg_sum(yc * yc, ones_ref, 2) * (1.0 / B_HD)
    yn = yc * lax.rsqrt(var + B_GN_EPS) * lnw_ref[...] + lnb_ref[...]
    cat_ref[:, A_WIDTH:] = ((yn + bonus_ref[...]) * gate_ref[...]).astype(BF16)
    mix = jnp.dot(cat_ref[...], w_ref[...], preferred_element_type=F32)
    o_ref[...] = _finish_residual(h_ref[...], mix, post_ref[...], valid)


def _mixout_even(h, oaf, oab, z, ybf, ybb, bonus, gate, w_out, a_norm, ln_w, ln_b, post_w, ones, geom):
    tm, rows = geom.tm, geom.rows
    row = lambda v: v.reshape(1, -1).astype(F32)
    a_blk = pl.BlockSpec((tm, A_WIDTH), lambda i: (i, 0))
    b_blk = pl.BlockSpec((tm, B_WIDTH), lambda i: (i, 0))
    return pl.pallas_call(
        functools.partial(_mixout_even_kernel, geom=geom),
        out_shape=jax.ShapeDtypeStruct((rows, D_MODEL), F32),
        grid=(rows // tm,),
        in_specs=[pl.BlockSpec((tm, D_MODEL), lambda i: (i, 0)), a_blk, a_blk,
                  a_blk,
                  b_blk, b_blk, b_blk, b_blk,
                  _const_spec((A_WIDTH + B_WIDTH, D_MODEL)), _const_spec((1, A_DV)),
                  _const_spec((1, B_WIDTH)), _const_spec((1, B_WIDTH)), _const_spec((1, D_MODEL)),
                  _const_spec((B_WIDTH, B_WIDTH))],
        out_specs=pl.BlockSpec((tm, D_MODEL), lambda i: (i, 0)),
        scratch_shapes=[pltpu.VMEM((tm, A_WIDTH + B_WIDTH), BF16)],
        compiler_params=_params(),
        name="mixout_even",
    )(h, oaf, oab, z, ybf, ybb, bonus, gate, w_out, row(a_norm), row(ln_w), row(ln_b), row(post_w), ones)


def _mixout_odd_kernel(h_ref, o_ref_in, w_ref, post_ref, o_ref, *, geom):
    tm = geom.tm
    valid = _row_valid(geom, pl.program_id(0) * tm, tm)
    mix = jnp.dot(o_ref_in[...].astype(BF16), w_ref[...], preferred_element_type=F32)
    o_ref[...] = _finish_residual(h_ref[...], mix, post_ref[...], valid)


def _mixout_odd(h, o, w_out, post_w, geom):
    tm, rows = geom.tm, geom.rows
    wd = C_HEADS * C_HD
    return pl.pallas_call(
        functools.partial(_mixout_odd_kernel, geom=geom),
        out_shape=jax.ShapeDtypeStruct((rows, D_MODEL), F32),
        grid=(rows // tm,),
        in_specs=[pl.BlockSpec((tm, D_MODEL), lambda i: (i, 0)), pl.BlockSpec((tm, wd), lambda i: (i, 0)),
                  _const_spec((wd, D_MODEL)), _const_spec((1, D_MODEL))],
        out_specs=pl.BlockSpec((tm, D_MODEL), lambda i: (i, 0)),
        compiler_params=_params(),
        name="mixout_odd",
    )(h, o, w_out, post_w.reshape(1, D_MODEL))


def _qkv_kernel(x_ref, nw_ref, w_ref, cos_ref, sin_ref, qn_ref, kn_ref, q_out, k_out, v_out, xn_ref):
    xn_ref[...] = _rms(x_ref[...], nw_ref[...]).astype(BF16)
    cos, sin = cos_ref[...], sin_ref[...]
    lane = lax.broadcasted_iota(jnp.int32, (1, C_HD), 1)
    first_half = (lane & (C_HD // 2 - 1)) < C_HD // 4

    def rope(xh, w, scale):
        xh = _rms(xh, w)
        partner = jnp.where(first_half, pltpu.roll(xh, C_HD - C_HD // 4, axis=1), pltpu.roll(xh, C_HD // 4, axis=1))
        return (xh * cos + partner * sin) * scale

    nq = C_HEADS * C_HD
    nk = C_KV_HEADS * C_HD
    per = 512 // C_HD
    for c in range(nq // 512):
        y = jnp.dot(xn_ref[...], w_ref[:, c * 512:(c + 1) * 512], preferred_element_type=F32)
        for hd in range(per):
            hs = slice((c * per + hd) * C_HD, (c * per + hd + 1) * C_HD)
            q_out[:, hs] = rope(y[:, hd * C_HD:(hd + 1) * C_HD], qn_ref[...], C_HD ** -0.5 * LOG2E).astype(BF16)
    y = jnp.dot(xn_ref[...], w_ref[:, nq:], preferred_element_type=F32)
    for hd in range(C_KV_HEADS):
        hs = slice(hd * C_HD, (hd + 1) * C_HD)
        k_out[:, hs] = rope(y[:, hs], kn_ref[...], 1.0).astype(BF16)
        v_out[:, 2 * hd * C_HD:(2 * hd + 1) * C_HD] = y[:, nk + hd * C_HD:nk + (hd + 1) * C_HD].astype(BF16)
        v_out[:, (2 * hd + 1) * C_HD:(2 * hd + 2) * C_HD] = jnp.ones((x_ref.shape[0], C_HD), BF16)


def _qkv_proj(h, norm_w, w_bf16, cos_t, sin_t, q_norm, k_norm, geom):
    tm, rows = geom.tm, geom.rows
    nq, nk = C_HEADS * C_HD, C_KV_HEADS * C_HD
    return pl.pallas_call(
        _qkv_kernel,
        out_shape=[jax.ShapeDtypeStruct((rows, nq), BF16), jax.ShapeDtypeStruct((rows, nk), BF16),
                   jax.ShapeDtypeStruct((rows, 2 * nk), BF16)],
        grid=(rows // tm,),
        in_specs=[pl.BlockSpec((tm, D_MODEL), lambda i: (i, 0)), _const_spec((1, D_MODEL)),
                  _const_spec((D_MODEL, nq + 2 * nk)),
                  pl.BlockSpec((tm, C_HD), lambda i: (i, 0)), pl.BlockSpec((tm, C_HD), lambda i: (i, 0)),
                  _const_spec((1, C_HD)), _const_spec((1, C_HD))],
        out_specs=[pl.BlockSpec((tm, nq), lambda i: (i, 0)), pl.BlockSpec((tm, nk), lambda i: (i, 0)),
                   pl.BlockSpec((tm, 2 * nk), lambda i: (i, 0))],
        scratch_shapes=[pltpu.VMEM((tm, D_MODEL), BF16)],
        compiler_params=_params(),
        name="qkv_proj",
    )(h, norm_w.reshape(1, D_MODEL), w_bf16, cos_t, sin_t, q_norm.reshape(1, C_HD), k_norm.reshape(1, C_HD))


def _rope_tables(geom):
    nf = C_HD // 4
    inv = ROPE_THETA ** (-jnp.arange(nf, dtype=F32) / nf)
    cos_parts, sin_parts = [], []
    for g in geom.groups:
        t = jnp.arange(g.tp, dtype=jnp.int32) - (g.fv + N_META)
        is_meta = t < 0
        row = jnp.where(is_meta, -1, t // GRID_W).astype(F32)
        col = jnp.where(is_meta, t + N_META, t % GRID_W).astype(F32)
        ang_r, ang_c = row[:, None] * inv, col[:, None] * inv
        cos = jnp.concatenate([jnp.cos(ang_r)] * 2 + [jnp.cos(ang_c)] * 2, axis=1)
        sin = jnp.concatenate([-jnp.sin(ang_r), jnp.sin(ang_r), -jnp.sin(ang_c), jnp.sin(ang_c)], axis=1)
        cos_parts.append(jnp.tile(cos, (g.nb, 1)))
        sin_parts.append(jnp.tile(sin, (g.nb, 1)))
    return jnp.concatenate(cos_parts, axis=0), jnp.concatenate(sin_parts, axis=0)


def _attn_kernel(*refs, nsub, bounds, fv):
    (q_ref, k_ref, v_ref), (o_ref, s_ref, rm_ref, m_ref, acc_ref) = refs[:3], refs[-5:]
    tq = SEQ_BLK
    nc = len(bounds)
    assert nc % 2 == 0

    def q_rows(qi):
        return pl.ds(pl.multiple_of(qi * tq, tq), tq)

    def produce(qi, c, slot):
        start, size = bounds[c]
        q = jnp.concatenate([q_ref[q_rows(qi), g * C_HD:(g + 1) * C_HD] for g in range(C_GROUP)], axis=0)
        s = lax.dot_general(q, k_ref[start:start + size, :], (((1,), (1,)), ((), ())),
                            preferred_element_type=F32)
        if start < fv:
            key = start + lax.broadcasted_iota(jnp.int32, (1, size), 1)
            s = jnp.where(key >= fv, s, -1e30)
        s_ref[slot, :, :size] = s
        rm_ref[slot] = jnp.max(s, axis=-1, keepdims=True)

    def consume(c, slot):
        start, size = bounds[c]
        m_old = m_ref[...]
        m_new = jnp.maximum(m_old, rm_ref[slot])
        p = jnp.exp2(s_ref[slot, :, :size] - m_new).astype(BF16)
        acc_ref[...] = jnp.exp2(m_old - m_new) * acc_ref[...] + jnp.dot(p, v_ref[start:start + size, :],
                                                                        preferred_element_type=F32)
        m_ref[...] = m_new

    produce(0, 0, 0)

    def body(qi, carry):
        m_ref[...] = jnp.full_like(m_ref, -1e30)
        acc_ref[...] = jnp.zeros_like(acc_ref)
        for c in range(nc):
            if c + 1 < nc:
                produce(qi, c + 1, (c + 1) % 2)
            else:
                produce(jnp.minimum(qi + 1, nsub - 1), 0, 0)
            consume(c, c % 2)
        acc = acc_ref[...]
        out = acc[:, :C_HD] / acc[:, C_HD:]
        for g in range(C_GROUP):
            o_ref[q_rows(qi), g * C_HD:(g + 1) * C_HD] = out[g * tq:(g + 1) * tq]
        return carry

    lax.fori_loop(0, nsub, body, 0)


def _key_chunks(tp, fv):
    lo = fv // LANE
    blocks = tp // LANE - lo
    per = ATTN_MAX_TK // LANE
    nc = -(-blocks // per)
    nc += nc % 2
    pairs, single = divmod(blocks, 2)
    sizes = [2 * (pairs // nc + (1 if i < pairs % nc else 0)) for i in range(nc)]
    sizes[-1] += single
    if min(sizes) == 0:
        sizes = [blocks // nc + (1 if i < blocks % nc else 0) for i in range(nc)]
    assert min(sizes) > 0 and max(sizes) <= per
    starts = np.cumsum([lo] + sizes[:-1])
    return tuple((int(s) * LANE, int(z) * LANE) for s, z in zip(starts, sizes))


def _attention(qn, kn, vn, geom):
    out = None
    for g in geom.groups:
        bounds = _key_chunks(g.tp, g.fv)
        nblk = g.tp // SEQ_BLK
        nsub = max(d for d in range(1, ATTN_MAX_SUB + 1) if nblk % d == 0)
        tq, gw, rows = nsub * SEQ_BLK, C_GROUP * C_HD, C_GROUP * SEQ_BLK
        al = lambda x: pl.multiple_of(x, LANE)
        q_spec = pl.BlockSpec((pl.Element(tq), pl.Element(gw)),
                              lambda b, h, i, g=g, tq=tq: (al(g.start + b * g.tp + i * tq), al(h * gw)))
        args = [qn, kn, vn]
        in_specs = [q_spec,
                    pl.BlockSpec((pl.Element(g.tp), pl.Element(C_HD)),
                                 lambda b, h, i, g=g: (al(g.start + b * g.tp), al(h * C_HD))),
                    pl.BlockSpec((pl.Element(g.tp), pl.Element(2 * C_HD)),
                                 lambda b, h, i, g=g: (al(g.start + b * g.tp), al(h * 2 * C_HD)))]
        if out is not None:
            args.append(out)
            in_specs.append(pl.BlockSpec(memory_space=pl.ANY))
        out = pl.pallas_call(
            functools.partial(_attn_kernel, nsub=nsub, bounds=bounds, fv=g.fv),
            out_shape=jax.ShapeDtypeStruct((geom.rows, C_HEADS * C_HD), F32),
            grid=(g.nb, C_KV_HEADS, g.tp // tq),
            in_specs=in_specs,
            out_specs=q_spec,
            scratch_shapes=[pltpu.VMEM((2, rows, max(z for _, z in bounds)), F32), pltpu.VMEM((2, rows, 1), F32),
                            pltpu.VMEM((rows, 1), F32), pltpu.VMEM((rows, 2 * C_HD), F32)],
            input_output_aliases={3: 0} if len(args) == 4 else {},
            compiler_params=_params(("parallel", "parallel", "parallel")),
            name="attention",
        )(*args)
    return out


def _ffn_kernel(x_ref, xp_ref, xn_ref, pre_ref, wu_ref, wg_ref, cw_ref, cb_ref, wo_ref, post_ref,
                o_ref, xs_ref, g0_ref, g1_ref, act_ref, acc_ref, *, tm):
    i = pl.program_id(0)
    xs_ref[0:HALO] = jnp.where(i > 0, _rms(xp_ref[...], pre_ref[...]), 0.0).astype(BF16)
    xs_ref[HALO:HALO + tm] = _rms(x_ref[...], pre_ref[...]).astype(BF16)
    xs_ref[HALO + tm:2 * HALO + tm] = jnp.where(i < pl.num_programs(0) - 1,
                                                _rms(xn_ref[...], pre_ref[...]), 0.0).astype(BF16)
    half = FFN_CONV // 2
    n_chunks = D_FF // FF_CHUNK
    for c in range(n_chunks):
        cs = slice(c * FF_CHUNK, (c + 1) * FF_CHUNK)
        g_ref = (g0_ref, g1_ref)[c % 2]
        u = jnp.dot(xs_ref[HALO:HALO + tm], wu_ref[:, cs], preferred_element_type=F32)
        g_ref[...] = jnp.dot(xs_ref[...], wg_ref[:, cs], preferred_element_type=F32)
        gate = cb_ref[0:1, cs]
        for j in range(FFN_CONV):
            gate = gate + g_ref[HALO - half + j:HALO - half + j + tm] * cw_ref[j:j + 1, cs]
        th = jnp.tanh(gate * (GELU_C0 + GELU_C0 * 0.044715 * (gate * gate)))
        hu = 0.5 * gate * u
        act_ref[:, cs] = (hu + hu * th).astype(BF16)
        if c % FF_GROUP == FF_GROUP - 1 or c == n_chunks - 1:
            ks = slice((c // FF_GROUP) * FF_GROUP * FF_CHUNK, (c + 1) * FF_CHUNK)
            part = jnp.dot(act_ref[:, ks], wo_ref[ks, :], preferred_element_type=F32)
            if c < FF_GROUP:
                acc_ref[...] = part
            else:
                acc_ref[...] += part
    o_ref[...] = x_ref[...] + _rms(acc_ref[...], post_ref[...])


def _conv_ffn(h, pre_w, w_in, conv_w, conv_b, w_out, post_w, geom):
    tm, rows = geom.tm, geom.rows
    assert D_FF % FF_CHUNK == 0
    wu = w_in[:, :D_FF].astype(BF16)
    wg = w_in[:, D_FF:].astype(BF16)
    return pl.pallas_call(
        functools.partial(_ffn_kernel, tm=tm),
        out_shape=jax.ShapeDtypeStruct((rows, D_MODEL), F32),
        grid=(rows // tm,),
        in_specs=_halo_specs(tm, D_MODEL, rows) + [
            _const_spec((1, D_MODEL)), _const_spec((D_MODEL, D_FF)), _const_spec((D_MODEL, D_FF)),
            _const_spec((FFN_CONV, D_FF)), _const_spec((1, D_FF)), _const_spec((D_FF, D_MODEL)),
            _const_spec((1, D_MODEL))],
        out_specs=pl.BlockSpec((tm, D_MODEL), lambda i: (i, 0)),
        scratch_shapes=[pltpu.VMEM((tm + 2 * HALO, D_MODEL), BF16), pltpu.VMEM((tm + 2 * HALO, FF_CHUNK), F32),
                        pltpu.VMEM((tm + 2 * HALO, FF_CHUNK), F32), pltpu.VMEM((tm, D_FF), BF16),
                        pltpu.VMEM((tm, D_MODEL), F32)],
        compiler_params=_params(),
        name="conv_ffn",
    )(h, h, h, pre_w.reshape(1, D_MODEL), wu, wg, conv_w, conv_b.reshape(1, D_FF),
      w_out.astype(BF16), post_w.reshape(1, D_MODEL))


def _lead_kernel(meta_ref, *rest):
    o_ref = rest[-1]
    fv = o_ref.shape[0] - N_META
    o_ref[0:fv] = jnp.zeros((fv, D_MODEL), o_ref.dtype)
    o_ref[fv:] = meta_ref[...].astype(o_ref.dtype)


def _rows_in_kernel(x_ref, _, o_ref):
    o_ref[...] = x_ref[0]


def _rows_out_kernel(h_ref, o_ref):
    o_ref[0] = h_ref[...]


def _copy_block(n):
    return max(b for b in (2048, 1024, 512, 256, 128) if n % b == 0)


def _pack_rows(xs, meta, geom):
    al = lambda v: pl.multiple_of(v, LANE)
    shape = jax.ShapeDtypeStruct((geom.rows, D_MODEL), xs[0].dtype)
    h = None
    for x, g in zip(xs, geom.groups):
        pad = g.tp - g.n
        args = [meta] if h is None else [meta, h]
        h = pl.pallas_call(
            _lead_kernel, out_shape=shape, grid=(g.nb,),
            in_specs=[_const_spec((N_META, D_MODEL))] + [pl.BlockSpec(memory_space=pl.ANY)] * (len(args) - 1),
            out_specs=pl.BlockSpec((pl.Element(pad), pl.Element(D_MODEL)),
                                   lambda b, g=g: (al(g.start + b * g.tp), 0)),
            input_output_aliases={1: 0} if len(args) == 2 else {},
            compiler_params=_params(), name="pack_lead",
        )(*args)
        blk = _copy_block(g.n)
        h = pl.pallas_call(
            _rows_in_kernel, out_shape=shape, grid=(g.nb, g.n // blk),
            in_specs=[pl.BlockSpec((1, blk, D_MODEL), lambda b, j: (b, j, 0)), pl.BlockSpec(memory_space=pl.ANY)],
            out_specs=pl.BlockSpec((pl.Element(blk), pl.Element(D_MODEL)),
                                   lambda b, j, g=g, pad=pad, blk=blk: (al(g.start + b * g.tp + pad + j * blk), 0)),
            input_output_aliases={1: 0},
            compiler_params=_params(("parallel", "parallel")), name="pack_rows",
        )(x, h)
    return h


def _unpack_rows(h, geom):
    al = lambda v: pl.multiple_of(v, LANE)
    outs = []
    for g in geom.groups:
        pad = g.tp - g.n
        blk = _copy_block(g.n)
        outs.append(pl.pallas_call(
            _rows_out_kernel, out_shape=jax.ShapeDtypeStruct((g.nb, g.n, D_MODEL), h.dtype),
            grid=(g.nb, g.n // blk),
            in_specs=[pl.BlockSpec((pl.Element(blk), pl.Element(D_MODEL)),
                                   lambda b, j, g=g, pad=pad, blk=blk: (al(g.start + b * g.tp + pad + j * blk), 0))],
            out_specs=pl.BlockSpec((1, blk, D_MODEL), lambda b, j: (b, j, 0)),
            compiler_params=_params(("parallel", "parallel")), name="unpack_rows",
        )(h))
    return tuple(outs)


def _even_layer(h, j, geom, tables, ones, p):
    aw, bw = A_WIDTH, B_WIDTH
    a_cols = 4 * aw + 4 * A_HEADS
    w_in = p["e_w_in"][j]
    lora = B_W_LORA + B_A_LORA + B_G_LORA
    zeros = lambda n: jnp.zeros((D_MODEL, n), F32)
    w_cat = jnp.concatenate([
        w_in[:, :3 * aw],
        w_in[:, a_cols:a_cols + 3 * bw + lora], zeros(B_LORA_PAD - lora),
        w_in[:, 4 * aw:a_cols], zeros(LANE - 4 * A_HEADS),
        w_in[:, 3 * aw:4 * aw]], axis=1).astype(BF16)
    mu = jnp.pad(p["b_shift"][j], ((0, 0), (0, B_LORA_PAD - lora)))
    lora_rows = lambda m, off: jnp.pad(m, ((off, B_LORA_PAD - off - m.shape[0]), (0, 0)))
    w2 = lora_rows(jnp.concatenate([p["b_w2"][j, 0], p["b_w2"][j, 1]], axis=1), 0)
    a2 = lora_rows(p["b_a2"][j], B_W_LORA)
    g2 = lora_rows(p["b_g2"][j], B_W_LORA + B_A_LORA)
    qn, kn, vv, gts, z, r, k, v, kkn, eta, cf, cb, bonus, gate = _even_prep(
        h, p["mix_pre_norm"][2 * j], w_cat, p["a_conv_w"][j], p["a_log"][j], p["a_dt_bias"][j],
        mu, p["b_w0"][j].reshape(1, 2 * bw), w2, p["b_a0"][j], a2, g2,
        p["b_k_k"][j], p["b_k_a"][j], p["b_r_k"][j], ones, geom)
    oaf, oab = _a_scan(qn, kn, vv, gts, geom, tables)
    ybf, ybb = _b_scan(r, k, v, kkn, eta, cf, cb, geom, tables)

    return _mixout_even(h, oaf, oab, z, ybf, ybb, bonus, gate, p["e_w_out"][j].astype(BF16),
                        p["a_out_norm"][j], p["b_ln_w"][j], p["b_ln_b"][j], p["mix_post_norm"][2 * j], ones, geom)


def _odd_layer(h, j, geom, rope, p):
    qn, kn, vn = _qkv_proj(h, p["mix_pre_norm"][2 * j + 1], p["o_w_qkv"][j].astype(BF16), rope[0], rope[1],
                           p["o_q_norm"][j], p["o_k_norm"][j], geom)
    o = _attention(qn, kn, vn, geom)
    return _mixout_odd(h, o, p["o_w_out"][j].astype(BF16), p["mix_post_norm"][2 * j + 1], geom)


def kernel(x_prompt, x_sample, meta, mix_pre_norm, mix_post_norm, ffn_pre_norm, ffn_post_norm, e_w_in, a_conv_w, a_log, a_dt_bias, a_out_norm, b_shift, b_w0, b_w2, b_a0, b_a2, b_g2, b_k_k, b_k_a, b_r_k, b_ln_w, b_ln_b, e_w_out, o_w_qkv, o_q_norm, o_k_norm, o_w_out, f_w_in, f_conv_w, f_conv_b, f_w_out):
    p = dict(mix_pre_norm=mix_pre_norm, mix_post_norm=mix_post_norm, e_w_in=e_w_in, a_conv_w=a_conv_w,
             a_log=a_log, a_dt_bias=a_dt_bias, a_out_norm=a_out_norm, b_shift=b_shift, b_w0=b_w0, b_w2=b_w2,
             b_a0=b_a0, b_a2=b_a2, b_g2=b_g2, b_k_k=b_k_k, b_k_a=b_k_a, b_r_k=b_r_k, b_ln_w=b_ln_w,
             b_ln_b=b_ln_b, e_w_out=e_w_out, o_w_qkv=o_w_qkv, o_q_norm=o_q_norm, o_k_norm=o_k_norm,
             o_w_out=o_w_out)
    xs = (x_prompt, x_sample)
    geom = _choose_geom([(x.shape[0], x.shape[1]) for x in xs])
    tables = _scan_tables(geom)
    rope = _rope_tables(geom)
    seg = np.arange(B_WIDTH) // B_HD
    ones = jnp.asarray(seg[:, None] == seg[None, :], BF16)

    h = _pack_rows(xs, meta, geom)
    for i in range(DEPTH):
        h = _even_layer(h, i // 2, geom, tables, ones, p) if i % 2 == 0 else _odd_layer(h, i // 2, geom, rope, p)
        h = _conv_ffn(h, ffn_pre_norm[i], f_w_in[i], f_conv_w[i], f_conv_b[i], f_w_out[i], ffn_post_norm[i], geom)
    return _unpack_rows(h, geom)
```

```python
import functools
import itertools
from typing import NamedTuple

import numpy as np
import jax
import jax.numpy as jnp
from jax import lax
from jax.experimental import pallas as pl
from jax.experimental.pallas import tpu as pltpu

F32 = jnp.float32
BF16 = jnp.bfloat16

D_MODEL = 1024
DEPTH = 2
N_META = 16
GRID_W = 64
NORM_EPS = 1e-6

A_HEADS = 4
A_DK = 128
A_DV = 128
A_CONV = 5
A_WIDTH = A_HEADS * A_DV

B_HEADS = 8
B_HD = 64
B_WIDTH = B_HEADS * B_HD
B_W_LORA = 32
B_A_LORA = 32
B_G_LORA = 96
B_GN_EPS = 64e-5
B_LORA_PAD = 256

C_HEADS = 8
C_KV_HEADS = 2
C_GROUP = C_HEADS // C_KV_HEADS
C_HD = 128
ROPE_THETA = 10000.0

D_FF = 2816
FFN_CONV = 3

LANE = 128
CHUNK = 64
SEQ_BLK = 128
HALO = 16
FF_CHUNK = 256
FF_GROUP = 4
ATTN_MAX_TK = 1792
ATTN_MAX_SUB = 17
LOG2E = 1.4426950408889634
GELU_C0 = 0.7978845608028654
VMEM_LIMIT = 56 * 1024 * 1024


class _Group(NamedTuple):
    start: int
    nb: int
    n: int
    tp: int
    fv: int


class _Geom(NamedTuple):
    groups: tuple
    rows: int
    tm: int


def _choose_geom(shapes):
    tm = 512 if min(n for _, n in shapes) >= 512 else 128
    best = None
    for pads in itertools.product((128, 256, 384, 512), repeat=len(shapes)):
        start, ok, groups = 0, True, []
        for (nb, n), p in zip(shapes, pads):
            ok = ok and start % tm == 0
            groups.append(_Group(start, nb, n, n + p, p - N_META))
            start += nb * (n + p)
        if ok and start % tm == 0 and (best is None or start < best.rows):
            best = _Geom(tuple(groups), start, tm)
    assert best is not None
    return best


def _row_valid(geom, base, tm):
    start = jnp.int32(geom.groups[0].start)
    tp = jnp.int32(geom.groups[0].tp)
    fv = jnp.int32(geom.groups[0].fv)
    for g in geom.groups[1:]:
        assert tm <= g.tp
        inside = base >= g.start
        start = jnp.where(inside, g.start, start)
        tp = jnp.where(inside, g.tp, tp)
        fv = jnp.where(inside, g.fv, fv)
    pos = lax.rem(base - start, tp) + lax.broadcasted_iota(jnp.int32, (tm, 1), 0)
    pos = jnp.where(pos >= tp, pos - tp, pos)
    return pos >= fv


def _rms(x, w, eps=NORM_EPS):
    return x * lax.rsqrt(jnp.mean(x * x, axis=-1, keepdims=True) + eps) * w


def _sigmoid(x):
    return 1.0 / (1.0 + jnp.exp(-x))


def _softplus(x):
    return jnp.maximum(x, 0.0) + jnp.log1p(jnp.exp(-jnp.abs(x)))


def _dot(a, b):
    return jnp.dot(a.astype(BF16), b.astype(BF16), preferred_element_type=F32)


def _dot_nt(a, b):
    return lax.dot_general(a.astype(BF16), b.astype(BF16), (((1,), (1,)), ((), ())),
                           preferred_element_type=F32)


def _seg_sum(x, ones_ref, terms):
    acc = None
    rem = x
    for t in range(terms):
        piece = rem.astype(BF16)
        part = jnp.dot(piece, ones_ref[...], preferred_element_type=F32)
        acc = part if acc is None else acc + part
        if t + 1 < terms:
            rem = rem - piece.astype(F32)
    return acc


def _chunk_cumsum(x, reverse):
    tm = x.shape[0]
    pos = lax.broadcasted_iota(jnp.int32, (tm, 1), 0) & (CHUNK - 1)
    s = 1
    while s < CHUNK:
        if reverse:
            x = x + jnp.where(pos < CHUNK - s, pltpu.roll(x, tm - s, axis=0), 0.0)
        else:
            x = x + jnp.where(pos >= s, pltpu.roll(x, s, axis=0), 0.0)
        s *= 2
    return x


def _neumann_inverses(n_mats, eye):
    n = eye.shape[0]
    prods = [eye + m for m in n_mats]
    powers = [_dot(m, m) for m in n_mats]
    s = 4
    while s < CHUNK:
        both = [_dot(jnp.concatenate([pr, pw], axis=0), pw) for pr, pw in zip(prods, powers)]
        prods = [pr + b[:n] for pr, b in zip(prods, both)]
        powers = [b[n:] for b in both]
        s *= 2
    return [pr + _dot(pr, pw) for pr, pw in zip(prods, powers)]


def _shifted_rows(win, d, tm):
    if d == 0:
        return win[HALO:HALO + tm]
    return pltpu.roll(win, (-d) % win.shape[0], axis=0)[HALO:HALO + tm]


def _halo_specs(tm, width, rows):
    per = tm // HALO
    last = rows // HALO - 1
    return [
        pl.BlockSpec((tm, width), lambda i: (i, 0)),
        pl.BlockSpec((HALO, width), lambda i: (jnp.maximum(i * per - 1, 0), 0)),
        pl.BlockSpec((HALO, width), lambda i: (jnp.minimum((i + 1) * per, last), 0)),
    ]


def _const_spec(shape):
    return pl.BlockSpec(shape, lambda *_: (0,) * len(shape))


def _params(sem=("parallel",)):
    return pltpu.CompilerParams(dimension_semantics=sem, vmem_limit_bytes=VMEM_LIMIT)


def _a_prep_body(xw_ref, al, cw_ref, alog_ref, dtb_ref, q_ref, k_ref, v_ref, go_ref, valid, tm):
    half = A_CONV // 2
    for c, o_ref in enumerate((q_ref, k_ref, v_ref)):
        cs = slice(c * A_WIDTH, (c + 1) * A_WIDTH)
        win = xw_ref[:, cs]
        acc = _shifted_rows(win, -half, tm) * cw_ref[0:1, cs]
        for j in range(1, A_CONV):
            acc = acc + _shifted_rows(win, j - half, tm) * cw_ref[j:j + 1, cs]
        y = jnp.where(valid, acc * _sigmoid(acc), 0.0)
        for hd in range(A_HEADS):
            hs = slice(hd * A_DK, (hd + 1) * A_DK)
            yh = y[:, hs]
            if c < 2:
                yh = yh * lax.rsqrt(jnp.sum(yh * yh, axis=-1, keepdims=True) + 1e-6)
                if c == 0:
                    yh = yh * (A_DK ** -0.5)
            o_ref[:, hs] = yh.astype(o_ref.dtype)
    lane = lax.broadcasted_iota(jnp.int32, (1, LANE), 1)
    gval = jnp.where(valid, -jnp.exp(alog_ref[...]) * _softplus(al + dtb_ref[...]), 0.0)
    beta = jnp.where(valid, _sigmoid(al), 0.0)
    cum_f = _chunk_cumsum(gval, False)
    cum_b = _chunk_cumsum(gval, True)
    go_ref[...] = jnp.where(lane < A_HEADS, cum_f,
                            jnp.where(lane < 2 * A_HEADS, cum_b,
                                      jnp.where(lane < 4 * A_HEADS, beta, 0.0)))


def _a_scan_kernel(fi_ref, bi_ref, fr_ref,
                   qf, kf, vf, gf, gtf, qb, kb, vb, gb, gtb, of, ob, sf, sb):
    del fi_ref, bi_ref
    @pl.when(fr_ref[pl.program_id(0)] == 1)
    def _():
        sf[...] = jnp.zeros_like(sf)
        sb[...] = jnp.zeros_like(sb)
    r = lax.broadcasted_iota(jnp.int32, (CHUNK, CHUNK), 0)
    c = lax.broadcasted_iota(jnp.int32, (CHUNK, CHUNK), 1)
    eye = jnp.where(r == c, 1.0, 0.0)
    nc = SEQ_BLK // CHUNK
    dirs = ((qf, kf, vf, gf, gtf, of, sf, False, r >= c, r > c),
            (qb, kb, vb, gb, gtb, ob, sb, True, r <= c, r < c))

    probs = []
    for d, (q_ref, k_ref, v_ref, g_ref, gt_ref, _, _, reverse, tri, strict) in enumerate(dirs):
        edge = 0 if reverse else CHUNK - 1
        for step in range(nc):
            jj = nc - 1 - step if reverse else step
            rows = slice(jj * CHUNK, (jj + 1) * CHUNK)
            for hd in range(A_HEADS):
                hs = slice(hd * A_DK, (hd + 1) * A_DK)
                col = hd + (A_HEADS if reverse else 0)
                q, k, v = (ref[rows, hs].astype(F32) for ref in (q_ref, k_ref, v_ref))
                g_col = g_ref[rows, col:col + 1]
                b_col = g_ref[rows, 2 * A_HEADS + col:2 * A_HEADS + col + 1]
                g_row = gt_ref[jj, col:col + 1, :]
                decay = jnp.where(tri, jnp.exp(jnp.where(tri, g_col - g_row, 0.0)), 0.0)
                kb_ = k * b_col
                e_g = jnp.exp(g_col)
                g_last = g_col[edge:edge + 1]
                probs.append(dict(d=d, step=step, hd=hd, rows=rows, hs=hs, tri=tri, strict=strict, decay=decay,
                                  k=k, kbq=jnp.concatenate([kb_, q], axis=0),
                                  rhs=jnp.concatenate([v * b_col, kb_ * e_g], axis=1),
                                  qg=q * e_g, kg_t=(k * jnp.exp(g_last - g_col)).T, g_end=jnp.exp(g_last)))
    for p in probs:
        kq = _dot_nt(p["kbq"], p["k"])
        p["l"] = jnp.where(p["strict"], kq[:CHUNK] * p["decay"], 0.0)
        p["attn"] = jnp.where(p["tri"], kq[CHUNK:] * p["decay"], 0.0)
    t_invs = _neumann_inverses([-p["l"] for p in probs], eye)
    for p, t_inv in zip(probs, t_invs):
        p["uw"] = _dot(t_inv, p["rhs"])

    states = {(d, hd): dirs[d][6][hd] for d in range(2) for hd in range(A_HEADS)}
    for step in range(nc):
        cur = [p for p in probs if p["step"] == step]
        v_news = [p["uw"][:, :A_DV] - _dot(p["uw"][:, A_DV:], states[p["d"], p["hd"]]) for p in cur]
        for p, v_new in zip(cur, v_news):
            state = states[p["d"], p["hd"]]
            dirs[p["d"]][5][p["rows"], p["hs"]] = _dot(jnp.concatenate([p["qg"], p["attn"]], axis=1),
                                                      jnp.concatenate([state, v_new], axis=0))
        for p, v_new in zip(cur, v_news):
            key = (p["d"], p["hd"])
            states[key] = states[key] * p["g_end"] + _dot(p["kg_t"], v_new)
    for (d, hd), state in states.items():
        dirs[d][6][hd] = state


def _scan_tables(geom):
    fi, bi, fr = [], [], []
    for g in geom.groups:
        nblk = g.tp // SEQ_BLK
        for b in range(g.nb):
            base = (g.start + b * g.tp) // SEQ_BLK
            for i in range(nblk):
                fi.append(base + i)
                bi.append(base + nblk - 1 - i)
                fr.append(1 if i == 0 else 0)
    mk = lambda v: jnp.asarray(np.asarray(v, np.int32))
    return mk(fi), mk(bi), mk(fr)


def _a_scan(qn, kn, vv, gts, geom, tables):
    rows = geom.rows
    nc = SEQ_BLK // CHUNK
    gts_t = gts[:, :4 * A_HEADS].reshape(rows // CHUNK, CHUNK, 4 * A_HEADS).transpose(0, 2, 1)
    wide_f = pl.BlockSpec((SEQ_BLK, A_WIDTH), lambda s, fi, bi, fr: (fi[s], 0))
    wide_b = pl.BlockSpec((SEQ_BLK, A_WIDTH), lambda s, fi, bi, fr: (bi[s], 0))
    gate_f = pl.BlockSpec((SEQ_BLK, LANE), lambda s, fi, bi, fr: (fi[s], 0))
    gate_b = pl.BlockSpec((SEQ_BLK, LANE), lambda s, fi, bi, fr: (bi[s], 0))
    gt_f = pl.BlockSpec((nc, 4 * A_HEADS, CHUNK), lambda s, fi, bi, fr: (fi[s], 0, 0))
    gt_b = pl.BlockSpec((nc, 4 * A_HEADS, CHUNK), lambda s, fi, bi, fr: (bi[s], 0, 0))
    out = jax.ShapeDtypeStruct((rows, A_WIDTH), F32)
    return pl.pallas_call(
        _a_scan_kernel,
        out_shape=[out, out],
        grid_spec=pltpu.PrefetchScalarGridSpec(
            num_scalar_prefetch=3,
            grid=(tables[0].shape[0],),
            in_specs=[wide_f, wide_f, wide_f, gate_f, gt_f, wide_b, wide_b, wide_b, gate_b, gt_b],
            out_specs=[wide_f, wide_b],
            scratch_shapes=[pltpu.VMEM((A_HEADS, A_DK, A_DV), F32)] * 2),
        compiler_params=_params(("arbitrary",)),
        name="a_scan",
    )(*tables, qn, kn, vv, gts, gts_t, qn, kn, vv, gts, gts_t)


def _b_prep_body(xw_ref, lw_ref, mu_ref, w0_ref, w2_ref, a0_ref, a2_ref, g2_ref, kk_ref, ka_ref, rk_ref, ones_ref,
                 r_out, k_out, v_out, kkn_out, eta_out, cf_out, cb_out, bonus_out, gate_out, tm):
    def shifted(win_ref, cs, mu_off):
        win = win_ref[:, cs]
        cur, prev, nxt = _shifted_rows(win, 0, tm), _shifted_rows(win, -1, tm), _shifted_rows(win, 1, tm)
        ms = slice(mu_off + cs.start, mu_off + cs.stop)
        return cur + mu_ref[0:1, ms] * (prev - cur) + mu_ref[1:2, ms] * (nxt - cur)

    w = B_WIDTH
    r = shifted(xw_ref, slice(0, w), 0)
    k = shifted(xw_ref, slice(w, 2 * w), 0)
    v = shifted(xw_ref, slice(2 * w, 3 * w), 0)
    lo = shifted(lw_ref, slice(0, B_LORA_PAD), 3 * w)
    wl = _dot(jnp.tanh(lo), w2_ref[...]) + w0_ref[...]
    log_decay = -float(np.exp(-0.5)) * _sigmoid(wl)
    cf_out[...] = _chunk_cumsum(log_decay[:, :w], False)
    cb_out[...] = _chunk_cumsum(log_decay[:, w:], True)
    eta = _sigmoid(a0_ref[...] + _dot(lo, a2_ref[...]))
    gate_out[...] = _dot(_sigmoid(lo), g2_ref[...])
    kx = k * kk_ref[...]
    kkn_out[...] = (kx * lax.rsqrt(_seg_sum(kx * kx, ones_ref, 2) + 1e-6)).astype(kkn_out.dtype)
    k = k * (1.0 + (eta - 1.0) * ka_ref[...])
    bonus_out[...] = _seg_sum(r * k * rk_ref[...], ones_ref, 2) * v
    r_out[...] = r.astype(r_out.dtype)
    k_out[...] = k.astype(k_out.dtype)
    v_out[...] = v.astype(v_out.dtype)
    eta_out[...] = eta.astype(eta_out.dtype)


def _even_prep_kernel(h_ref, hp_ref, hn_ref, nw_ref, w_ref, cw_ref, alog_ref, dtb_ref,
                      mu_ref, w0_ref, w2_ref, a0_ref, a2_ref, g2_ref, kk_ref, ka_ref, rk_ref, ones_ref,
                      q_out, ka_out, va_out, go_out, z_out,
                      r_out, kb_out, vb_out, kkn_out, eta_out, cf_out, cb_out, bonus_out, gate_out,
                      xn_ref, aw_ref, bw_ref, lw_ref, *, geom):
    tm = geom.tm
    i = pl.program_id(0)
    valid = _row_valid(geom, i * tm, tm)
    xn_ref[0:HALO] = jnp.where(i > 0, _rms(hp_ref[...], nw_ref[...]), 0.0).astype(BF16)
    xn_ref[HALO:HALO + tm] = _rms(h_ref[...], nw_ref[...]).astype(BF16)
    xn_ref[HALO + tm:2 * HALO + tm] = jnp.where(i < pl.num_programs(0) - 1,
                                                _rms(hn_ref[...], nw_ref[...]), 0.0).astype(BF16)
    aw, bw = 3 * A_WIDTH, 3 * B_WIDTH
    c0 = 0
    for win_ref, wd in ((aw_ref, aw), (bw_ref, bw), (lw_ref, B_LORA_PAD)):
        for s in range(0, wd, 512):
            e = min(s + 512, wd)
            win_ref[:, s:e] = jnp.dot(xn_ref[...], w_ref[:, c0 + s:c0 + e], preferred_element_type=F32)
        c0 += wd
    centre = xn_ref[HALO:HALO + tm]
    gates = jnp.dot(centre, w_ref[:, c0:c0 + LANE], preferred_element_type=F32)
    z_out[...] = jnp.dot(centre, w_ref[:, c0 + LANE:], preferred_element_type=F32).astype(z_out.dtype)
    _a_prep_body(aw_ref, gates, cw_ref, alog_ref, dtb_ref, q_out, ka_out, va_out, go_out, valid, tm)
    _b_prep_body(bw_ref, lw_ref, mu_ref, w0_ref, w2_ref, a0_ref, a2_ref, g2_ref, kk_ref, ka_ref, rk_ref, ones_ref,
                 r_out, kb_out, vb_out, kkn_out, eta_out, cf_out, cb_out, bonus_out, gate_out, tm)


def _even_prep(h, norm_w, w_bf16, conv_w, a_log, dt_bias, mu, w0, w2, a0, a2, g2, k_k, k_a, r_k, ones, geom):
    tm, rows = geom.tm, geom.rows
    aw, bw = 3 * A_WIDTH, 3 * B_WIDTH
    row = lambda v: v.reshape(1, -1).astype(F32)
    pad = lambda v: jnp.pad(row(v), ((0, 0), (0, LANE - v.size)))
    narrow = lambda wd: jax.ShapeDtypeStruct((rows, wd), BF16)
    wide = lambda wd: jax.ShapeDtypeStruct((rows, wd), F32)
    blk = lambda wd: pl.BlockSpec((tm, wd), lambda i: (i, 0))
    win = tm + 2 * HALO
    return pl.pallas_call(
        functools.partial(_even_prep_kernel, geom=geom),
        out_shape=[narrow(A_WIDTH)] * 3 + [wide(LANE), narrow(A_WIDTH)] + [narrow(B_WIDTH)] * 5 + [wide(B_WIDTH)] * 4,
        grid=(rows // tm,),
        in_specs=_halo_specs(tm, D_MODEL, rows) + [
            _const_spec((1, D_MODEL)), _const_spec(w_bf16.shape),
            _const_spec((A_CONV, aw)), _const_spec((1, LANE)), _const_spec((1, LANE)),
            _const_spec((2, bw + B_LORA_PAD)), _const_spec((1, 2 * B_WIDTH)), _const_spec((B_LORA_PAD, 2 * B_WIDTH)),
            _const_spec((1, B_WIDTH)), _const_spec((B_LORA_PAD, B_WIDTH)), _const_spec((B_LORA_PAD, B_WIDTH)),
            _const_spec((1, B_WIDTH)), _const_spec((1, B_WIDTH)), _const_spec((1, B_WIDTH)),
            _const_spec((B_WIDTH, B_WIDTH))],
        out_specs=[blk(A_WIDTH)] * 3 + [blk(LANE), blk(A_WIDTH)] + [blk(B_WIDTH)] * 9,
        scratch_shapes=[pltpu.VMEM((win, D_MODEL), BF16), pltpu.VMEM((win, aw), F32), pltpu.VMEM((win, bw), F32),
                        pltpu.VMEM((win, B_LORA_PAD), F32)],
        compiler_params=_params(),
        name="even_prep",
    )(h, h, h, row(norm_w), w_bf16, conv_w, pad(a_log), pad(dt_bias), mu, w0, w2, row(a0), a2, g2,
      row(k_k), row(k_a), row(r_k), ones)


def _b_scan_kernel(fi_ref, bi_ref, fr_ref,
                   rf, kf, vf, kkf, ef, cf, rb, kb, vb, kkb, eb, cb, yf, yb, pf, pb):
    del fi_ref, bi_ref
    @pl.when(fr_ref[pl.program_id(0)] == 1)
    def _():
        pf[...] = jnp.zeros_like(pf)
        pb[...] = jnp.zeros_like(pb)
    n2 = 2 * CHUNK
    r = lax.broadcasted_iota(jnp.int32, (n2, n2), 0)
    c = lax.broadcasted_iota(jnp.int32, (n2, n2), 1)
    same = (r >= CHUNK) == (c >= CHUNK)
    rt, ct = r & (CHUNK - 1), c & (CHUNK - 1)
    eye = jnp.where(r == c, 1.0, 0.0)
    lane_lo = lax.broadcasted_iota(jnp.int32, (1, n2), 1) < B_HD
    row_pos = lax.broadcasted_iota(jnp.int32, (CHUNK, 1), 0)
    nc = SEQ_BLK // CHUNK
    pair_w = 2 * B_HD
    n_pairs = B_HEADS // 2
    dirs = (((rf, kf, vf, kkf, ef, cf), yf, pf, False, same & (rt > ct), same & (rt >= ct)),
            ((rb, kb, vb, kkb, eb, cb), yb, pb, True, same & (rt < ct), same & (rt <= ct)))

    def expand(x):
        return jnp.concatenate([jnp.where(lane_lo, x, 0.0), jnp.where(lane_lo, 0.0, x)], axis=0)

    probs = []
    for d, (refs, _, _, reverse, strict, incl) in enumerate(dirs):
        r_ref, k_ref, v_ref, kk_ref, eta_ref, c_ref = refs
        for step in range(nc):
            jj = nc - 1 - step if reverse else step
            rows = slice(jj * CHUNK, (jj + 1) * CHUNK)
            for pr in range(n_pairs):
                ls = slice(pr * pair_w, (pr + 1) * pair_w)
                r, k, v, kk, eta = (ref[rows, ls].astype(F32) for ref in (r_ref, k_ref, v_ref, kk_ref, eta_ref))
                c_in = c_ref[rows, ls]
                if reverse:
                    c_ex = jnp.where(row_pos < CHUNK - 1, pltpu.roll(c_in, CHUNK - 1, axis=0), 0.0)
                    c_tot = c_in[0:1]
                else:
                    c_ex = jnp.where(row_pos >= 1, pltpu.roll(c_in, 1, axis=0), 0.0)
                    c_tot = c_in[CHUNK - 1:CHUNK]
                b = kk * eta
                inv_w = jnp.exp(-c_in)
                rest_w = jnp.exp(c_tot - c_in)
                a_e = expand(-kk * jnp.exp(c_ex))
                r_e = expand(r * jnp.exp(c_in))
                probs.append(dict(
                    d=d, step=step, pr=pr, rows=rows, ls=ls, strict=strict, incl=incl, v=v,
                    a_e=a_e, r_e=r_e, v_e=expand(v),
                    lhs=jnp.concatenate([a_e, r_e], axis=0),
                    rhs=jnp.concatenate([expand(b * inv_w), expand(k * inv_w)], axis=0),
                    kd_t=jnp.concatenate([b * rest_w, k * rest_w], axis=0).T,
                    w_col=jnp.sum(jnp.where(eye > 0.0, jnp.exp(c_tot), 0.0), axis=1, keepdims=True)))
    for p in probs:
        m1 = _dot_nt(p["lhs"], p["rhs"])
        p["n_ab"] = jnp.where(p["strict"], m1[:n2, :n2], 0.0)
        p["a_ak"] = jnp.where(p["strict"], m1[:n2, n2:], 0.0)
        p["a_r"] = jnp.concatenate([jnp.where(p["incl"], m1[n2:, :n2], 0.0),
                                    jnp.where(p["incl"], m1[n2:, n2:], 0.0)], axis=1)
    t_invs = _neumann_inverses([p["n_ab"] for p in probs], eye)
    for p in probs:
        p["akv"] = _dot(p["a_ak"], p["v_e"])
    for p, t_inv in zip(probs, t_invs):
        p["x"] = _dot(t_inv, jnp.concatenate([p["akv"], p["a_e"]], axis=1))

    states = {(d, pr): dirs[d][2][pr] for d in range(2) for pr in range(n_pairs)}
    for step in range(nc):
        cur = [p for p in probs if p["step"] == step]
        u_es = [p["x"][:, :pair_w] + _dot(p["x"][:, pair_w:], states[p["d"], p["pr"]]) for p in cur]
        for p, u_e in zip(cur, u_es):
            y_e = _dot(jnp.concatenate([p["r_e"], p["a_r"]], axis=1),
                       jnp.concatenate([states[p["d"], p["pr"]], u_e, p["v_e"]], axis=0))
            dirs[p["d"]][1][p["rows"], p["ls"]] = y_e[:CHUNK] + y_e[CHUNK:]
        for p, u_e in zip(cur, u_es):
            key = (p["d"], p["pr"])
            upd = _dot(p["kd_t"], jnp.concatenate([u_e[:CHUNK] + u_e[CHUNK:], p["v"]], axis=0))
            states[key] = states[key] * p["w_col"] + jnp.where(same, upd, 0.0)
    for (d, pr), state in states.items():
        dirs[d][2][pr] = state


def _b_scan(r, k, v, kkn, eta, cf, cb, geom, tables):
    wide_f = pl.BlockSpec((SEQ_BLK, B_WIDTH), lambda s, fi, bi, fr: (fi[s], 0))
    wide_b = pl.BlockSpec((SEQ_BLK, B_WIDTH), lambda s, fi, bi, fr: (bi[s], 0))
    out = jax.ShapeDtypeStruct((geom.rows, B_WIDTH), F32)
    return pl.pallas_call(
        _b_scan_kernel,
        out_shape=[out, out],
        grid_spec=pltpu.PrefetchScalarGridSpec(
            num_scalar_prefetch=3,
            grid=(tables[0].shape[0],),
            in_specs=[wide_f] * 6 + [wide_b] * 6,
            out_specs=[wide_f, wide_b],
            scratch_shapes=[pltpu.VMEM((B_HEADS // 2, 2 * B_HD, 2 * B_HD), F32)] * 2),
        compiler_params=_params(("arbitrary",)),
        name="b_scan",
    )(*tables, r, k, v, kkn, eta, cf, r, k, v, kkn, eta, cb)


def _finish_residual(h, mix, post_w, valid):
    return jnp.where(valid, h + _rms(mix, post_w), 0.0)


def _mixout_even_kernel(h_ref, oaf_ref, oab_ref, z_ref, ybf_ref, ybb_ref, bonus_ref, gate_ref,
                        w_ref, an_ref, lnw_ref, lnb_ref, post_ref, ones_ref, o_ref, cat_ref, *, geom):
    tm = geom.tm
    valid = _row_valid(geom, pl.program_id(0) * tm, tm)
    for hd in range(A_HEADS):
        hs = slice(hd * A_DV, (hd + 1) * A_DV)
        o = oaf_ref[:, hs] + oab_ref[:, hs]
        z = z_ref[:, hs].astype(F32)
        cat_ref[:, hs] = (_rms(o, an_ref[...]) * (z * _sigmoid(z))).astype(BF16)
    y = ybf_ref[...] + ybb_ref[...]
    yc = y - _seg_sum(y, ones_ref, 3) * (1.0 / B_HD)
    var = _seg_sum(yc * yc, ones_ref, 2) * (1.0 / B_HD)
    yn = yc * lax.rsqrt(var + B_GN_EPS) * lnw_ref[...] + lnb_ref[...]
    cat_ref[:, A_WIDTH:] = ((yn + bonus_ref[...]) * gate_ref[...]).astype(BF16)
    mix = jnp.dot(cat_ref[...], w_ref[...], preferred_element_type=F32)
    o_ref[...] = _finish_residual(h_ref[...], mix, post_ref[...], valid)


def _mixout_even(h, oaf, oab, z, ybf, ybb, bonus, gate, w_out, a_norm, ln_w, ln_b, post_w, ones, geom):
    tm, rows = geom.tm, geom.rows
    row = lambda v: v.reshape(1, -1).astype(F32)
    a_blk = pl.BlockSpec((tm, A_WIDTH), lambda i: (i, 0))
    b_blk = pl.BlockSpec((tm, B_WIDTH), lambda i: (i, 0))
    return pl.pallas_call(
        functools.partial(_mixout_even_kernel, geom=geom),
        out_shape=jax.ShapeDtypeStruct((rows, D_MODEL), F32),
        grid=(rows // tm,),
        in_specs=[pl.BlockSpec((tm, D_MODEL), lambda i: (i, 0)), a_blk, a_blk,
                  a_blk,
                  b_blk, b_blk, b_blk, b_blk,
                  _const_spec((A_WIDTH + B_WIDTH, D_MODEL)), _const_spec((1, A_DV)),
                  _const_spec((1, B_WIDTH)), _const_spec((1, B_WIDTH)), _const_spec((1, D_MODEL)),
                  _const_spec((B_WIDTH, B_WIDTH))],
        out_specs=pl.BlockSpec((tm, D_MODEL), lambda i: (i, 0)),
        scratch_shapes=[pltpu.VMEM((tm, A_WIDTH + B_WIDTH), BF16)],
        compiler_params=_params(),
        name="mixout_even",
    )(h, oaf, oab, z, ybf, ybb, bonus, gate, w_out, row(a_norm), row(ln_w), row(ln_b), row(post_w), ones)


def _mixout_odd_kernel(h_ref, o_ref_in, w_ref, post_ref, o_ref, *, geom):
    tm = geom.tm
    valid = _row_valid(geom, pl.program_id(0) * tm, tm)
    mix = jnp.dot(o_ref_in[...].astype(BF16), w_ref[...], preferred_element_type=F32)
    o_ref[...] = _finish_residual(h_ref[...], mix, post_ref[...], valid)


def _mixout_odd(h, o, w_out, post_w, geom):
    tm, rows = geom.tm, geom.rows
    wd = C_HEADS * C_HD
    return pl.pallas_call(
        functools.partial(_mixout_odd_kernel, geom=geom),
        out_shape=jax.ShapeDtypeStruct((rows, D_MODEL), F32),
        grid=(rows // tm,),
        in_specs=[pl.BlockSpec((tm, D_MODEL), lambda i: (i, 0)), pl.BlockSpec((tm, wd), lambda i: (i, 0)),
                  _const_spec((wd, D_MODEL)), _const_spec((1, D_MODEL))],
        out_specs=pl.BlockSpec((tm, D_MODEL), lambda i: (i, 0)),
        compiler_params=_params(),
        name="mixout_odd",
    )(h, o, w_out, post_w.reshape(1, D_MODEL))


def _qkv_kernel(x_ref, nw_ref, w_ref, cos_ref, sin_ref, qn_ref, kn_ref, q_out, k_out, v_out, xn_ref):
    xn_ref[...] = _rms(x_ref[...], nw_ref[...]).astype(BF16)
    cos, sin = cos_ref[...], sin_ref[...]

    def rope(y, yp, n_ref, scale):
        rs = lax.rsqrt(jnp.mean(y * y, axis=-1, keepdims=True) + NORM_EPS) * scale
        return ((y * n_ref[0:1]) * cos + (yp * n_ref[1:2]) * sin) * rs

    nq = C_HEADS * C_HD
    nk = C_KV_HEADS * C_HD
    per = 512 // C_HD
    for c in range(nq // 512):
        y = jnp.dot(xn_ref[...], w_ref[:, c * 512:(c + 1) * 512], preferred_element_type=F32)
        yp = jnp.dot(xn_ref[...], w_ref[:, nq + 2 * nk + c * 512:nq + 2 * nk + (c + 1) * 512],
                     preferred_element_type=F32)
        for hd in range(per):
            hs = slice((c * per + hd) * C_HD, (c * per + hd + 1) * C_HD)
            ys = slice(hd * C_HD, (hd + 1) * C_HD)
            q_out[:, hs] = rope(y[:, ys], yp[:, ys], qn_ref, C_HD ** -0.5 * LOG2E).astype(BF16)
    y = jnp.dot(xn_ref[...], w_ref[:, nq:nq + 2 * nk], preferred_element_type=F32)
    yp = jnp.dot(xn_ref[...], w_ref[:, 2 * nq + 2 * nk:], preferred_element_type=F32)
    for hd in range(C_KV_HEADS):
        hs = slice(hd * C_HD, (hd + 1) * C_HD)
        k_out[:, hs] = rope(y[:, hs], yp[:, hs], kn_ref, 1.0).astype(BF16)
        v_out[:, 2 * hd * C_HD:(2 * hd + 1) * C_HD] = y[:, nk + hd * C_HD:nk + (hd + 1) * C_HD].astype(BF16)
        v_out[:, (2 * hd + 1) * C_HD:(2 * hd + 2) * C_HD] = jnp.ones((x_ref.shape[0], C_HD), BF16)


def _qkv_proj(h, norm_w, w_qkv, cos_t, sin_t, q_norm, k_norm, geom):
    tm, rows = geom.tm, geom.rows
    nq, nk = C_HEADS * C_HD, C_KV_HEADS * C_HD
    lane = np.arange(C_HD)
    partner = np.where(lane % (C_HD // 2) < C_HD // 4, lane + C_HD // 4, lane - C_HD // 4)
    cols = np.concatenate([hd * C_HD + partner for hd in range(C_HEADS + C_KV_HEADS)])
    w_bf16 = jnp.concatenate([w_qkv, w_qkv[:, cols]], axis=1).astype(BF16)
    q_norm = jnp.stack([q_norm, q_norm[partner]])
    k_norm = jnp.stack([k_norm, k_norm[partner]])
    return pl.pallas_call(
        _qkv_kernel,
        out_shape=[jax.ShapeDtypeStruct((rows, nq), BF16), jax.ShapeDtypeStruct((rows, nk), BF16),
                   jax.ShapeDtypeStruct((rows, 2 * nk), BF16)],
        grid=(rows // tm,),
        in_specs=[pl.BlockSpec((tm, D_MODEL), lambda i: (i, 0)), _const_spec((1, D_MODEL)),
                  _const_spec((D_MODEL, 2 * nq + 3 * nk)),
                  pl.BlockSpec((tm, C_HD), lambda i: (i, 0)), pl.BlockSpec((tm, C_HD), lambda i: (i, 0)),
                  _const_spec((2, C_HD)), _const_spec((2, C_HD))],
        out_specs=[pl.BlockSpec((tm, nq), lambda i: (i, 0)), pl.BlockSpec((tm, nk), lambda i: (i, 0)),
                   pl.BlockSpec((tm, 2 * nk), lambda i: (i, 0))],
        scratch_shapes=[pltpu.VMEM((tm, D_MODEL), BF16)],
        compiler_params=_params(),
        name="qkv_proj",
    )(h, norm_w.reshape(1, D_MODEL), w_bf16, cos_t, sin_t, q_norm, k_norm)


def _rope_tables(geom):
    nf = C_HD // 4
    inv = ROPE_THETA ** (-jnp.arange(nf, dtype=F32) / nf)
    cos_parts, sin_parts = [], []
    for g in geom.groups:
        t = jnp.arange(g.tp, dtype=jnp.int32) - (g.fv + N_META)
        is_meta = t < 0
        row = jnp.where(is_meta, -1, t // GRID_W).astype(F32)
        col = jnp.where(is_meta, t + N_META, t % GRID_W).astype(F32)
        ang_r, ang_c = row[:, None] * inv, col[:, None] * inv
        cos = jnp.concatenate([jnp.cos(ang_r)] * 2 + [jnp.cos(ang_c)] * 2, axis=1)
        sin = jnp.concatenate([-jnp.sin(ang_r), jnp.sin(ang_r), -jnp.sin(ang_c), jnp.sin(ang_c)], axis=1)
        cos_parts.append(jnp.tile(cos, (g.nb, 1)))
        sin_parts.append(jnp.tile(sin, (g.nb, 1)))
    return jnp.concatenate(cos_parts, axis=0), jnp.concatenate(sin_parts, axis=0)


def _attn_kernel(*refs, nsub, bounds, fv):
    (q_ref, k_ref, v_ref), (o_ref, s_ref, rm_ref, m_ref, acc_ref) = refs[:3], refs[-5:]
    tq = SEQ_BLK
    nc = len(bounds)
    assert nc % 2 == 0

    def q_rows(qi):
        return pl.ds(pl.multiple_of(qi * tq, tq), tq)

    def produce(qi, c, slot):
        start, size = bounds[c]
        q = jnp.concatenate([q_ref[q_rows(qi), g * C_HD:(g + 1) * C_HD] for g in range(C_GROUP)], axis=0)
        s = lax.dot_general(q, k_ref[start:start + size, :], (((1,), (1,)), ((), ())),
                            preferred_element_type=F32)
        if start < fv:
            key = start + lax.broadcasted_iota(jnp.int32, (1, size), 1)
            s = jnp.where(key >= fv, s, -1e30)
        s_ref[slot, :, :size] = s
        rm_ref[slot] = jnp.max(s, axis=-1, keepdims=True)

    def consume(c, slot):
        start, size = bounds[c]
        m_old = m_ref[...]
        m_new = jnp.maximum(m_old, rm_ref[slot])
        p = jnp.exp2(s_ref[slot, :, :size] - m_new).astype(BF16)
        acc_ref[...] = jnp.exp2(m_old - m_new) * acc_ref[...] + jnp.dot(p, v_ref[start:start + size, :],
                                                                        preferred_element_type=F32)
        m_ref[...] = m_new

    produce(0, 0, 0)

    def body(qi, carry):
        m_ref[...] = jnp.full_like(m_ref, -1e30)
        acc_ref[...] = jnp.zeros_like(acc_ref)
        for c in range(nc):
            if c + 1 < nc:
                produce(qi, c + 1, (c + 1) % 2)
            else:
                produce(jnp.minimum(qi + 1, nsub - 1), 0, 0)
            consume(c, c % 2)
        acc = acc_ref[...]
        out = acc[:, :C_HD] / acc[:, C_HD:]
        for g in range(C_GROUP):
            o_ref[q_rows(qi), g * C_HD:(g + 1) * C_HD] = out[g * tq:(g + 1) * tq]
        return carry

    lax.fori_loop(0, nsub, body, 0)


def _key_chunks(tp, fv):
    lo = fv // LANE
    blocks = tp // LANE - lo
    per = ATTN_MAX_TK // LANE
    nc = -(-blocks // per)
    nc += nc % 2
    pairs, single = divmod(blocks, 2)
    sizes = [2 * (pairs // nc + (1 if i < pairs % nc else 0)) for i in range(nc)]
    sizes[-1] += single
    if min(sizes) == 0:
        sizes = [blocks // nc + (1 if i < blocks % nc else 0) for i in range(nc)]
    assert min(sizes) > 0 and max(sizes) <= per
    starts = np.cumsum([lo] + sizes[:-1])
    return tuple((int(s) * LANE, int(z) * LANE) for s, z in zip(starts, sizes))


def _attention(qn, kn, vn, geom):
    out = None
    for g in geom.groups:
        bounds = _key_chunks(g.tp, g.fv)
        nblk = g.tp // SEQ_BLK
        nsub = max(d for d in range(1, ATTN_MAX_SUB + 1) if nblk % d == 0)
        tq, gw, rows = nsub * SEQ_BLK, C_GROUP * C_HD, C_GROUP * SEQ_BLK
        al = lambda x: pl.multiple_of(x, LANE)
        q_spec = pl.BlockSpec((pl.Element(tq), pl.Element(gw)),
                              lambda b, h, i, g=g, tq=tq: (al(g.start + b * g.tp + i * tq), al(h * gw)))
        args = [qn, kn, vn]
        in_specs = [q_spec,
                    pl.BlockSpec((pl.Element(g.tp), pl.Element(C_HD)),
                                 lambda b, h, i, g=g: (al(g.start + b * g.tp), al(h * C_HD))),
                    pl.BlockSpec((pl.Element(g.tp), pl.Element(2 * C_HD)),
                                 lambda b, h, i, g=g: (al(g.start + b * g.tp), al(h * 2 * C_HD)))]
        if out is not None:
            args.append(out)
            in_specs.append(pl.BlockSpec(memory_space=pl.ANY))
        out = pl.pallas_call(
            functools.partial(_attn_kernel, nsub=nsub, bounds=bounds, fv=g.fv),
            out_shape=jax.ShapeDtypeStruct((geom.rows, C_HEADS * C_HD), F32),
            grid=(g.nb, C_KV_HEADS, g.tp // tq),
            in_specs=in_specs,
            out_specs=q_spec,
            scratch_shapes=[pltpu.VMEM((2, rows, max(z for _, z in bounds)), F32), pltpu.VMEM((2, rows, 1), F32),
                            pltpu.VMEM((rows, 1), F32), pltpu.VMEM((rows, 2 * C_HD), F32)],
            input_output_aliases={3: 0} if len(args) == 4 else {},
            compiler_params=_params(("parallel", "parallel", "parallel")),
            name="attention",
        )(*args)
    return out


def _ffn_kernel(x_ref, xp_ref, xn_ref, pre_ref, wu_ref, wg_ref, cw_ref, cb_ref, wo_ref, post_ref,
                o_ref, xs_ref, act_ref, acc_ref, *, tm):
    i = pl.program_id(0)
    xs_ref[0:HALO] = jnp.where(i > 0, _rms(xp_ref[...], pre_ref[...]), 0.0).astype(BF16)
    xs_ref[HALO:HALO + tm] = _rms(x_ref[...], pre_ref[...]).astype(BF16)
    xs_ref[HALO + tm:2 * HALO + tm] = jnp.where(i < pl.num_programs(0) - 1,
                                                _rms(xn_ref[...], pre_ref[...]), 0.0).astype(BF16)
    half = FFN_CONV // 2
    n_chunks = D_FF // FF_CHUNK
    for c in range(n_chunks):
        cs = slice(c * FF_CHUNK, (c + 1) * FF_CHUNK)
        u = jnp.dot(xs_ref[HALO:HALO + tm], wu_ref[:, cs], preferred_element_type=F32)
        g_win = jnp.dot(xs_ref[...], wg_ref[:, cs], preferred_element_type=F32)
        gate = cb_ref[0:1, cs]
        for j in range(FFN_CONV):
            gate = gate + _shifted_rows(g_win, j - half, tm) * cw_ref[j:j + 1, cs]
        th = jnp.tanh(gate * (GELU_C0 + GELU_C0 * 0.044715 * (gate * gate)))
        hu = 0.5 * gate * u
        act_ref[:, cs] = (hu + hu * th).astype(BF16)
        if c % FF_GROUP == FF_GROUP - 1 or c == n_chunks - 1:
            ks = slice((c // FF_GROUP) * FF_GROUP * FF_CHUNK, (c + 1) * FF_CHUNK)
            part = jnp.dot(act_ref[:, ks], wo_ref[ks, :], preferred_element_type=F32)
            if c < FF_GROUP:
                acc_ref[...] = part
            else:
                acc_ref[...] += part
    o_ref[...] = x_ref[...] + _rms(acc_ref[...], post_ref[...])


def _conv_ffn(h, pre_w, w_in, conv_w, conv_b, w_out, post_w, geom):
    tm, rows = geom.tm, geom.rows
    assert D_FF % FF_CHUNK == 0
    wu = w_in[:, :D_FF].astype(BF16)
    wg = w_in[:, D_FF:].astype(BF16)
    return pl.pallas_call(
        functools.partial(_ffn_kernel, tm=tm),
        out_shape=jax.ShapeDtypeStruct((rows, D_MODEL), F32),
        grid=(rows // tm,),
        in_specs=_halo_specs(tm, D_MODEL, rows) + [
            _const_spec((1, D_MODEL)), _const_spec((D_MODEL, D_FF)), _const_spec((D_MODEL, D_FF)),
            _const_spec((FFN_CONV, D_FF)), _const_spec((1, D_FF)), _const_spec((D_FF, D_MODEL)),
            _const_spec((1, D_MODEL))],
        out_specs=pl.BlockSpec((tm, D_MODEL), lambda i: (i, 0)),
        scratch_shapes=[pltpu.VMEM((tm + 2 * HALO, D_MODEL), BF16), pltpu.VMEM((tm, D_FF), BF16),
                        pltpu.VMEM((tm, D_MODEL), F32)],
        compiler_params=_params(),
        name="conv_ffn",
    )(h, h, h, pre_w.reshape(1, D_MODEL), wu, wg, conv_w, conv_b.reshape(1, D_FF),
      w_out.astype(BF16), post_w.reshape(1, D_MODEL))


def _lead_kernel(meta_ref, *rest):
    o_ref = rest[-1]
    fv = o_ref.shape[0] - N_META
    o_ref[0:fv] = jnp.zeros((fv, D_MODEL), o_ref.dtype)
    o_ref[fv:] = meta_ref[...].astype(o_ref.dtype)


def _rows_in_kernel(x_ref, _, o_ref):
    o_ref[...] = x_ref[0]


def _rows_out_kernel(h_ref, o_ref):
    o_ref[0] = h_ref[...]


def _copy_block(n):
    return max(b for b in (2048, 1024, 512, 256, 128) if n % b == 0)


def _pack_rows(xs, meta, geom):
    al = lambda v: pl.multiple_of(v, LANE)
    shape = jax.ShapeDtypeStruct((geom.rows, D_MODEL), xs[0].dtype)
    h = None
    for x, g in zip(xs, geom.groups):
        pad = g.tp - g.n
        args = [meta] if h is None else [meta, h]
        h = pl.pallas_call(
            _lead_kernel, out_shape=shape, grid=(g.nb,),
            in_specs=[_const_spec((N_META, D_MODEL))] + [pl.BlockSpec(memory_space=pl.ANY)] * (len(args) - 1),
            out_specs=pl.BlockSpec((pl.Element(pad), pl.Element(D_MODEL)),
                                   lambda b, g=g: (al(g.start + b * g.tp), 0)),
            input_output_aliases={1: 0} if len(args) == 2 else {},
            compiler_params=_params(), name="pack_lead",
        )(*args)
        blk = _copy_block(g.n)
        h = pl.pallas_call(
            _rows_in_kernel, out_shape=shape, grid=(g.nb, g.n // blk),
            in_specs=[pl.BlockSpec((1, blk, D_MODEL), lambda b, j: (b, j, 0)), pl.BlockSpec(memory_space=pl.ANY)],
            out_specs=pl.BlockSpec((pl.Element(blk), pl.Element(D_MODEL)),
                                   lambda b, j, g=g, pad=pad, blk=blk: (al(g.start + b * g.tp + pad + j * blk), 0)),
            input_output_aliases={1: 0},
            compiler_params=_params(("parallel", "parallel")), name="pack_rows",
        )(x, h)
    return h


def _unpack_rows(h, geom):
    al = lambda v: pl.multiple_of(v, LANE)
    outs = []
    for g in geom.groups:
        pad = g.tp - g.n
        blk = _copy_block(g.n)
        outs.append(pl.pallas_call(
            _rows_out_kernel, out_shape=jax.ShapeDtypeStruct((g.nb, g.n, D_MODEL), h.dtype),
            grid=(g.nb, g.n // blk),
            in_specs=[pl.BlockSpec((pl.Element(blk), pl.Element(D_MODEL)),
                                   lambda b, j, g=g, pad=pad, blk=blk: (al(g.start + b * g.tp + pad + j * blk), 0))],
            out_specs=pl.BlockSpec((1, blk, D_MODEL), lambda b, j: (b, j, 0)),
            compiler_params=_params(("parallel", "parallel")), name="unpack_rows",
        )(h))
    return tuple(outs)


def _even_layer(h, j, geom, tables, ones, p):
    aw, bw = A_WIDTH, B_WIDTH
    a_cols = 4 * aw + 4 * A_HEADS
    w_in = p["e_w_in"][j]
    lora = B_W_LORA + B_A_LORA + B_G_LORA
    zeros = lambda n: jnp.zeros((D_MODEL, n), F32)
    w_cat = jnp.concatenate([
        w_in[:, :3 * aw],
        w_in[:, a_cols:a_cols + 3 * bw + lora], zeros(B_LORA_PAD - lora),
        w_in[:, 4 * aw:a_cols], zeros(LANE - 4 * A_HEADS),
        w_in[:, 3 * aw:4 * aw]], axis=1).astype(BF16)
    mu = jnp.pad(p["b_shift"][j], ((0, 0), (0, B_LORA_PAD - lora)))
    lora_rows = lambda m, off: jnp.pad(m, ((off, B_LORA_PAD - off - m.shape[0]), (0, 0)))
    w2 = lora_rows(jnp.concatenate([p["b_w2"][j, 0], p["b_w2"][j, 1]], axis=1), 0)
    a2 = lora_rows(p["b_a2"][j], B_W_LORA)
    g2 = lora_rows(p["b_g2"][j], B_W_LORA + B_A_LORA)
    qn, kn, vv, gts, z, r, k, v, kkn, eta, cf, cb, bonus, gate = _even_prep(
        h, p["mix_pre_norm"][2 * j], w_cat, p["a_conv_w"][j], p["a_log"][j], p["a_dt_bias"][j],
        mu, p["b_w0"][j].reshape(1, 2 * bw), w2, p["b_a0"][j], a2, g2,
        p["b_k_k"][j], p["b_k_a"][j], p["b_r_k"][j], ones, geom)
    oaf, oab = _a_scan(qn, kn, vv, gts, geom, tables)
    ybf, ybb = _b_scan(r, k, v, kkn, eta, cf, cb, geom, tables)

    return _mixout_even(h, oaf, oab, z, ybf, ybb, bonus, gate, p["e_w_out"][j].astype(BF16),
                        p["a_out_norm"][j], p["b_ln_w"][j], p["b_ln_b"][j], p["mix_post_norm"][2 * j], ones, geom)


def _odd_layer(h, j, geom, rope, p):
    qn, kn, vn = _qkv_proj(h, p["mix_pre_norm"][2 * j + 1], p["o_w_qkv"][j].astype(BF16), rope[0], rope[1],
                           p["o_q_norm"][j], p["o_k_norm"][j], geom)
    o = _attention(qn, kn, vn, geom)
    return _mixout_odd(h, o, p["o_w_out"][j].astype(BF16), p["mix_post_norm"][2 * j + 1], geom)


def kernel(x_prompt, x_sample, meta, mix_pre_norm, mix_post_norm, ffn_pre_norm, ffn_post_norm, e_w_in, a_conv_w, a_log, a_dt_bias, a_out_norm, b_shift, b_w0, b_w2, b_a0, b_a2, b_g2, b_k_k, b_k_a, b_r_k, b_ln_w, b_ln_b, e_w_out, o_w_qkv, o_q_norm, o_k_norm, o_w_out, f_w_in, f_conv_w, f_conv_b, f_w_out):
    p = dict(mix_pre_norm=mix_pre_norm, mix_post_norm=mix_post_norm, e_w_in=e_w_in, a_conv_w=a_conv_w,
             a_log=a_log, a_dt_bias=a_dt_bias, a_out_norm=a_out_norm, b_shift=b_shift, b_w0=b_w0, b_w2=b_w2,
             b_a0=b_a0, b_a2=b_a2, b_g2=b_g2, b_k_k=b_k_k, b_k_a=b_k_a, b_r_k=b_r_k, b_ln_w=b_ln_w,
             b_ln_b=b_ln_b, e_w_out=e_w_out, o_w_qkv=o_w_qkv, o_q_norm=o_q_norm, o_k_norm=o_k_norm,
             o_w_out=o_w_out)
    xs = (x_prompt, x_sample)
    geom = _choose_geom([(x.shape[0], x.shape[1]) for x in xs])
    tables = _scan_tables(geom)
    rope = _rope_tables(geom)
    seg = np.arange(B_WIDTH) // B_HD
    ones = jnp.asarray(seg[:, None] == seg[None, :], BF16)

    h = _pack_rows(xs, meta, geom)
    for i in range(DEPTH):
        h = _even_layer(h, i // 2, geom, tables, ones, p) if i % 2 == 0 else _odd_layer(h, i // 2, geom, rope, p)
        h = _conv_ffn(h, ffn_pre_norm[i], f_w_in[i], f_conv_w[i], f_conv_b[i], f_w_out[i], ffn_post_norm[i], geom)
    return _unpack_rows(h, geom)
```

```python
import functools
import itertools
from typing import NamedTuple

import numpy as np
import jax
import jax.numpy as jnp
from jax import lax
from jax.experimental import pallas as pl
from jax.experimental.pallas import tpu as pltpu

F32 = jnp.float32
BF16 = jnp.bfloat16

D_MODEL = 1024
DEPTH = 2
N_META = 16
GRID_W = 64
NORM_EPS = 1e-6

A_HEADS = 4
A_DK = 128
A_DV = 128
A_CONV = 5
A_WIDTH = A_HEADS * A_DV

B_HEADS = 8
B_HD = 64
B_WIDTH = B_HEADS * B_HD
B_W_LORA = 32
B_A_LORA = 32
B_G_LORA = 96
B_GN_EPS = 64e-5
B_LORA_PAD = 256

C_HEADS = 8
C_KV_HEADS = 2
C_GROUP = C_HEADS // C_KV_HEADS
C_HD = 128
ROPE_THETA = 10000.0

D_FF = 2816
FFN_CONV = 3

LANE = 128
CHUNK = 64
SEQ_BLK = 128
HALO = 16
FF_CHUNK = 256
FF_GROUP = 4
ATTN_MAX_TK = 1792
ATTN_MAX_SUB = 17
LOG2E = 1.4426950408889634
GELU_C0 = 0.7978845608028654
VMEM_LIMIT = 56 * 1024 * 1024


class _Group(NamedTuple):
    start: int
    nb: int
    n: int
    tp: int
    fv: int


class _Geom(NamedTuple):
    groups: tuple
    rows: int
    tm: int


def _choose_geom(shapes):
    tm = 512 if min(n for _, n in shapes) >= 512 else 128
    best = None
    for pads in itertools.product((128, 256, 384, 512), repeat=len(shapes)):
        start, ok, groups = 0, True, []
        for (nb, n), p in zip(shapes, pads):
            ok = ok and start % tm == 0
            groups.append(_Group(start, nb, n, n + p, p - N_META))
            start += nb * (n + p)
        if ok and start % tm == 0 and (best is None or start < best.rows):
            best = _Geom(tuple(groups), start, tm)
    assert best is not None
    return best


def _row_valid(geom, base, tm):
    start = jnp.int32(geom.groups[0].start)
    tp = jnp.int32(geom.groups[0].tp)
    fv = jnp.int32(geom.groups[0].fv)
    for g in geom.groups[1:]:
        assert tm <= g.tp
        inside = base >= g.start
        start = jnp.where(inside, g.start, start)
        tp = jnp.where(inside, g.tp, tp)
        fv = jnp.where(inside, g.fv, fv)
    pos = lax.rem(base - start, tp) + lax.broadcasted_iota(jnp.int32, (tm, 1), 0)
    pos = jnp.where(pos >= tp, pos - tp, pos)
    return pos >= fv


def _rms(x, w, eps=NORM_EPS):
    return x * lax.rsqrt(jnp.mean(x * x, axis=-1, keepdims=True) + eps) * w


def _sigmoid(x):
    return 1.0 / (1.0 + jnp.exp(-x))


def _softplus(x):
    return jnp.maximum(x, 0.0) + jnp.log1p(jnp.exp(-jnp.abs(x)))


def _dot(a, b):
    return jnp.dot(a.astype(BF16), b.astype(BF16), preferred_element_type=F32)


def _dot_nt(a, b):
    return lax.dot_general(a.astype(BF16), b.astype(BF16), (((1,), (1,)), ((), ())),
                           preferred_element_type=F32)


def _seg_sum(x, ones_ref, terms):
    acc = None
    rem = x
    for t in range(terms):
        piece = rem.astype(BF16)
        part = jnp.dot(piece, ones_ref[...], preferred_element_type=F32)
        acc = part if acc is None else acc + part
        if t + 1 < terms:
            rem = rem - piece.astype(F32)
    return acc


def _chunk_cumsum(x, reverse):
    tm = x.shape[0]
    pos = lax.broadcasted_iota(jnp.int32, (tm, 1), 0) & (CHUNK - 1)
    s = 1
    while s < CHUNK:
        if reverse:
            x = x + jnp.where(pos < CHUNK - s, pltpu.roll(x, tm - s, axis=0), 0.0)
        else:
            x = x + jnp.where(pos >= s, pltpu.roll(x, s, axis=0), 0.0)
        s *= 2
    return x


def _neumann_inverses(n_mats, eye):
    n = eye.shape[0]
    prods = [eye + m for m in n_mats]
    powers = [_dot(m, m) for m in n_mats]
    s = 4
    while s < CHUNK:
        both = [_dot(jnp.concatenate([pr, pw], axis=0), pw) for pr, pw in zip(prods, powers)]
        prods = [pr + b[:n] for pr, b in zip(prods, both)]
        powers = [b[n:] for b in both]
        s *= 2
    return [pr + _dot(pr, pw) for pr, pw in zip(prods, powers)]


def _shifted_rows(win, d, tm):
    if d == 0:
        return win[HALO:HALO + tm]
    return pltpu.roll(win, (-d) % win.shape[0], axis=0)[HALO:HALO + tm]


def _halo_specs(tm, width, rows):
    per = tm // HALO
    last = rows // HALO - 1
    return [
        pl.BlockSpec((tm, width), lambda i: (i, 0)),
        pl.BlockSpec((HALO, width), lambda i: (jnp.maximum(i * per - 1, 0), 0)),
        pl.BlockSpec((HALO, width), lambda i: (jnp.minimum((i + 1) * per, last), 0)),
    ]


def _const_spec(shape):
    return pl.BlockSpec(shape, lambda *_: (0,) * len(shape))


def _params(sem=("parallel",)):
    return pltpu.CompilerParams(dimension_semantics=sem, vmem_limit_bytes=VMEM_LIMIT)


def _a_prep_body(xw_ref, al, cw_ref, alog_ref, dtb_ref, q_ref, k_ref, v_ref, go_ref, valid, tm):
    half = A_CONV // 2
    for c, o_ref in enumerate((q_ref, k_ref, v_ref)):
        cs = slice(c * A_WIDTH, (c + 1) * A_WIDTH)
        win = xw_ref[:, cs]
        acc = _shifted_rows(win, -half, tm) * cw_ref[0:1, cs]
        for j in range(1, A_CONV):
            acc = acc + _shifted_rows(win, j - half, tm) * cw_ref[j:j + 1, cs]
        y = jnp.where(valid, acc * _sigmoid(acc), 0.0)
        for hd in range(A_HEADS):
            hs = slice(hd * A_DK, (hd + 1) * A_DK)
            yh = y[:, hs]
            if c < 2:
                yh = yh * lax.rsqrt(jnp.sum(yh * yh, axis=-1, keepdims=True) + 1e-6)
                if c == 0:
                    yh = yh * (A_DK ** -0.5)
            o_ref[:, hs] = yh.astype(o_ref.dtype)
    lane = lax.broadcasted_iota(jnp.int32, (1, LANE), 1)
    gval = jnp.where(valid, -jnp.exp(alog_ref[...]) * _softplus(al + dtb_ref[...]), 0.0)
    beta = jnp.where(valid, _sigmoid(al), 0.0)
    cum_f = _chunk_cumsum(gval, False)
    cum_b = _chunk_cumsum(gval, True)
    go_ref[...] = jnp.where(lane < A_HEADS, cum_f,
                            jnp.where(lane < 2 * A_HEADS, cum_b,
                                      jnp.where(lane < 4 * A_HEADS, beta, 0.0)))


def _a_scan_kernel(fi_ref, bi_ref, fr_ref,
                   qf, kf, vf, gf, gtf, qb, kb, vb, gb, gtb, of, ob, sf, sb):
    del fi_ref, bi_ref
    @pl.when(fr_ref[pl.program_id(0)] == 1)
    def _():
        sf[...] = jnp.zeros_like(sf)
        sb[...] = jnp.zeros_like(sb)
    r = lax.broadcasted_iota(jnp.int32, (CHUNK, CHUNK), 0)
    c = lax.broadcasted_iota(jnp.int32, (CHUNK, CHUNK), 1)
    eye = jnp.where(r == c, 1.0, 0.0)
    nc = SEQ_BLK // CHUNK
    dirs = ((qf, kf, vf, gf, gtf, of, sf, False, r >= c, r > c),
            (qb, kb, vb, gb, gtb, ob, sb, True, r <= c, r < c))

    probs = []
    for d, (q_ref, k_ref, v_ref, g_ref, gt_ref, _, _, reverse, tri, strict) in enumerate(dirs):
        edge = 0 if reverse else CHUNK - 1
        for step in range(nc):
            jj = nc - 1 - step if reverse else step
            rows = slice(jj * CHUNK, (jj + 1) * CHUNK)
            for hd in range(A_HEADS):
                hs = slice(hd * A_DK, (hd + 1) * A_DK)
                col = hd + (A_HEADS if reverse else 0)
                q, k, v = (ref[rows, hs].astype(F32) for ref in (q_ref, k_ref, v_ref))
                g_col = g_ref[rows, col:col + 1]
                b_col = g_ref[rows, 2 * A_HEADS + col:2 * A_HEADS + col + 1]
                g_row = gt_ref[jj, col:col + 1, :]
                decay = jnp.where(tri, jnp.exp(jnp.where(tri, g_col - g_row, 0.0)), 0.0)
                kb_ = k * b_col
                e_g = jnp.exp(g_col)
                g_last = g_col[edge:edge + 1]
                probs.append(dict(d=d, step=step, hd=hd, rows=rows, hs=hs, tri=tri, strict=strict, decay=decay,
                                  k=k, kbq=jnp.concatenate([kb_, q], axis=0),
                                  rhs=jnp.concatenate([v * b_col, kb_ * e_g], axis=1),
                                  qg=q * e_g, kg_t=(k * jnp.exp(g_last - g_col)).T, g_end=jnp.exp(g_last)))
    for p in probs:
        kq = _dot_nt(p["kbq"], p["k"])
        p["l"] = jnp.where(p["strict"], kq[:CHUNK] * p["decay"], 0.0)
        p["attn"] = jnp.where(p["tri"], kq[CHUNK:] * p["decay"], 0.0)
    t_invs = _neumann_inverses([-p["l"] for p in probs], eye)
    for p, t_inv in zip(probs, t_invs):
        p["uw"] = _dot(t_inv, p["rhs"])

    states = {(d, hd): dirs[d][6][hd] for d in range(2) for hd in range(A_HEADS)}
    for step in range(nc):
        cur = [p for p in probs if p["step"] == step]
        v_news = [p["uw"][:, :A_DV] - _dot(p["uw"][:, A_DV:], states[p["d"], p["hd"]]) for p in cur]
        for p, v_new in zip(cur, v_news):
            state = states[p["d"], p["hd"]]
            dirs[p["d"]][5][p["rows"], p["hs"]] = _dot(jnp.concatenate([p["qg"], p["attn"]], axis=1),
                                                      jnp.concatenate([state, v_new], axis=0))
        for p, v_new in zip(cur, v_news):
            key = (p["d"], p["hd"])
            states[key] = states[key] * p["g_end"] + _dot(p["kg_t"], v_new)
    for (d, hd), state in states.items():
        dirs[d][6][hd] = state


def _scan_tables(geom):
    fi, bi, fr = [], [], []
    for g in geom.groups:
        nblk = g.tp // SEQ_BLK
        for b in range(g.nb):
            base = (g.start + b * g.tp) // SEQ_BLK
            for i in range(nblk):
                fi.append(base + i)
                bi.append(base + nblk - 1 - i)
                fr.append(1 if i == 0 else 0)
    mk = lambda v: jnp.asarray(np.asarray(v, np.int32))
    return mk(fi), mk(bi), mk(fr)


def _a_scan(qn, kn, vv, gts, geom, tables):
    rows = geom.rows
    nc = SEQ_BLK // CHUNK
    gts_t = gts[:, :4 * A_HEADS].reshape(rows // CHUNK, CHUNK, 4 * A_HEADS).transpose(0, 2, 1)
    wide_f = pl.BlockSpec((SEQ_BLK, A_WIDTH), lambda s, fi, bi, fr: (fi[s], 0))
    wide_b = pl.BlockSpec((SEQ_BLK, A_WIDTH), lambda s, fi, bi, fr: (bi[s], 0))
    gate_f = pl.BlockSpec((SEQ_BLK, LANE), lambda s, fi, bi, fr: (fi[s], 0))
    gate_b = pl.BlockSpec((SEQ_BLK, LANE), lambda s, fi, bi, fr: (bi[s], 0))
    gt_f = pl.BlockSpec((nc, 4 * A_HEADS, CHUNK), lambda s, fi, bi, fr: (fi[s], 0, 0))
    gt_b = pl.BlockSpec((nc, 4 * A_HEADS, CHUNK), lambda s, fi, bi, fr: (bi[s], 0, 0))
    out = jax.ShapeDtypeStruct((rows, A_WIDTH), F32)
    return pl.pallas_call(
        _a_scan_kernel,
        out_shape=[out, out],
        grid_spec=pltpu.PrefetchScalarGridSpec(
            num_scalar_prefetch=3,
            grid=(tables[0].shape[0],),
            in_specs=[wide_f, wide_f, wide_f, gate_f, gt_f, wide_b, wide_b, wide_b, gate_b, gt_b],
            out_specs=[wide_f, wide_b],
            scratch_shapes=[pltpu.VMEM((A_HEADS, A_DK, A_DV), F32)] * 2),
        compiler_params=_params(("arbitrary",)),
        name="a_scan",
    )(*tables, qn, kn, vv, gts, gts_t, qn, kn, vv, gts, gts_t)


def _b_prep_body(xw_ref, lw_ref, mu_ref, w0_ref, w2_ref, a0_ref, a2_ref, g2_ref, kk_ref, ka_ref, rk_ref, ones_ref,
                 r_out, k_out, v_out, kkn_out, eta_out, cf_out, cb_out, bonus_out, gate_out, tm):
    def shifted(win_ref, cs, mu_off):
        win = win_ref[:, cs]
        cur, prev, nxt = _shifted_rows(win, 0, tm), _shifted_rows(win, -1, tm), _shifted_rows(win, 1, tm)
        ms = slice(mu_off + cs.start, mu_off + cs.stop)
        return cur + mu_ref[0:1, ms] * (prev - cur) + mu_ref[1:2, ms] * (nxt - cur)

    w = B_WIDTH
    r = shifted(xw_ref, slice(0, w), 0)
    k = shifted(xw_ref, slice(w, 2 * w), 0)
    v = shifted(xw_ref, slice(2 * w, 3 * w), 0)
    lo = shifted(lw_ref, slice(0, B_LORA_PAD), 3 * w)
    wl = _dot(jnp.tanh(lo), w2_ref[...]) + w0_ref[...]
    log_decay = -float(np.exp(-0.5)) * _sigmoid(wl)
    cf_out[...] = _chunk_cumsum(log_decay[:, :w], False)
    cb_out[...] = _chunk_cumsum(log_decay[:, w:], True)
    eta = _sigmoid(a0_ref[...] + _dot(lo, a2_ref[...]))
    gate_out[...] = _dot(_sigmoid(lo), g2_ref[...])
    kx = k * kk_ref[...]
    kkn_out[...] = (kx * lax.rsqrt(_seg_sum(kx * kx, ones_ref, 2) + 1e-6)).astype(kkn_out.dtype)
    k = k * (1.0 + (eta - 1.0) * ka_ref[...])
    bonus_out[...] = _seg_sum(r * k * rk_ref[...], ones_ref, 2) * v
    r_out[...] = r.astype(r_out.dtype)
    k_out[...] = k.astype(k_out.dtype)
    v_out[...] = v.astype(v_out.dtype)
    eta_out[...] = eta.astype(eta_out.dtype)


def _even_prep_kernel(h_ref, hp_ref, hn_ref, nw_ref, w_ref, cw_ref, alog_ref, dtb_ref,
                      mu_ref, w0_ref, w2_ref, a0_ref, a2_ref, g2_ref, kk_ref, ka_ref, rk_ref, ones_ref,
                      q_out, ka_out, va_out, go_out, z_out,
                      r_out, kb_out, vb_out, kkn_out, eta_out, cf_out, cb_out, bonus_out, gate_out,
                      xn_ref, aw_ref, bw_ref, lw_ref, *, geom):
    tm = geom.tm
    i = pl.program_id(0)
    valid = _row_valid(geom, i * tm, tm)
    xn_ref[0:HALO] = jnp.where(i > 0, _rms(hp_ref[...], nw_ref[...]), 0.0).astype(BF16)
    xn_ref[HALO:HALO + tm] = _rms(h_ref[...], nw_ref[...]).astype(BF16)
    xn_ref[HALO + tm:2 * HALO + tm] = jnp.where(i < pl.num_programs(0) - 1,
                                                _rms(hn_ref[...], nw_ref[...]), 0.0).astype(BF16)
    aw, bw = 3 * A_WIDTH, 3 * B_WIDTH
    c0 = 0
    for win_ref, wd in ((aw_ref, aw), (bw_ref, bw), (lw_ref, B_LORA_PAD)):
        for s in range(0, wd, 512):
            e = min(s + 512, wd)
            win_ref[:, s:e] = jnp.dot(xn_ref[...], w_ref[:, c0 + s:c0 + e], preferred_element_type=F32)
        c0 += wd
    centre = xn_ref[HALO:HALO + tm]
    gates = jnp.dot(centre, w_ref[:, c0:c0 + LANE], preferred_element_type=F32)
    z_out[...] = jnp.dot(centre, w_ref[:, c0 + LANE:], preferred_element_type=F32).astype(z_out.dtype)
    _a_prep_body(aw_ref, gates, cw_ref, alog_ref, dtb_ref, q_out, ka_out, va_out, go_out, valid, tm)
    _b_prep_body(bw_ref, lw_ref, mu_ref, w0_ref, w2_ref, a0_ref, a2_ref, g2_ref, kk_ref, ka_ref, rk_ref, ones_ref,
                 r_out, kb_out, vb_out, kkn_out, eta_out, cf_out, cb_out, bonus_out, gate_out, tm)


def _even_prep(h, norm_w, w_bf16, conv_w, a_log, dt_bias, mu, w0, w2, a0, a2, g2, k_k, k_a, r_k, ones, geom):
    tm, rows = geom.tm, geom.rows
    aw, bw = 3 * A_WIDTH, 3 * B_WIDTH
    row = lambda v: v.reshape(1, -1).astype(F32)
    pad = lambda v: jnp.pad(row(v), ((0, 0), (0, LANE - v.size)))
    narrow = lambda wd: jax.ShapeDtypeStruct((rows, wd), BF16)
    wide = lambda wd: jax.ShapeDtypeStruct((rows, wd), F32)
    blk = lambda wd: pl.BlockSpec((tm, wd), lambda i: (i, 0))
    win = tm + 2 * HALO
    return pl.pallas_call(
        functools.partial(_even_prep_kernel, geom=geom),
        out_shape=[narrow(A_WIDTH)] * 3 + [wide(LANE), narrow(A_WIDTH)] + [narrow(B_WIDTH)] * 5 + [wide(B_WIDTH)] * 4,
        grid=(rows // tm,),
        in_specs=_halo_specs(tm, D_MODEL, rows) + [
            _const_spec((1, D_MODEL)), _const_spec(w_bf16.shape),
            _const_spec((A_CONV, aw)), _const_spec((1, LANE)), _const_spec((1, LANE)),
            _const_spec((2, bw + B_LORA_PAD)), _const_spec((1, 2 * B_WIDTH)), _const_spec((B_LORA_PAD, 2 * B_WIDTH)),
            _const_spec((1, B_WIDTH)), _const_spec((B_LORA_PAD, B_WIDTH)), _const_spec((B_LORA_PAD, B_WIDTH)),
            _const_spec((1, B_WIDTH)), _const_spec((1, B_WIDTH)), _const_spec((1, B_WIDTH)),
            _const_spec((B_WIDTH, B_WIDTH))],
        out_specs=[blk(A_WIDTH)] * 3 + [blk(LANE), blk(A_WIDTH)] + [blk(B_WIDTH)] * 9,
        scratch_shapes=[pltpu.VMEM((win, D_MODEL), BF16), pltpu.VMEM((win, aw), F32), pltpu.VMEM((win, bw), F32),
                        pltpu.VMEM((win, B_LORA_PAD), F32)],
        compiler_params=_params(),
        name="even_prep",
    )(h, h, h, row(norm_w), w_bf16, conv_w, pad(a_log), pad(dt_bias), mu, w0, w2, row(a0), a2, g2,
      row(k_k), row(k_a), row(r_k), ones)


def _b_scan_kernel(fi_ref, bi_ref, fr_ref,
                   rf, kf, vf, kkf, ef, cf, rb, kb, vb, kkb, eb, cb, yf, yb, pf, pb):
    del fi_ref, bi_ref
    @pl.when(fr_ref[pl.program_id(0)] == 1)
    def _():
        pf[...] = jnp.zeros_like(pf)
        pb[...] = jnp.zeros_like(pb)
    n2 = 2 * CHUNK
    r = lax.broadcasted_iota(jnp.int32, (n2, n2), 0)
    c = lax.broadcasted_iota(jnp.int32, (n2, n2), 1)
    same = (r >= CHUNK) == (c >= CHUNK)
    rt, ct = r & (CHUNK - 1), c & (CHUNK - 1)
    eye = jnp.where(r == c, 1.0, 0.0)
    lane_lo = lax.broadcasted_iota(jnp.int32, (1, n2), 1) < B_HD
    row_pos = lax.broadcasted_iota(jnp.int32, (CHUNK, 1), 0)
    nc = SEQ_BLK // CHUNK
    pair_w = 2 * B_HD
    n_pairs = B_HEADS // 2
    dirs = (((rf, kf, vf, kkf, ef, cf), yf, pf, False, same & (rt > ct), same & (rt >= ct)),
            ((rb, kb, vb, kkb, eb, cb), yb, pb, True, same & (rt < ct), same & (rt <= ct)))

    def expand(x):
        return jnp.concatenate([jnp.where(lane_lo, x, 0.0), jnp.where(lane_lo, 0.0, x)], axis=0)

    probs = []
    for d, (refs, _, _, reverse, strict, incl) in enumerate(dirs):
        r_ref, k_ref, v_ref, kk_ref, eta_ref, c_ref = refs
        for step in range(nc):
            jj = nc - 1 - step if reverse else step
            rows = slice(jj * CHUNK, (jj + 1) * CHUNK)
            for pr in range(n_pairs):
                ls = slice(pr * pair_w, (pr + 1) * pair_w)
                r, k, v, kk, eta = (ref[rows, ls].astype(F32) for ref in (r_ref, k_ref, v_ref, kk_ref, eta_ref))
                c_in = c_ref[rows, ls]
                if reverse:
                    c_ex = jnp.where(row_pos < CHUNK - 1, pltpu.roll(c_in, CHUNK - 1, axis=0), 0.0)
                    c_tot = c_in[0:1]
                else:
                    c_ex = jnp.where(row_pos >= 1, pltpu.roll(c_in, 1, axis=0), 0.0)
                    c_tot = c_in[CHUNK - 1:CHUNK]
                b = kk * eta
                inv_w = jnp.exp(-c_in)
                rest_w = jnp.exp(c_tot - c_in)
                a_e = expand(-kk * jnp.exp(c_ex))
                r_e = expand(r * jnp.exp(c_in))
                probs.append(dict(
                    d=d, step=step, pr=pr, rows=rows, ls=ls, strict=strict, incl=incl, v=v,
                    a_e=a_e, r_e=r_e, v_e=expand(v),
                    lhs=jnp.concatenate([a_e, r_e], axis=0),
                    rhs=jnp.concatenate([expand(b * inv_w), expand(k * inv_w)], axis=0),
                    kd_t=jnp.concatenate([b * rest_w, k * rest_w], axis=0).T,
                    w_col=jnp.sum(jnp.where(eye > 0.0, jnp.exp(c_tot), 0.0), axis=1, keepdims=True)))
    for p in probs:
        m1 = _dot_nt(p["lhs"], p["rhs"])
        p["n_ab"] = jnp.where(p["strict"], m1[:n2, :n2], 0.0)
        p["a_ak"] = jnp.where(p["strict"], m1[:n2, n2:], 0.0)
        p["a_r"] = jnp.concatenate([jnp.where(p["incl"], m1[n2:, :n2], 0.0),
                                    jnp.where(p["incl"], m1[n2:, n2:], 0.0)], axis=1)
    t_invs = _neumann_inverses([p["n_ab"] for p in probs], eye)
    for p in probs:
        p["akv"] = _dot(p["a_ak"], p["v_e"])
    for p, t_inv in zip(probs, t_invs):
        p["x"] = _dot(t_inv, jnp.concatenate([p["akv"], p["a_e"]], axis=1))

    states = {(d, pr): dirs[d][2][pr] for d in range(2) for pr in range(n_pairs)}
    for step in range(nc):
        cur = [p for p in probs if p["step"] == step]
        u_es = [p["x"][:, :pair_w] + _dot(p["x"][:, pair_w:], states[p["d"], p["pr"]]) for p in cur]
        for p, u_e in zip(cur, u_es):
            y_e = _dot(jnp.concatenate([p["r_e"], p["a_r"]], axis=1),
                       jnp.concatenate([states[p["d"], p["pr"]], u_e, p["v_e"]], axis=0))
            dirs[p["d"]][1][p["rows"], p["ls"]] = y_e[:CHUNK] + y_e[CHUNK:]
        for p, u_e in zip(cur, u_es):
            key = (p["d"], p["pr"])
            upd = _dot(p["kd_t"], jnp.concatenate([u_e[:CHUNK] + u_e[CHUNK:], p["v"]], axis=0))
            states[key] = states[key] * p["w_col"] + jnp.where(same, upd, 0.0)
    for (d, pr), state in states.items():
        dirs[d][2][pr] = state


def _b_scan(r, k, v, kkn, eta, cf, cb, geom, tables):
    wide_f = pl.BlockSpec((SEQ_BLK, B_WIDTH), lambda s, fi, bi, fr: (fi[s], 0))
    wide_b = pl.BlockSpec((SEQ_BLK, B_WIDTH), lambda s, fi, bi, fr: (bi[s], 0))
    out = jax.ShapeDtypeStruct((geom.rows, B_WIDTH), F32)
    return pl.pallas_call(
        _b_scan_kernel,
        out_shape=[out, out],
        grid_spec=pltpu.PrefetchScalarGridSpec(
            num_scalar_prefetch=3,
            grid=(tables[0].shape[0],),
            in_specs=[wide_f] * 6 + [wide_b] * 6,
            out_specs=[wide_f, wide_b],
            scratch_shapes=[pltpu.VMEM((B_HEADS // 2, 2 * B_HD, 2 * B_HD), F32)] * 2),
        compiler_params=_params(("arbitrary",)),
        name="b_scan",
    )(*tables, r, k, v, kkn, eta, cf, r, k, v, kkn, eta, cb)


def _finish_residual(h, mix, post_w, valid):
    return jnp.where(valid, h + _rms(mix, post_w), 0.0)


def _mixout_even_kernel(h_ref, oaf_ref, oab_ref, z_ref, ybf_ref, ybb_ref, bonus_ref, gate_ref,
                        w_ref, an_ref, lnw_ref, lnb_ref, post_ref, ones_ref, o_ref, cat_ref, *, geom):
    tm = geom.tm
    valid = _row_valid(geom, pl.program_id(0) * tm, tm)
    for hd in range(A_HEADS):
        hs = slice(hd * A_DV, (hd + 1) * A_DV)
        o = oaf_ref[:, hs] + oab_ref[:, hs]
        z = z_ref[:, hs].astype(F32)
        cat_ref[:, hs] = (_rms(o, an_ref[...]) * (z * _sigmoid(z))).astype(BF16)
    y = ybf_ref[...] + ybb_ref[...]
    yc = y - _seg_sum(y, ones_ref, 3) * (1.0 / B_HD)
    var = _seg_sum(yc * yc, ones_ref, 2) * (1.0 / B_HD)
    yn = yc * lax.rsqrt(var + B_GN_EPS) * lnw_ref[...] + lnb_ref[...]
    cat_ref[:, A_WIDTH:] = ((yn + bonus_ref[...]) * gate_ref[...]).astype(BF16)
    mix = jnp.dot(cat_ref[...], w_ref[...], preferred_element_type=F32)
    o_ref[...] = _finish_residual(h_ref[...], mix, post_ref[...], valid)


def _mixout_even(h, oaf, oab, z, ybf, ybb, bonus, gate, w_out, a_norm, ln_w, ln_b, post_w, ones, geom):
    tm, rows = geom.tm, geom.rows
    row = lambda v: v.reshape(1, -1).astype(F32)
    a_blk = pl.BlockSpec((tm, A_WIDTH), lambda i: (i, 0))
    b_blk = pl.BlockSpec((tm, B_WIDTH), lambda i: (i, 0))
    return pl.pallas_call(
        functools.partial(_mixout_even_kernel, geom=geom),
        out_shape=jax.ShapeDtypeStruct((rows, D_MODEL), F32),
        grid=(rows // tm,),
        in_specs=[pl.BlockSpec((tm, D_MODEL), lambda i: (i, 0)), a_blk, a_blk,
                  a_blk,
                  b_blk, b_blk, b_blk, b_blk,
                  _const_spec((A_WIDTH + B_WIDTH, D_MODEL)), _const_spec((1, A_DV)),
                  _const_spec((1, B_WIDTH)), _const_spec((1, B_WIDTH)), _const_spec((1, D_MODEL)),
                  _const_spec((B_WIDTH, B_WIDTH))],
        out_specs=pl.BlockSpec((tm, D_MODEL), lambda i: (i, 0)),
        scratch_shapes=[pltpu.VMEM((tm, A_WIDTH + B_WIDTH), BF16)],
        compiler_params=_params(),
        name="mixout_even",
    )(h, oaf, oab, z, ybf, ybb, bonus, gate, w_out, row(a_norm), row(ln_w), row(ln_b), row(post_w), ones)


def _mixout_odd_kernel(h_ref, o_ref_in, w_ref, post_ref, o_ref, *, geom):
    tm = geom.tm
    valid = _row_valid(geom, pl.program_id(0) * tm, tm)
    mix = jnp.dot(o_ref_in[...], w_ref[...], preferred_element_type=F32)
    o_ref[...] = _finish_residual(h_ref[...], mix, post_ref[...], valid)


def _mixout_odd(h, o, w_out, post_w, geom):
    tm, rows = geom.tm, geom.rows
    wd = C_HEADS * C_HD
    return pl.pallas_call(
        functools.partial(_mixout_odd_kernel, geom=geom),
        out_shape=jax.ShapeDtypeStruct((rows, D_MODEL), F32),
        grid=(rows // tm,),
        in_specs=[pl.BlockSpec((tm, D_MODEL), lambda i: (i, 0)), pl.BlockSpec((tm, wd), lambda i: (i, 0)),
                  _const_spec((wd, D_MODEL)), _const_spec((1, D_MODEL))],
        out_specs=pl.BlockSpec((tm, D_MODEL), lambda i: (i, 0)),
        compiler_params=_params(),
        name="mixout_odd",
    )(h, o, w_out, post_w.reshape(1, D_MODEL))


def _qkv_kernel(x_ref, nw_ref, w_ref, cos_ref, sin_ref, qn_ref, kn_ref, q_out, k_out, v_out, xn_ref):
    xn_ref[...] = _rms(x_ref[...], nw_ref[...]).astype(BF16)
    cos, sin = cos_ref[...], sin_ref[...]

    def rope(y, yp, n_ref, scale):
        rs = lax.rsqrt(jnp.mean(y * y, axis=-1, keepdims=True) + NORM_EPS) * scale
        return ((y * n_ref[0:1]) * cos + (yp * n_ref[1:2]) * sin) * rs

    nq = C_HEADS * C_HD
    nk = C_KV_HEADS * C_HD
    per = 512 // C_HD
    for c in range(nq // 512):
        y = jnp.dot(xn_ref[...], w_ref[:, c * 512:(c + 1) * 512], preferred_element_type=F32)
        yp = jnp.dot(xn_ref[...], w_ref[:, nq + 2 * nk + c * 512:nq + 2 * nk + (c + 1) * 512],
                     preferred_element_type=F32)
        for hd in range(per):
            hs = slice((c * per + hd) * C_HD, (c * per + hd + 1) * C_HD)
            ys = slice(hd * C_HD, (hd + 1) * C_HD)
            q_out[:, hs] = rope(y[:, ys], yp[:, ys], qn_ref, C_HD ** -0.5 * LOG2E).astype(BF16)
    y = jnp.dot(xn_ref[...], w_ref[:, nq:nq + 2 * nk], preferred_element_type=F32)
    yp = jnp.dot(xn_ref[...], w_ref[:, 2 * nq + 2 * nk:], preferred_element_type=F32)
    for hd in range(C_KV_HEADS):
        hs = slice(hd * C_HD, (hd + 1) * C_HD)
        k_out[:, hs] = rope(y[:, hs], yp[:, hs], kn_ref, 1.0).astype(BF16)
        v_out[:, 2 * hd * C_HD:(2 * hd + 1) * C_HD] = y[:, nk + hd * C_HD:nk + (hd + 1) * C_HD].astype(BF16)
        v_out[:, (2 * hd + 1) * C_HD:(2 * hd + 2) * C_HD] = jnp.ones((x_ref.shape[0], C_HD), BF16)


def _qkv_proj(h, norm_w, w_qkv, cos_t, sin_t, q_norm, k_norm, geom):
    tm, rows = geom.tm, geom.rows
    nq, nk = C_HEADS * C_HD, C_KV_HEADS * C_HD
    lane = np.arange(C_HD)
    partner = np.where(lane % (C_HD // 2) < C_HD // 4, lane + C_HD // 4, lane - C_HD // 4)
    cols = np.concatenate([hd * C_HD + partner for hd in range(C_HEADS + C_KV_HEADS)])
    w_bf16 = jnp.concatenate([w_qkv, w_qkv[:, cols]], axis=1).astype(BF16)
    q_norm = jnp.stack([q_norm, q_norm[partner]])
    k_norm = jnp.stack([k_norm, k_norm[partner]])
    return pl.pallas_call(
        _qkv_kernel,
        out_shape=[jax.ShapeDtypeStruct((rows, nq), BF16), jax.ShapeDtypeStruct((rows, nk), BF16),
                   jax.ShapeDtypeStruct((rows, 2 * nk), BF16)],
        grid=(rows // tm,),
        in_specs=[pl.BlockSpec((tm, D_MODEL), lambda i: (i, 0)), _const_spec((1, D_MODEL)),
                  _const_spec((D_MODEL, 2 * nq + 3 * nk)),
                  pl.BlockSpec((tm, C_HD), lambda i: (i, 0)), pl.BlockSpec((tm, C_HD), lambda i: (i, 0)),
                  _const_spec((2, C_HD)), _const_spec((2, C_HD))],
        out_specs=[pl.BlockSpec((tm, nq), lambda i: (i, 0)), pl.BlockSpec((tm, nk), lambda i: (i, 0)),
                   pl.BlockSpec((tm, 2 * nk), lambda i: (i, 0))],
        scratch_shapes=[pltpu.VMEM((tm, D_MODEL), BF16)],
        compiler_params=_params(),
        name="qkv_proj",
    )(h, norm_w.reshape(1, D_MODEL), w_bf16, cos_t, sin_t, q_norm, k_norm)


def _rope_tables(geom):
    nf = C_HD // 4
    inv = ROPE_THETA ** (-jnp.arange(nf, dtype=F32) / nf)
    cos_parts, sin_parts = [], []
    for g in geom.groups:
        t = jnp.arange(g.tp, dtype=jnp.int32) - (g.fv + N_META)
        is_meta = t < 0
        row = jnp.where(is_meta, -1, t // GRID_W).astype(F32)
        col = jnp.where(is_meta, t + N_META, t % GRID_W).astype(F32)
        ang_r, ang_c = row[:, None] * inv, col[:, None] * inv
        cos = jnp.concatenate([jnp.cos(ang_r)] * 2 + [jnp.cos(ang_c)] * 2, axis=1)
        sin = jnp.concatenate([-jnp.sin(ang_r), jnp.sin(ang_r), -jnp.sin(ang_c), jnp.sin(ang_c)], axis=1)
        cos_parts.append(jnp.tile(cos, (g.nb, 1)))
        sin_parts.append(jnp.tile(sin, (g.nb, 1)))
    return jnp.concatenate(cos_parts, axis=0), jnp.concatenate(sin_parts, axis=0)


def _attn_kernel(*refs, nsub, bounds, fv):
    q_ref, k_ref, v_ref, _, o_ref, s_ref, rm_ref, m_ref, acc_ref = refs
    tq = SEQ_BLK
    nc = len(bounds)
    assert nc % 2 == 0

    def q_rows(qi):
        return pl.ds(pl.multiple_of(qi * tq, tq), tq)

    def produce(qi, c, slot):
        start, size = bounds[c]
        q = jnp.concatenate([q_ref[q_rows(qi), g * C_HD:(g + 1) * C_HD] for g in range(C_GROUP)], axis=0)
        s = lax.dot_general(q, k_ref[start:start + size, :], (((1,), (1,)), ((), ())),
                            preferred_element_type=F32)
        if start < fv:
            key = start + lax.broadcasted_iota(jnp.int32, (1, size), 1)
            s = jnp.where(key >= fv, s, -1e30)
        s_ref[slot, :, :size] = s
        rm_ref[slot] = jnp.max(s, axis=-1, keepdims=True)

    def consume(c, slot):
        start, size = bounds[c]
        m_old = m_ref[...]
        m_new = jnp.maximum(m_old, rm_ref[slot])
        p = jnp.exp2(s_ref[slot, :, :size] - m_new).astype(BF16)
        acc_ref[...] = jnp.exp2(m_old - m_new) * acc_ref[...] + jnp.dot(p, v_ref[start:start + size, :],
                                                                        preferred_element_type=F32)
        m_ref[...] = m_new

    produce(0, 0, 0)

    def body(qi, carry):
        m_ref[...] = jnp.full_like(m_ref, -1e30)
        acc_ref[...] = jnp.zeros_like(acc_ref)
        for c in range(nc):
            if c + 1 < nc:
                produce(qi, c + 1, (c + 1) % 2)
            else:
                produce(jnp.minimum(qi + 1, nsub - 1), 0, 0)
            consume(c, c % 2)
        acc = acc_ref[...]
        out = acc[:, :C_HD] / acc[:, C_HD:]
        for g in range(C_GROUP):
            o_ref[q_rows(qi), g * C_HD:(g + 1) * C_HD] = out[g * tq:(g + 1) * tq].astype(o_ref.dtype)
        return carry

    lax.fori_loop(0, nsub, body, 0)


def _key_chunks(tp, fv):
    lo = fv // LANE
    blocks = tp // LANE - lo
    per = ATTN_MAX_TK // LANE
    nc = -(-blocks // per)
    nc += nc % 2
    pairs, single = divmod(blocks, 2)
    sizes = [2 * (pairs // nc + (1 if i < pairs % nc else 0)) for i in range(nc)]
    sizes[-1] += single
    if min(sizes) == 0:
        sizes = [blocks // nc + (1 if i < blocks % nc else 0) for i in range(nc)]
    assert min(sizes) > 0 and max(sizes) <= per
    starts = np.cumsum([lo] + sizes[:-1])
    return tuple((int(s) * LANE, int(z) * LANE) for s, z in zip(starts, sizes))


def _attention(qn, kn, vn, geom):
    out = jnp.zeros((geom.rows, C_HEADS * C_HD), BF16)
    for g in geom.groups:
        bounds = _key_chunks(g.tp, g.fv)
        nblk = g.tp // SEQ_BLK
        nsub = max(d for d in range(1, ATTN_MAX_SUB + 1) if nblk % d == 0)
        tq, gw, rows = nsub * SEQ_BLK, C_GROUP * C_HD, C_GROUP * SEQ_BLK
        al = lambda x: pl.multiple_of(x, LANE)
        q_spec = pl.BlockSpec((pl.Element(tq), pl.Element(gw)),
                              lambda b, h, i, g=g, tq=tq: (al(g.start + b * g.tp + i * tq), al(h * gw)))
        in_specs = [q_spec,
                    pl.BlockSpec((pl.Element(g.tp), pl.Element(C_HD)),
                                 lambda b, h, i, g=g: (al(g.start + b * g.tp), al(h * C_HD))),
                    pl.BlockSpec((pl.Element(g.tp), pl.Element(2 * C_HD)),
                                 lambda b, h, i, g=g: (al(g.start + b * g.tp), al(h * 2 * C_HD))),
                    pl.BlockSpec(memory_space=pl.ANY)]
        out = pl.pallas_call(
            functools.partial(_attn_kernel, nsub=nsub, bounds=bounds, fv=g.fv),
            out_shape=jax.ShapeDtypeStruct(out.shape, out.dtype),
            grid=(g.nb, C_KV_HEADS, g.tp // tq),
            in_specs=in_specs,
            out_specs=q_spec,
            scratch_shapes=[pltpu.VMEM((2, rows, max(z for _, z in bounds)), F32), pltpu.VMEM((2, rows, 1), F32),
                            pltpu.VMEM((rows, 1), F32), pltpu.VMEM((rows, 2 * C_HD), F32)],
            input_output_aliases={3: 0},
            compiler_params=_params(("parallel", "parallel", "parallel")),
            name="attention",
        )(qn, kn, vn, out)
    return out


def _ffn_kernel(x_ref, xp_ref, xn_ref, pre_ref, wu_ref, wg_ref, cw_ref, cb_ref, wo_ref, post_ref,
                o_ref, xs_ref, act_ref, acc_ref, *, tm):
    i = pl.program_id(0)
    xs_ref[0:HALO] = jnp.where(i > 0, _rms(xp_ref[...], pre_ref[...]), 0.0).astype(BF16)
    xs_ref[HALO:HALO + tm] = _rms(x_ref[...], pre_ref[...]).astype(BF16)
    xs_ref[HALO + tm:2 * HALO + tm] = jnp.where(i < pl.num_programs(0) - 1,
                                                _rms(xn_ref[...], pre_ref[...]), 0.0).astype(BF16)
    half = FFN_CONV // 2
    n_chunks = D_FF // FF_CHUNK
    for c in range(n_chunks):
        cs = slice(c * FF_CHUNK, (c + 1) * FF_CHUNK)
        u = jnp.dot(xs_ref[HALO:HALO + tm], wu_ref[:, cs], preferred_element_type=F32)
        g_win = jnp.dot(xs_ref[...], wg_ref[:, cs], preferred_element_type=F32)
        gate = cb_ref[0:1, cs]
        for j in range(FFN_CONV):
            gate = gate + _shifted_rows(g_win, j - half, tm) * cw_ref[j:j + 1, cs]
        th = jnp.tanh(gate * (GELU_C0 + GELU_C0 * 0.044715 * (gate * gate)))
        hu = 0.5 * gate * u
        act_ref[:, cs] = (hu + hu * th).astype(BF16)
        if c % FF_GROUP == FF_GROUP - 1 or c == n_chunks - 1:
            ks = slice((c // FF_GROUP) * FF_GROUP * FF_CHUNK, (c + 1) * FF_CHUNK)
            part = jnp.dot(act_ref[:, ks], wo_ref[ks, :], preferred_element_type=F32)
            if c < FF_GROUP:
                acc_ref[...] = part
            else:
                acc_ref[...] += part
    o_ref[...] = x_ref[...] + _rms(acc_ref[...], post_ref[...])


def _conv_ffn(h, pre_w, w_in, conv_w, conv_b, w_out, post_w, geom):
    tm, rows = geom.tm, geom.rows
    assert D_FF % FF_CHUNK == 0
    wu = w_in[:, :D_FF].astype(BF16)
    wg = w_in[:, D_FF:].astype(BF16)
    return pl.pallas_call(
        functools.partial(_ffn_kernel, tm=tm),
        out_shape=jax.ShapeDtypeStruct((rows, D_MODEL), F32),
        grid=(rows // tm,),
        in_specs=_halo_specs(tm, D_MODEL, rows) + [
            _const_spec((1, D_MODEL)), _const_spec((D_MODEL, D_FF)), _const_spec((D_MODEL, D_FF)),
            _const_spec((FFN_CONV, D_FF)), _const_spec((1, D_FF)), _const_spec((D_FF, D_MODEL)),
            _const_spec((1, D_MODEL))],
        out_specs=pl.BlockSpec((tm, D_MODEL), lambda i: (i, 0)),
        scratch_shapes=[pltpu.VMEM((tm + 2 * HALO, D_MODEL), BF16), pltpu.VMEM((tm, D_FF), BF16),
                        pltpu.VMEM((tm, D_MODEL), F32)],
        compiler_params=_params(),
        name="conv_ffn",
    )(h, h, h, pre_w.reshape(1, D_MODEL), wu, wg, conv_w, conv_b.reshape(1, D_FF),
      w_out.astype(BF16), post_w.reshape(1, D_MODEL))


def _meta_kernel(meta_ref, _, o_ref):
    o_ref[...] = meta_ref[...].astype(o_ref.dtype)


def _rows_in_kernel(x_ref, _, o_ref):
    o_ref[...] = x_ref[0]


def _rows_out_kernel(h_ref, o_ref):
    o_ref[0] = h_ref[...]


def _copy_block(n):
    return max(b for b in (2048, 1024, 512, 256, 128) if n % b == 0)


def _pack_rows(xs, meta, geom):
    al = lambda v: pl.multiple_of(v, LANE)
    shape = jax.ShapeDtypeStruct((geom.rows, D_MODEL), xs[0].dtype)
    h = jnp.zeros(shape.shape, shape.dtype)
    for x, g in zip(xs, geom.groups):
        pad = g.tp - g.n
        h = pl.pallas_call(
            _meta_kernel, out_shape=shape, grid=(g.nb,),
            in_specs=[_const_spec((N_META, D_MODEL)), pl.BlockSpec(memory_space=pl.ANY)],
            out_specs=pl.BlockSpec((pl.Element(N_META), pl.Element(D_MODEL)),
                                   lambda b, g=g: (pl.multiple_of(g.start + b * g.tp + g.fv, N_META), 0)),
            input_output_aliases={1: 0},
            compiler_params=_params(), name="pack_meta",
        )(meta, h)
        blk = _copy_block(g.n)
        h = pl.pallas_call(
            _rows_in_kernel, out_shape=shape, grid=(g.nb, g.n // blk),
            in_specs=[pl.BlockSpec((1, blk, D_MODEL), lambda b, j: (b, j, 0)), pl.BlockSpec(memory_space=pl.ANY)],
            out_specs=pl.BlockSpec((pl.Element(blk), pl.Element(D_MODEL)),
                                   lambda b, j, g=g, pad=pad, blk=blk: (al(g.start + b * g.tp + pad + j * blk), 0)),
            input_output_aliases={1: 0},
            compiler_params=_params(("parallel", "parallel")), name="pack_rows",
        )(x, h)
    return h


def _unpack_rows(h, geom):
    al = lambda v: pl.multiple_of(v, LANE)
    outs = []
    for g in geom.groups:
        pad = g.tp - g.n
        blk = _copy_block(g.n)
        outs.append(pl.pallas_call(
            _rows_out_kernel, out_shape=jax.ShapeDtypeStruct((g.nb, g.n, D_MODEL), h.dtype),
            grid=(g.nb, g.n // blk),
            in_specs=[pl.BlockSpec((pl.Element(blk), pl.Element(D_MODEL)),
                                   lambda b, j, g=g, pad=pad, blk=blk: (al(g.start + b * g.tp + pad + j * blk), 0))],
            out_specs=pl.BlockSpec((1, blk, D_MODEL), lambda b, j: (b, j, 0)),
            compiler_params=_params(("parallel", "parallel")), name="unpack_rows",
        )(h))
    return tuple(outs)


def _even_layer(h, j, geom, tables, ones, p):
    aw, bw = A_WIDTH, B_WIDTH
    a_cols = 4 * aw + 4 * A_HEADS
    w_in = p["e_w_in"][j]
    lora = B_W_LORA + B_A_LORA + B_G_LORA
    zeros = lambda n: jnp.zeros((D_MODEL, n), F32)
    w_cat = jnp.concatenate([
        w_in[:, :3 * aw],
        w_in[:, a_cols:a_cols + 3 * bw + lora], zeros(B_LORA_PAD - lora),
        w_in[:, 4 * aw:a_cols], zeros(LANE - 4 * A_HEADS),
        w_in[:, 3 * aw:4 * aw]], axis=1).astype(BF16)
    mu = jnp.pad(p["b_shift"][j], ((0, 0), (0, B_LORA_PAD - lora)))
    lora_rows = lambda m, off: jnp.pad(m, ((off, B_LORA_PAD - off - m.shape[0]), (0, 0)))
    w2 = lora_rows(jnp.concatenate([p["b_w2"][j, 0], p["b_w2"][j, 1]], axis=1), 0)
    a2 = lora_rows(p["b_a2"][j], B_W_LORA)
    g2 = lora_rows(p["b_g2"][j], B_W_LORA + B_A_LORA)
    qn, kn, vv, gts, z, r, k, v, kkn, eta, cf, cb, bonus, gate = _even_prep(
        h, p["mix_pre_norm"][2 * j], w_cat, p["a_conv_w"][j], p["a_log"][j], p["a_dt_bias"][j],
        mu, p["b_w0"][j].reshape(1, 2 * bw), w2, p["b_a0"][j], a2, g2,
        p["b_k_k"][j], p["b_k_a"][j], p["b_r_k"][j], ones, geom)
    oaf, oab = _a_scan(qn, kn, vv, gts, geom, tables)
    ybf, ybb = _b_scan(r, k, v, kkn, eta, cf, cb, geom, tables)

    return _mixout_even(h, oaf, oab, z, ybf, ybb, bonus, gate, p["e_w_out"][j].astype(BF16),
                        p["a_out_norm"][j], p["b_ln_w"][j], p["b_ln_b"][j], p["mix_post_norm"][2 * j], ones, geom)


def _odd_layer(h, j, geom, rope, p):
    qn, kn, vn = _qkv_proj(h, p["mix_pre_norm"][2 * j + 1], p["o_w_qkv"][j].astype(BF16), rope[0], rope[1],
                           p["o_q_norm"][j], p["o_k_norm"][j], geom)
    o = _attention(qn, kn, vn, geom)
    return _mixout_odd(h, o, p["o_w_out"][j].astype(BF16), p["mix_post_norm"][2 * j + 1], geom)


def kernel(x_prompt, x_sample, meta, mix_pre_norm, mix_post_norm, ffn_pre_norm, ffn_post_norm, e_w_in, a_conv_w, a_log, a_dt_bias, a_out_norm, b_shift, b_w0, b_w2, b_a0, b_a2, b_g2, b_k_k, b_k_a, b_r_k, b_ln_w, b_ln_b, e_w_out, o_w_qkv, o_q_norm, o_k_norm, o_w_out, f_w_in, f_conv_w, f_conv_b, f_w_out):
    p = dict(mix_pre_norm=mix_pre_norm, mix_post_norm=mix_post_norm, e_w_in=e_w_in, a_conv_w=a_conv_w,
             a_log=a_log, a_dt_bias=a_dt_bias, a_out_norm=a_out_norm, b_shift=b_shift, b_w0=b_w0, b_w2=b_w2,
             b_a0=b_a0, b_a2=b_a2, b_g2=b_g2, b_k_k=b_k_k, b_k_a=b_k_a, b_r_k=b_r_k, b_ln_w=b_ln_w,
             b_ln_b=b_ln_b, e_w_out=e_w_out, o_w_qkv=o_w_qkv, o_q_norm=o_q_norm, o_k_norm=o_k_norm,
             o_w_out=o_w_out)
    xs = (x_prompt, x_sample)
    geom = _choose_geom([(x.shape[0], x.shape[1]) for x in xs])
    tables = _scan_tables(geom)
    rope = _rope_tables(geom)
    seg = np.arange(B_WIDTH) // B_HD
    ones = jnp.asarray(seg[:, None] == seg[None, :], BF16)

    h = _pack_rows(xs, meta, geom)
    for i in range(DEPTH):
        h = _even_layer(h, i // 2, geom, tables, ones, p) if i % 2 == 0 else _odd_layer(h, i // 2, geom, rope, p)
        h = _conv_ffn(h, ffn_pre_norm[i], f_w_in[i], f_conv_w[i], f_conv_b[i], f_w_out[i], ffn_post_norm[i], geom)
    return _unpack_rows(h, geom)
```

```python
import functools
import itertools
from typing import NamedTuple

import numpy as np
import jax
import jax.numpy as jnp
from jax import lax
from jax.experimental import pallas as pl
from jax.experimental.pallas import tpu as pltpu

F32 = jnp.float32
BF16 = jnp.bfloat16

D_MODEL = 1024
DEPTH = 2
N_META = 16
GRID_W = 64
NORM_EPS = 1e-6

A_HEADS = 4
A_DK = 128
A_DV = 128
A_CONV = 5
A_WIDTH = A_HEADS * A_DV

B_HEADS = 8
B_HD = 64
B_WIDTH = B_HEADS * B_HD
B_W_LORA = 32
B_A_LORA = 32
B_G_LORA = 96
B_GN_EPS = 64e-5
B_LORA_PAD = 256

C_HEADS = 8
C_KV_HEADS = 2
C_GROUP = C_HEADS // C_KV_HEADS
C_HD = 128
ROPE_THETA = 10000.0

D_FF = 2816
FFN_CONV = 3

LANE = 128
CHUNK = 64
SEQ_BLK = 128
HALO = 16
FF_CHUNK = 256
FF_GROUP = 4
ATTN_MAX_TK = 2304
ATTN_MAX_SUB = 17
SEG_TILE = 256
LOG2E = 1.4426950408889634
GELU_C0 = 0.7978845608028654
VMEM_LIMIT = 56 * 1024 * 1024


class _Group(NamedTuple):
    start: int
    nb: int
    n: int
    tp: int
    fv: int


class _Geom(NamedTuple):
    groups: tuple
    rows: int
    tm: int


def _choose_geom(shapes):
    tm = 512 if min(n for _, n in shapes) >= 512 else 128
    best = None
    for pads in itertools.product((128, 256, 384, 512), repeat=len(shapes)):
        start, ok, groups = 0, True, []
        for (nb, n), p in zip(shapes, pads):
            ok = ok and start % tm == 0
            groups.append(_Group(start, nb, n, n + p, p - N_META))
            start += nb * (n + p)
        if ok and start % tm == 0 and (best is None or start < best.rows):
            best = _Geom(tuple(groups), start, tm)
    assert best is not None
    return best


def _row_valid(geom, base, tm):
    start = jnp.int32(geom.groups[0].start)
    tp = jnp.int32(geom.groups[0].tp)
    fv = jnp.int32(geom.groups[0].fv)
    for g in geom.groups[1:]:
        assert tm <= g.tp
        inside = base >= g.start
        start = jnp.where(inside, g.start, start)
        tp = jnp.where(inside, g.tp, tp)
        fv = jnp.where(inside, g.fv, fv)
    pos = lax.rem(base - start, tp) + lax.broadcasted_iota(jnp.int32, (tm, 1), 0)
    pos = jnp.where(pos >= tp, pos - tp, pos)
    return pos >= fv


def _rms(x, w, eps=NORM_EPS):
    return x * lax.rsqrt(jnp.mean(x * x, axis=-1, keepdims=True) + eps) * w


def _sigmoid(x):
    return 1.0 / (1.0 + jnp.exp(-x))


def _softplus(x):
    return jnp.maximum(x, 0.0) + jnp.log1p(jnp.exp(-jnp.abs(x)))


def _dot(a, b):
    return jnp.dot(a.astype(BF16), b.astype(BF16), preferred_element_type=F32)


def _dot_nt(a, b):
    return lax.dot_general(a.astype(BF16), b.astype(BF16), (((1,), (1,)), ((), ())),
                           preferred_element_type=F32)


def _seg_sum(x, ones_ref, terms):
    acc = None
    rem = x
    for t in range(terms):
        piece = rem.astype(BF16)
        part = jnp.concatenate([jnp.dot(piece[:, s:s + SEG_TILE], ones_ref[...], preferred_element_type=F32)
                                for s in range(0, x.shape[1], SEG_TILE)], axis=1)
        acc = part if acc is None else acc + part
        if t + 1 < terms:
            rem = rem - piece.astype(F32)
    return acc


def _chunk_cumsum(x, reverse):
    tm = x.shape[0]
    pos = lax.broadcasted_iota(jnp.int32, (tm, 1), 0) & (CHUNK - 1)
    s = 1
    while s < CHUNK:
        if reverse:
            x = x + jnp.where(pos < CHUNK - s, pltpu.roll(x, tm - s, axis=0), 0.0)
        else:
            x = x + jnp.where(pos >= s, pltpu.roll(x, s, axis=0), 0.0)
        s *= 2
    return x


def _neumann_inverses(n_mats, eye):
    n = eye.shape[0]
    prods = [eye + m for m in n_mats]
    powers = [_dot(m, m) for m in n_mats]
    s = 4
    while s < CHUNK:
        both = [_dot(jnp.concatenate([pr, pw], axis=0), pw) for pr, pw in zip(prods, powers)]
        prods = [pr + b[:n] for pr, b in zip(prods, both)]
        powers = [b[n:] for b in both]
        s *= 2
    return [pr + _dot(pr, pw) for pr, pw in zip(prods, powers)]


def _shifted_rows(win, d, tm):
    if d == 0:
        return win[HALO:HALO + tm]
    return pltpu.roll(win, (-d) % win.shape[0], axis=0)[HALO:HALO + tm]


def _halo_specs(tm, width, rows):
    per = tm // HALO
    last = rows // HALO - 1
    return [
        pl.BlockSpec((tm, width), lambda i: (i, 0)),
        pl.BlockSpec((HALO, width), lambda i: (jnp.maximum(i * per - 1, 0), 0)),
        pl.BlockSpec((HALO, width), lambda i: (jnp.minimum((i + 1) * per, last), 0)),
    ]


def _const_spec(shape):
    return pl.BlockSpec(shape, lambda *_: (0,) * len(shape))


def _params(sem=("parallel",)):
    return pltpu.CompilerParams(dimension_semantics=sem, vmem_limit_bytes=VMEM_LIMIT)


def _a_prep_body(xw_ref, al, cw_ref, alog_ref, dtb_ref, q_ref, k_ref, v_ref, go_ref, valid, tm):
    half = A_CONV // 2
    for c, o_ref in enumerate((q_ref, k_ref, v_ref)):
        cs = slice(c * A_WIDTH, (c + 1) * A_WIDTH)
        win = xw_ref[:, cs]
        acc = _shifted_rows(win, -half, tm) * cw_ref[0:1, cs]
        for j in range(1, A_CONV):
            acc = acc + _shifted_rows(win, j - half, tm) * cw_ref[j:j + 1, cs]
        y = jnp.where(valid, acc * _sigmoid(acc), 0.0)
        for hd in range(A_HEADS):
            hs = slice(hd * A_DK, (hd + 1) * A_DK)
            yh = y[:, hs]
            if c < 2:
                yh = yh * lax.rsqrt(jnp.sum(yh * yh, axis=-1, keepdims=True) + 1e-6)
                if c == 0:
                    yh = yh * (A_DK ** -0.5)
            o_ref[:, hs] = yh.astype(o_ref.dtype)
    lane = lax.broadcasted_iota(jnp.int32, (1, LANE), 1)
    gval = jnp.where(valid, -jnp.exp(alog_ref[...]) * _softplus(al + dtb_ref[...]), 0.0)
    beta = jnp.where(valid, _sigmoid(al), 0.0)
    cum_f = _chunk_cumsum(gval, False)
    cum_b = _chunk_cumsum(gval, True)
    go_ref[...] = jnp.where(lane < A_HEADS, cum_f,
                            jnp.where(lane < 2 * A_HEADS, cum_b,
                                      jnp.where(lane < 4 * A_HEADS, beta, 0.0)))


def _a_scan_kernel(fi_ref, bi_ref, fr_ref,
                   qf, kf, vf, gf, gtf, qb, kb, vb, gb, gtb, of, ob, sf, sb):
    del fi_ref, bi_ref
    @pl.when(fr_ref[pl.program_id(0)] == 1)
    def _():
        sf[...] = jnp.zeros_like(sf)
        sb[...] = jnp.zeros_like(sb)
    r = lax.broadcasted_iota(jnp.int32, (CHUNK, CHUNK), 0)
    c = lax.broadcasted_iota(jnp.int32, (CHUNK, CHUNK), 1)
    eye = jnp.where(r == c, 1.0, 0.0)
    nc = SEQ_BLK // CHUNK
    dirs = ((qf, kf, vf, gf, gtf, of, sf, False, r >= c, r > c),
            (qb, kb, vb, gb, gtb, ob, sb, True, r <= c, r < c))

    probs = []
    for d, (q_ref, k_ref, v_ref, g_ref, gt_ref, _, _, reverse, tri, strict) in enumerate(dirs):
        edge = 0 if reverse else CHUNK - 1
        for step in range(nc):
            jj = nc - 1 - step if reverse else step
            rows = slice(jj * CHUNK, (jj + 1) * CHUNK)
            for hd in range(A_HEADS):
                hs = slice(hd * A_DK, (hd + 1) * A_DK)
                col = hd + (A_HEADS if reverse else 0)
                q, k, v = (ref[rows, hs].astype(F32) for ref in (q_ref, k_ref, v_ref))
                g_col = g_ref[rows, col:col + 1]
                b_col = g_ref[rows, 2 * A_HEADS + col:2 * A_HEADS + col + 1]
                g_row = gt_ref[jj, col:col + 1, :]
                decay = jnp.where(tri, jnp.exp(jnp.where(tri, g_col - g_row, 0.0)), 0.0)
                kb_ = k * b_col
                e_g = jnp.exp(g_col)
                g_last = g_col[edge:edge + 1]
                probs.append(dict(d=d, step=step, hd=hd, rows=rows, hs=hs, tri=tri, strict=strict, decay=decay,
                                  k=k, kbq=jnp.concatenate([kb_, q], axis=0),
                                  rhs=jnp.concatenate([v * b_col, kb_ * e_g], axis=1),
                                  qg=q * e_g, kg_t=(k * jnp.exp(g_last - g_col)).T, g_end=jnp.exp(g_last)))
    for p in probs:
        kq = _dot_nt(p["kbq"], p["k"])
        p["l"] = jnp.where(p["strict"], kq[:CHUNK] * p["decay"], 0.0)
        p["attn"] = jnp.where(p["tri"], kq[CHUNK:] * p["decay"], 0.0)
    t_invs = _neumann_inverses([-p["l"] for p in probs], eye)
    for p, t_inv in zip(probs, t_invs):
        p["uw"] = _dot(t_inv, p["rhs"])

    states = {(d, hd): dirs[d][6][hd] for d in range(2) for hd in range(A_HEADS)}
    for step in range(nc):
        cur = [p for p in probs if p["step"] == step]
        v_news = [p["uw"][:, :A_DV] - _dot(p["uw"][:, A_DV:], states[p["d"], p["hd"]]) for p in cur]
        for p, v_new in zip(cur, v_news):
            state = states[p["d"], p["hd"]]
            dirs[p["d"]][5][p["rows"], p["hs"]] = _dot(jnp.concatenate([p["qg"], p["attn"]], axis=1),
                                                      jnp.concatenate([state, v_new], axis=0))
        for p, v_new in zip(cur, v_news):
            key = (p["d"], p["hd"])
            states[key] = states[key] * p["g_end"] + _dot(p["kg_t"], v_new)
    for (d, hd), state in states.items():
        dirs[d][6][hd] = state


def _scan_tables(geom):
    fi, bi, fr = [], [], []
    for g in geom.groups:
        nblk = g.tp // SEQ_BLK
        for b in range(g.nb):
            base = (g.start + b * g.tp) // SEQ_BLK
            for i in range(nblk):
                fi.append(base + i)
                bi.append(base + nblk - 1 - i)
                fr.append(1 if i == 0 else 0)
    mk = lambda v: jnp.asarray(np.asarray(v, np.int32))
    return mk(fi), mk(bi), mk(fr)


def _a_scan(qn, kn, vv, gts, geom, tables):
    rows = geom.rows
    nc = SEQ_BLK // CHUNK
    gts_t = gts[:, :4 * A_HEADS].reshape(rows // CHUNK, CHUNK, 4 * A_HEADS).transpose(0, 2, 1)
    wide_f = pl.BlockSpec((SEQ_BLK, A_WIDTH), lambda s, fi, bi, fr: (fi[s], 0))
    wide_b = pl.BlockSpec((SEQ_BLK, A_WIDTH), lambda s, fi, bi, fr: (bi[s], 0))
    gate_f = pl.BlockSpec((SEQ_BLK, LANE), lambda s, fi, bi, fr: (fi[s], 0))
    gate_b = pl.BlockSpec((SEQ_BLK, LANE), lambda s, fi, bi, fr: (bi[s], 0))
    gt_f = pl.BlockSpec((nc, 4 * A_HEADS, CHUNK), lambda s, fi, bi, fr: (fi[s], 0, 0))
    gt_b = pl.BlockSpec((nc, 4 * A_HEADS, CHUNK), lambda s, fi, bi, fr: (bi[s], 0, 0))
    out = jax.ShapeDtypeStruct((rows, A_WIDTH), F32)
    return pl.pallas_call(
        _a_scan_kernel,
        out_shape=[out, out],
        grid_spec=pltpu.PrefetchScalarGridSpec(
            num_scalar_prefetch=3,
            grid=(tables[0].shape[0],),
            in_specs=[wide_f, wide_f, wide_f, gate_f, gt_f, wide_b, wide_b, wide_b, gate_b, gt_b],
            out_specs=[wide_f, wide_b],
            scratch_shapes=[pltpu.VMEM((A_HEADS, A_DK, A_DV), F32)] * 2),
        compiler_params=_params(("arbitrary",)),
        name="a_scan",
    )(*tables, qn, kn, vv, gts, gts_t, qn, kn, vv, gts, gts_t)


def _b_prep_body(xw_ref, lw_ref, mu_ref, w0_ref, w2_ref, a0_ref, a2_ref, g2_ref, kk_ref, ka_ref, rk_ref, ones_ref,
                 r_out, k_out, v_out, kkn_out, eta_out, cf_out, cb_out, bonus_out, gate_out, tm):
    def shifted(win_ref, cs, mu_off):
        win = win_ref[:, cs]
        cur, prev, nxt = _shifted_rows(win, 0, tm), _shifted_rows(win, -1, tm), _shifted_rows(win, 1, tm)
        ms = slice(mu_off + cs.start, mu_off + cs.stop)
        return cur + mu_ref[0:1, ms] * (prev - cur) + mu_ref[1:2, ms] * (nxt - cur)

    w = B_WIDTH
    r = shifted(xw_ref, slice(0, w), 0)
    k = shifted(xw_ref, slice(w, 2 * w), 0)
    v = shifted(xw_ref, slice(2 * w, 3 * w), 0)
    lo = shifted(lw_ref, slice(0, B_LORA_PAD), 3 * w)
    wl = _dot(jnp.tanh(lo), w2_ref[...]) + w0_ref[...]
    log_decay = -float(np.exp(-0.5)) * _sigmoid(wl)
    cf_out[...] = _chunk_cumsum(log_decay[:, :w], False)
    cb_out[...] = _chunk_cumsum(log_decay[:, w:], True)
    eta = _sigmoid(a0_ref[...] + _dot(lo, a2_ref[...]))
    gate_out[...] = _dot(_sigmoid(lo), g2_ref[...])
    kx = k * kk_ref[...]
    kkn_out[...] = (kx * lax.rsqrt(_seg_sum(kx * kx, ones_ref, 2) + 1e-6)).astype(kkn_out.dtype)
    k = k * (1.0 + (eta - 1.0) * ka_ref[...])
    bonus_out[...] = _seg_sum(r * k * rk_ref[...], ones_ref, 2) * v
    r_out[...] = r.astype(r_out.dtype)
    k_out[...] = k.astype(k_out.dtype)
    v_out[...] = v.astype(v_out.dtype)
    eta_out[...] = eta.astype(eta_out.dtype)


def _even_prep_kernel(h_ref, hp_ref, hn_ref, nw_ref, w_ref, cw_ref, alog_ref, dtb_ref,
                      mu_ref, w0_ref, w2_ref, a0_ref, a2_ref, g2_ref, kk_ref, ka_ref, rk_ref, ones_ref,
                      q_out, ka_out, va_out, go_out, z_out,
                      r_out, kb_out, vb_out, kkn_out, eta_out, cf_out, cb_out, bonus_out, gate_out,
                      xn_ref, aw_ref, bw_ref, lw_ref, *, geom):
    tm = geom.tm
    i = pl.program_id(0)
    valid = _row_valid(geom, i * tm, tm)
    xn_ref[0:HALO] = jnp.where(i > 0, _rms(hp_ref[...], nw_ref[...]), 0.0).astype(BF16)
    xn_ref[HALO:HALO + tm] = _rms(h_ref[...], nw_ref[...]).astype(BF16)
    xn_ref[HALO + tm:2 * HALO + tm] = jnp.where(i < pl.num_programs(0) - 1,
                                                _rms(hn_ref[...], nw_ref[...]), 0.0).astype(BF16)
    aw, bw = 3 * A_WIDTH, 3 * B_WIDTH
    c0 = 0
    for win_ref, wd in ((aw_ref, aw), (bw_ref, bw), (lw_ref, B_LORA_PAD)):
        for s in range(0, wd, 512):
            e = min(s + 512, wd)
            win_ref[:, s:e] = jnp.dot(xn_ref[...], w_ref[:, c0 + s:c0 + e], preferred_element_type=F32)
        c0 += wd
    centre = xn_ref[HALO:HALO + tm]
    gates = jnp.dot(centre, w_ref[:, c0:c0 + LANE], preferred_element_type=F32)
    z_out[...] = jnp.dot(centre, w_ref[:, c0 + LANE:], preferred_element_type=F32).astype(z_out.dtype)
    _a_prep_body(aw_ref, gates, cw_ref, alog_ref, dtb_ref, q_out, ka_out, va_out, go_out, valid, tm)
    _b_prep_body(bw_ref, lw_ref, mu_ref, w0_ref, w2_ref, a0_ref, a2_ref, g2_ref, kk_ref, ka_ref, rk_ref, ones_ref,
                 r_out, kb_out, vb_out, kkn_out, eta_out, cf_out, cb_out, bonus_out, gate_out, tm)


def _even_prep(h, norm_w, w_bf16, conv_w, a_log, dt_bias, mu, w0, w2, a0, a2, g2, k_k, k_a, r_k, ones, geom):
    tm, rows = geom.tm, geom.rows
    aw, bw = 3 * A_WIDTH, 3 * B_WIDTH
    row = lambda v: v.reshape(1, -1).astype(F32)
    pad = lambda v: jnp.pad(row(v), ((0, 0), (0, LANE - v.size)))
    narrow = lambda wd: jax.ShapeDtypeStruct((rows, wd), BF16)
    wide = lambda wd: jax.ShapeDtypeStruct((rows, wd), F32)
    blk = lambda wd: pl.BlockSpec((tm, wd), lambda i: (i, 0))
    win = tm + 2 * HALO
    return pl.pallas_call(
        functools.partial(_even_prep_kernel, geom=geom),
        out_shape=[narrow(A_WIDTH)] * 3 + [wide(LANE), narrow(A_WIDTH)] + [narrow(B_WIDTH)] * 5 + [wide(B_WIDTH)] * 4,
        grid=(rows // tm,),
        in_specs=_halo_specs(tm, D_MODEL, rows) + [
            _const_spec((1, D_MODEL)), _const_spec(w_bf16.shape),
            _const_spec((A_CONV, aw)), _const_spec((1, LANE)), _const_spec((1, LANE)),
            _const_spec((2, bw + B_LORA_PAD)), _const_spec((1, 2 * B_WIDTH)), _const_spec((B_LORA_PAD, 2 * B_WIDTH)),
            _const_spec((1, B_WIDTH)), _const_spec((B_LORA_PAD, B_WIDTH)), _const_spec((B_LORA_PAD, B_WIDTH)),
            _const_spec((1, B_WIDTH)), _const_spec((1, B_WIDTH)), _const_spec((1, B_WIDTH)),
            _const_spec((SEG_TILE, SEG_TILE))],
        out_specs=[blk(A_WIDTH)] * 3 + [blk(LANE), blk(A_WIDTH)] + [blk(B_WIDTH)] * 9,
        scratch_shapes=[pltpu.VMEM((win, D_MODEL), BF16), pltpu.VMEM((win, aw), F32), pltpu.VMEM((win, bw), F32),
                        pltpu.VMEM((win, B_LORA_PAD), F32)],
        compiler_params=_params(),
        name="even_prep",
    )(h, h, h, row(norm_w), w_bf16, conv_w, pad(a_log), pad(dt_bias), mu, w0, w2, row(a0), a2, g2,
      row(k_k), row(k_a), row(r_k), ones)


def _b_scan_kernel(fi_ref, bi_ref, fr_ref,
                   rf, kf, vf, kkf, ef, cf, rb, kb, vb, kkb, eb, cb, yf, yb, pf, pb):
    del fi_ref, bi_ref
    @pl.when(fr_ref[pl.program_id(0)] == 1)
    def _():
        pf[...] = jnp.zeros_like(pf)
        pb[...] = jnp.zeros_like(pb)
    n2 = 2 * CHUNK
    r = lax.broadcasted_iota(jnp.int32, (n2, n2), 0)
    c = lax.broadcasted_iota(jnp.int32, (n2, n2), 1)
    same = (r >= CHUNK) == (c >= CHUNK)
    rt, ct = r & (CHUNK - 1), c & (CHUNK - 1)
    eye = jnp.where(r == c, 1.0, 0.0)
    lane_lo = lax.broadcasted_iota(jnp.int32, (1, n2), 1) < B_HD
    row_pos = lax.broadcasted_iota(jnp.int32, (CHUNK, 1), 0)
    nc = SEQ_BLK // CHUNK
    pair_w = 2 * B_HD
    n_pairs = B_HEADS // 2
    dirs = (((rf, kf, vf, kkf, ef, cf), yf, pf, False, same & (rt > ct), same & (rt >= ct)),
            ((rb, kb, vb, kkb, eb, cb), yb, pb, True, same & (rt < ct), same & (rt <= ct)))

    def expand(x):
        return jnp.concatenate([jnp.where(lane_lo, x, 0.0), jnp.where(lane_lo, 0.0, x)], axis=0)

    probs = []
    for d, (refs, _, _, reverse, strict, incl) in enumerate(dirs):
        r_ref, k_ref, v_ref, kk_ref, eta_ref, c_ref = refs
        for step in range(nc):
            jj = nc - 1 - step if reverse else step
            rows = slice(jj * CHUNK, (jj + 1) * CHUNK)
            for pr in range(n_pairs):
                ls = slice(pr * pair_w, (pr + 1) * pair_w)
                r, k, v, kk, eta = (ref[rows, ls].astype(F32) for ref in (r_ref, k_ref, v_ref, kk_ref, eta_ref))
                c_in = c_ref[rows, ls]
                if reverse:
                    c_ex = jnp.where(row_pos < CHUNK - 1, pltpu.roll(c_in, CHUNK - 1, axis=0), 0.0)
                    c_tot = c_in[0:1]
                else:
                    c_ex = jnp.where(row_pos >= 1, pltpu.roll(c_in, 1, axis=0), 0.0)
                    c_tot = c_in[CHUNK - 1:CHUNK]
                b = kk * eta
                inv_w = jnp.exp(-c_in)
                rest_w = jnp.exp(c_tot - c_in)
                a_e = expand(-kk * jnp.exp(c_ex))
                r_e = expand(r * jnp.exp(c_in))
                probs.append(dict(
                    d=d, step=step, pr=pr, rows=rows, ls=ls, strict=strict, incl=incl, v=v,
                    a_e=a_e, r_e=r_e, v_e=expand(v),
                    lhs=jnp.concatenate([a_e, r_e], axis=0),
                    rhs=jnp.concatenate([expand(b * inv_w), expand(k * inv_w)], axis=0),
                    kd_t=jnp.concatenate([b * rest_w, k * rest_w], axis=0).T,
                    w_col=jnp.sum(jnp.where(eye > 0.0, jnp.exp(c_tot), 0.0), axis=1, keepdims=True)))
    for p in probs:
        m1 = _dot_nt(p["lhs"], p["rhs"])
        p["n_ab"] = jnp.where(p["strict"], m1[:n2, :n2], 0.0)
        p["a_ak"] = jnp.where(p["strict"], m1[:n2, n2:], 0.0)
        p["a_r"] = jnp.concatenate([jnp.where(p["incl"], m1[n2:, :n2], 0.0),
                                    jnp.where(p["incl"], m1[n2:, n2:], 0.0)], axis=1)
    t_invs = _neumann_inverses([p["n_ab"] for p in probs], eye)
    for p in probs:
        p["akv"] = _dot(p["a_ak"], p["v_e"])
    for p, t_inv in zip(probs, t_invs):
        p["x"] = _dot(t_inv, jnp.concatenate([p["akv"], p["a_e"]], axis=1))

    states = {(d, pr): dirs[d][2][pr] for d in range(2) for pr in range(n_pairs)}
    for step in range(nc):
        cur = [p for p in probs if p["step"] == step]
        u_es = [p["x"][:, :pair_w] + _dot(p["x"][:, pair_w:], states[p["d"], p["pr"]]) for p in cur]
        for p, u_e in zip(cur, u_es):
            y_e = _dot(jnp.concatenate([p["r_e"], p["a_r"]], axis=1),
                       jnp.concatenate([states[p["d"], p["pr"]], u_e, p["v_e"]], axis=0))
            dirs[p["d"]][1][p["rows"], p["ls"]] = y_e[:CHUNK] + y_e[CHUNK:]
        for p, u_e in zip(cur, u_es):
            key = (p["d"], p["pr"])
            upd = _dot(p["kd_t"], jnp.concatenate([u_e[:CHUNK] + u_e[CHUNK:], p["v"]], axis=0))
            states[key] = states[key] * p["w_col"] + jnp.where(same, upd, 0.0)
    for (d, pr), state in states.items():
        dirs[d][2][pr] = state


def _b_scan(r, k, v, kkn, eta, cf, cb, geom, tables):
    wide_f = pl.BlockSpec((SEQ_BLK, B_WIDTH), lambda s, fi, bi, fr: (fi[s], 0))
    wide_b = pl.BlockSpec((SEQ_BLK, B_WIDTH), lambda s, fi, bi, fr: (bi[s], 0))
    out = jax.ShapeDtypeStruct((geom.rows, B_WIDTH), F32)
    return pl.pallas_call(
        _b_scan_kernel,
        out_shape=[out, out],
        grid_spec=pltpu.PrefetchScalarGridSpec(
            num_scalar_prefetch=3,
            grid=(tables[0].shape[0],),
            in_specs=[wide_f] * 6 + [wide_b] * 6,
            out_specs=[wide_f, wide_b],
            scratch_shapes=[pltpu.VMEM((B_HEADS // 2, 2 * B_HD, 2 * B_HD), F32)] * 2),
        compiler_params=_params(("arbitrary",)),
        name="b_scan",
    )(*tables, r, k, v, kkn, eta, cf, r, k, v, kkn, eta, cb)


def _finish_residual(h, mix, post_w, valid):
    return jnp.where(valid, h + _rms(mix, post_w), 0.0)


def _mixout_even_kernel(h_ref, oaf_ref, oab_ref, z_ref, ybf_ref, ybb_ref, bonus_ref, gate_ref,
                        w_ref, an_ref, lnw_ref, lnb_ref, post_ref, ones_ref, o_ref, cat_ref, *, geom):
    tm = geom.tm
    valid = _row_valid(geom, pl.program_id(0) * tm, tm)
    for hd in range(A_HEADS):
        hs = slice(hd * A_DV, (hd + 1) * A_DV)
        o = oaf_ref[:, hs] + oab_ref[:, hs]
        z = z_ref[:, hs].astype(F32)
        cat_ref[:, hs] = (_rms(o, an_ref[...]) * (z * _sigmoid(z))).astype(BF16)
    y = ybf_ref[...] + ybb_ref[...]
    yc = y - _seg_sum(y, ones_ref, 3) * (1.0 / B_HD)
    var = _seg_sum(yc * yc, ones_ref, 2) * (1.0 / B_HD)
    yn = yc * lax.rsqrt(var + B_GN_EPS) * lnw_ref[...] + lnb_ref[...]
    cat_ref[:, A_WIDTH:] = ((yn + bonus_ref[...]) * gate_ref[...]).astype(BF16)
    mix = jnp.dot(cat_ref[...], w_ref[...], preferred_element_type=F32)
    o_ref[...] = _finish_residual(h_ref[...], mix, post_ref[...], valid)


def _mixout_even(h, oaf, oab, z, ybf, ybb, bonus, gate, w_out, a_norm, ln_w, ln_b, post_w, ones, geom):
    tm, rows = geom.tm, geom.rows
    row = lambda v: v.reshape(1, -1).astype(F32)
    a_blk = pl.BlockSpec((tm, A_WIDTH), lambda i: (i, 0))
    b_blk = pl.BlockSpec((tm, B_WIDTH), lambda i: (i, 0))
    return pl.pallas_call(
        functools.partial(_mixout_even_kernel, geom=geom),
        out_shape=jax.ShapeDtypeStruct((rows, D_MODEL), F32),
        grid=(rows // tm,),
        in_specs=[pl.BlockSpec((tm, D_MODEL), lambda i: (i, 0)), a_blk, a_blk,
                  a_blk,
                  b_blk, b_blk, b_blk, b_blk,
                  _const_spec((A_WIDTH + B_WIDTH, D_MODEL)), _const_spec((1, A_DV)),
                  _const_spec((1, B_WIDTH)), _const_spec((1, B_WIDTH)), _const_spec((1, D_MODEL)),
                  _const_spec((SEG_TILE, SEG_TILE))],
        out_specs=pl.BlockSpec((tm, D_MODEL), lambda i: (i, 0)),
        scratch_shapes=[pltpu.VMEM((tm, A_WIDTH + B_WIDTH), BF16)],
        compiler_params=_params(),
        name="mixout_even",
    )(h, oaf, oab, z, ybf, ybb, bonus, gate, w_out, row(a_norm), row(ln_w), row(ln_b), row(post_w), ones)


def _mixout_odd_kernel(h_ref, o_ref_in, w_ref, post_ref, o_ref, *, geom):
    tm = geom.tm
    valid = _row_valid(geom, pl.program_id(0) * tm, tm)
    mix = jnp.dot(o_ref_in[...], w_ref[...], preferred_element_type=F32)
    o_ref[...] = _finish_residual(h_ref[...], mix, post_ref[...], valid)


def _mixout_odd(h, o, w_out, post_w, geom):
    tm, rows = geom.tm, geom.rows
    wd = C_HEADS * C_HD
    return pl.pallas_call(
        functools.partial(_mixout_odd_kernel, geom=geom),
        out_shape=jax.ShapeDtypeStruct((rows, D_MODEL), F32),
        grid=(rows // tm,),
        in_specs=[pl.BlockSpec((tm, D_MODEL), lambda i: (i, 0)), pl.BlockSpec((tm, wd), lambda i: (i, 0)),
                  _const_spec((wd, D_MODEL)), _const_spec((1, D_MODEL))],
        out_specs=pl.BlockSpec((tm, D_MODEL), lambda i: (i, 0)),
        compiler_params=_params(),
        name="mixout_odd",
    )(h, o, w_out, post_w.reshape(1, D_MODEL))


def _qkv_kernel(x_ref, nw_ref, w_ref, cos_ref, sin_ref, qn_ref, kn_ref, q_out, k_out, v_out, xn_ref):
    xn_ref[...] = _rms(x_ref[...], nw_ref[...]).astype(BF16)
    cos, sin = cos_ref[...], sin_ref[...]

    def rope(y, yp, n_ref, scale):
        rs = lax.rsqrt(jnp.mean(y * y, axis=-1, keepdims=True) + NORM_EPS) * scale
        return ((y * n_ref[0:1]) * cos + (yp * n_ref[1:2]) * sin) * rs

    nq = C_HEADS * C_HD
    nk = C_KV_HEADS * C_HD
    per = 512 // C_HD
    for c in range(nq // 512):
        y = jnp.dot(xn_ref[...], w_ref[:, c * 512:(c + 1) * 512], preferred_element_type=F32)
        yp = jnp.dot(xn_ref[...], w_ref[:, nq + 2 * nk + c * 512:nq + 2 * nk + (c + 1) * 512],
                     preferred_element_type=F32)
        for hd in range(per):
            hs = slice((c * per + hd) * C_HD, (c * per + hd + 1) * C_HD)
            ys = slice(hd * C_HD, (hd + 1) * C_HD)
            q_out[:, hs] = rope(y[:, ys], yp[:, ys], qn_ref, C_HD ** -0.5 * LOG2E).astype(BF16)
    y = jnp.dot(xn_ref[...], w_ref[:, nq:nq + 2 * nk], preferred_element_type=F32)
    yp = jnp.dot(xn_ref[...], w_ref[:, 2 * nq + 2 * nk:], preferred_element_type=F32)
    for hd in range(C_KV_HEADS):
        hs = slice(hd * C_HD, (hd + 1) * C_HD)
        k_out[:, hs] = rope(y[:, hs], yp[:, hs], kn_ref, 1.0).astype(BF16)
        v_out[:, 2 * hd * C_HD:(2 * hd + 1) * C_HD] = y[:, nk + hd * C_HD:nk + (hd + 1) * C_HD].astype(BF16)
        v_out[:, (2 * hd + 1) * C_HD:(2 * hd + 2) * C_HD] = jnp.ones((x_ref.shape[0], C_HD), BF16)


def _qkv_proj(h, norm_w, w_qkv, cos_t, sin_t, q_norm, k_norm, geom):
    tm, rows = geom.tm, geom.rows
    nq, nk = C_HEADS * C_HD, C_KV_HEADS * C_HD
    lane = np.arange(C_HD)
    partner = np.where(lane % (C_HD // 2) < C_HD // 4, lane + C_HD // 4, lane - C_HD // 4)
    cols = np.concatenate([hd * C_HD + partner for hd in range(C_HEADS + C_KV_HEADS)])
    w_bf16 = jnp.concatenate([w_qkv, w_qkv[:, cols]], axis=1).astype(BF16)
    q_norm = jnp.stack([q_norm, q_norm[partner]])
    k_norm = jnp.stack([k_norm, k_norm[partner]])
    return pl.pallas_call(
        _qkv_kernel,
        out_shape=[jax.ShapeDtypeStruct((rows, nq), BF16), jax.ShapeDtypeStruct((rows, nk), BF16),
                   jax.ShapeDtypeStruct((rows, 2 * nk), BF16)],
        grid=(rows // tm,),
        in_specs=[pl.BlockSpec((tm, D_MODEL), lambda i: (i, 0)), _const_spec((1, D_MODEL)),
                  _const_spec((D_MODEL, 2 * nq + 3 * nk)),
                  pl.BlockSpec((tm, C_HD), lambda i: (i, 0)), pl.BlockSpec((tm, C_HD), lambda i: (i, 0)),
                  _const_spec((2, C_HD)), _const_spec((2, C_HD))],
        out_specs=[pl.BlockSpec((tm, nq), lambda i: (i, 0)), pl.BlockSpec((tm, nk), lambda i: (i, 0)),
                   pl.BlockSpec((tm, 2 * nk), lambda i: (i, 0))],
        scratch_shapes=[pltpu.VMEM((tm, D_MODEL), BF16)],
        compiler_params=_params(),
        name="qkv_proj",
    )(h, norm_w.reshape(1, D_MODEL), w_bf16, cos_t, sin_t, q_norm, k_norm)


def _rope_tables(geom):
    nf = C_HD // 4
    inv = ROPE_THETA ** (-jnp.arange(nf, dtype=F32) / nf)
    cos_parts, sin_parts = [], []
    for g in geom.groups:
        t = jnp.arange(g.tp, dtype=jnp.int32) - (g.fv + N_META)
        is_meta = t < 0
        row = jnp.where(is_meta, -1, t // GRID_W).astype(F32)
        col = jnp.where(is_meta, t + N_META, t % GRID_W).astype(F32)
        ang_r, ang_c = row[:, None] * inv, col[:, None] * inv
        cos = jnp.concatenate([jnp.cos(ang_r)] * 2 + [jnp.cos(ang_c)] * 2, axis=1)
        sin = jnp.concatenate([-jnp.sin(ang_r), jnp.sin(ang_r), -jnp.sin(ang_c), jnp.sin(ang_c)], axis=1)
        cos_parts.append(jnp.tile(cos, (g.nb, 1)))
        sin_parts.append(jnp.tile(sin, (g.nb, 1)))
    return jnp.concatenate(cos_parts, axis=0), jnp.concatenate(sin_parts, axis=0)


def _attn_kernel(*refs, nsub, bounds, fv):
    q_ref, k_ref, v_ref, _, o_ref, s_ref, rm_ref, m_ref, acc_ref = refs
    tq = SEQ_BLK
    nc = len(bounds)
    assert nc % 2 == 0

    def q_rows(qi):
        return pl.ds(pl.multiple_of(qi * tq, tq), tq)

    def produce(qi, c, slot):
        start, size = bounds[c]
        q = jnp.concatenate([q_ref[q_rows(qi), g * C_HD:(g + 1) * C_HD] for g in range(C_GROUP)], axis=0)
        s = lax.dot_general(q, k_ref[start:start + size, :], (((1,), (1,)), ((), ())),
                            preferred_element_type=F32)
        if start < fv:
            key = start + lax.broadcasted_iota(jnp.int32, (1, size), 1)
            s = jnp.where(key >= fv, s, -1e30)
        s_ref[slot, :, :size] = s
        rm_ref[slot] = jnp.max(s, axis=-1, keepdims=True)

    def consume(c, slot):
        start, size = bounds[c]
        m_old = m_ref[...]
        m_new = jnp.maximum(m_old, rm_ref[slot])
        p = jnp.exp2(s_ref[slot, :, :size] - m_new).astype(BF16)
        acc_ref[...] = jnp.exp2(m_old - m_new) * acc_ref[...] + jnp.dot(p, v_ref[start:start + size, :],
                                                                        preferred_element_type=F32)
        m_ref[...] = m_new

    produce(0, 0, 0)

    def body(qi, carry):
        m_ref[...] = jnp.full_like(m_ref, -1e30)
        acc_ref[...] = jnp.zeros_like(acc_ref)
        for c in range(nc):
            if c + 1 < nc:
                produce(qi, c + 1, (c + 1) % 2)
            else:
                produce(jnp.minimum(qi + 1, nsub - 1), 0, 0)
            consume(c, c % 2)
        acc = acc_ref[...]
        out = acc[:, :C_HD] / acc[:, C_HD:]
        for g in range(C_GROUP):
            o_ref[q_rows(qi), g * C_HD:(g + 1) * C_HD] = out[g * tq:(g + 1) * tq].astype(o_ref.dtype)
        return carry

    lax.fori_loop(0, nsub, body, 0)


def _key_chunks(tp, fv):
    lo = fv // LANE
    blocks = tp // LANE - lo
    per = ATTN_MAX_TK // LANE
    nc = -(-blocks // per)
    nc += nc % 2
    pairs, single = divmod(blocks, 2)
    sizes = [2 * (pairs // nc + (1 if i < pairs % nc else 0)) for i in range(nc)]
    sizes[-1] += single
    if min(sizes) == 0:
        sizes = [blocks // nc + (1 if i < blocks % nc else 0) for i in range(nc)]
    assert min(sizes) > 0 and max(sizes) <= per
    starts = np.cumsum([lo] + sizes[:-1])
    return tuple((int(s) * LANE, int(z) * LANE) for s, z in zip(starts, sizes))


def _attention(qn, kn, vn, geom):
    out = jnp.zeros((geom.rows, C_HEADS * C_HD), BF16)
    for g in geom.groups:
        bounds = _key_chunks(g.tp, g.fv)
        nblk = g.tp // SEQ_BLK
        nsub = max(d for d in range(1, ATTN_MAX_SUB + 1) if nblk % d == 0)
        tq, gw, rows = nsub * SEQ_BLK, C_GROUP * C_HD, C_GROUP * SEQ_BLK
        al = lambda x: pl.multiple_of(x, LANE)
        q_spec = pl.BlockSpec((pl.Element(tq), pl.Element(gw)),
                              lambda b, h, i, g=g, tq=tq: (al(g.start + b * g.tp + i * tq), al(h * gw)))
        in_specs = [q_spec,
                    pl.BlockSpec((pl.Element(g.tp), pl.Element(C_HD)),
                                 lambda b, h, i, g=g: (al(g.start + b * g.tp), al(h * C_HD))),
                    pl.BlockSpec((pl.Element(g.tp), pl.Element(2 * C_HD)),
                                 lambda b, h, i, g=g: (al(g.start + b * g.tp), al(h * 2 * C_HD))),
                    pl.BlockSpec(memory_space=pl.ANY)]
        out = pl.pallas_call(
            functools.partial(_attn_kernel, nsub=nsub, bounds=bounds, fv=g.fv),
            out_shape=jax.ShapeDtypeStruct(out.shape, out.dtype),
            grid=(g.nb, C_KV_HEADS, g.tp // tq),
            in_specs=in_specs,
            out_specs=q_spec,
            scratch_shapes=[pltpu.VMEM((2, rows, max(z for _, z in bounds)), F32), pltpu.VMEM((2, rows, 1), F32),
                            pltpu.VMEM((rows, 1), F32), pltpu.VMEM((rows, 2 * C_HD), F32)],
            input_output_aliases={3: 0},
            compiler_params=_params(("parallel", "parallel", "parallel")),
            name="attention",
        )(qn, kn, vn, out)
    return out


def _ffn_kernel(x_ref, xp_ref, xn_ref, pre_ref, wu_ref, wg_ref, cw_ref, cb_ref, wo_ref, post_ref,
                o_ref, xs_ref, act_ref, acc_ref, *, tm):
    i = pl.program_id(0)
    xs_ref[0:HALO] = jnp.where(i > 0, _rms(xp_ref[...], pre_ref[...]), 0.0).astype(BF16)
    xs_ref[HALO:HALO + tm] = _rms(x_ref[...], pre_ref[...]).astype(BF16)
    xs_ref[HALO + tm:2 * HALO + tm] = jnp.where(i < pl.num_programs(0) - 1,
                                                _rms(xn_ref[...], pre_ref[...]), 0.0).astype(BF16)
    half = FFN_CONV // 2
    n_chunks = D_FF // FF_CHUNK
    for c in range(n_chunks):
        cs = slice(c * FF_CHUNK, (c + 1) * FF_CHUNK)
        u = jnp.dot(xs_ref[HALO:HALO + tm], wu_ref[:, cs], preferred_element_type=F32)
        g_win = jnp.dot(xs_ref[...], wg_ref[:, cs], preferred_element_type=F32)
        gate = cb_ref[0:1, cs]
        for j in range(FFN_CONV):
            gate = gate + _shifted_rows(g_win, j - half, tm) * cw_ref[j:j + 1, cs]
        th = jnp.tanh(gate * (GELU_C0 + GELU_C0 * 0.044715 * (gate * gate)))
        hu = 0.5 * gate * u
        act_ref[:, cs] = (hu + hu * th).astype(BF16)
        if c % FF_GROUP == FF_GROUP - 1 or c == n_chunks - 1:
            ks = slice((c // FF_GROUP) * FF_GROUP * FF_CHUNK, (c + 1) * FF_CHUNK)
            part = jnp.dot(act_ref[:, ks], wo_ref[ks, :], preferred_element_type=F32)
            if c < FF_GROUP:
                acc_ref[...] = part
            else:
                acc_ref[...] += part
    o_ref[...] = x_ref[...] + _rms(acc_ref[...], post_ref[...])


def _conv_ffn(h, pre_w, w_in, conv_w, conv_b, w_out, post_w, geom):
    tm, rows = geom.tm, geom.rows
    assert D_FF % FF_CHUNK == 0
    wu = w_in[:, :D_FF].astype(BF16)
    wg = w_in[:, D_FF:].astype(BF16)
    return pl.pallas_call(
        functools.partial(_ffn_kernel, tm=tm),
        out_shape=jax.ShapeDtypeStruct((rows, D_MODEL), F32),
        grid=(rows // tm,),
        in_specs=_halo_specs(tm, D_MODEL, rows) + [
            _const_spec((1, D_MODEL)), _const_spec((D_MODEL, D_FF)), _const_spec((D_MODEL, D_FF)),
            _const_spec((FFN_CONV, D_FF)), _const_spec((1, D_FF)), _const_spec((D_FF, D_MODEL)),
            _const_spec((1, D_MODEL))],
        out_specs=pl.BlockSpec((tm, D_MODEL), lambda i: (i, 0)),
        scratch_shapes=[pltpu.VMEM((tm + 2 * HALO, D_MODEL), BF16), pltpu.VMEM((tm, D_FF), BF16),
                        pltpu.VMEM((tm, D_MODEL), F32)],
        compiler_params=_params(),
        name="conv_ffn",
    )(h, h, h, pre_w.reshape(1, D_MODEL), wu, wg, conv_w, conv_b.reshape(1, D_FF),
      w_out.astype(BF16), post_w.reshape(1, D_MODEL))


def _meta_kernel(meta_ref, _, o_ref):
    o_ref[...] = meta_ref[...].astype(o_ref.dtype)


def _rows_in_kernel(x_ref, _, o_ref):
    o_ref[...] = x_ref[0]


def _rows_out_kernel(h_ref, o_ref):
    o_ref[0] = h_ref[...]


def _copy_block(n):
    return max(b for b in (2048, 1024, 512, 256, 128) if n % b == 0)


def _pack_rows(xs, meta, geom):
    al = lambda v: pl.multiple_of(v, LANE)
    shape = jax.ShapeDtypeStruct((geom.rows, D_MODEL), xs[0].dtype)
    h = jnp.zeros(shape.shape, shape.dtype)
    for x, g in zip(xs, geom.groups):
        pad = g.tp - g.n
        h = pl.pallas_call(
            _meta_kernel, out_shape=shape, grid=(g.nb,),
            in_specs=[_const_spec((N_META, D_MODEL)), pl.BlockSpec(memory_space=pl.ANY)],
            out_specs=pl.BlockSpec((pl.Element(N_META), pl.Element(D_MODEL)),
                                   lambda b, g=g: (pl.multiple_of(g.start + b * g.tp + g.fv, N_META), 0)),
            input_output_aliases={1: 0},
            compiler_params=_params(), name="pack_meta",
        )(meta, h)
        blk = _copy_block(g.n)
        h = pl.pallas_call(
            _rows_in_kernel, out_shape=shape, grid=(g.nb, g.n // blk),
            in_specs=[pl.BlockSpec((1, blk, D_MODEL), lambda b, j: (b, j, 0)), pl.BlockSpec(memory_space=pl.ANY)],
            out_specs=pl.BlockSpec((pl.Element(blk), pl.Element(D_MODEL)),
                                   lambda b, j, g=g, pad=pad, blk=blk: (al(g.start + b * g.tp + pad + j * blk), 0)),
            input_output_aliases={1: 0},
            compiler_params=_params(("parallel", "parallel")), name="pack_rows",
        )(x, h)
    return h


def _unpack_rows(h, geom):
    al = lambda v: pl.multiple_of(v, LANE)
    outs = []
    for g in geom.groups:
        pad = g.tp - g.n
        blk = _copy_block(g.n)
        outs.append(pl.pallas_call(
            _rows_out_kernel, out_shape=jax.ShapeDtypeStruct((g.nb, g.n, D_MODEL), h.dtype),
            grid=(g.nb, g.n // blk),
            in_specs=[pl.BlockSpec((pl.Element(blk), pl.Element(D_MODEL)),
                                   lambda b, j, g=g, pad=pad, blk=blk: (al(g.start + b * g.tp + pad + j * blk), 0))],
            out_specs=pl.BlockSpec((1, blk, D_MODEL), lambda b, j: (b, j, 0)),
            compiler_params=_params(("parallel", "parallel")), name="unpack_rows",
        )(h))
    return tuple(outs)


def _even_layer(h, j, geom, tables, ones, p):
    aw, bw = A_WIDTH, B_WIDTH
    a_cols = 4 * aw + 4 * A_HEADS
    w_in = p["e_w_in"][j]
    lora = B_W_LORA + B_A_LORA + B_G_LORA
    zeros = lambda n: jnp.zeros((D_MODEL, n), F32)
    w_cat = jnp.concatenate([
        w_in[:, :3 * aw],
        w_in[:, a_cols:a_cols + 3 * bw + lora], zeros(B_LORA_PAD - lora),
        w_in[:, 4 * aw:a_cols], zeros(LANE - 4 * A_HEADS),
        w_in[:, 3 * aw:4 * aw]], axis=1).astype(BF16)
    mu = jnp.pad(p["b_shift"][j], ((0, 0), (0, B_LORA_PAD - lora)))
    lora_rows = lambda m, off: jnp.pad(m, ((off, B_LORA_PAD - off - m.shape[0]), (0, 0)))
    w2 = lora_rows(jnp.concatenate([p["b_w2"][j, 0], p["b_w2"][j, 1]], axis=1), 0)
    a2 = lora_rows(p["b_a2"][j], B_W_LORA)
    g2 = lora_rows(p["b_g2"][j], B_W_LORA + B_A_LORA)
    qn, kn, vv, gts, z, r, k, v, kkn, eta, cf, cb, bonus, gate = _even_prep(
        h, p["mix_pre_norm"][2 * j], w_cat, p["a_conv_w"][j], p["a_log"][j], p["a_dt_bias"][j],
        mu, p["b_w0"][j].reshape(1, 2 * bw), w2, p["b_a0"][j], a2, g2,
        p["b_k_k"][j], p["b_k_a"][j], p["b_r_k"][j], ones, geom)
    oaf, oab = _a_scan(qn, kn, vv, gts, geom, tables)
    ybf, ybb = _b_scan(r, k, v, kkn, eta, cf, cb, geom, tables)

    return _mixout_even(h, oaf, oab, z, ybf, ybb, bonus, gate, p["e_w_out"][j].astype(BF16),
                        p["a_out_norm"][j], p["b_ln_w"][j], p["b_ln_b"][j], p["mix_post_norm"][2 * j], ones, geom)


def _odd_layer(h, j, geom, rope, p):
    qn, kn, vn = _qkv_proj(h, p["mix_pre_norm"][2 * j + 1], p["o_w_qkv"][j].astype(BF16), rope[0], rope[1],
                           p["o_q_norm"][j], p["o_k_norm"][j], geom)
    o = _attention(qn, kn, vn, geom)
    return _mixout_odd(h, o, p["o_w_out"][j].astype(BF16), p["mix_post_norm"][2 * j + 1], geom)


def kernel(x_prompt, x_sample, meta, mix_pre_norm, mix_post_norm, ffn_pre_norm, ffn_post_norm, e_w_in, a_conv_w, a_log, a_dt_bias, a_out_norm, b_shift, b_w0, b_w2, b_a0, b_a2, b_g2, b_k_k, b_k_a, b_r_k, b_ln_w, b_ln_b, e_w_out, o_w_qkv, o_q_norm, o_k_norm, o_w_out, f_w_in, f_conv_w, f_conv_b, f_w_out):
    p = dict(mix_pre_norm=mix_pre_norm, mix_post_norm=mix_post_norm, e_w_in=e_w_in, a_conv_w=a_conv_w,
             a_log=a_log, a_dt_bias=a_dt_bias, a_out_norm=a_out_norm, b_shift=b_shift, b_w0=b_w0, b_w2=b_w2,
             b_a0=b_a0, b_a2=b_a2, b_g2=b_g2, b_k_k=b_k_k, b_k_a=b_k_a, b_r_k=b_r_k, b_ln_w=b_ln_w,
             b_ln_b=b_ln_b, e_w_out=e_w_out, o_w_qkv=o_w_qkv, o_q_norm=o_q_norm, o_k_norm=o_k_norm,
             o_w_out=o_w_out)
    xs = (x_prompt, x_sample)
    geom = _choose_geom([(x.shape[0], x.shape[1]) for x in xs])
    tables = _scan_tables(geom)
    rope = _rope_tables(geom)
    seg = np.arange(SEG_TILE) // B_HD
    ones = jnp.asarray(seg[:, None] == seg[None, :], BF16)

    h = _pack_rows(xs, meta, geom)
    for i in range(DEPTH):
        h = _even_layer(h, i // 2, geom, tables, ones, p) if i % 2 == 0 else _odd_layer(h, i // 2, geom, rope, p)
        h = _conv_ffn(h, ffn_pre_norm[i], f_w_in[i], f_conv_w[i], f_conv_b[i], f_w_out[i], ffn_post_norm[i], geom)
    return _unpack_rows(h, geom)
```

```python
import functools
import itertools
from typing import NamedTuple

import numpy as np
import jax
import jax.numpy as jnp
from jax import lax
from jax.experimental import pallas as pl
from jax.experimental.pallas import tpu as pltpu

F32 = jnp.float32
BF16 = jnp.bfloat16

D_MODEL = 1024
DEPTH = 2
N_META = 16
GRID_W = 64
NORM_EPS = 1e-6

A_HEADS = 4
A_DK = 128
A_DV = 128
A_CONV = 5
A_WIDTH = A_HEADS * A_DV

B_HEADS = 8
B_HD = 64
B_WIDTH = B_HEADS * B_HD
B_W_LORA = 32
B_A_LORA = 32
B_G_LORA = 96
B_GN_EPS = 64e-5
B_LORA_PAD = 256

C_HEADS = 8
C_KV_HEADS = 2
C_GROUP = C_HEADS // C_KV_HEADS
C_HD = 128
ROPE_THETA = 10000.0

D_FF = 2816
FFN_CONV = 3

LANE = 128
CHUNK = 64
SEQ_BLK = 128
HALO = 16
FF_CHUNK = 256
FF_GROUP = 4
ATTN_MAX_TK = 2304
ATTN_MAX_SUB = 17
SEG_TILE = 256
LOG2E = 1.4426950408889634
GELU_C0 = 0.7978845608028654
VMEM_LIMIT = 56 * 1024 * 1024


class _Group(NamedTuple):
    start: int
    nb: int
    n: int
    tp: int
    fv: int


class _Geom(NamedTuple):
    groups: tuple
    rows: int
    tm: int


def _choose_geom(shapes):
    tm = 512 if min(n for _, n in shapes) >= 512 else 128
    best = None
    for pads in itertools.product((128, 256, 384, 512), repeat=len(shapes)):
        start, ok, groups = 0, True, []
        for (nb, n), p in zip(shapes, pads):
            ok = ok and start % tm == 0
            groups.append(_Group(start, nb, n, n + p, p - N_META))
            start += nb * (n + p)
        if ok and start % tm == 0 and (best is None or start < best.rows):
            best = _Geom(tuple(groups), start, tm)
    assert best is not None
    return best


def _row_valid(geom, base, tm):
    start = jnp.int32(geom.groups[0].start)
    tp = jnp.int32(geom.groups[0].tp)
    fv = jnp.int32(geom.groups[0].fv)
    for g in geom.groups[1:]:
        assert tm <= g.tp
        inside = base >= g.start
        start = jnp.where(inside, g.start, start)
        tp = jnp.where(inside, g.tp, tp)
        fv = jnp.where(inside, g.fv, fv)
    pos = lax.rem(base - start, tp) + lax.broadcasted_iota(jnp.int32, (tm, 1), 0)
    pos = jnp.where(pos >= tp, pos - tp, pos)
    return pos >= fv


def _rms(x, w, eps=NORM_EPS):
    return x * lax.rsqrt(jnp.mean(x * x, axis=-1, keepdims=True) + eps) * w


def _sigmoid(x):
    return 1.0 / (1.0 + jnp.exp(-x))


def _softplus(x):
    return jnp.maximum(x, 0.0) + jnp.log1p(jnp.exp(-jnp.abs(x)))


def _dot(a, b):
    return jnp.dot(a.astype(BF16), b.astype(BF16), preferred_element_type=F32)


def _dot_nt(a, b):
    return lax.dot_general(a.astype(BF16), b.astype(BF16), (((1,), (1,)), ((), ())),
                           preferred_element_type=F32)


def _seg_sum(x, ones_ref, terms):
    acc = None
    rem = x
    for t in range(terms):
        piece = rem.astype(BF16)
        part = jnp.concatenate([jnp.dot(piece[:, s:s + SEG_TILE], ones_ref[...], preferred_element_type=F32)
                                for s in range(0, x.shape[1], SEG_TILE)], axis=1)
        acc = part if acc is None else acc + part
        if t + 1 < terms:
            rem = rem - piece.astype(F32)
    return acc


def _chunk_cumsum(x, reverse):
    tm = x.shape[0]
    pos = lax.broadcasted_iota(jnp.int32, (tm, 1), 0) & (CHUNK - 1)
    s = 1
    while s < CHUNK:
        if reverse:
            x = x + jnp.where(pos < CHUNK - s, pltpu.roll(x, tm - s, axis=0), 0.0)
        else:
            x = x + jnp.where(pos >= s, pltpu.roll(x, s, axis=0), 0.0)
        s *= 2
    return x


def _neumann_inverses(n_mats, eye):
    n = eye.shape[0]
    prods = [eye + m for m in n_mats]
    powers = [_dot(m, m) for m in n_mats]
    s = 4
    while s < CHUNK:
        both = [_dot(jnp.concatenate([pr, pw], axis=0), pw) for pr, pw in zip(prods, powers)]
        prods = [pr + b[:n] for pr, b in zip(prods, both)]
        powers = [b[n:] for b in both]
        s *= 2
    return [pr + _dot(pr, pw) for pr, pw in zip(prods, powers)]


def _shifted_rows(win, d, tm):
    if d == 0:
        return win[HALO:HALO + tm]
    return pltpu.roll(win, (-d) % win.shape[0], axis=0)[HALO:HALO + tm]


def _halo_specs(tm, width, rows):
    per = tm // HALO
    last = rows // HALO - 1
    return [
        pl.BlockSpec((tm, width), lambda i: (i, 0)),
        pl.BlockSpec((HALO, width), lambda i: (jnp.maximum(i * per - 1, 0), 0)),
        pl.BlockSpec((HALO, width), lambda i: (jnp.minimum((i + 1) * per, last), 0)),
    ]


def _const_spec(shape):
    return pl.BlockSpec(shape, lambda *_: (0,) * len(shape))


def _params(sem=("parallel",)):
    return pltpu.CompilerParams(dimension_semantics=sem, vmem_limit_bytes=VMEM_LIMIT)


def _a_prep_body(xw_ref, al, cw_ref, alog_ref, dtb_ref, q_ref, k_ref, v_ref, go_ref, valid, tm):
    half = A_CONV // 2
    for c, o_ref in enumerate((q_ref, k_ref, v_ref)):
        cs = slice(c * A_WIDTH, (c + 1) * A_WIDTH)
        win = xw_ref[:, cs]
        acc = _shifted_rows(win, -half, tm) * cw_ref[0:1, cs]
        for j in range(1, A_CONV):
            acc = acc + _shifted_rows(win, j - half, tm) * cw_ref[j:j + 1, cs]
        y = jnp.where(valid, acc * _sigmoid(acc), 0.0)
        for hd in range(A_HEADS):
            hs = slice(hd * A_DK, (hd + 1) * A_DK)
            yh = y[:, hs]
            if c < 2:
                yh = yh * lax.rsqrt(jnp.sum(yh * yh, axis=-1, keepdims=True) + 1e-6)
                if c == 0:
                    yh = yh * (A_DK ** -0.5)
            o_ref[:, hs] = yh.astype(o_ref.dtype)
    lane = lax.broadcasted_iota(jnp.int32, (1, LANE), 1)
    gval = jnp.where(valid, -jnp.exp(alog_ref[...]) * _softplus(al + dtb_ref[...]), 0.0)
    beta = jnp.where(valid, _sigmoid(al), 0.0)
    cum_f = _chunk_cumsum(gval, False)
    cum_b = _chunk_cumsum(gval, True)
    go_ref[...] = jnp.where(lane < A_HEADS, cum_f,
                            jnp.where(lane < 2 * A_HEADS, cum_b,
                                      jnp.where(lane < 4 * A_HEADS, beta, 0.0)))


def _a_scan_kernel(fi_ref, bi_ref, fr_ref,
                   qkvf, gf, gtf, qkvb, gb, gtb, of, ob, sf, sb):
    del fi_ref, bi_ref
    qf, kf, vf = (qkvf.at[:, c * A_WIDTH:(c + 1) * A_WIDTH] for c in range(3))
    qb, kb, vb = (qkvb.at[:, c * A_WIDTH:(c + 1) * A_WIDTH] for c in range(3))
    @pl.when(fr_ref[pl.program_id(0)] == 1)
    def _():
        sf[...] = jnp.zeros_like(sf)
        sb[...] = jnp.zeros_like(sb)
    r = lax.broadcasted_iota(jnp.int32, (CHUNK, CHUNK), 0)
    c = lax.broadcasted_iota(jnp.int32, (CHUNK, CHUNK), 1)
    eye = jnp.where(r == c, 1.0, 0.0)
    nc = SEQ_BLK // CHUNK
    dirs = ((qf, kf, vf, gf, gtf, of, sf, False, r >= c, r > c),
            (qb, kb, vb, gb, gtb, ob, sb, True, r <= c, r < c))

    probs = []
    for d, (q_ref, k_ref, v_ref, g_ref, gt_ref, _, _, reverse, tri, strict) in enumerate(dirs):
        edge = 0 if reverse else CHUNK - 1
        for step in range(nc):
            jj = nc - 1 - step if reverse else step
            rows = slice(jj * CHUNK, (jj + 1) * CHUNK)
            for hd in range(A_HEADS):
                hs = slice(hd * A_DK, (hd + 1) * A_DK)
                col = hd + (A_HEADS if reverse else 0)
                q, k, v = (ref[rows, hs].astype(F32) for ref in (q_ref, k_ref, v_ref))
                g_col = g_ref[rows, col:col + 1]
                b_col = g_ref[rows, 2 * A_HEADS + col:2 * A_HEADS + col + 1]
                g_row = gt_ref[jj, col:col + 1, :]
                decay = jnp.where(tri, jnp.exp(jnp.where(tri, g_col - g_row, 0.0)), 0.0)
                kb_ = k * b_col
                e_g = jnp.exp(g_col)
                g_last = g_col[edge:edge + 1]
                probs.append(dict(d=d, step=step, hd=hd, rows=rows, hs=hs, tri=tri, strict=strict, decay=decay,
                                  k=k, kbq=jnp.concatenate([kb_, q], axis=0),
                                  rhs=jnp.concatenate([v * b_col, kb_ * e_g], axis=1),
                                  qg=q * e_g, kg_t=(k * jnp.exp(g_last - g_col)).T, g_end=jnp.exp(g_last)))
    for p in probs:
        kq = _dot_nt(p["kbq"], p["k"])
        p["l"] = jnp.where(p["strict"], kq[:CHUNK] * p["decay"], 0.0)
        p["attn"] = jnp.where(p["tri"], kq[CHUNK:] * p["decay"], 0.0)
    t_invs = _neumann_inverses([-p["l"] for p in probs], eye)
    for p, t_inv in zip(probs, t_invs):
        p["uw"] = _dot(t_inv, p["rhs"])

    states = {(d, hd): dirs[d][6][hd] for d in range(2) for hd in range(A_HEADS)}
    for step in range(nc):
        cur = [p for p in probs if p["step"] == step]
        v_news = [p["uw"][:, :A_DV] - _dot(p["uw"][:, A_DV:], states[p["d"], p["hd"]]) for p in cur]
        for p, v_new in zip(cur, v_news):
            state = states[p["d"], p["hd"]]
            dirs[p["d"]][5][p["rows"], p["hs"]] = _dot(jnp.concatenate([p["qg"], p["attn"]], axis=1),
                                                      jnp.concatenate([state, v_new], axis=0))
        for p, v_new in zip(cur, v_news):
            key = (p["d"], p["hd"])
            states[key] = states[key] * p["g_end"] + _dot(p["kg_t"], v_new)
    for (d, hd), state in states.items():
        dirs[d][6][hd] = state


def _scan_tables(geom):
    fi, bi, fr = [], [], []
    for g in geom.groups:
        nblk = g.tp // SEQ_BLK
        for b in range(g.nb):
            base = (g.start + b * g.tp) // SEQ_BLK
            for i in range(nblk):
                fi.append(base + i)
                bi.append(base + nblk - 1 - i)
                fr.append(1 if i == 0 else 0)
    mk = lambda v: jnp.asarray(np.asarray(v, np.int32))
    return mk(fi), mk(bi), mk(fr)


def _a_scan(qkv, gts, geom, tables):
    rows = geom.rows
    nc = SEQ_BLK // CHUNK
    gts_t = gts[:, :4 * A_HEADS].reshape(rows // CHUNK, CHUNK, 4 * A_HEADS).transpose(0, 2, 1)
    wide_f = pl.BlockSpec((SEQ_BLK, A_WIDTH), lambda s, fi, bi, fr: (fi[s], 0))
    wide_b = pl.BlockSpec((SEQ_BLK, A_WIDTH), lambda s, fi, bi, fr: (bi[s], 0))
    qkv_f = pl.BlockSpec((SEQ_BLK, 3 * A_WIDTH), lambda s, fi, bi, fr: (fi[s], 0))
    qkv_b = pl.BlockSpec((SEQ_BLK, 3 * A_WIDTH), lambda s, fi, bi, fr: (bi[s], 0))
    gate_f = pl.BlockSpec((SEQ_BLK, LANE), lambda s, fi, bi, fr: (fi[s], 0))
    gate_b = pl.BlockSpec((SEQ_BLK, LANE), lambda s, fi, bi, fr: (bi[s], 0))
    gt_f = pl.BlockSpec((nc, 4 * A_HEADS, CHUNK), lambda s, fi, bi, fr: (fi[s], 0, 0))
    gt_b = pl.BlockSpec((nc, 4 * A_HEADS, CHUNK), lambda s, fi, bi, fr: (bi[s], 0, 0))
    out = jax.ShapeDtypeStruct((rows, A_WIDTH), F32)
    return pl.pallas_call(
        _a_scan_kernel,
        out_shape=[out, out],
        grid_spec=pltpu.PrefetchScalarGridSpec(
            num_scalar_prefetch=3,
            grid=(tables[0].shape[0],),
            in_specs=[qkv_f, gate_f, gt_f, qkv_b, gate_b, gt_b],
            out_specs=[wide_f, wide_b],
            scratch_shapes=[pltpu.VMEM((A_HEADS, A_DK, A_DV), F32)] * 2),
        compiler_params=_params(("arbitrary",)),
        name="a_scan",
    )(*tables, qkv, gts, gts_t, qkv, gts, gts_t)


def _b_prep_body(xw_ref, lw_ref, mu_ref, w0_ref, w2_ref, a0_ref, a2_ref, g2_ref, kk_ref, ka_ref, rk_ref, ones_ref,
                 r_out, k_out, v_out, kkn_out, eta_out, cf_out, cb_out, bonus_out, gate_out, tm):
    def shifted(win_ref, cs, mu_off):
        win = win_ref[:, cs]
        cur, prev, nxt = _shifted_rows(win, 0, tm), _shifted_rows(win, -1, tm), _shifted_rows(win, 1, tm)
        ms = slice(mu_off + cs.start, mu_off + cs.stop)
        return cur + mu_ref[0:1, ms] * (prev - cur) + mu_ref[1:2, ms] * (nxt - cur)

    w = B_WIDTH
    r = shifted(xw_ref, slice(0, w), 0)
    k = shifted(xw_ref, slice(w, 2 * w), 0)
    v = shifted(xw_ref, slice(2 * w, 3 * w), 0)
    lo = shifted(lw_ref, slice(0, B_LORA_PAD), 3 * w)
    wl = _dot(jnp.tanh(lo), w2_ref[...]) + w0_ref[...]
    log_decay = -float(np.exp(-0.5)) * _sigmoid(wl)
    cf_out[...] = _chunk_cumsum(log_decay[:, :w], False)
    cb_out[...] = _chunk_cumsum(log_decay[:, w:], True)
    eta = _sigmoid(a0_ref[...] + _dot(lo, a2_ref[...]))
    gate_out[...] = _dot(_sigmoid(lo), g2_ref[...])
    kx = k * kk_ref[...]
    kkn_out[...] = (kx * lax.rsqrt(_seg_sum(kx * kx, ones_ref, 2) + 1e-6)).astype(kkn_out.dtype)
    k = k * (1.0 + (eta - 1.0) * ka_ref[...])
    bonus_out[...] = _seg_sum(r * k * rk_ref[...], ones_ref, 2) * v
    r_out[...] = r.astype(r_out.dtype)
    k_out[...] = k.astype(k_out.dtype)
    v_out[...] = v.astype(v_out.dtype)
    eta_out[...] = eta.astype(eta_out.dtype)


def _even_prep_kernel(h_ref, hp_ref, hn_ref, nw_ref, w_ref, cw_ref, alog_ref, dtb_ref,
                      mu_ref, w0_ref, w2_ref, a0_ref, a2_ref, g2_ref, kk_ref, ka_ref, rk_ref, ones_ref,
                      qkv_out, go_out, z_out, rkv_out, cf_out, cb_out, bonus_out, gate_out,
                      xn_ref, aw_ref, bw_ref, lw_ref, *, geom):
    q_out, ka_out, va_out = (qkv_out.at[:, c * A_WIDTH:(c + 1) * A_WIDTH] for c in range(3))
    r_out, kb_out, vb_out, kkn_out, eta_out = (rkv_out.at[:, c * B_WIDTH:(c + 1) * B_WIDTH] for c in range(5))
    tm = geom.tm
    i = pl.program_id(0)
    valid = _row_valid(geom, i * tm, tm)
    xn_ref[0:HALO] = jnp.where(i > 0, _rms(hp_ref[...], nw_ref[...]), 0.0).astype(BF16)
    xn_ref[HALO:HALO + tm] = _rms(h_ref[...], nw_ref[...]).astype(BF16)
    xn_ref[HALO + tm:2 * HALO + tm] = jnp.where(i < pl.num_programs(0) - 1,
                                                _rms(hn_ref[...], nw_ref[...]), 0.0).astype(BF16)
    aw, bw = 3 * A_WIDTH, 3 * B_WIDTH
    c0 = 0
    for win_ref, wd in ((aw_ref, aw), (bw_ref, bw), (lw_ref, B_LORA_PAD)):
        for s in range(0, wd, 512):
            e = min(s + 512, wd)
            win_ref[:, s:e] = jnp.dot(xn_ref[...], w_ref[:, c0 + s:c0 + e], preferred_element_type=F32)
        c0 += wd
    centre = xn_ref[HALO:HALO + tm]
    gates = jnp.dot(centre, w_ref[:, c0:c0 + LANE], preferred_element_type=F32)
    z_out[...] = jnp.dot(centre, w_ref[:, c0 + LANE:], preferred_element_type=F32).astype(z_out.dtype)
    _a_prep_body(aw_ref, gates, cw_ref, alog_ref, dtb_ref, q_out, ka_out, va_out, go_out, valid, tm)
    _b_prep_body(bw_ref, lw_ref, mu_ref, w0_ref, w2_ref, a0_ref, a2_ref, g2_ref, kk_ref, ka_ref, rk_ref, ones_ref,
                 r_out, kb_out, vb_out, kkn_out, eta_out, cf_out, cb_out, bonus_out, gate_out, tm)


def _even_prep(h, norm_w, w_bf16, conv_w, a_log, dt_bias, mu, w0, w2, a0, a2, g2, k_k, k_a, r_k, ones, geom):
    tm, rows = geom.tm, geom.rows
    aw, bw = 3 * A_WIDTH, 3 * B_WIDTH
    row = lambda v: v.reshape(1, -1).astype(F32)
    pad = lambda v: jnp.pad(row(v), ((0, 0), (0, LANE - v.size)))
    narrow = lambda wd: jax.ShapeDtypeStruct((rows, wd), BF16)
    wide = lambda wd: jax.ShapeDtypeStruct((rows, wd), F32)
    blk = lambda wd: pl.BlockSpec((tm, wd), lambda i: (i, 0))
    win = tm + 2 * HALO
    return pl.pallas_call(
        functools.partial(_even_prep_kernel, geom=geom),
        out_shape=[narrow(3 * A_WIDTH), wide(LANE), narrow(A_WIDTH), narrow(5 * B_WIDTH)] + [wide(B_WIDTH)] * 4,
        grid=(rows // tm,),
        in_specs=_halo_specs(tm, D_MODEL, rows) + [
            _const_spec((1, D_MODEL)), _const_spec(w_bf16.shape),
            _const_spec((A_CONV, aw)), _const_spec((1, LANE)), _const_spec((1, LANE)),
            _const_spec((2, bw + B_LORA_PAD)), _const_spec((1, 2 * B_WIDTH)), _const_spec((B_LORA_PAD, 2 * B_WIDTH)),
            _const_spec((1, B_WIDTH)), _const_spec((B_LORA_PAD, B_WIDTH)), _const_spec((B_LORA_PAD, B_WIDTH)),
            _const_spec((1, B_WIDTH)), _const_spec((1, B_WIDTH)), _const_spec((1, B_WIDTH)),
            _const_spec((SEG_TILE, SEG_TILE))],
        out_specs=[blk(3 * A_WIDTH), blk(LANE), blk(A_WIDTH), blk(5 * B_WIDTH)] + [blk(B_WIDTH)] * 4,
        scratch_shapes=[pltpu.VMEM((win, D_MODEL), BF16), pltpu.VMEM((win, aw), F32), pltpu.VMEM((win, bw), F32),
                        pltpu.VMEM((win, B_LORA_PAD), F32)],
        compiler_params=_params(),
        name="even_prep",
    )(h, h, h, row(norm_w), w_bf16, conv_w, pad(a_log), pad(dt_bias), mu, w0, w2, row(a0), a2, g2,
      row(k_k), row(k_a), row(r_k), ones)


def _b_scan_kernel(fi_ref, bi_ref, fr_ref,
                   rkvf, cf, rkvb, cb, yf, yb, pf, pb):
    del fi_ref, bi_ref
    rf, kf, vf, kkf, ef = (rkvf.at[:, c * B_WIDTH:(c + 1) * B_WIDTH] for c in range(5))
    rb, kb, vb, kkb, eb = (rkvb.at[:, c * B_WIDTH:(c + 1) * B_WIDTH] for c in range(5))
    @pl.when(fr_ref[pl.program_id(0)] == 1)
    def _():
        pf[...] = jnp.zeros_like(pf)
        pb[...] = jnp.zeros_like(pb)
    n2 = 2 * CHUNK
    r = lax.broadcasted_iota(jnp.int32, (n2, n2), 0)
    c = lax.broadcasted_iota(jnp.int32, (n2, n2), 1)
    same = (r >= CHUNK) == (c >= CHUNK)
    rt, ct = r & (CHUNK - 1), c & (CHUNK - 1)
    eye = jnp.where(r == c, 1.0, 0.0)
    lane_lo = lax.broadcasted_iota(jnp.int32, (1, n2), 1) < B_HD
    row_pos = lax.broadcasted_iota(jnp.int32, (CHUNK, 1), 0)
    nc = SEQ_BLK // CHUNK
    pair_w = 2 * B_HD
    n_pairs = B_HEADS // 2
    dirs = (((rf, kf, vf, kkf, ef, cf), yf, pf, False, same & (rt > ct), same & (rt >= ct)),
            ((rb, kb, vb, kkb, eb, cb), yb, pb, True, same & (rt < ct), same & (rt <= ct)))

    def expand(x):
        return jnp.concatenate([jnp.where(lane_lo, x, 0.0), jnp.where(lane_lo, 0.0, x)], axis=0)

    probs = []
    for d, (refs, _, _, reverse, strict, incl) in enumerate(dirs):
        r_ref, k_ref, v_ref, kk_ref, eta_ref, c_ref = refs
        for step in range(nc):
            jj = nc - 1 - step if reverse else step
            rows = slice(jj * CHUNK, (jj + 1) * CHUNK)
            for pr in range(n_pairs):
                ls = slice(pr * pair_w, (pr + 1) * pair_w)
                r, k, v, kk, eta = (ref[rows, ls].astype(F32) for ref in (r_ref, k_ref, v_ref, kk_ref, eta_ref))
                c_in = c_ref[rows, ls]
                if reverse:
                    c_ex = jnp.where(row_pos < CHUNK - 1, pltpu.roll(c_in, CHUNK - 1, axis=0), 0.0)
                    c_tot = c_in[0:1]
                else:
                    c_ex = jnp.where(row_pos >= 1, pltpu.roll(c_in, 1, axis=0), 0.0)
                    c_tot = c_in[CHUNK - 1:CHUNK]
                b = kk * eta
                inv_w = jnp.exp(-c_in)
                rest_w = jnp.exp(c_tot - c_in)
                a_e = expand(-kk * jnp.exp(c_ex))
                r_e = expand(r * jnp.exp(c_in))
                probs.append(dict(
                    d=d, step=step, pr=pr, rows=rows, ls=ls, strict=strict, incl=incl, v=v,
                    a_e=a_e, r_e=r_e, v_e=expand(v),
                    lhs=jnp.concatenate([a_e, r_e], axis=0),
                    rhs=jnp.concatenate([expand(b * inv_w), expand(k * inv_w)], axis=0),
                    kd_t=jnp.concatenate([b * rest_w, k * rest_w], axis=0).T,
                    w_col=jnp.sum(jnp.where(eye > 0.0, jnp.exp(c_tot), 0.0), axis=1, keepdims=True)))
    for p in probs:
        m1 = _dot_nt(p["lhs"], p["rhs"])
        p["n_ab"] = jnp.where(p["strict"], m1[:n2, :n2], 0.0)
        p["a_ak"] = jnp.where(p["strict"], m1[:n2, n2:], 0.0)
        p["a_r"] = jnp.concatenate([jnp.where(p["incl"], m1[n2:, :n2], 0.0),
                                    jnp.where(p["incl"], m1[n2:, n2:], 0.0)], axis=1)
    t_invs = _neumann_inverses([p["n_ab"] for p in probs], eye)
    for p in probs:
        p["akv"] = _dot(p["a_ak"], p["v_e"])
    for p, t_inv in zip(probs, t_invs):
        p["x"] = _dot(t_inv, jnp.concatenate([p["akv"], p["a_e"]], axis=1))

    states = {(d, pr): dirs[d][2][pr] for d in range(2) for pr in range(n_pairs)}
    for step in range(nc):
        cur = [p for p in probs if p["step"] == step]
        u_es = [p["x"][:, :pair_w] + _dot(p["x"][:, pair_w:], states[p["d"], p["pr"]]) for p in cur]
        for p, u_e in zip(cur, u_es):
            y_e = _dot(jnp.concatenate([p["r_e"], p["a_r"]], axis=1),
                       jnp.concatenate([states[p["d"], p["pr"]], u_e, p["v_e"]], axis=0))
            dirs[p["d"]][1][p["rows"], p["ls"]] = y_e[:CHUNK] + y_e[CHUNK:]
        for p, u_e in zip(cur, u_es):
            key = (p["d"], p["pr"])
            upd = _dot(p["kd_t"], jnp.concatenate([u_e[:CHUNK] + u_e[CHUNK:], p["v"]], axis=0))
            states[key] = states[key] * p["w_col"] + jnp.where(same, upd, 0.0)
    for (d, pr), state in states.items():
        dirs[d][2][pr] = state


def _b_scan(rkv, cf, cb, geom, tables):
    wide_f = pl.BlockSpec((SEQ_BLK, B_WIDTH), lambda s, fi, bi, fr: (fi[s], 0))
    wide_b = pl.BlockSpec((SEQ_BLK, B_WIDTH), lambda s, fi, bi, fr: (bi[s], 0))
    rkv_f = pl.BlockSpec((SEQ_BLK, 5 * B_WIDTH), lambda s, fi, bi, fr: (fi[s], 0))
    rkv_b = pl.BlockSpec((SEQ_BLK, 5 * B_WIDTH), lambda s, fi, bi, fr: (bi[s], 0))
    out = jax.ShapeDtypeStruct((geom.rows, B_WIDTH), F32)
    return pl.pallas_call(
        _b_scan_kernel,
        out_shape=[out, out],
        grid_spec=pltpu.PrefetchScalarGridSpec(
            num_scalar_prefetch=3,
            grid=(tables[0].shape[0],),
            in_specs=[rkv_f, wide_f, rkv_b, wide_b],
            out_specs=[wide_f, wide_b],
            scratch_shapes=[pltpu.VMEM((B_HEADS // 2, 2 * B_HD, 2 * B_HD), F32)] * 2),
        compiler_params=_params(("arbitrary",)),
        name="b_scan",
    )(*tables, rkv, cf, rkv, cb)


def _finish_residual(h, mix, post_w, valid):
    return jnp.where(valid, h + _rms(mix, post_w), 0.0)


def _mixout_even_kernel(h_ref, oaf_ref, oab_ref, z_ref, ybf_ref, ybb_ref, bonus_ref, gate_ref,
                        w_ref, an_ref, lnw_ref, lnb_ref, post_ref, ones_ref, o_ref, cat_ref, *, geom):
    tm = geom.tm
    valid = _row_valid(geom, pl.program_id(0) * tm, tm)
    for hd in range(A_HEADS):
        hs = slice(hd * A_DV, (hd + 1) * A_DV)
        o = oaf_ref[:, hs] + oab_ref[:, hs]
        z = z_ref[:, hs].astype(F32)
        cat_ref[:, hs] = (_rms(o, an_ref[...]) * (z * _sigmoid(z))).astype(BF16)
    y = ybf_ref[...] + ybb_ref[...]
    yc = y - _seg_sum(y, ones_ref, 3) * (1.0 / B_HD)
    var = _seg_sum(yc * yc, ones_ref, 2) * (1.0 / B_HD)
    yn = yc * lax.rsqrt(var + B_GN_EPS) * lnw_ref[...] + lnb_ref[...]
    cat_ref[:, A_WIDTH:] = ((yn + bonus_ref[...]) * gate_ref[...]).astype(BF16)
    mix = jnp.dot(cat_ref[...], w_ref[...], preferred_element_type=F32)
    o_ref[...] = _finish_residual(h_ref[...], mix, post_ref[...], valid)


def _mixout_even(h, oaf, oab, z, ybf, ybb, bonus, gate, w_out, a_norm, ln_w, ln_b, post_w, ones, geom):
    tm, rows = geom.tm, geom.rows
    row = lambda v: v.reshape(1, -1).astype(F32)
    a_blk = pl.BlockSpec((tm, A_WIDTH), lambda i: (i, 0))
    b_blk = pl.BlockSpec((tm, B_WIDTH), lambda i: (i, 0))
    return pl.pallas_call(
        functools.partial(_mixout_even_kernel, geom=geom),
        out_shape=jax.ShapeDtypeStruct((rows, D_MODEL), F32),
        grid=(rows // tm,),
        in_specs=[pl.BlockSpec((tm, D_MODEL), lambda i: (i, 0)), a_blk, a_blk,
                  a_blk,
                  b_blk, b_blk, b_blk, b_blk,
                  _const_spec((A_WIDTH + B_WIDTH, D_MODEL)), _const_spec((1, A_DV)),
                  _const_spec((1, B_WIDTH)), _const_spec((1, B_WIDTH)), _const_spec((1, D_MODEL)),
                  _const_spec((SEG_TILE, SEG_TILE))],
        out_specs=pl.BlockSpec((tm, D_MODEL), lambda i: (i, 0)),
        scratch_shapes=[pltpu.VMEM((tm, A_WIDTH + B_WIDTH), BF16)],
        compiler_params=_params(),
        name="mixout_even",
    )(h, oaf, oab, z, ybf, ybb, bonus, gate, w_out, row(a_norm), row(ln_w), row(ln_b), row(post_w), ones)


def _mixout_odd_kernel(h_ref, o_ref_in, w_ref, post_ref, o_ref, *, geom):
    tm = geom.tm
    valid = _row_valid(geom, pl.program_id(0) * tm, tm)
    mix = jnp.dot(o_ref_in[...], w_ref[...], preferred_element_type=F32)
    o_ref[...] = _finish_residual(h_ref[...], mix, post_ref[...], valid)


def _mixout_odd(h, o, w_out, post_w, geom):
    tm, rows = geom.tm, geom.rows
    wd = C_HEADS * C_HD
    return pl.pallas_call(
        functools.partial(_mixout_odd_kernel, geom=geom),
        out_shape=jax.ShapeDtypeStruct((rows, D_MODEL), F32),
        grid=(rows // tm,),
        in_specs=[pl.BlockSpec((tm, D_MODEL), lambda i: (i, 0)), pl.BlockSpec((tm, wd), lambda i: (i, 0)),
                  _const_spec((wd, D_MODEL)), _const_spec((1, D_MODEL))],
        out_specs=pl.BlockSpec((tm, D_MODEL), lambda i: (i, 0)),
        compiler_params=_params(),
        name="mixout_odd",
    )(h, o, w_out, post_w.reshape(1, D_MODEL))


def _qkv_kernel(x_ref, nw_ref, w_ref, cos_ref, sin_ref, qn_ref, kn_ref, q_out, k_out, v_out, xn_ref):
    xn_ref[...] = _rms(x_ref[...], nw_ref[...]).astype(BF16)
    cos, sin = cos_ref[...], sin_ref[...]

    def rope(y, yp, n_ref, scale):
        rs = lax.rsqrt(jnp.mean(y * y, axis=-1, keepdims=True) + NORM_EPS) * scale
        return ((y * n_ref[0:1]) * cos + (yp * n_ref[1:2]) * sin) * rs

    nq = C_HEADS * C_HD
    nk = C_KV_HEADS * C_HD
    per = 512 // C_HD
    for c in range(nq // 512):
        y = jnp.dot(xn_ref[...], w_ref[:, c * 512:(c + 1) * 512], preferred_element_type=F32)
        yp = jnp.dot(xn_ref[...], w_ref[:, nq + 2 * nk + c * 512:nq + 2 * nk + (c + 1) * 512],
                     preferred_element_type=F32)
        for hd in range(per):
            hs = slice((c * per + hd) * C_HD, (c * per + hd + 1) * C_HD)
            ys = slice(hd * C_HD, (hd + 1) * C_HD)
            q_out[:, hs] = rope(y[:, ys], yp[:, ys], qn_ref, C_HD ** -0.5 * LOG2E).astype(BF16)
    y = jnp.dot(xn_ref[...], w_ref[:, nq:nq + 2 * nk], preferred_element_type=F32)
    yp = jnp.dot(xn_ref[...], w_ref[:, 2 * nq + 2 * nk:], preferred_element_type=F32)
    for hd in range(C_KV_HEADS):
        hs = slice(hd * C_HD, (hd + 1) * C_HD)
        k_out[:, hs] = rope(y[:, hs], yp[:, hs], kn_ref, 1.0).astype(BF16)
        v_out[:, 2 * hd * C_HD:(2 * hd + 1) * C_HD] = y[:, nk + hd * C_HD:nk + (hd + 1) * C_HD].astype(BF16)
        v_out[:, (2 * hd + 1) * C_HD:(2 * hd + 2) * C_HD] = jnp.ones((x_ref.shape[0], C_HD), BF16)


def _qkv_proj(h, norm_w, w_qkv, cos_t, sin_t, q_norm, k_norm, geom):
    tm, rows = geom.tm, geom.rows
    nq, nk = C_HEADS * C_HD, C_KV_HEADS * C_HD
    lane = np.arange(C_HD)
    partner = np.where(lane % (C_HD // 2) < C_HD // 4, lane + C_HD // 4, lane - C_HD // 4)
    cols = np.concatenate([hd * C_HD + partner for hd in range(C_HEADS + C_KV_HEADS)])
    w_bf16 = jnp.concatenate([w_qkv, w_qkv[:, cols]], axis=1).astype(BF16)
    q_norm = jnp.stack([q_norm, q_norm[partner]])
    k_norm = jnp.stack([k_norm, k_norm[partner]])
    return pl.pallas_call(
        _qkv_kernel,
        out_shape=[jax.ShapeDtypeStruct((rows, nq), BF16), jax.ShapeDtypeStruct((rows, nk), BF16),
                   jax.ShapeDtypeStruct((rows, 2 * nk), BF16)],
        grid=(rows // tm,),
        in_specs=[pl.BlockSpec((tm, D_MODEL), lambda i: (i, 0)), _const_spec((1, D_MODEL)),
                  _const_spec((D_MODEL, 2 * nq + 3 * nk)),
                  pl.BlockSpec((tm, C_HD), lambda i: (i, 0)), pl.BlockSpec((tm, C_HD), lambda i: (i, 0)),
                  _const_spec((2, C_HD)), _const_spec((2, C_HD))],
        out_specs=[pl.BlockSpec((tm, nq), lambda i: (i, 0)), pl.BlockSpec((tm, nk), lambda i: (i, 0)),
                   pl.BlockSpec((tm, 2 * nk), lambda i: (i, 0))],
        scratch_shapes=[pltpu.VMEM((tm, D_MODEL), BF16)],
        compiler_params=_params(),
        name="qkv_proj",
    )(h, norm_w.reshape(1, D_MODEL), w_bf16, cos_t, sin_t, q_norm, k_norm)


def _rope_tables(geom):
    nf = C_HD // 4
    inv = ROPE_THETA ** (-jnp.arange(nf, dtype=F32) / nf)
    cos_parts, sin_parts = [], []
    for g in geom.groups:
        t = jnp.arange(g.tp, dtype=jnp.int32) - (g.fv + N_META)
        is_meta = t < 0
        row = jnp.where(is_meta, -1, t // GRID_W).astype(F32)
        col = jnp.where(is_meta, t + N_META, t % GRID_W).astype(F32)
        ang_r, ang_c = row[:, None] * inv, col[:, None] * inv
        cos = jnp.concatenate([jnp.cos(ang_r)] * 2 + [jnp.cos(ang_c)] * 2, axis=1)
        sin = jnp.concatenate([-jnp.sin(ang_r), jnp.sin(ang_r), -jnp.sin(ang_c), jnp.sin(ang_c)], axis=1)
        cos_parts.append(jnp.tile(cos, (g.nb, 1)))
        sin_parts.append(jnp.tile(sin, (g.nb, 1)))
    return jnp.concatenate(cos_parts, axis=0), jnp.concatenate(sin_parts, axis=0)


def _attn_kernel(*refs, nsub, bounds, fv):
    q_ref, k_ref, v_ref, _, o_ref, s_ref, rm_ref, m_ref, acc_ref = refs
    tq = SEQ_BLK
    nc = len(bounds)
    assert nc % 2 == 0

    def q_rows(qi):
        return pl.ds(pl.multiple_of(qi * tq, tq), tq)

    def produce(qi, c, slot):
        start, size = bounds[c]
        q = jnp.concatenate([q_ref[q_rows(qi), g * C_HD:(g + 1) * C_HD] for g in range(C_GROUP)], axis=0)
        s = lax.dot_general(q, k_ref[start:start + size, :], (((1,), (1,)), ((), ())),
                            preferred_element_type=F32)
        if start < fv:
            key = start + lax.broadcasted_iota(jnp.int32, (1, size), 1)
            s = jnp.where(key >= fv, s, -1e30)
        s_ref[slot, :, :size] = s
        rm_ref[slot] = jnp.max(s, axis=-1, keepdims=True)

    def consume(c, slot):
        start, size = bounds[c]
        m_old = m_ref[...]
        m_new = jnp.maximum(m_old, rm_ref[slot])
        p = jnp.exp2(s_ref[slot, :, :size] - m_new).astype(BF16)
        acc_ref[...] = jnp.exp2(m_old - m_new) * acc_ref[...] + jnp.dot(p, v_ref[start:start + size, :],
                                                                        preferred_element_type=F32)
        m_ref[...] = m_new

    produce(0, 0, 0)

    def body(qi, carry):
        m_ref[...] = jnp.full_like(m_ref, -1e30)
        acc_ref[...] = jnp.zeros_like(acc_ref)
        for c in range(nc):
            if c + 1 < nc:
                produce(qi, c + 1, (c + 1) % 2)
            else:
                produce(jnp.minimum(qi + 1, nsub - 1), 0, 0)
            consume(c, c % 2)
        acc = acc_ref[...]
        out = acc[:, :C_HD] / acc[:, C_HD:]
        for g in range(C_GROUP):
            o_ref[q_rows(qi), g * C_HD:(g + 1) * C_HD] = out[g * tq:(g + 1) * tq].astype(o_ref.dtype)
        return carry

    lax.fori_loop(0, nsub, body, 0)


def _key_chunks(tp, fv):
    lo = fv // LANE
    blocks = tp // LANE - lo
    per = ATTN_MAX_TK // LANE
    nc = -(-blocks // per)
    nc += nc % 2
    pairs, single = divmod(blocks, 2)
    sizes = [2 * (pairs // nc + (1 if i < pairs % nc else 0)) for i in range(nc)]
    sizes[-1] += single
    if min(sizes) == 0:
        sizes = [blocks // nc + (1 if i < blocks % nc else 0) for i in range(nc)]
    assert min(sizes) > 0 and max(sizes) <= per
    starts = np.cumsum([lo] + sizes[:-1])
    return tuple((int(s) * LANE, int(z) * LANE) for s, z in zip(starts, sizes))


def _attention(qn, kn, vn, geom):
    out = jnp.zeros((geom.rows, C_HEADS * C_HD), BF16)
    for g in geom.groups:
        bounds = _key_chunks(g.tp, g.fv)
        nblk = g.tp // SEQ_BLK
        nsub = max(d for d in range(1, ATTN_MAX_SUB + 1) if nblk % d == 0)
        tq, gw, rows = nsub * SEQ_BLK, C_GROUP * C_HD, C_GROUP * SEQ_BLK
        al = lambda x: pl.multiple_of(x, LANE)
        q_spec = pl.BlockSpec((pl.Element(tq), pl.Element(gw)),
                              lambda b, h, i, g=g, tq=tq: (al(g.start + b * g.tp + i * tq), al(h * gw)))
        in_specs = [q_spec,
                    pl.BlockSpec((pl.Element(g.tp), pl.Element(C_HD)),
                                 lambda b, h, i, g=g: (al(g.start + b * g.tp), al(h * C_HD))),
                    pl.BlockSpec((pl.Element(g.tp), pl.Element(2 * C_HD)),
                                 lambda b, h, i, g=g: (al(g.start + b * g.tp), al(h * 2 * C_HD))),
                    pl.BlockSpec(memory_space=pl.ANY)]
        out = pl.pallas_call(
            functools.partial(_attn_kernel, nsub=nsub, bounds=bounds, fv=g.fv),
            out_shape=jax.ShapeDtypeStruct(out.shape, out.dtype),
            grid=(g.nb, C_KV_HEADS, g.tp // tq),
            in_specs=in_specs,
            out_specs=q_spec,
            scratch_shapes=[pltpu.VMEM((2, rows, max(z for _, z in bounds)), F32), pltpu.VMEM((2, rows, 1), F32),
                            pltpu.VMEM((rows, 1), F32), pltpu.VMEM((rows, 2 * C_HD), F32)],
            input_output_aliases={3: 0},
            compiler_params=_params(("parallel", "parallel", "parallel")),
            name="attention",
        )(qn, kn, vn, out)
    return out


def _ffn_kernel(x_ref, xp_ref, xn_ref, pre_ref, wu_ref, wg_ref, cw_ref, cb_ref, wo_ref, post_ref,
                o_ref, xs_ref, act_ref, acc_ref, *, tm):
    i = pl.program_id(0)
    xs_ref[0:HALO] = jnp.where(i > 0, _rms(xp_ref[...], pre_ref[...]), 0.0).astype(BF16)
    xs_ref[HALO:HALO + tm] = _rms(x_ref[...], pre_ref[...]).astype(BF16)
    xs_ref[HALO + tm:2 * HALO + tm] = jnp.where(i < pl.num_programs(0) - 1,
                                                _rms(xn_ref[...], pre_ref[...]), 0.0).astype(BF16)
    half = FFN_CONV // 2
    n_chunks = D_FF // FF_CHUNK
    for c in range(n_chunks):
        cs = slice(c * FF_CHUNK, (c + 1) * FF_CHUNK)
        u = jnp.dot(xs_ref[HALO:HALO + tm], wu_ref[:, cs], preferred_element_type=F32)
        g_win = jnp.dot(xs_ref[...], wg_ref[:, cs], preferred_element_type=F32)
        gate = cb_ref[0:1, cs]
        for j in range(FFN_CONV):
            gate = gate + _shifted_rows(g_win, j - half, tm) * cw_ref[j:j + 1, cs]
        th = jnp.tanh(gate * (GELU_C0 + GELU_C0 * 0.044715 * (gate * gate)))
        hu = 0.5 * gate * u
        act_ref[:, cs] = (hu + hu * th).astype(BF16)
        if c % FF_GROUP == FF_GROUP - 1 or c == n_chunks - 1:
            ks = slice((c // FF_GROUP) * FF_GROUP * FF_CHUNK, (c + 1) * FF_CHUNK)
            part = jnp.dot(act_ref[:, ks], wo_ref[ks, :], preferred_element_type=F32)
            if c < FF_GROUP:
                acc_ref[...] = part
            else:
                acc_ref[...] += part
    o_ref[...] = x_ref[...] + _rms(acc_ref[...], post_ref[...])


def _conv_ffn(h, pre_w, w_in, conv_w, conv_b, w_out, post_w, geom):
    tm, rows = geom.tm, geom.rows
    assert D_FF % FF_CHUNK == 0
    wu = w_in[:, :D_FF].astype(BF16)
    wg = w_in[:, D_FF:].astype(BF16)
    return pl.pallas_call(
        functools.partial(_ffn_kernel, tm=tm),
        out_shape=jax.ShapeDtypeStruct((rows, D_MODEL), F32),
        grid=(rows // tm,),
        in_specs=_halo_specs(tm, D_MODEL, rows) + [
            _const_spec((1, D_MODEL)), _const_spec((D_MODEL, D_FF)), _const_spec((D_MODEL, D_FF)),
            _const_spec((FFN_CONV, D_FF)), _const_spec((1, D_FF)), _const_spec((D_FF, D_MODEL)),
            _const_spec((1, D_MODEL))],
        out_specs=pl.BlockSpec((tm, D_MODEL), lambda i: (i, 0)),
        scratch_shapes=[pltpu.VMEM((tm + 2 * HALO, D_MODEL), BF16), pltpu.VMEM((tm, D_FF), BF16),
                        pltpu.VMEM((tm, D_MODEL), F32)],
        compiler_params=_params(),
        name="conv_ffn",
    )(h, h, h, pre_w.reshape(1, D_MODEL), wu, wg, conv_w, conv_b.reshape(1, D_FF),
      w_out.astype(BF16), post_w.reshape(1, D_MODEL))


def _meta_kernel(meta_ref, _, o_ref):
    o_ref[...] = meta_ref[...].astype(o_ref.dtype)


def _rows_in_kernel(x_ref, _, o_ref):
    o_ref[...] = x_ref[0]


def _rows_out_kernel(h_ref, o_ref):
    o_ref[0] = h_ref[...]


def _copy_block(n):
    return max(b for b in (2048, 1024, 512, 256, 128) if n % b == 0)


def _pack_rows(xs, meta, geom):
    al = lambda v: pl.multiple_of(v, LANE)
    shape = jax.ShapeDtypeStruct((geom.rows, D_MODEL), xs[0].dtype)
    h = jnp.zeros(shape.shape, shape.dtype)
    for x, g in zip(xs, geom.groups):
        pad = g.tp - g.n
        h = pl.pallas_call(
            _meta_kernel, out_shape=shape, grid=(g.nb,),
            in_specs=[_const_spec((N_META, D_MODEL)), pl.BlockSpec(memory_space=pl.ANY)],
            out_specs=pl.BlockSpec((pl.Element(N_META), pl.Element(D_MODEL)),
                                   lambda b, g=g: (pl.multiple_of(g.start + b * g.tp + g.fv, N_META), 0)),
            input_output_aliases={1: 0},
            compiler_params=_params(), name="pack_meta",
        )(meta, h)
        blk = _copy_block(g.n)
        h = pl.pallas_call(
            _rows_in_kernel, out_shape=shape, grid=(g.nb, g.n // blk),
            in_specs=[pl.BlockSpec((1, blk, D_MODEL), lambda b, j: (b, j, 0)), pl.BlockSpec(memory_space=pl.ANY)],
            out_specs=pl.BlockSpec((pl.Element(blk), pl.Element(D_MODEL)),
                                   lambda b, j, g=g, pad=pad, blk=blk: (al(g.start + b * g.tp + pad + j * blk), 0)),
            input_output_aliases={1: 0},
            compiler_params=_params(("parallel", "parallel")), name="pack_rows",
        )(x, h)
    return h


def _unpack_rows(h, geom):
    al = lambda v: pl.multiple_of(v, LANE)
    outs = []
    for g in geom.groups:
        pad = g.tp - g.n
        blk = _copy_block(g.n)
        outs.append(pl.pallas_call(
            _rows_out_kernel, out_shape=jax.ShapeDtypeStruct((g.nb, g.n, D_MODEL), h.dtype),
            grid=(g.nb, g.n // blk),
            in_specs=[pl.BlockSpec((pl.Element(blk), pl.Element(D_MODEL)),
                                   lambda b, j, g=g, pad=pad, blk=blk: (al(g.start + b * g.tp + pad + j * blk), 0))],
            out_specs=pl.BlockSpec((1, blk, D_MODEL), lambda b, j: (b, j, 0)),
            compiler_params=_params(("parallel", "parallel")), name="unpack_rows",
        )(h))
    return tuple(outs)


def _even_layer(h, j, geom, tables, ones, p):
    aw, bw = A_WIDTH, B_WIDTH
    a_cols = 4 * aw + 4 * A_HEADS
    w_in = p["e_w_in"][j]
    lora = B_W_LORA + B_A_LORA + B_G_LORA
    zeros = lambda n: jnp.zeros((D_MODEL, n), F32)
    w_cat = jnp.concatenate([
        w_in[:, :3 * aw],
        w_in[:, a_cols:a_cols + 3 * bw + lora], zeros(B_LORA_PAD - lora),
        w_in[:, 4 * aw:a_cols], zeros(LANE - 4 * A_HEADS),
        w_in[:, 3 * aw:4 * aw]], axis=1).astype(BF16)
    mu = jnp.pad(p["b_shift"][j], ((0, 0), (0, B_LORA_PAD - lora)))
    lora_rows = lambda m, off: jnp.pad(m, ((off, B_LORA_PAD - off - m.shape[0]), (0, 0)))
    w2 = lora_rows(jnp.concatenate([p["b_w2"][j, 0], p["b_w2"][j, 1]], axis=1), 0)
    a2 = lora_rows(p["b_a2"][j], B_W_LORA)
    g2 = lora_rows(p["b_g2"][j], B_W_LORA + B_A_LORA)
    qkv, gts, z, rkv, cf, cb, bonus, gate = _even_prep(
        h, p["mix_pre_norm"][2 * j], w_cat, p["a_conv_w"][j], p["a_log"][j], p["a_dt_bias"][j],
        mu, p["b_w0"][j].reshape(1, 2 * bw), w2, p["b_a0"][j], a2, g2,
        p["b_k_k"][j], p["b_k_a"][j], p["b_r_k"][j], ones, geom)
    oaf, oab = _a_scan(qkv, gts, geom, tables)
    ybf, ybb = _b_scan(rkv, cf, cb, geom, tables)

    return _mixout_even(h, oaf, oab, z, ybf, ybb, bonus, gate, p["e_w_out"][j].astype(BF16),
                        p["a_out_norm"][j], p["b_ln_w"][j], p["b_ln_b"][j], p["mix_post_norm"][2 * j], ones, geom)


def _odd_layer(h, j, geom, rope, p):
    qn, kn, vn = _qkv_proj(h, p["mix_pre_norm"][2 * j + 1], p["o_w_qkv"][j].astype(BF16), rope[0], rope[1],
                           p["o_q_norm"][j], p["o_k_norm"][j], geom)
    o = _attention(qn, kn, vn, geom)
    return _mixout_odd(h, o, p["o_w_out"][j].astype(BF16), p["mix_post_norm"][2 * j + 1], geom)


def kernel(x_prompt, x_sample, meta, mix_pre_norm, mix_post_norm, ffn_pre_norm, ffn_post_norm, e_w_in, a_conv_w, a_log, a_dt_bias, a_out_norm, b_shift, b_w0, b_w2, b_a0, b_a2, b_g2, b_k_k, b_k_a, b_r_k, b_ln_w, b_ln_b, e_w_out, o_w_qkv, o_q_norm, o_k_norm, o_w_out, f_w_in, f_conv_w, f_conv_b, f_w_out):
    p = dict(mix_pre_norm=mix_pre_norm, mix_post_norm=mix_post_norm, e_w_in=e_w_in, a_conv_w=a_conv_w,
             a_log=a_log, a_dt_bias=a_dt_bias, a_out_norm=a_out_norm, b_shift=b_shift, b_w0=b_w0, b_w2=b_w2,
             b_a0=b_a0, b_a2=b_a2, b_g2=b_g2, b_k_k=b_k_k, b_k_a=b_k_a, b_r_k=b_r_k, b_ln_w=b_ln_w,
             b_ln_b=b_ln_b, e_w_out=e_w_out, o_w_qkv=o_w_qkv, o_q_norm=o_q_norm, o_k_norm=o_k_norm,
             o_w_out=o_w_out)
    xs = (x_prompt, x_sample)
    geom = _choose_geom([(x.shape[0], x.shape[1]) for x in xs])
    tables = _scan_tables(geom)
    rope = _rope_tables(geom)
    seg = np.arange(SEG_TILE) // B_HD
    ones = jnp.asarray(seg[:, None] == seg[None, :], BF16)

    h = _pack_rows(xs, meta, geom)
    for i in range(DEPTH):
        h = _even_layer(h, i // 2, geom, tables, ones, p) if i % 2 == 0 else _odd_layer(h, i // 2, geom, rope, p)
        h = _conv_ffn(h, ffn_pre_norm[i], f_w_in[i], f_conv_w[i], f_conv_b[i], f_w_out[i], ffn_post_norm[i], geom)
    return _unpack_rows(h, geom)
```

```python
import functools
import itertools
from typing import NamedTuple

import numpy as np
import jax
import jax.numpy as jnp
from jax import lax
from jax.experimental import pallas as pl
from jax.experimental.pallas import tpu as pltpu

F32 = jnp.float32
BF16 = jnp.bfloat16

D_MODEL = 1024
DEPTH = 2
N_META = 16
GRID_W = 64
NORM_EPS = 1e-6

A_HEADS = 4
A_DK = 128
A_DV = 128
A_CONV = 5
A_WIDTH = A_HEADS * A_DV

B_HEADS = 8
B_HD = 64
B_WIDTH = B_HEADS * B_HD
B_W_LORA = 32
B_A_LORA = 32
B_G_LORA = 96
B_GN_EPS = 64e-5
B_LORA_PAD = 256

C_HEADS = 8
C_KV_HEADS = 2
C_GROUP = C_HEADS // C_KV_HEADS
C_HD = 128
ROPE_THETA = 10000.0

D_FF = 2816
FFN_CONV = 3

LANE = 128
CHUNK = 64
SEQ_BLK = 128
HALO = 16
FF_CHUNK = 256
FF_GROUP = 4
ATTN_MAX_TK = 2304
ATTN_MAX_SUB = 17
SEG_TILE = 256
LOG2E = 1.4426950408889634
GELU_C0 = 0.7978845608028654
VMEM_LIMIT = 56 * 1024 * 1024


class _Group(NamedTuple):
    start: int
    nb: int
    n: int
    tp: int
    fv: int


class _Geom(NamedTuple):
    groups: tuple
    rows: int
    tm: int


def _choose_geom(shapes):
    tm = 512 if min(n for _, n in shapes) >= 512 else 128
    best = None
    for pads in itertools.product((128, 256, 384, 512), repeat=len(shapes)):
        start, ok, groups = 0, True, []
        for (nb, n), p in zip(shapes, pads):
            ok = ok and start % tm == 0
            groups.append(_Group(start, nb, n, n + p, p - N_META))
            start += nb * (n + p)
        if ok and start % tm == 0 and (best is None or start < best.rows):
            best = _Geom(tuple(groups), start, tm)
    assert best is not None
    return best


def _row_valid(geom, base, tm):
    start = jnp.int32(geom.groups[0].start)
    tp = jnp.int32(geom.groups[0].tp)
    fv = jnp.int32(geom.groups[0].fv)
    for g in geom.groups[1:]:
        assert tm <= g.tp
        inside = base >= g.start
        start = jnp.where(inside, g.start, start)
        tp = jnp.where(inside, g.tp, tp)
        fv = jnp.where(inside, g.fv, fv)
    pos = lax.rem(base - start, tp) + lax.broadcasted_iota(jnp.int32, (tm, 1), 0)
    pos = jnp.where(pos >= tp, pos - tp, pos)
    return pos >= fv


def _rms(x, w, eps=NORM_EPS):
    return x * lax.rsqrt(jnp.mean(x * x, axis=-1, keepdims=True) + eps) * w


def _sigmoid(x):
    return 1.0 / (1.0 + jnp.exp(-x))


def _softplus(x):
    return jnp.maximum(x, 0.0) + jnp.log1p(jnp.exp(-jnp.abs(x)))


def _dot(a, b):
    return jnp.dot(a.astype(BF16), b.astype(BF16), preferred_element_type=F32)


def _dot_nt(a, b):
    return lax.dot_general(a.astype(BF16), b.astype(BF16), (((1,), (1,)), ((), ())),
                           preferred_element_type=F32)


def _seg_sum(x, ones_ref, terms):
    acc = None
    rem = x
    for t in range(terms):
        piece = rem.astype(BF16)
        part = jnp.concatenate([jnp.dot(piece[:, s:s + SEG_TILE], ones_ref[...], preferred_element_type=F32)
                                for s in range(0, x.shape[1], SEG_TILE)], axis=1)
        acc = part if acc is None else acc + part
        if t + 1 < terms:
            rem = rem - piece.astype(F32)
    return acc


def _chunk_cumsum(x, reverse):
    tm = x.shape[0]
    pos = lax.broadcasted_iota(jnp.int32, (tm, 1), 0) & (CHUNK - 1)
    s = 1
    while s < CHUNK:
        if reverse:
            x = x + jnp.where(pos < CHUNK - s, pltpu.roll(x, tm - s, axis=0), 0.0)
        else:
            x = x + jnp.where(pos >= s, pltpu.roll(x, s, axis=0), 0.0)
        s *= 2
    return x


def _chunk_cumsum_mxu(x, tri):
    tile = tri.shape[0]
    acc = None
    rem = x
    for t in range(3):
        piece = rem.astype(BF16)
        part = jnp.concatenate([jnp.dot(tri, piece[s:s + tile], preferred_element_type=F32)
                                for s in range(0, x.shape[0], tile)], axis=0)
        acc = part if acc is None else acc + part
        if t < 2:
            rem = rem - piece.astype(F32)
    return acc


def _neumann_inverses(n_mats, eye):
    n = eye.shape[0]
    prods = [eye + m for m in n_mats]
    powers = [_dot(m, m) for m in n_mats]
    s = 4
    while s < CHUNK:
        both = [_dot(jnp.concatenate([pr, pw], axis=0), pw) for pr, pw in zip(prods, powers)]
        prods = [pr + b[:n] for pr, b in zip(prods, both)]
        powers = [b[n:] for b in both]
        s *= 2
    return [pr + _dot(pr, pw) for pr, pw in zip(prods, powers)]


def _shifted_rows(win, d, tm):
    if d == 0:
        return win[HALO:HALO + tm]
    return pltpu.roll(win, (-d) % win.shape[0], axis=0)[HALO:HALO + tm]


def _halo_specs(tm, width, rows):
    per = tm // HALO
    last = rows // HALO - 1
    return [
        pl.BlockSpec((tm, width), lambda i: (i, 0)),
        pl.BlockSpec((HALO, width), lambda i: (jnp.maximum(i * per - 1, 0), 0)),
        pl.BlockSpec((HALO, width), lambda i: (jnp.minimum((i + 1) * per, last), 0)),
    ]


def _const_spec(shape):
    return pl.BlockSpec(shape, lambda *_: (0,) * len(shape))


def _params(sem=("parallel",)):
    return pltpu.CompilerParams(dimension_semantics=sem, vmem_limit_bytes=VMEM_LIMIT)


def _a_prep_body(xw_ref, al, cw_ref, alog_ref, dtb_ref, q_ref, k_ref, v_ref, go_ref, valid, tm):
    half = A_CONV // 2
    for c, o_ref in enumerate((q_ref, k_ref, v_ref)):
        cs = slice(c * A_WIDTH, (c + 1) * A_WIDTH)
        win = xw_ref[:, cs]
        acc = _shifted_rows(win, -half, tm) * cw_ref[0:1, cs]
        for j in range(1, A_CONV):
            acc = acc + _shifted_rows(win, j - half, tm) * cw_ref[j:j + 1, cs]
        y = jnp.where(valid, acc * _sigmoid(acc), 0.0)
        for hd in range(A_HEADS):
            hs = slice(hd * A_DK, (hd + 1) * A_DK)
            yh = y[:, hs]
            if c < 2:
                yh = yh * lax.rsqrt(jnp.sum(yh * yh, axis=-1, keepdims=True) + 1e-6)
                if c == 0:
                    yh = yh * (A_DK ** -0.5)
            o_ref[:, hs] = yh.astype(o_ref.dtype)
    lane = lax.broadcasted_iota(jnp.int32, (1, LANE), 1)
    gval = jnp.where(valid, -jnp.exp(alog_ref[...]) * _softplus(al + dtb_ref[...]), 0.0)
    beta = jnp.where(valid, _sigmoid(al), 0.0)
    cum_f = _chunk_cumsum(gval, False)
    cum_b = _chunk_cumsum(gval, True)
    go_ref[...] = jnp.where(lane < A_HEADS, cum_f,
                            jnp.where(lane < 2 * A_HEADS, cum_b,
                                      jnp.where(lane < 4 * A_HEADS, beta, 0.0)))


def _a_scan_kernel(fi_ref, bi_ref, fr_ref,
                   qkvf, gf, gtf, qkvb, gb, gtb, of, ob, sf, sb):
    del fi_ref, bi_ref
    qf, kf, vf = (qkvf.at[:, c * A_WIDTH:(c + 1) * A_WIDTH] for c in range(3))
    qb, kb, vb = (qkvb.at[:, c * A_WIDTH:(c + 1) * A_WIDTH] for c in range(3))
    @pl.when(fr_ref[pl.program_id(0)] == 1)
    def _():
        sf[...] = jnp.zeros_like(sf)
        sb[...] = jnp.zeros_like(sb)
    r = lax.broadcasted_iota(jnp.int32, (CHUNK, CHUNK), 0)
    c = lax.broadcasted_iota(jnp.int32, (CHUNK, CHUNK), 1)
    eye = jnp.where(r == c, 1.0, 0.0)
    nc = SEQ_BLK // CHUNK
    dirs = ((qf, kf, vf, gf, gtf, of, sf, False, r >= c, r > c),
            (qb, kb, vb, gb, gtb, ob, sb, True, r <= c, r < c))

    probs = []
    for d, (q_ref, k_ref, v_ref, g_ref, gt_ref, _, _, reverse, tri, strict) in enumerate(dirs):
        edge = 0 if reverse else CHUNK - 1
        for step in range(nc):
            jj = nc - 1 - step if reverse else step
            rows = slice(jj * CHUNK, (jj + 1) * CHUNK)
            for hd in range(A_HEADS):
                hs = slice(hd * A_DK, (hd + 1) * A_DK)
                col = hd + (A_HEADS if reverse else 0)
                q, k, v = (ref[rows, hs].astype(F32) for ref in (q_ref, k_ref, v_ref))
                g_col = g_ref[rows, col:col + 1]
                b_col = g_ref[rows, 2 * A_HEADS + col:2 * A_HEADS + col + 1]
                g_row = gt_ref[jj, col:col + 1, :]
                decay = jnp.where(tri, jnp.exp(jnp.where(tri, g_col - g_row, 0.0)), 0.0)
                kb_ = k * b_col
                e_g = jnp.exp(g_col)
                g_last = g_col[edge:edge + 1]
                probs.append(dict(d=d, step=step, hd=hd, rows=rows, hs=hs, tri=tri, strict=strict, decay=decay,
                                  k=k, kbq=jnp.concatenate([kb_, q], axis=0),
                                  rhs=jnp.concatenate([v * b_col, kb_ * e_g], axis=1),
                                  qg=q * e_g, kg_t=(k * jnp.exp(g_last - g_col)).T, g_end=jnp.exp(g_last)))
    for p in probs:
        kq = _dot_nt(p["kbq"], p["k"])
        p["l"] = jnp.where(p["strict"], kq[:CHUNK] * p["decay"], 0.0)
        p["attn"] = jnp.where(p["tri"], kq[CHUNK:] * p["decay"], 0.0)
    t_invs = _neumann_inverses([-p["l"] for p in probs], eye)
    for p, t_inv in zip(probs, t_invs):
        p["uw"] = _dot(t_inv, p["rhs"])

    states = {(d, hd): dirs[d][6][hd] for d in range(2) for hd in range(A_HEADS)}
    for step in range(nc):
        cur = [p for p in probs if p["step"] == step]
        v_news = [p["uw"][:, :A_DV] - _dot(p["uw"][:, A_DV:], states[p["d"], p["hd"]]) for p in cur]
        for p, v_new in zip(cur, v_news):
            state = states[p["d"], p["hd"]]
            dirs[p["d"]][5][p["rows"], p["hs"]] = _dot(jnp.concatenate([p["qg"], p["attn"]], axis=1),
                                                      jnp.concatenate([state, v_new], axis=0))
        for p, v_new in zip(cur, v_news):
            key = (p["d"], p["hd"])
            states[key] = states[key] * p["g_end"] + _dot(p["kg_t"], v_new)
    for (d, hd), state in states.items():
        dirs[d][6][hd] = state


def _scan_tables(geom):
    fi, bi, fr = [], [], []
    for g in geom.groups:
        nblk = g.tp // SEQ_BLK
        for b in range(g.nb):
            base = (g.start + b * g.tp) // SEQ_BLK
            for i in range(nblk):
                fi.append(base + i)
                bi.append(base + nblk - 1 - i)
                fr.append(1 if i == 0 else 0)
    mk = lambda v: jnp.asarray(np.asarray(v, np.int32))
    return mk(fi), mk(bi), mk(fr)


def _a_scan(qkv, gts, geom, tables):
    rows = geom.rows
    nc = SEQ_BLK // CHUNK
    gts_t = gts[:, :4 * A_HEADS].reshape(rows // CHUNK, CHUNK, 4 * A_HEADS).transpose(0, 2, 1)
    wide_f = pl.BlockSpec((SEQ_BLK, A_WIDTH), lambda s, fi, bi, fr: (fi[s], 0))
    wide_b = pl.BlockSpec((SEQ_BLK, A_WIDTH), lambda s, fi, bi, fr: (bi[s], 0))
    qkv_f = pl.BlockSpec((SEQ_BLK, 3 * A_WIDTH), lambda s, fi, bi, fr: (fi[s], 0))
    qkv_b = pl.BlockSpec((SEQ_BLK, 3 * A_WIDTH), lambda s, fi, bi, fr: (bi[s], 0))
    gate_f = pl.BlockSpec((SEQ_BLK, LANE), lambda s, fi, bi, fr: (fi[s], 0))
    gate_b = pl.BlockSpec((SEQ_BLK, LANE), lambda s, fi, bi, fr: (bi[s], 0))
    gt_f = pl.BlockSpec((nc, 4 * A_HEADS, CHUNK), lambda s, fi, bi, fr: (fi[s], 0, 0))
    gt_b = pl.BlockSpec((nc, 4 * A_HEADS, CHUNK), lambda s, fi, bi, fr: (bi[s], 0, 0))
    out = jax.ShapeDtypeStruct((rows, A_WIDTH), F32)
    return pl.pallas_call(
        _a_scan_kernel,
        out_shape=[out, out],
        grid_spec=pltpu.PrefetchScalarGridSpec(
            num_scalar_prefetch=3,
            grid=(tables[0].shape[0],),
            in_specs=[qkv_f, gate_f, gt_f, qkv_b, gate_b, gt_b],
            out_specs=[wide_f, wide_b],
            scratch_shapes=[pltpu.VMEM((A_HEADS, A_DK, A_DV), F32)] * 2),
        compiler_params=_params(("arbitrary",)),
        name="a_scan",
    )(*tables, qkv, gts, gts_t, qkv, gts, gts_t)


def _b_prep_body(xw_ref, lw_ref, mu_ref, w0_ref, w2_ref, a0_ref, a2_ref, g2_ref, kk_ref, ka_ref, rk_ref, ones_ref,
                 tri_ref, r_out, k_out, v_out, kkn_out, eta_out, cf_out, cb_out, bonus_out, gate_out, tm):
    def shifted(win_ref, cs, mu_off):
        win = win_ref[:, cs]
        cur, prev, nxt = _shifted_rows(win, 0, tm), _shifted_rows(win, -1, tm), _shifted_rows(win, 1, tm)
        ms = slice(mu_off + cs.start, mu_off + cs.stop)
        return cur + mu_ref[0:1, ms] * (prev - cur) + mu_ref[1:2, ms] * (nxt - cur)

    w = B_WIDTH
    r = shifted(xw_ref, slice(0, w), 0)
    k = shifted(xw_ref, slice(w, 2 * w), 0)
    v = shifted(xw_ref, slice(2 * w, 3 * w), 0)
    lo = shifted(lw_ref, slice(0, B_LORA_PAD), 3 * w)
    wl = _dot(jnp.tanh(lo), w2_ref[...]) + w0_ref[...]
    log_decay = -float(np.exp(-0.5)) * _sigmoid(wl)
    tile = min(SEG_TILE, tm)
    cf_out[...] = _chunk_cumsum_mxu(log_decay[:, :w], tri_ref[0, :tile, :tile])
    cb_out[...] = _chunk_cumsum_mxu(log_decay[:, w:], tri_ref[1, :tile, :tile])
    eta = _sigmoid(a0_ref[...] + _dot(lo, a2_ref[...]))
    gate_out[...] = _dot(_sigmoid(lo), g2_ref[...])
    kx = k * kk_ref[...]
    kkn_out[...] = (kx * lax.rsqrt(_seg_sum(kx * kx, ones_ref, 2) + 1e-6)).astype(kkn_out.dtype)
    k = k * (1.0 + (eta - 1.0) * ka_ref[...])
    bonus_out[...] = _seg_sum(r * k * rk_ref[...], ones_ref, 2) * v
    r_out[...] = r.astype(r_out.dtype)
    k_out[...] = k.astype(k_out.dtype)
    v_out[...] = v.astype(v_out.dtype)
    eta_out[...] = eta.astype(eta_out.dtype)


def _even_prep_kernel(h_ref, hp_ref, hn_ref, nw_ref, w_ref, cw_ref, alog_ref, dtb_ref,
                      mu_ref, w0_ref, w2_ref, a0_ref, a2_ref, g2_ref, kk_ref, ka_ref, rk_ref, ones_ref, tri_ref,
                      qkv_out, go_out, z_out, rkv_out, cf_out, cb_out, bonus_out, gate_out,
                      xn_ref, aw_ref, bw_ref, lw_ref, *, geom):
    q_out, ka_out, va_out = (qkv_out.at[:, c * A_WIDTH:(c + 1) * A_WIDTH] for c in range(3))
    r_out, kb_out, vb_out, kkn_out, eta_out = (rkv_out.at[:, c * B_WIDTH:(c + 1) * B_WIDTH] for c in range(5))
    tm = geom.tm
    i = pl.program_id(0)
    valid = _row_valid(geom, i * tm, tm)
    xn_ref[0:HALO] = jnp.where(i > 0, _rms(hp_ref[...], nw_ref[...]), 0.0).astype(BF16)
    xn_ref[HALO:HALO + tm] = _rms(h_ref[...], nw_ref[...]).astype(BF16)
    xn_ref[HALO + tm:2 * HALO + tm] = jnp.where(i < pl.num_programs(0) - 1,
                                                _rms(hn_ref[...], nw_ref[...]), 0.0).astype(BF16)
    aw, bw = 3 * A_WIDTH, 3 * B_WIDTH
    c0 = 0
    for win_ref, wd in ((aw_ref, aw), (bw_ref, bw), (lw_ref, B_LORA_PAD)):
        for s in range(0, wd, 512):
            e = min(s + 512, wd)
            win_ref[:, s:e] = jnp.dot(xn_ref[...], w_ref[:, c0 + s:c0 + e], preferred_element_type=F32)
        c0 += wd
    centre = xn_ref[HALO:HALO + tm]
    gates = jnp.dot(centre, w_ref[:, c0:c0 + LANE], preferred_element_type=F32)
    z_out[...] = jnp.dot(centre, w_ref[:, c0 + LANE:], preferred_element_type=F32).astype(z_out.dtype)
    _a_prep_body(aw_ref, gates, cw_ref, alog_ref, dtb_ref, q_out, ka_out, va_out, go_out, valid, tm)
    _b_prep_body(bw_ref, lw_ref, mu_ref, w0_ref, w2_ref, a0_ref, a2_ref, g2_ref, kk_ref, ka_ref, rk_ref, ones_ref,
                 tri_ref, r_out, kb_out, vb_out, kkn_out, eta_out, cf_out, cb_out, bonus_out, gate_out, tm)


def _even_prep(h, norm_w, w_bf16, conv_w, a_log, dt_bias, mu, w0, w2, a0, a2, g2, k_k, k_a, r_k, ones, geom):
    tm, rows = geom.tm, geom.rows
    aw, bw = 3 * A_WIDTH, 3 * B_WIDTH
    row = lambda v: v.reshape(1, -1).astype(F32)
    pad = lambda v: jnp.pad(row(v), ((0, 0), (0, LANE - v.size)))
    narrow = lambda wd: jax.ShapeDtypeStruct((rows, wd), BF16)
    wide = lambda wd: jax.ShapeDtypeStruct((rows, wd), F32)
    blk = lambda wd: pl.BlockSpec((tm, wd), lambda i: (i, 0))
    win = tm + 2 * HALO
    ri, ci = np.arange(SEG_TILE)[:, None], np.arange(SEG_TILE)[None, :]
    same = ri // CHUNK == ci // CHUNK
    tris = jnp.asarray(np.stack([same & (ri >= ci), same & (ri <= ci)]), BF16)
    return pl.pallas_call(
        functools.partial(_even_prep_kernel, geom=geom),
        out_shape=[narrow(3 * A_WIDTH), wide(LANE), narrow(A_WIDTH), narrow(5 * B_WIDTH)] + [wide(B_WIDTH)] * 4,
        grid=(rows // tm,),
        in_specs=_halo_specs(tm, D_MODEL, rows) + [
            _const_spec((1, D_MODEL)), _const_spec(w_bf16.shape),
            _const_spec((A_CONV, aw)), _const_spec((1, LANE)), _const_spec((1, LANE)),
            _const_spec((2, bw + B_LORA_PAD)), _const_spec((1, 2 * B_WIDTH)), _const_spec((B_LORA_PAD, 2 * B_WIDTH)),
            _const_spec((1, B_WIDTH)), _const_spec((B_LORA_PAD, B_WIDTH)), _const_spec((B_LORA_PAD, B_WIDTH)),
            _const_spec((1, B_WIDTH)), _const_spec((1, B_WIDTH)), _const_spec((1, B_WIDTH)),
            _const_spec((SEG_TILE, SEG_TILE)), _const_spec((2, SEG_TILE, SEG_TILE))],
        out_specs=[blk(3 * A_WIDTH), blk(LANE), blk(A_WIDTH), blk(5 * B_WIDTH)] + [blk(B_WIDTH)] * 4,
        scratch_shapes=[pltpu.VMEM((win, D_MODEL), BF16), pltpu.VMEM((win, aw), F32), pltpu.VMEM((win, bw), F32),
                        pltpu.VMEM((win, B_LORA_PAD), F32)],
        compiler_params=_params(),
        name="even_prep",
    )(h, h, h, row(norm_w), w_bf16, conv_w, pad(a_log), pad(dt_bias), mu, w0, w2, row(a0), a2, g2,
      row(k_k), row(k_a), row(r_k), ones, tris)


def _b_scan_kernel(fi_ref, bi_ref, fr_ref,
                   rkvf, cf, rkvb, cb, yf, yb, pf, pb):
    del fi_ref, bi_ref
    rf, kf, vf, kkf, ef = (rkvf.at[:, c * B_WIDTH:(c + 1) * B_WIDTH] for c in range(5))
    rb, kb, vb, kkb, eb = (rkvb.at[:, c * B_WIDTH:(c + 1) * B_WIDTH] for c in range(5))
    @pl.when(fr_ref[pl.program_id(0)] == 1)
    def _():
        pf[...] = jnp.zeros_like(pf)
        pb[...] = jnp.zeros_like(pb)
    n2 = 2 * CHUNK
    r = lax.broadcasted_iota(jnp.int32, (n2, n2), 0)
    c = lax.broadcasted_iota(jnp.int32, (n2, n2), 1)
    same = (r >= CHUNK) == (c >= CHUNK)
    rt, ct = r & (CHUNK - 1), c & (CHUNK - 1)
    eye = jnp.where(r == c, 1.0, 0.0)
    lane_lo = lax.broadcasted_iota(jnp.int32, (1, n2), 1) < B_HD
    row_pos = lax.broadcasted_iota(jnp.int32, (CHUNK, 1), 0)
    nc = SEQ_BLK // CHUNK
    pair_w = 2 * B_HD
    n_pairs = B_HEADS // 2
    dirs = (((rf, kf, vf, kkf, ef, cf), yf, pf, False, same & (rt > ct), same & (rt >= ct)),
            ((rb, kb, vb, kkb, eb, cb), yb, pb, True, same & (rt < ct), same & (rt <= ct)))

    def expand(x):
        return jnp.concatenate([jnp.where(lane_lo, x, 0.0), jnp.where(lane_lo, 0.0, x)], axis=0)

    probs = []
    for d, (refs, _, _, reverse, strict, incl) in enumerate(dirs):
        r_ref, k_ref, v_ref, kk_ref, eta_ref, c_ref = refs
        for step in range(nc):
            jj = nc - 1 - step if reverse else step
            rows = slice(jj * CHUNK, (jj + 1) * CHUNK)
            for pr in range(n_pairs):
                ls = slice(pr * pair_w, (pr + 1) * pair_w)
                r, k, v, kk, eta = (ref[rows, ls].astype(F32) for ref in (r_ref, k_ref, v_ref, kk_ref, eta_ref))
                c_in = c_ref[rows, ls]
                if reverse:
                    c_ex = jnp.where(row_pos < CHUNK - 1, pltpu.roll(c_in, CHUNK - 1, axis=0), 0.0)
                    c_tot = c_in[0:1]
                else:
                    c_ex = jnp.where(row_pos >= 1, pltpu.roll(c_in, 1, axis=0), 0.0)
                    c_tot = c_in[CHUNK - 1:CHUNK]
                b = kk * eta
                inv_w = jnp.exp(-c_in)
                rest_w = jnp.exp(c_tot - c_in)
                a_e = expand(-kk * jnp.exp(c_ex))
                r_e = expand(r * jnp.exp(c_in))
                probs.append(dict(
                    d=d, step=step, pr=pr, rows=rows, ls=ls, strict=strict, incl=incl, v=v,
                    a_e=a_e, r_e=r_e, v_e=expand(v),
                    lhs=jnp.concatenate([a_e, r_e], axis=0),
                    rhs=jnp.concatenate([expand(b * inv_w), expand(k * inv_w)], axis=0),
                    kd_t=jnp.concatenate([b * rest_w, k * rest_w], axis=0).T,
                    w_col=jnp.sum(jnp.where(eye > 0.0, jnp.exp(c_tot), 0.0), axis=1, keepdims=True)))
    for p in probs:
        m1 = _dot_nt(p["lhs"], p["rhs"])
        p["n_ab"] = jnp.where(p["strict"], m1[:n2, :n2], 0.0)
        p["a_ak"] = jnp.where(p["strict"], m1[:n2, n2:], 0.0)
        p["a_r"] = jnp.concatenate([jnp.where(p["incl"], m1[n2:, :n2], 0.0),
                                    jnp.where(p["incl"], m1[n2:, n2:], 0.0)], axis=1)
    t_invs = _neumann_inverses([p["n_ab"] for p in probs], eye)
    for p in probs:
        p["akv"] = _dot(p["a_ak"], p["v_e"])
    for p, t_inv in zip(probs, t_invs):
        p["x"] = _dot(t_inv, jnp.concatenate([p["akv"], p["a_e"]], axis=1))

    states = {(d, pr): dirs[d][2][pr] for d in range(2) for pr in range(n_pairs)}
    for step in range(nc):
        cur = [p for p in probs if p["step"] == step]
        u_es = [p["x"][:, :pair_w] + _dot(p["x"][:, pair_w:], states[p["d"], p["pr"]]) for p in cur]
        for p, u_e in zip(cur, u_es):
            y_e = _dot(jnp.concatenate([p["r_e"], p["a_r"]], axis=1),
                       jnp.concatenate([states[p["d"], p["pr"]], u_e, p["v_e"]], axis=0))
            dirs[p["d"]][1][p["rows"], p["ls"]] = y_e[:CHUNK] + y_e[CHUNK:]
        for p, u_e in zip(cur, u_es):
            key = (p["d"], p["pr"])
            upd = _dot(p["kd_t"], jnp.concatenate([u_e[:CHUNK] + u_e[CHUNK:], p["v"]], axis=0))
            states[key] = states[key] * p["w_col"] + jnp.where(same, upd, 0.0)
    for (d, pr), state in states.items():
        dirs[d][2][pr] = state


def _b_scan(rkv, cf, cb, geom, tables):
    wide_f = pl.BlockSpec((SEQ_BLK, B_WIDTH), lambda s, fi, bi, fr: (fi[s], 0))
    wide_b = pl.BlockSpec((SEQ_BLK, B_WIDTH), lambda s, fi, bi, fr: (bi[s], 0))
    rkv_f = pl.BlockSpec((SEQ_BLK, 5 * B_WIDTH), lambda s, fi, bi, fr: (fi[s], 0))
    rkv_b = pl.BlockSpec((SEQ_BLK, 5 * B_WIDTH), lambda s, fi, bi, fr: (bi[s], 0))
    out = jax.ShapeDtypeStruct((geom.rows, B_WIDTH), F32)
    return pl.pallas_call(
        _b_scan_kernel,
        out_shape=[out, out],
        grid_spec=pltpu.PrefetchScalarGridSpec(
            num_scalar_prefetch=3,
            grid=(tables[0].shape[0],),
            in_specs=[rkv_f, wide_f, rkv_b, wide_b],
            out_specs=[wide_f, wide_b],
            scratch_shapes=[pltpu.VMEM((B_HEADS // 2, 2 * B_HD, 2 * B_HD), F32)] * 2),
        compiler_params=_params(("arbitrary",)),
        name="b_scan",
    )(*tables, rkv, cf, rkv, cb)


def _finish_residual(h, mix, post_w, valid):
    return jnp.where(valid, h + _rms(mix, post_w), 0.0)


def _mixout_even_kernel(h_ref, oaf_ref, oab_ref, z_ref, ybf_ref, ybb_ref, bonus_ref, gate_ref,
                        w_ref, an_ref, lnw_ref, lnb_ref, post_ref, ones_ref, o_ref, cat_ref, *, geom):
    tm = geom.tm
    valid = _row_valid(geom, pl.program_id(0) * tm, tm)
    for hd in range(A_HEADS):
        hs = slice(hd * A_DV, (hd + 1) * A_DV)
        o = oaf_ref[:, hs] + oab_ref[:, hs]
        z = z_ref[:, hs].astype(F32)
        cat_ref[:, hs] = (_rms(o, an_ref[...]) * (z * _sigmoid(z))).astype(BF16)
    y = ybf_ref[...] + ybb_ref[...]
    yc = y - _seg_sum(y, ones_ref, 3) * (1.0 / B_HD)
    var = _seg_sum(yc * yc, ones_ref, 2) * (1.0 / B_HD)
    yn = yc * lax.rsqrt(var + B_GN_EPS) * lnw_ref[...] + lnb_ref[...]
    cat_ref[:, A_WIDTH:] = ((yn + bonus_ref[...]) * gate_ref[...]).astype(BF16)
    mix = jnp.dot(cat_ref[...], w_ref[...], preferred_element_type=F32)
    o_ref[...] = _finish_residual(h_ref[...], mix, post_ref[...], valid)


def _mixout_even(h, oaf, oab, z, ybf, ybb, bonus, gate, w_out, a_norm, ln_w, ln_b, post_w, ones, geom):
    tm, rows = geom.tm, geom.rows
    row = lambda v: v.reshape(1, -1).astype(F32)
    a_blk = pl.BlockSpec((tm, A_WIDTH), lambda i: (i, 0))
    b_blk = pl.BlockSpec((tm, B_WIDTH), lambda i: (i, 0))
    return pl.pallas_call(
        functools.partial(_mixout_even_kernel, geom=geom),
        out_shape=jax.ShapeDtypeStruct((rows, D_MODEL), F32),
        grid=(rows // tm,),
        in_specs=[pl.BlockSpec((tm, D_MODEL), lambda i: (i, 0)), a_blk, a_blk,
                  a_blk,
                  b_blk, b_blk, b_blk, b_blk,
                  _const_spec((A_WIDTH + B_WIDTH, D_MODEL)), _const_spec((1, A_DV)),
                  _const_spec((1, B_WIDTH)), _const_spec((1, B_WIDTH)), _const_spec((1, D_MODEL)),
                  _const_spec((SEG_TILE, SEG_TILE))],
        out_specs=pl.BlockSpec((tm, D_MODEL), lambda i: (i, 0)),
        scratch_shapes=[pltpu.VMEM((tm, A_WIDTH + B_WIDTH), BF16)],
        compiler_params=_params(),
        name="mixout_even",
    )(h, oaf, oab, z, ybf, ybb, bonus, gate, w_out, row(a_norm), row(ln_w), row(ln_b), row(post_w), ones)


def _mixout_odd_kernel(h_ref, o_ref_in, w_ref, post_ref, o_ref, *, geom):
    tm = geom.tm
    valid = _row_valid(geom, pl.program_id(0) * tm, tm)
    mix = jnp.dot(o_ref_in[...], w_ref[...], preferred_element_type=F32)
    o_ref[...] = _finish_residual(h_ref[...], mix, post_ref[...], valid)


def _mixout_odd(h, o, w_out, post_w, geom):
    tm, rows = geom.tm, geom.rows
    wd = C_HEADS * C_HD
    return pl.pallas_call(
        functools.partial(_mixout_odd_kernel, geom=geom),
        out_shape=jax.ShapeDtypeStruct((rows, D_MODEL), F32),
        grid=(rows // tm,),
        in_specs=[pl.BlockSpec((tm, D_MODEL), lambda i: (i, 0)), pl.BlockSpec((tm, wd), lambda i: (i, 0)),
                  _const_spec((wd, D_MODEL)), _const_spec((1, D_MODEL))],
        out_specs=pl.BlockSpec((tm, D_MODEL), lambda i: (i, 0)),
        compiler_params=_params(),
        name="mixout_odd",
    )(h, o, w_out, post_w.reshape(1, D_MODEL))


def _qkv_kernel(x_ref, nw_ref, w_ref, cos_ref, sin_ref, qn_ref, kn_ref, q_out, k_out, v_out, xn_ref):
    xn_ref[...] = _rms(x_ref[...], nw_ref[...]).astype(BF16)
    cos, sin = cos_ref[...], sin_ref[...]

    def rope(y, yp, n_ref, scale):
        rs = lax.rsqrt(jnp.mean(y * y, axis=-1, keepdims=True) + NORM_EPS) * scale
        return ((y * n_ref[0:1]) * cos + (yp * n_ref[1:2]) * sin) * rs

    nq = C_HEADS * C_HD
    nk = C_KV_HEADS * C_HD
    per = 512 // C_HD
    for c in range(nq // 512):
        y = jnp.dot(xn_ref[...], w_ref[:, c * 512:(c + 1) * 512], preferred_element_type=F32)
        yp = jnp.dot(xn_ref[...], w_ref[:, nq + 2 * nk + c * 512:nq + 2 * nk + (c + 1) * 512],
                     preferred_element_type=F32)
        for hd in range(per):
            hs = slice((c * per + hd) * C_HD, (c * per + hd + 1) * C_HD)
            ys = slice(hd * C_HD, (hd + 1) * C_HD)
            q_out[:, hs] = rope(y[:, ys], yp[:, ys], qn_ref, C_HD ** -0.5 * LOG2E).astype(BF16)
    y = jnp.dot(xn_ref[...], w_ref[:, nq:nq + 2 * nk], preferred_element_type=F32)
    yp = jnp.dot(xn_ref[...], w_ref[:, 2 * nq + 2 * nk:], preferred_element_type=F32)
    for hd in range(C_KV_HEADS):
        hs = slice(hd * C_HD, (hd + 1) * C_HD)
        k_out[:, hs] = rope(y[:, hs], yp[:, hs], kn_ref, 1.0).astype(BF16)
        v_out[:, 2 * hd * C_HD:(2 * hd + 1) * C_HD] = y[:, nk + hd * C_HD:nk + (hd + 1) * C_HD].astype(BF16)
        v_out[:, (2 * hd + 1) * C_HD:(2 * hd + 2) * C_HD] = jnp.ones((x_ref.shape[0], C_HD), BF16)


def _qkv_proj(h, norm_w, w_qkv, cos_t, sin_t, q_norm, k_norm, geom):
    tm, rows = geom.tm, geom.rows
    nq, nk = C_HEADS * C_HD, C_KV_HEADS * C_HD
    lane = np.arange(C_HD)
    partner = np.where(lane % (C_HD // 2) < C_HD // 4, lane + C_HD // 4, lane - C_HD // 4)
    cols = np.concatenate([hd * C_HD + partner for hd in range(C_HEADS + C_KV_HEADS)])
    w_bf16 = jnp.concatenate([w_qkv, w_qkv[:, cols]], axis=1).astype(BF16)
    q_norm = jnp.stack([q_norm, q_norm[partner]])
    k_norm = jnp.stack([k_norm, k_norm[partner]])
    return pl.pallas_call(
        _qkv_kernel,
        out_shape=[jax.ShapeDtypeStruct((rows, nq), BF16), jax.ShapeDtypeStruct((rows, nk), BF16),
                   jax.ShapeDtypeStruct((rows, 2 * nk), BF16)],
        grid=(rows // tm,),
        in_specs=[pl.BlockSpec((tm, D_MODEL), lambda i: (i, 0)), _const_spec((1, D_MODEL)),
                  _const_spec((D_MODEL, 2 * nq + 3 * nk)),
                  pl.BlockSpec((tm, C_HD), lambda i: (i, 0)), pl.BlockSpec((tm, C_HD), lambda i: (i, 0)),
                  _const_spec((2, C_HD)), _const_spec((2, C_HD))],
        out_specs=[pl.BlockSpec((tm, nq), lambda i: (i, 0)), pl.BlockSpec((tm, nk), lambda i: (i, 0)),
                   pl.BlockSpec((tm, 2 * nk), lambda i: (i, 0))],
        scratch_shapes=[pltpu.VMEM((tm, D_MODEL), BF16)],
        compiler_params=_params(),
        name="qkv_proj",
    )(h, norm_w.reshape(1, D_MODEL), w_bf16, cos_t, sin_t, q_norm, k_norm)


def _rope_tables(geom):
    nf = C_HD // 4
    inv = ROPE_THETA ** (-jnp.arange(nf, dtype=F32) / nf)
    cos_parts, sin_parts = [], []
    for g in geom.groups:
        t = jnp.arange(g.tp, dtype=jnp.int32) - (g.fv + N_META)
        is_meta = t < 0
        row = jnp.where(is_meta, -1, t // GRID_W).astype(F32)
        col = jnp.where(is_meta, t + N_META, t % GRID_W).astype(F32)
        ang_r, ang_c = row[:, None] * inv, col[:, None] * inv
        cos = jnp.concatenate([jnp.cos(ang_r)] * 2 + [jnp.cos(ang_c)] * 2, axis=1)
        sin = jnp.concatenate([-jnp.sin(ang_r), jnp.sin(ang_r), -jnp.sin(ang_c), jnp.sin(ang_c)], axis=1)
        cos_parts.append(jnp.tile(cos, (g.nb, 1)))
        sin_parts.append(jnp.tile(sin, (g.nb, 1)))
    return jnp.concatenate(cos_parts, axis=0), jnp.concatenate(sin_parts, axis=0)


def _attn_kernel(*refs, nsub, bounds, fv):
    q_ref, k_ref, v_ref, _, o_ref, s_ref, rm_ref, m_ref, acc_ref = refs
    tq = SEQ_BLK
    nc = len(bounds)
    assert nc % 2 == 0

    def q_rows(qi):
        return pl.ds(pl.multiple_of(qi * tq, tq), tq)

    def produce(qi, c, slot):
        start, size = bounds[c]
        q = jnp.concatenate([q_ref[q_rows(qi), g * C_HD:(g + 1) * C_HD] for g in range(C_GROUP)], axis=0)
        s = lax.dot_general(q, k_ref[start:start + size, :], (((1,), (1,)), ((), ())),
                            preferred_element_type=F32)
        if start < fv:
            key = start + lax.broadcasted_iota(jnp.int32, (1, size), 1)
            s = jnp.where(key >= fv, s, -1e30)
        s_ref[slot, :, :size] = s
        rm_ref[slot] = jnp.max(s, axis=-1, keepdims=True)

    def consume(c, slot):
        start, size = bounds[c]
        m_old = m_ref[...]
        m_new = jnp.maximum(m_old, rm_ref[slot])
        p = jnp.exp2(s_ref[slot, :, :size] - m_new).astype(BF16)
        acc_ref[...] = jnp.exp2(m_old - m_new) * acc_ref[...] + jnp.dot(p, v_ref[start:start + size, :],
                                                                        preferred_element_type=F32)
        m_ref[...] = m_new

    produce(0, 0, 0)

    def body(qi, carry):
        m_ref[...] = jnp.full_like(m_ref, -1e30)
        acc_ref[...] = jnp.zeros_like(acc_ref)
        for c in range(nc):
            if c + 1 < nc:
                produce(qi, c + 1, (c + 1) % 2)
            else:
                produce(jnp.minimum(qi + 1, nsub - 1), 0, 0)
            consume(c, c % 2)
        acc = acc_ref[...]
        out = acc[:, :C_HD] / acc[:, C_HD:]
        for g in range(C_GROUP):
            o_ref[q_rows(qi), g * C_HD:(g + 1) * C_HD] = out[g * tq:(g + 1) * tq].astype(o_ref.dtype)
        return carry

    lax.fori_loop(0, nsub, body, 0)


def _key_chunks(tp, fv):
    lo = fv // LANE
    blocks = tp // LANE - lo
    per = ATTN_MAX_TK // LANE
    nc = -(-blocks // per)
    nc += nc % 2
    pairs, single = divmod(blocks, 2)
    sizes = [2 * (pairs // nc + (1 if i < pairs % nc else 0)) for i in range(nc)]
    sizes[-1] += single
    if min(sizes) == 0:
        sizes = [blocks // nc + (1 if i < blocks % nc else 0) for i in range(nc)]
    assert min(sizes) > 0 and max(sizes) <= per
    starts = np.cumsum([lo] + sizes[:-1])
    return tuple((int(s) * LANE, int(z) * LANE) for s, z in zip(starts, sizes))


def _attention(qn, kn, vn, geom):
    out = jnp.zeros((geom.rows, C_HEADS * C_HD), BF16)
    for g in geom.groups:
        bounds = _key_chunks(g.tp, g.fv)
        nblk = g.tp // SEQ_BLK
        nsub = max(d for d in range(1, ATTN_MAX_SUB + 1) if nblk % d == 0)
        tq, gw, rows = nsub * SEQ_BLK, C_GROUP * C_HD, C_GROUP * SEQ_BLK
        al = lambda x: pl.multiple_of(x, LANE)
        q_spec = pl.BlockSpec((pl.Element(tq), pl.Element(gw)),
                              lambda b, h, i, g=g, tq=tq: (al(g.start + b * g.tp + i * tq), al(h * gw)))
        in_specs = [q_spec,
                    pl.BlockSpec((pl.Element(g.tp), pl.Element(C_HD)),
                                 lambda b, h, i, g=g: (al(g.start + b * g.tp), al(h * C_HD))),
                    pl.BlockSpec((pl.Element(g.tp), pl.Element(2 * C_HD)),
                                 lambda b, h, i, g=g: (al(g.start + b * g.tp), al(h * 2 * C_HD))),
                    pl.BlockSpec(memory_space=pl.ANY)]
        out = pl.pallas_call(
            functools.partial(_attn_kernel, nsub=nsub, bounds=bounds, fv=g.fv),
            out_shape=jax.ShapeDtypeStruct(out.shape, out.dtype),
            grid=(g.nb, C_KV_HEADS, g.tp // tq),
            in_specs=in_specs,
            out_specs=q_spec,
            scratch_shapes=[pltpu.VMEM((2, rows, max(z for _, z in bounds)), F32), pltpu.VMEM((2, rows, 1), F32),
                            pltpu.VMEM((rows, 1), F32), pltpu.VMEM((rows, 2 * C_HD), F32)],
            input_output_aliases={3: 0},
            compiler_params=_params(("parallel", "parallel", "parallel")),
            name="attention",
        )(qn, kn, vn, out)
    return out


def _ffn_kernel(x_ref, xp_ref, xn_ref, pre_ref, wu_ref, wg_ref, cw_ref, cb_ref, wo_ref, post_ref,
                o_ref, xs_ref, act_ref, acc_ref, *, tm):
    i = pl.program_id(0)
    xs_ref[0:HALO] = jnp.where(i > 0, _rms(xp_ref[...], pre_ref[...]), 0.0).astype(BF16)
    xs_ref[HALO:HALO + tm] = _rms(x_ref[...], pre_ref[...]).astype(BF16)
    xs_ref[HALO + tm:2 * HALO + tm] = jnp.where(i < pl.num_programs(0) - 1,
                                                _rms(xn_ref[...], pre_ref[...]), 0.0).astype(BF16)
    half = FFN_CONV // 2
    n_chunks = D_FF // FF_CHUNK
    for c in range(n_chunks):
        cs = slice(c * FF_CHUNK, (c + 1) * FF_CHUNK)
        u = jnp.dot(xs_ref[HALO:HALO + tm], wu_ref[:, cs], preferred_element_type=F32)
        g_win = jnp.dot(xs_ref[...], wg_ref[:, cs], preferred_element_type=F32)
        gate = cb_ref[0:1, cs]
        for j in range(FFN_CONV):
            gate = gate + _shifted_rows(g_win, j - half, tm) * cw_ref[j:j + 1, cs]
        th = jnp.tanh(gate * (GELU_C0 + GELU_C0 * 0.044715 * (gate * gate)))
        hu = 0.5 * gate * u
        act_ref[:, cs] = (hu + hu * th).astype(BF16)
        if c % FF_GROUP == FF_GROUP - 1 or c == n_chunks - 1:
            ks = slice((c // FF_GROUP) * FF_GROUP * FF_CHUNK, (c + 1) * FF_CHUNK)
            part = jnp.dot(act_ref[:, ks], wo_ref[ks, :], preferred_element_type=F32)
            if c < FF_GROUP:
                acc_ref[...] = part
            else:
                acc_ref[...] += part
    o_ref[...] = x_ref[...] + _rms(acc_ref[...], post_ref[...])


def _conv_ffn(h, pre_w, w_in, conv_w, conv_b, w_out, post_w, geom):
    tm, rows = geom.tm, geom.rows
    assert D_FF % FF_CHUNK == 0
    wu = w_in[:, :D_FF].astype(BF16)
    wg = w_in[:, D_FF:].astype(BF16)
    return pl.pallas_call(
        functools.partial(_ffn_kernel, tm=tm),
        out_shape=jax.ShapeDtypeStruct((rows, D_MODEL), F32),
        grid=(rows // tm,),
        in_specs=_halo_specs(tm, D_MODEL, rows) + [
            _const_spec((1, D_MODEL)), _const_spec((D_MODEL, D_FF)), _const_spec((D_MODEL, D_FF)),
            _const_spec((FFN_CONV, D_FF)), _const_spec((1, D_FF)), _const_spec((D_FF, D_MODEL)),
            _const_spec((1, D_MODEL))],
        out_specs=pl.BlockSpec((tm, D_MODEL), lambda i: (i, 0)),
        scratch_shapes=[pltpu.VMEM((tm + 2 * HALO, D_MODEL), BF16), pltpu.VMEM((tm, D_FF), BF16),
                        pltpu.VMEM((tm, D_MODEL), F32)],
        compiler_params=_params(),
        name="conv_ffn",
    )(h, h, h, pre_w.reshape(1, D_MODEL), wu, wg, conv_w, conv_b.reshape(1, D_FF),
      w_out.astype(BF16), post_w.reshape(1, D_MODEL))


def _meta_kernel(meta_ref, _, o_ref):
    o_ref[...] = meta_ref[...].astype(o_ref.dtype)


def _rows_in_kernel(x_ref, _, o_ref):
    o_ref[...] = x_ref[0]


def _rows_out_kernel(h_ref, o_ref):
    o_ref[0] = h_ref[...]


def _copy_block(n):
    return max(b for b in (2048, 1024, 512, 256, 128) if n % b == 0)


def _pack_rows(xs, meta, geom):
    al = lambda v: pl.multiple_of(v, LANE)
    shape = jax.ShapeDtypeStruct((geom.rows, D_MODEL), xs[0].dtype)
    h = jnp.zeros(shape.shape, shape.dtype)
    for x, g in zip(xs, geom.groups):
        pad = g.tp - g.n
        h = pl.pallas_call(
            _meta_kernel, out_shape=shape, grid=(g.nb,),
            in_specs=[_const_spec((N_META, D_MODEL)), pl.BlockSpec(memory_space=pl.ANY)],
            out_specs=pl.BlockSpec((pl.Element(N_META), pl.Element(D_MODEL)),
                                   lambda b, g=g: (pl.multiple_of(g.start + b * g.tp + g.fv, N_META), 0)),
            input_output_aliases={1: 0},
            compiler_params=_params(), name="pack_meta",
        )(meta, h)
        blk = _copy_block(g.n)
        h = pl.pallas_call(
            _rows_in_kernel, out_shape=shape, grid=(g.nb, g.n // blk),
            in_specs=[pl.BlockSpec((1, blk, D_MODEL), lambda b, j: (b, j, 0)), pl.BlockSpec(memory_space=pl.ANY)],
            out_specs=pl.BlockSpec((pl.Element(blk), pl.Element(D_MODEL)),
                                   lambda b, j, g=g, pad=pad, blk=blk: (al(g.start + b * g.tp + pad + j * blk), 0)),
            input_output_aliases={1: 0},
            compiler_params=_params(("parallel", "parallel")), name="pack_rows",
        )(x, h)
    return h


def _unpack_rows(h, geom):
    al = lambda v: pl.multiple_of(v, LANE)
    outs = []
    for g in geom.groups:
        pad = g.tp - g.n
        blk = _copy_block(g.n)
        outs.append(pl.pallas_call(
            _rows_out_kernel, out_shape=jax.ShapeDtypeStruct((g.nb, g.n, D_MODEL), h.dtype),
            grid=(g.nb, g.n // blk),
            in_specs=[pl.BlockSpec((pl.Element(blk), pl.Element(D_MODEL)),
                                   lambda b, j, g=g, pad=pad, blk=blk: (al(g.start + b * g.tp + pad + j * blk), 0))],
            out_specs=pl.BlockSpec((1, blk, D_MODEL), lambda b, j: (b, j, 0)),
            compiler_params=_params(("parallel", "parallel")), name="unpack_rows",
        )(h))
    return tuple(outs)


def _even_layer(h, j, geom, tables, ones, p):
    aw, bw = A_WIDTH, B_WIDTH
    a_cols = 4 * aw + 4 * A_HEADS
    w_in = p["e_w_in"][j]
    lora = B_W_LORA + B_A_LORA + B_G_LORA
    zeros = lambda n: jnp.zeros((D_MODEL, n), F32)
    w_cat = jnp.concatenate([
        w_in[:, :3 * aw],
        w_in[:, a_cols:a_cols + 3 * bw + lora], zeros(B_LORA_PAD - lora),
        w_in[:, 4 * aw:a_cols], zeros(LANE - 4 * A_HEADS),
        w_in[:, 3 * aw:4 * aw]], axis=1).astype(BF16)
    mu = jnp.pad(p["b_shift"][j], ((0, 0), (0, B_LORA_PAD - lora)))
    lora_rows = lambda m, off: jnp.pad(m, ((off, B_LORA_PAD - off - m.shape[0]), (0, 0)))
    w2 = lora_rows(jnp.concatenate([p["b_w2"][j, 0], p["b_w2"][j, 1]], axis=1), 0)
    a2 = lora_rows(p["b_a2"][j], B_W_LORA)
    g2 = lora_rows(p["b_g2"][j], B_W_LORA + B_A_LORA)
    qkv, gts, z, rkv, cf, cb, bonus, gate = _even_prep(
        h, p["mix_pre_norm"][2 * j], w_cat, p["a_conv_w"][j], p["a_log"][j], p["a_dt_bias"][j],
        mu, p["b_w0"][j].reshape(1, 2 * bw), w2, p["b_a0"][j], a2, g2,
        p["b_k_k"][j], p["b_k_a"][j], p["b_r_k"][j], ones, geom)
    oaf, oab = _a_scan(qkv, gts, geom, tables)
    ybf, ybb = _b_scan(rkv, cf, cb, geom, tables)

    return _mixout_even(h, oaf, oab, z, ybf, ybb, bonus, gate, p["e_w_out"][j].astype(BF16),
                        p["a_out_norm"][j], p["b_ln_w"][j], p["b_ln_b"][j], p["mix_post_norm"][2 * j], ones, geom)


def _odd_layer(h, j, geom, rope, p):
    qn, kn, vn = _qkv_proj(h, p["mix_pre_norm"][2 * j + 1], p["o_w_qkv"][j].astype(BF16), rope[0], rope[1],
                           p["o_q_norm"][j], p["o_k_norm"][j], geom)
    o = _attention(qn, kn, vn, geom)
    return _mixout_odd(h, o, p["o_w_out"][j].astype(BF16), p["mix_post_norm"][2 * j + 1], geom)


def kernel(x_prompt, x_sample, meta, mix_pre_norm, mix_post_norm, ffn_pre_norm, ffn_post_norm, e_w_in, a_conv_w, a_log, a_dt_bias, a_out_norm, b_shift, b_w0, b_w2, b_a0, b_a2, b_g2, b_k_k, b_k_a, b_r_k, b_ln_w, b_ln_b, e_w_out, o_w_qkv, o_q_norm, o_k_norm, o_w_out, f_w_in, f_conv_w, f_conv_b, f_w_out):
    p = dict(mix_pre_norm=mix_pre_norm, mix_post_norm=mix_post_norm, e_w_in=e_w_in, a_conv_w=a_conv_w,
             a_log=a_log, a_dt_bias=a_dt_bias, a_out_norm=a_out_norm, b_shift=b_shift, b_w0=b_w0, b_w2=b_w2,
             b_a0=b_a0, b_a2=b_a2, b_g2=b_g2, b_k_k=b_k_k, b_k_a=b_k_a, b_r_k=b_r_k, b_ln_w=b_ln_w,
             b_ln_b=b_ln_b, e_w_out=e_w_out, o_w_qkv=o_w_qkv, o_q_norm=o_q_norm, o_k_norm=o_k_norm,
             o_w_out=o_w_out)
    xs = (x_prompt, x_sample)
    geom = _choose_geom([(x.shape[0], x.shape[1]) for x in xs])
    tables = _scan_tables(geom)
    rope = _rope_tables(geom)
    seg = np.arange(SEG_TILE) // B_HD
    ones = jnp.asarray(seg[:, None] == seg[None, :], BF16)

    h = _pack_rows(xs, meta, geom)
    for i in range(DEPTH):
        h = _even_layer(h, i // 2, geom, tables, ones, p) if i % 2 == 0 else _odd_layer(h, i // 2, geom, rope, p)
        h = _conv_ffn(h, ffn_pre_norm[i], f_w_in[i], f_conv_w[i], f_conv_b[i], f_w_out[i], ffn_post_norm[i], geom)
    return _unpack_rows(h, geom)
```

```python
import functools
import itertools
from typing import NamedTuple

import numpy as np
import jax
import jax.numpy as jnp
from jax import lax
from jax.experimental import pallas as pl
from jax.experimental.pallas import tpu as pltpu

F32 = jnp.float32
BF16 = jnp.bfloat16

D_MODEL = 1024
DEPTH = 2
N_META = 16
GRID_W = 64
NORM_EPS = 1e-6

A_HEADS = 4
A_DK = 128
A_DV = 128
A_CONV = 5
A_WIDTH = A_HEADS * A_DV

B_HEADS = 8
B_HD = 64
B_WIDTH = B_HEADS * B_HD
B_W_LORA = 32
B_A_LORA = 32
B_G_LORA = 96
B_GN_EPS = 64e-5
B_LORA_PAD = 256

C_HEADS = 8
C_KV_HEADS = 2
C_GROUP = C_HEADS // C_KV_HEADS
C_HD = 128
ROPE_THETA = 10000.0

D_FF = 2816
FFN_CONV = 3

LANE = 128
CHUNK = 64
SEQ_BLK = 128
HALO = 16
FF_CHUNK = 256
FF_GROUP = 4
ATTN_MAX_TK = 2304
ATTN_MAX_SUB = 17
SEG_TILE = 256
LOG2E = 1.4426950408889634
GELU_C0 = 0.7978845608028654
VMEM_LIMIT = 56 * 1024 * 1024


class _Group(NamedTuple):
    start: int
    nb: int
    n: int
    tp: int
    fv: int


class _Geom(NamedTuple):
    groups: tuple
    rows: int
    tm: int


def _choose_geom(shapes):
    tm = 512 if min(n for _, n in shapes) >= 512 else 128
    best = None
    for pads in itertools.product((128, 256, 384, 512), repeat=len(shapes)):
        start, ok, groups = 0, True, []
        for (nb, n), p in zip(shapes, pads):
            ok = ok and start % tm == 0
            groups.append(_Group(start, nb, n, n + p, p - N_META))
            start += nb * (n + p)
        if ok and start % tm == 0 and (best is None or start < best.rows):
            best = _Geom(tuple(groups), start, tm)
    assert best is not None
    return best


def _row_valid(geom, base, tm):
    start = jnp.int32(geom.groups[0].start)
    tp = jnp.int32(geom.groups[0].tp)
    fv = jnp.int32(geom.groups[0].fv)
    for g in geom.groups[1:]:
        assert tm <= g.tp
        inside = base >= g.start
        start = jnp.where(inside, g.start, start)
        tp = jnp.where(inside, g.tp, tp)
        fv = jnp.where(inside, g.fv, fv)
    pos = lax.rem(base - start, tp) + lax.broadcasted_iota(jnp.int32, (tm, 1), 0)
    pos = jnp.where(pos >= tp, pos - tp, pos)
    return pos >= fv


def _rms(x, w, eps=NORM_EPS):
    return x * lax.rsqrt(jnp.mean(x * x, axis=-1, keepdims=True) + eps) * w


def _sigmoid(x):
    return 1.0 / (1.0 + jnp.exp(-x))


def _softplus(x):
    return jnp.maximum(x, 0.0) + jnp.log1p(jnp.exp(-jnp.abs(x)))


def _dot(a, b):
    return jnp.dot(a.astype(BF16), b.astype(BF16), preferred_element_type=F32)


def _dot_nt(a, b):
    return lax.dot_general(a.astype(BF16), b.astype(BF16), (((1,), (1,)), ((), ())),
                           preferred_element_type=F32)


def _seg_sum(x, ones_ref, terms):
    acc = None
    rem = x
    for t in range(terms):
        piece = rem.astype(BF16)
        part = jnp.concatenate([jnp.dot(piece[:, s:s + SEG_TILE], ones_ref[...], preferred_element_type=F32)
                                for s in range(0, x.shape[1], SEG_TILE)], axis=1)
        acc = part if acc is None else acc + part
        if t + 1 < terms:
            rem = rem - piece.astype(F32)
    return acc


def _chunk_cumsum_mxu(x, tri):
    tile = tri.shape[0]
    acc = None
    rem = x
    for t in range(3):
        piece = rem.astype(BF16)
        part = jnp.concatenate([jnp.dot(tri, piece[s:s + tile], preferred_element_type=F32)
                                for s in range(0, x.shape[0], tile)], axis=0)
        acc = part if acc is None else acc + part
        if t < 2:
            rem = rem - piece.astype(F32)
    return acc


def _neumann_inverses(n_mats, eye):
    n = eye.shape[0]
    prods = [eye + m for m in n_mats]
    powers = [_dot(m, m) for m in n_mats]
    s = 4
    while s < CHUNK:
        both = [_dot(jnp.concatenate([pr, pw], axis=0), pw) for pr, pw in zip(prods, powers)]
        prods = [pr + b[:n] for pr, b in zip(prods, both)]
        powers = [b[n:] for b in both]
        s *= 2
    return [pr + _dot(pr, pw) for pr, pw in zip(prods, powers)]


def _shifted_rows(win, d, tm):
    if d == 0:
        return win[HALO:HALO + tm]
    return pltpu.roll(win, (-d) % win.shape[0], axis=0)[HALO:HALO + tm]


def _halo_specs(tm, width, rows):
    per = tm // HALO
    last = rows // HALO - 1
    return [
        pl.BlockSpec((tm, width), lambda i: (i, 0)),
        pl.BlockSpec((HALO, width), lambda i: (jnp.maximum(i * per - 1, 0), 0)),
        pl.BlockSpec((HALO, width), lambda i: (jnp.minimum((i + 1) * per, last), 0)),
    ]


def _const_spec(shape):
    return pl.BlockSpec(shape, lambda *_: (0,) * len(shape))


def _params(sem=("parallel",)):
    return pltpu.CompilerParams(dimension_semantics=sem, vmem_limit_bytes=VMEM_LIMIT)


def _a_prep_body(xw_ref, al, cw_ref, alog_ref, dtb_ref, tri_ref, q_ref, k_ref, v_ref, go_ref, valid, tm):
    half = A_CONV // 2
    for c, o_ref in enumerate((q_ref, k_ref, v_ref)):
        cs = slice(c * A_WIDTH, (c + 1) * A_WIDTH)
        win = xw_ref[:, cs]
        acc = _shifted_rows(win, -half, tm) * cw_ref[0:1, cs]
        for j in range(1, A_CONV):
            acc = acc + _shifted_rows(win, j - half, tm) * cw_ref[j:j + 1, cs]
        y = jnp.where(valid, acc * _sigmoid(acc), 0.0)
        for hd in range(A_HEADS):
            hs = slice(hd * A_DK, (hd + 1) * A_DK)
            yh = y[:, hs]
            if c < 2:
                yh = yh * lax.rsqrt(jnp.sum(yh * yh, axis=-1, keepdims=True) + 1e-6)
                if c == 0:
                    yh = yh * (A_DK ** -0.5)
            o_ref[:, hs] = yh.astype(o_ref.dtype)
    lane = lax.broadcasted_iota(jnp.int32, (1, LANE), 1)
    gval = jnp.where(valid, -jnp.exp(alog_ref[...]) * _softplus(al + dtb_ref[...]), 0.0)
    beta = jnp.where(valid, _sigmoid(al), 0.0)
    tile = min(SEG_TILE, tm)
    cum_f = _chunk_cumsum_mxu(gval, tri_ref[0, :tile, :tile])
    cum_b = _chunk_cumsum_mxu(gval, tri_ref[1, :tile, :tile])
    go_ref[...] = jnp.where(lane < A_HEADS, cum_f,
                            jnp.where(lane < 2 * A_HEADS, cum_b,
                                      jnp.where(lane < 4 * A_HEADS, beta, 0.0)))


def _a_scan_kernel(fi_ref, bi_ref, fr_ref,
                   qkvf, gf, gtf, qkvb, gb, gtb, of, ob, sf, sb):
    del fi_ref, bi_ref
    qf, kf, vf = (qkvf.at[:, c * A_WIDTH:(c + 1) * A_WIDTH] for c in range(3))
    qb, kb, vb = (qkvb.at[:, c * A_WIDTH:(c + 1) * A_WIDTH] for c in range(3))
    @pl.when(fr_ref[pl.program_id(0)] == 1)
    def _():
        sf[...] = jnp.zeros_like(sf)
        sb[...] = jnp.zeros_like(sb)
    r = lax.broadcasted_iota(jnp.int32, (CHUNK, CHUNK), 0)
    c = lax.broadcasted_iota(jnp.int32, (CHUNK, CHUNK), 1)
    eye = jnp.where(r == c, 1.0, 0.0)
    nc = SEQ_BLK // CHUNK
    dirs = ((qf, kf, vf, gf, gtf, of, sf, False, r >= c, r > c),
            (qb, kb, vb, gb, gtb, ob, sb, True, r <= c, r < c))

    probs = []
    for d, (q_ref, k_ref, v_ref, g_ref, gt_ref, _, _, reverse, tri, strict) in enumerate(dirs):
        edge = 0 if reverse else CHUNK - 1
        for step in range(nc):
            jj = nc - 1 - step if reverse else step
            rows = slice(jj * CHUNK, (jj + 1) * CHUNK)
            for hd in range(A_HEADS):
                hs = slice(hd * A_DK, (hd + 1) * A_DK)
                col = hd + (A_HEADS if reverse else 0)
                q, k, v = (ref[rows, hs].astype(F32) for ref in (q_ref, k_ref, v_ref))
                g_col = g_ref[rows, col:col + 1]
                b_col = g_ref[rows, 2 * A_HEADS + col:2 * A_HEADS + col + 1]
                g_row = gt_ref[jj, col:col + 1, :]
                decay = jnp.where(tri, jnp.exp(jnp.where(tri, g_col - g_row, 0.0)), 0.0)
                kb_ = k * b_col
                e_g = jnp.exp(g_col)
                g_last = g_col[edge:edge + 1]
                probs.append(dict(d=d, step=step, hd=hd, rows=rows, hs=hs, tri=tri, strict=strict, decay=decay,
                                  k=k, kbq=jnp.concatenate([kb_, q], axis=0),
                                  rhs=jnp.concatenate([v * b_col, kb_ * e_g], axis=1),
                                  qg=q * e_g, kg_t=(k * jnp.exp(g_last - g_col)).T, g_end=jnp.exp(g_last)))
    for p in probs:
        kq = _dot_nt(p["kbq"], p["k"])
        p["l"] = jnp.where(p["strict"], kq[:CHUNK] * p["decay"], 0.0)
        p["attn"] = jnp.where(p["tri"], kq[CHUNK:] * p["decay"], 0.0)
    t_invs = _neumann_inverses([-p["l"] for p in probs], eye)
    for p, t_inv in zip(probs, t_invs):
        p["uw"] = _dot(t_inv, p["rhs"])

    states = {(d, hd): dirs[d][6][hd] for d in range(2) for hd in range(A_HEADS)}
    for step in range(nc):
        cur = [p for p in probs if p["step"] == step]
        v_news = [p["uw"][:, :A_DV] - _dot(p["uw"][:, A_DV:], states[p["d"], p["hd"]]) for p in cur]
        for p, v_new in zip(cur, v_news):
            state = states[p["d"], p["hd"]]
            dirs[p["d"]][5][p["rows"], p["hs"]] = _dot(jnp.concatenate([p["qg"], p["attn"]], axis=1),
                                                      jnp.concatenate([state, v_new], axis=0))
        for p, v_new in zip(cur, v_news):
            key = (p["d"], p["hd"])
            states[key] = states[key] * p["g_end"] + _dot(p["kg_t"], v_new)
    for (d, hd), state in states.items():
        dirs[d][6][hd] = state


def _scan_tables(geom):
    fi, bi, fr = [], [], []
    for g in geom.groups:
        nblk = g.tp // SEQ_BLK
        for b in range(g.nb):
            base = (g.start + b * g.tp) // SEQ_BLK
            for i in range(nblk):
                fi.append(base + i)
                bi.append(base + nblk - 1 - i)
                fr.append(1 if i == 0 else 0)
    mk = lambda v: jnp.asarray(np.asarray(v, np.int32))
    return mk(fi), mk(bi), mk(fr)


def _a_scan(qkv, gts, geom, tables):
    rows = geom.rows
    nc = SEQ_BLK // CHUNK
    gts_t = gts[:, :4 * A_HEADS].reshape(rows // CHUNK, CHUNK, 4 * A_HEADS).transpose(0, 2, 1)
    wide_f = pl.BlockSpec((SEQ_BLK, A_WIDTH), lambda s, fi, bi, fr: (fi[s], 0))
    wide_b = pl.BlockSpec((SEQ_BLK, A_WIDTH), lambda s, fi, bi, fr: (bi[s], 0))
    qkv_f = pl.BlockSpec((SEQ_BLK, 3 * A_WIDTH), lambda s, fi, bi, fr: (fi[s], 0))
    qkv_b = pl.BlockSpec((SEQ_BLK, 3 * A_WIDTH), lambda s, fi, bi, fr: (bi[s], 0))
    gate_f = pl.BlockSpec((SEQ_BLK, LANE), lambda s, fi, bi, fr: (fi[s], 0))
    gate_b = pl.BlockSpec((SEQ_BLK, LANE), lambda s, fi, bi, fr: (bi[s], 0))
    gt_f = pl.BlockSpec((nc, 4 * A_HEADS, CHUNK), lambda s, fi, bi, fr: (fi[s], 0, 0))
    gt_b = pl.BlockSpec((nc, 4 * A_HEADS, CHUNK), lambda s, fi, bi, fr: (bi[s], 0, 0))
    out = jax.ShapeDtypeStruct((rows, A_WIDTH), F32)
    return pl.pallas_call(
        _a_scan_kernel,
        out_shape=[out, out],
        grid_spec=pltpu.PrefetchScalarGridSpec(
            num_scalar_prefetch=3,
            grid=(tables[0].shape[0],),
            in_specs=[qkv_f, gate_f, gt_f, qkv_b, gate_b, gt_b],
            out_specs=[wide_f, wide_b],
            scratch_shapes=[pltpu.VMEM((A_HEADS, A_DK, A_DV), F32)] * 2),
        compiler_params=_params(("arbitrary",)),
        name="a_scan",
    )(*tables, qkv, gts, gts_t, qkv, gts, gts_t)


def _b_prep_body(xw_ref, lw_ref, mu_ref, w0_ref, w2_ref, a0_ref, a2_ref, g2_ref, kk_ref, ka_ref, rk_ref, ones_ref,
                 tri_ref, r_out, k_out, v_out, kkn_out, eta_out, cf_out, cb_out, bonus_out, gate_out, tm):
    def shifted(win_ref, cs, mu_off):
        win = win_ref[:, cs]
        cur, prev, nxt = _shifted_rows(win, 0, tm), _shifted_rows(win, -1, tm), _shifted_rows(win, 1, tm)
        ms = slice(mu_off + cs.start, mu_off + cs.stop)
        return cur + mu_ref[0:1, ms] * (prev - cur) + mu_ref[1:2, ms] * (nxt - cur)

    w = B_WIDTH
    r = shifted(xw_ref, slice(0, w), 0)
    k = shifted(xw_ref, slice(w, 2 * w), 0)
    v = shifted(xw_ref, slice(2 * w, 3 * w), 0)
    lo = shifted(lw_ref, slice(0, B_LORA_PAD), 3 * w)
    wl = _dot(jnp.tanh(lo), w2_ref[...]) + w0_ref[...]
    log_decay = -float(np.exp(-0.5)) * _sigmoid(wl)
    tile = min(SEG_TILE, tm)
    cf_out[...] = _chunk_cumsum_mxu(log_decay[:, :w], tri_ref[0, :tile, :tile])
    cb_out[...] = _chunk_cumsum_mxu(log_decay[:, w:], tri_ref[1, :tile, :tile])
    eta = _sigmoid(a0_ref[...] + _dot(lo, a2_ref[...]))
    gate_out[...] = _dot(_sigmoid(lo), g2_ref[...])
    kx = k * kk_ref[...]
    kkn_out[...] = (kx * lax.rsqrt(_seg_sum(kx * kx, ones_ref, 2) + 1e-6)).astype(kkn_out.dtype)
    k = k * (1.0 + (eta - 1.0) * ka_ref[...])
    bonus_out[...] = _seg_sum(r * k * rk_ref[...], ones_ref, 2) * v
    r_out[...] = r.astype(r_out.dtype)
    k_out[...] = k.astype(k_out.dtype)
    v_out[...] = v.astype(v_out.dtype)
    eta_out[...] = eta.astype(eta_out.dtype)


def _even_prep_kernel(h_ref, hp_ref, hn_ref, nw_ref, w_ref, cw_ref, alog_ref, dtb_ref,
                      mu_ref, w0_ref, w2_ref, a0_ref, a2_ref, g2_ref, kk_ref, ka_ref, rk_ref, ones_ref, tri_ref,
                      qkv_out, go_out, z_out, rkv_out, cf_out, cb_out, bonus_out, gate_out,
                      xn_ref, aw_ref, bw_ref, lw_ref, *, geom):
    q_out, ka_out, va_out = (qkv_out.at[:, c * A_WIDTH:(c + 1) * A_WIDTH] for c in range(3))
    r_out, kb_out, vb_out, kkn_out, eta_out = (rkv_out.at[:, c * B_WIDTH:(c + 1) * B_WIDTH] for c in range(5))
    tm = geom.tm
    i = pl.program_id(0)
    valid = _row_valid(geom, i * tm, tm)
    xn_ref[0:HALO] = jnp.where(i > 0, _rms(hp_ref[...], nw_ref[...]), 0.0).astype(BF16)
    xn_ref[HALO:HALO + tm] = _rms(h_ref[...], nw_ref[...]).astype(BF16)
    xn_ref[HALO + tm:2 * HALO + tm] = jnp.where(i < pl.num_programs(0) - 1,
                                                _rms(hn_ref[...], nw_ref[...]), 0.0).astype(BF16)
    aw, bw = 3 * A_WIDTH, 3 * B_WIDTH
    c0 = 0
    for win_ref, wd in ((aw_ref, aw), (bw_ref, bw), (lw_ref, B_LORA_PAD)):
        for s in range(0, wd, 512):
            e = min(s + 512, wd)
            win_ref[:, s:e] = jnp.dot(xn_ref[...], w_ref[:, c0 + s:c0 + e], preferred_element_type=F32)
        c0 += wd
    centre = xn_ref[HALO:HALO + tm]
    gates = jnp.dot(centre, w_ref[:, c0:c0 + LANE], preferred_element_type=F32)
    z_out[...] = jnp.dot(centre, w_ref[:, c0 + LANE:], preferred_element_type=F32).astype(z_out.dtype)
    _a_prep_body(aw_ref, gates, cw_ref, alog_ref, dtb_ref, tri_ref, q_out, ka_out, va_out, go_out, valid, tm)
    _b_prep_body(bw_ref, lw_ref, mu_ref, w0_ref, w2_ref, a0_ref, a2_ref, g2_ref, kk_ref, ka_ref, rk_ref, ones_ref,
                 tri_ref, r_out, kb_out, vb_out, kkn_out, eta_out, cf_out, cb_out, bonus_out, gate_out, tm)


def _even_prep(h, norm_w, w_bf16, conv_w, a_log, dt_bias, mu, w0, w2, a0, a2, g2, k_k, k_a, r_k, ones, geom):
    tm, rows = geom.tm, geom.rows
    aw, bw = 3 * A_WIDTH, 3 * B_WIDTH
    row = lambda v: v.reshape(1, -1).astype(F32)
    pad = lambda v: jnp.pad(row(v), ((0, 0), (0, LANE - v.size)))
    narrow = lambda wd: jax.ShapeDtypeStruct((rows, wd), BF16)
    wide = lambda wd: jax.ShapeDtypeStruct((rows, wd), F32)
    blk = lambda wd: pl.BlockSpec((tm, wd), lambda i: (i, 0))
    win = tm + 2 * HALO
    ri, ci = np.arange(SEG_TILE)[:, None], np.arange(SEG_TILE)[None, :]
    same = ri // CHUNK == ci // CHUNK
    tris = jnp.asarray(np.stack([same & (ri >= ci), same & (ri <= ci)]), BF16)
    return pl.pallas_call(
        functools.partial(_even_prep_kernel, geom=geom),
        out_shape=[narrow(3 * A_WIDTH), wide(LANE), narrow(A_WIDTH), narrow(5 * B_WIDTH)] + [wide(B_WIDTH)] * 4,
        grid=(rows // tm,),
        in_specs=_halo_specs(tm, D_MODEL, rows) + [
            _const_spec((1, D_MODEL)), _const_spec(w_bf16.shape),
            _const_spec((A_CONV, aw)), _const_spec((1, LANE)), _const_spec((1, LANE)),
            _const_spec((2, bw + B_LORA_PAD)), _const_spec((1, 2 * B_WIDTH)), _const_spec((B_LORA_PAD, 2 * B_WIDTH)),
            _const_spec((1, B_WIDTH)), _const_spec((B_LORA_PAD, B_WIDTH)), _const_spec((B_LORA_PAD, B_WIDTH)),
            _const_spec((1, B_WIDTH)), _const_spec((1, B_WIDTH)), _const_spec((1, B_WIDTH)),
            _const_spec((SEG_TILE, SEG_TILE)), _const_spec((2, SEG_TILE, SEG_TILE))],
        out_specs=[blk(3 * A_WIDTH), blk(LANE), blk(A_WIDTH), blk(5 * B_WIDTH)] + [blk(B_WIDTH)] * 4,
        scratch_shapes=[pltpu.VMEM((win, D_MODEL), BF16), pltpu.VMEM((win, aw), F32), pltpu.VMEM((win, bw), F32),
                        pltpu.VMEM((win, B_LORA_PAD), F32)],
        compiler_params=_params(),
        name="even_prep",
    )(h, h, h, row(norm_w), w_bf16, conv_w, pad(a_log), pad(dt_bias), mu, w0, w2, row(a0), a2, g2,
      row(k_k), row(k_a), row(r_k), ones, tris)


def _b_scan_kernel(fi_ref, bi_ref, fr_ref,
                   rkvf, cf, rkvb, cb, yf, yb, pf, pb):
    del fi_ref, bi_ref
    rf, kf, vf, kkf, ef = (rkvf.at[:, c * B_WIDTH:(c + 1) * B_WIDTH] for c in range(5))
    rb, kb, vb, kkb, eb = (rkvb.at[:, c * B_WIDTH:(c + 1) * B_WIDTH] for c in range(5))
    @pl.when(fr_ref[pl.program_id(0)] == 1)
    def _():
        pf[...] = jnp.zeros_like(pf)
        pb[...] = jnp.zeros_like(pb)
    n2 = 2 * CHUNK
    r = lax.broadcasted_iota(jnp.int32, (n2, n2), 0)
    c = lax.broadcasted_iota(jnp.int32, (n2, n2), 1)
    same = (r >= CHUNK) == (c >= CHUNK)
    rt, ct = r & (CHUNK - 1), c & (CHUNK - 1)
    eye = jnp.where(r == c, 1.0, 0.0)
    lane_lo = lax.broadcasted_iota(jnp.int32, (1, n2), 1) < B_HD
    row_pos = lax.broadcasted_iota(jnp.int32, (CHUNK, 1), 0)
    nc = SEQ_BLK // CHUNK
    pair_w = 2 * B_HD
    n_pairs = B_HEADS // 2
    dirs = (((rf, kf, vf, kkf, ef, cf), yf, pf, False, same & (rt > ct), same & (rt >= ct)),
            ((rb, kb, vb, kkb, eb, cb), yb, pb, True, same & (rt < ct), same & (rt <= ct)))

    def expand(x):
        return jnp.concatenate([jnp.where(lane_lo, x, 0.0), jnp.where(lane_lo, 0.0, x)], axis=0)

    probs = []
    for d, (refs, _, _, reverse, strict, incl) in enumerate(dirs):
        r_ref, k_ref, v_ref, kk_ref, eta_ref, c_ref = refs
        for step in range(nc):
            jj = nc - 1 - step if reverse else step
            rows = slice(jj * CHUNK, (jj + 1) * CHUNK)
            for pr in range(n_pairs):
                ls = slice(pr * pair_w, (pr + 1) * pair_w)
                r, k, v, kk, eta = (ref[rows, ls].astype(F32) for ref in (r_ref, k_ref, v_ref, kk_ref, eta_ref))
                c_in = c_ref[rows, ls]
                if reverse:
                    c_ex = jnp.where(row_pos < CHUNK - 1, pltpu.roll(c_in, CHUNK - 1, axis=0), 0.0)
                    c_tot = c_in[0:1]
                else:
                    c_ex = jnp.where(row_pos >= 1, pltpu.roll(c_in, 1, axis=0), 0.0)
                    c_tot = c_in[CHUNK - 1:CHUNK]
                b = kk * eta
                inv_w = jnp.exp(-c_in)
                rest_w = jnp.exp(c_tot - c_in)
                a_e = expand(-kk * jnp.exp(c_ex))
                r_e = expand(r * jnp.exp(c_in))
                probs.append(dict(
                    d=d, step=step, pr=pr, rows=rows, ls=ls, strict=strict, incl=incl, v=v,
                    a_e=a_e, r_e=r_e, v_e=expand(v),
                    lhs=jnp.concatenate([a_e, r_e], axis=0),
                    rhs=jnp.concatenate([expand(b * inv_w), expand(k * inv_w)], axis=0),
                    kd_t=jnp.concatenate([b * rest_w, k * rest_w], axis=0).T,
                    w_col=jnp.sum(jnp.where(eye > 0.0, jnp.exp(c_tot), 0.0), axis=1, keepdims=True)))
    for p in probs:
        m1 = _dot_nt(p["lhs"], p["rhs"])
        p["n_ab"] = jnp.where(p["strict"], m1[:n2, :n2], 0.0)
        p["a_ak"] = jnp.where(p["strict"], m1[:n2, n2:], 0.0)
        p["a_r"] = jnp.concatenate([jnp.where(p["incl"], m1[n2:, :n2], 0.0),
                                    jnp.where(p["incl"], m1[n2:, n2:], 0.0)], axis=1)
    t_invs = _neumann_inverses([p["n_ab"] for p in probs], eye)
    for p in probs:
        p["akv"] = _dot(p["a_ak"], p["v_e"])
    for p, t_inv in zip(probs, t_invs):
        p["x"] = _dot(t_inv, jnp.concatenate([p["akv"], p["a_e"]], axis=1))

    states = {(d, pr): dirs[d][2][pr] for d in range(2) for pr in range(n_pairs)}
    for step in range(nc):
        cur = [p for p in probs if p["step"] == step]
        u_es = [p["x"][:, :pair_w] + _dot(p["x"][:, pair_w:], states[p["d"], p["pr"]]) for p in cur]
        for p, u_e in zip(cur, u_es):
            y_e = _dot(jnp.concatenate([p["r_e"], p["a_r"]], axis=1),
                       jnp.concatenate([states[p["d"], p["pr"]], u_e, p["v_e"]], axis=0))
            dirs[p["d"]][1][p["rows"], p["ls"]] = y_e[:CHUNK] + y_e[CHUNK:]
        for p, u_e in zip(cur, u_es):
            key = (p["d"], p["pr"])
            upd = _dot(p["kd_t"], jnp.concatenate([u_e[:CHUNK] + u_e[CHUNK:], p["v"]], axis=0))
            states[key] = states[key] * p["w_col"] + jnp.where(same, upd, 0.0)
    for (d, pr), state in states.items():
        dirs[d][2][pr] = state


def _b_scan(rkv, cf, cb, geom, tables):
    wide_f = pl.BlockSpec((SEQ_BLK, B_WIDTH), lambda s, fi, bi, fr: (fi[s], 0))
    wide_b = pl.BlockSpec((SEQ_BLK, B_WIDTH), lambda s, fi, bi, fr: (bi[s], 0))
    rkv_f = pl.BlockSpec((SEQ_BLK, 5 * B_WIDTH), lambda s, fi, bi, fr: (fi[s], 0))
    rkv_b = pl.BlockSpec((SEQ_BLK, 5 * B_WIDTH), lambda s, fi, bi, fr: (bi[s], 0))
    out = jax.ShapeDtypeStruct((geom.rows, B_WIDTH), F32)
    return pl.pallas_call(
        _b_scan_kernel,
        out_shape=[out, out],
        grid_spec=pltpu.PrefetchScalarGridSpec(
            num_scalar_prefetch=3,
            grid=(tables[0].shape[0],),
            in_specs=[rkv_f, wide_f, rkv_b, wide_b],
            out_specs=[wide_f, wide_b],
            scratch_shapes=[pltpu.VMEM((B_HEADS // 2, 2 * B_HD, 2 * B_HD), F32)] * 2),
        compiler_params=_params(("arbitrary",)),
        name="b_scan",
    )(*tables, rkv, cf, rkv, cb)


def _finish_residual(h, mix, post_w, valid):
    return jnp.where(valid, h + _rms(mix, post_w), 0.0)


def _mixout_even_kernel(h_ref, oaf_ref, oab_ref, z_ref, ybf_ref, ybb_ref, bonus_ref, gate_ref,
                        w_ref, an_ref, lnw_ref, lnb_ref, post_ref, ones_ref, o_ref, cat_ref, *, geom):
    tm = geom.tm
    valid = _row_valid(geom, pl.program_id(0) * tm, tm)
    for hd in range(A_HEADS):
        hs = slice(hd * A_DV, (hd + 1) * A_DV)
        o = oaf_ref[:, hs] + oab_ref[:, hs]
        z = z_ref[:, hs].astype(F32)
        cat_ref[:, hs] = (_rms(o, an_ref[...]) * (z * _sigmoid(z))).astype(BF16)
    y = ybf_ref[...] + ybb_ref[...]
    yc = y - _seg_sum(y, ones_ref, 3) * (1.0 / B_HD)
    var = _seg_sum(yc * yc, ones_ref, 2) * (1.0 / B_HD)
    yn = yc * lax.rsqrt(var + B_GN_EPS) * lnw_ref[...] + lnb_ref[...]
    cat_ref[:, A_WIDTH:] = ((yn + bonus_ref[...]) * gate_ref[...]).astype(BF16)
    mix = jnp.dot(cat_ref[...], w_ref[...], preferred_element_type=F32)
    o_ref[...] = _finish_residual(h_ref[...], mix, post_ref[...], valid)


def _mixout_even(h, oaf, oab, z, ybf, ybb, bonus, gate, w_out, a_norm, ln_w, ln_b, post_w, ones, geom):
    tm, rows = geom.tm, geom.rows
    row = lambda v: v.reshape(1, -1).astype(F32)
    a_blk = pl.BlockSpec((tm, A_WIDTH), lambda i: (i, 0))
    b_blk = pl.BlockSpec((tm, B_WIDTH), lambda i: (i, 0))
    return pl.pallas_call(
        functools.partial(_mixout_even_kernel, geom=geom),
        out_shape=jax.ShapeDtypeStruct((rows, D_MODEL), F32),
        grid=(rows // tm,),
        in_specs=[pl.BlockSpec((tm, D_MODEL), lambda i: (i, 0)), a_blk, a_blk,
                  a_blk,
                  b_blk, b_blk, b_blk, b_blk,
                  _const_spec((A_WIDTH + B_WIDTH, D_MODEL)), _const_spec((1, A_DV)),
                  _const_spec((1, B_WIDTH)), _const_spec((1, B_WIDTH)), _const_spec((1, D_MODEL)),
                  _const_spec((SEG_TILE, SEG_TILE))],
        out_specs=pl.BlockSpec((tm, D_MODEL), lambda i: (i, 0)),
        scratch_shapes=[pltpu.VMEM((tm, A_WIDTH + B_WIDTH), BF16)],
        compiler_params=_params(),
        name="mixout_even",
    )(h, oaf, oab, z, ybf, ybb, bonus, gate, w_out, row(a_norm), row(ln_w), row(ln_b), row(post_w), ones)


def _mixout_odd_kernel(h_ref, o_ref_in, w_ref, post_ref, o_ref, *, geom):
    tm = geom.tm
    valid = _row_valid(geom, pl.program_id(0) * tm, tm)
    mix = jnp.dot(o_ref_in[...], w_ref[...], preferred_element_type=F32)
    o_ref[...] = _finish_residual(h_ref[...], mix, post_ref[...], valid)


def _mixout_odd(h, o, w_out, post_w, geom):
    tm, rows = geom.tm, geom.rows
    wd = C_HEADS * C_HD
    return pl.pallas_call(
        functools.partial(_mixout_odd_kernel, geom=geom),
        out_shape=jax.ShapeDtypeStruct((rows, D_MODEL), F32),
        grid=(rows // tm,),
        in_specs=[pl.BlockSpec((tm, D_MODEL), lambda i: (i, 0)), pl.BlockSpec((tm, wd), lambda i: (i, 0)),
                  _const_spec((wd, D_MODEL)), _const_spec((1, D_MODEL))],
        out_specs=pl.BlockSpec((tm, D_MODEL), lambda i: (i, 0)),
        compiler_params=_params(),
        name="mixout_odd",
    )(h, o, w_out, post_w.reshape(1, D_MODEL))


def _qkv_kernel(x_ref, nw_ref, w_ref, cos_ref, sin_ref, qn_ref, kn_ref, q_out, k_out, v_out, xn_ref):
    xn_ref[...] = _rms(x_ref[...], nw_ref[...]).astype(BF16)
    cos, sin = cos_ref[...], sin_ref[...]

    def rope(y, yp, n_ref, scale):
        rs = lax.rsqrt(jnp.mean(y * y, axis=-1, keepdims=True) + NORM_EPS) * scale
        return ((y * n_ref[0:1]) * cos + (yp * n_ref[1:2]) * sin) * rs

    nq = C_HEADS * C_HD
    nk = C_KV_HEADS * C_HD
    per = 512 // C_HD
    for c in range(nq // 512):
        y = jnp.dot(xn_ref[...], w_ref[:, c * 512:(c + 1) * 512], preferred_element_type=F32)
        yp = jnp.dot(xn_ref[...], w_ref[:, nq + 2 * nk + c * 512:nq + 2 * nk + (c + 1) * 512],
                     preferred_element_type=F32)
        for hd in range(per):
            hs = slice((c * per + hd) * C_HD, (c * per + hd + 1) * C_HD)
            ys = slice(hd * C_HD, (hd + 1) * C_HD)
            q_out[:, hs] = rope(y[:, ys], yp[:, ys], qn_ref, C_HD ** -0.5 * LOG2E).astype(BF16)
    y = jnp.dot(xn_ref[...], w_ref[:, nq:nq + 2 * nk], preferred_element_type=F32)
    yp = jnp.dot(xn_ref[...], w_ref[:, 2 * nq + 2 * nk:], preferred_element_type=F32)
    for hd in range(C_KV_HEADS):
        hs = slice(hd * C_HD, (hd + 1) * C_HD)
        k_out[:, hs] = rope(y[:, hs], yp[:, hs], kn_ref, 1.0).astype(BF16)
        v_out[:, 2 * hd * C_HD:(2 * hd + 1) * C_HD] = y[:, nk + hd * C_HD:nk + (hd + 1) * C_HD].astype(BF16)
        v_out[:, (2 * hd + 1) * C_HD:(2 * hd + 2) * C_HD] = jnp.ones((x_ref.shape[0], C_HD), BF16)


def _qkv_proj(h, norm_w, w_qkv, cos_t, sin_t, q_norm, k_norm, geom):
    tm, rows = geom.tm, geom.rows
    nq, nk = C_HEADS * C_HD, C_KV_HEADS * C_HD
    lane = np.arange(C_HD)
    partner = np.where(lane % (C_HD // 2) < C_HD // 4, lane + C_HD // 4, lane - C_HD // 4)
    cols = np.concatenate([hd * C_HD + partner for hd in range(C_HEADS + C_KV_HEADS)])
    w_bf16 = jnp.concatenate([w_qkv, w_qkv[:, cols]], axis=1).astype(BF16)
    q_norm = jnp.stack([q_norm, q_norm[partner]])
    k_norm = jnp.stack([k_norm, k_norm[partner]])
    return pl.pallas_call(
        _qkv_kernel,
        out_shape=[jax.ShapeDtypeStruct((rows, nq), BF16), jax.ShapeDtypeStruct((rows, nk), BF16),
                   jax.ShapeDtypeStruct((rows, 2 * nk), BF16)],
        grid=(rows // tm,),
        in_specs=[pl.BlockSpec((tm, D_MODEL), lambda i: (i, 0)), _const_spec((1, D_MODEL)),
                  _const_spec((D_MODEL, 2 * nq + 3 * nk)),
                  pl.BlockSpec((tm, C_HD), lambda i: (i, 0)), pl.BlockSpec((tm, C_HD), lambda i: (i, 0)),
                  _const_spec((2, C_HD)), _const_spec((2, C_HD))],
        out_specs=[pl.BlockSpec((tm, nq), lambda i: (i, 0)), pl.BlockSpec((tm, nk), lambda i: (i, 0)),
                   pl.BlockSpec((tm, 2 * nk), lambda i: (i, 0))],
        scratch_shapes=[pltpu.VMEM((tm, D_MODEL), BF16)],
        compiler_params=_params(),
        name="qkv_proj",
    )(h, norm_w.reshape(1, D_MODEL), w_bf16, cos_t, sin_t, q_norm, k_norm)


def _rope_tables(geom):
    nf = C_HD // 4
    inv = ROPE_THETA ** (-jnp.arange(nf, dtype=F32) / nf)
    cos_parts, sin_parts = [], []
    for g in geom.groups:
        t = jnp.arange(g.tp, dtype=jnp.int32) - (g.fv + N_META)
        is_meta = t < 0
        row = jnp.where(is_meta, -1, t // GRID_W).astype(F32)
        col = jnp.where(is_meta, t + N_META, t % GRID_W).astype(F32)
        ang_r, ang_c = row[:, None] * inv, col[:, None] * inv
        cos = jnp.concatenate([jnp.cos(ang_r)] * 2 + [jnp.cos(ang_c)] * 2, axis=1)
        sin = jnp.concatenate([-jnp.sin(ang_r), jnp.sin(ang_r), -jnp.sin(ang_c), jnp.sin(ang_c)], axis=1)
        cos_parts.append(jnp.tile(cos, (g.nb, 1)))
        sin_parts.append(jnp.tile(sin, (g.nb, 1)))
    return jnp.concatenate(cos_parts, axis=0), jnp.concatenate(sin_parts, axis=0)


def _attn_kernel(*refs, nsub, bounds, fv):
    q_ref, k_ref, v_ref, _, o_ref, s_ref, rm_ref, m_ref, acc_ref = refs
    tq = SEQ_BLK
    nc = len(bounds)
    assert nc % 2 == 0

    def q_rows(qi):
        return pl.ds(pl.multiple_of(qi * tq, tq), tq)

    def produce(qi, c, slot):
        start, size = bounds[c]
        q = jnp.concatenate([q_ref[q_rows(qi), g * C_HD:(g + 1) * C_HD] for g in range(C_GROUP)], axis=0)
        s = lax.dot_general(q, k_ref[start:start + size, :], (((1,), (1,)), ((), ())),
                            preferred_element_type=F32)
        if start < fv:
            key = start + lax.broadcasted_iota(jnp.int32, (1, size), 1)
            s = jnp.where(key >= fv, s, -1e30)
        s_ref[slot, :, :size] = s
        rm_ref[slot] = jnp.max(s, axis=-1, keepdims=True)

    def consume(c, slot):
        start, size = bounds[c]
        m_old = m_ref[...]
        m_new = jnp.maximum(m_old, rm_ref[slot])
        p = jnp.exp2(s_ref[slot, :, :size] - m_new).astype(BF16)
        acc_ref[...] = jnp.exp2(m_old - m_new) * acc_ref[...] + jnp.dot(p, v_ref[start:start + size, :],
                                                                        preferred_element_type=F32)
        m_ref[...] = m_new

    produce(0, 0, 0)

    def body(qi, carry):
        m_ref[...] = jnp.full_like(m_ref, -1e30)
        acc_ref[...] = jnp.zeros_like(acc_ref)
        for c in range(nc):
            if c + 1 < nc:
                produce(qi, c + 1, (c + 1) % 2)
            else:
                produce(jnp.minimum(qi + 1, nsub - 1), 0, 0)
            consume(c, c % 2)
        acc = acc_ref[...]
        out = acc[:, :C_HD] / acc[:, C_HD:]
        for g in range(C_GROUP):
            o_ref[q_rows(qi), g * C_HD:(g + 1) * C_HD] = out[g * tq:(g + 1) * tq].astype(o_ref.dtype)
        return carry

    lax.fori_loop(0, nsub, body, 0)


def _key_chunks(tp, fv):
    lo = fv // LANE
    blocks = tp // LANE - lo
    per = ATTN_MAX_TK // LANE
    nc = -(-blocks // per)
    nc += nc % 2
    pairs, single = divmod(blocks, 2)
    sizes = [2 * (pairs // nc + (1 if i < pairs % nc else 0)) for i in range(nc)]
    sizes[-1] += single
    if min(sizes) == 0:
        sizes = [blocks // nc + (1 if i < blocks % nc else 0) for i in range(nc)]
    assert min(sizes) > 0 and max(sizes) <= per
    starts = np.cumsum([lo] + sizes[:-1])
    return tuple((int(s) * LANE, int(z) * LANE) for s, z in zip(starts, sizes))


def _attention(qn, kn, vn, geom):
    out = jnp.zeros((geom.rows, C_HEADS * C_HD), BF16)
    for g in geom.groups:
        bounds = _key_chunks(g.tp, g.fv)
        nblk = g.tp // SEQ_BLK
        nsub = max(d for d in range(1, ATTN_MAX_SUB + 1) if nblk % d == 0)
        tq, gw, rows = nsub * SEQ_BLK, C_GROUP * C_HD, C_GROUP * SEQ_BLK
        al = lambda x: pl.multiple_of(x, LANE)
        q_spec = pl.BlockSpec((pl.Element(tq), pl.Element(gw)),
                              lambda b, h, i, g=g, tq=tq: (al(g.start + b * g.tp + i * tq), al(h * gw)))
        in_specs = [q_spec,
                    pl.BlockSpec((pl.Element(g.tp), pl.Element(C_HD)),
                                 lambda b, h, i, g=g: (al(g.start + b * g.tp), al(h * C_HD))),
                    pl.BlockSpec((pl.Element(g.tp), pl.Element(2 * C_HD)),
                                 lambda b, h, i, g=g: (al(g.start + b * g.tp), al(h * 2 * C_HD))),
                    pl.BlockSpec(memory_space=pl.ANY)]
        out = pl.pallas_call(
            functools.partial(_attn_kernel, nsub=nsub, bounds=bounds, fv=g.fv),
            out_shape=jax.ShapeDtypeStruct(out.shape, out.dtype),
            grid=(g.nb, C_KV_HEADS, g.tp // tq),
            in_specs=in_specs,
            out_specs=q_spec,
            scratch_shapes=[pltpu.VMEM((2, rows, max(z for _, z in bounds)), F32), pltpu.VMEM((2, rows, 1), F32),
                            pltpu.VMEM((rows, 1), F32), pltpu.VMEM((rows, 2 * C_HD), F32)],
            input_output_aliases={3: 0},
            compiler_params=_params(("parallel", "parallel", "parallel")),
            name="attention",
        )(qn, kn, vn, out)
    return out


def _ffn_kernel(x_ref, xp_ref, xn_ref, pre_ref, wu_ref, wg_ref, cw_ref, cb_ref, wo_ref, post_ref,
                o_ref, xs_ref, act_ref, acc_ref, *, tm):
    i = pl.program_id(0)
    xs_ref[0:HALO] = jnp.where(i > 0, _rms(xp_ref[...], pre_ref[...]), 0.0).astype(BF16)
    xs_ref[HALO:HALO + tm] = _rms(x_ref[...], pre_ref[...]).astype(BF16)
    xs_ref[HALO + tm:2 * HALO + tm] = jnp.where(i < pl.num_programs(0) - 1,
                                                _rms(xn_ref[...], pre_ref[...]), 0.0).astype(BF16)
    half = FFN_CONV // 2
    n_chunks = D_FF // FF_CHUNK
    for c in range(n_chunks):
        cs = slice(c * FF_CHUNK, (c + 1) * FF_CHUNK)
        u = jnp.dot(xs_ref[HALO:HALO + tm], wu_ref[:, cs], preferred_element_type=F32)
        g_win = jnp.dot(xs_ref[...], wg_ref[:, cs], preferred_element_type=F32)
        gate = cb_ref[0:1, cs]
        for j in range(FFN_CONV):
            gate = gate + _shifted_rows(g_win, j - half, tm) * cw_ref[j:j + 1, cs]
        th = jnp.tanh(gate * (GELU_C0 + GELU_C0 * 0.044715 * (gate * gate)))
        hu = 0.5 * gate * u
        act_ref[:, cs] = (hu + hu * th).astype(BF16)
        if c % FF_GROUP == FF_GROUP - 1 or c == n_chunks - 1:
            ks = slice((c // FF_GROUP) * FF_GROUP * FF_CHUNK, (c + 1) * FF_CHUNK)
            part = jnp.dot(act_ref[:, ks], wo_ref[ks, :], preferred_element_type=F32)
            if c < FF_GROUP:
                acc_ref[...] = part
            else:
                acc_ref[...] += part
    o_ref[...] = x_ref[...] + _rms(acc_ref[...], post_ref[...])


def _conv_ffn(h, pre_w, w_in, conv_w, conv_b, w_out, post_w, geom):
    tm, rows = geom.tm, geom.rows
    assert D_FF % FF_CHUNK == 0
    wu = w_in[:, :D_FF].astype(BF16)
    wg = w_in[:, D_FF:].astype(BF16)
    return pl.pallas_call(
        functools.partial(_ffn_kernel, tm=tm),
        out_shape=jax.ShapeDtypeStruct((rows, D_MODEL), F32),
        grid=(rows // tm,),
        in_specs=_halo_specs(tm, D_MODEL, rows) + [
            _const_spec((1, D_MODEL)), _const_spec((D_MODEL, D_FF)), _const_spec((D_MODEL, D_FF)),
            _const_spec((FFN_CONV, D_FF)), _const_spec((1, D_FF)), _const_spec((D_FF, D_MODEL)),
            _const_spec((1, D_MODEL))],
        out_specs=pl.BlockSpec((tm, D_MODEL), lambda i: (i, 0)),
        scratch_shapes=[pltpu.VMEM((tm + 2 * HALO, D_MODEL), BF16), pltpu.VMEM((tm, D_FF), BF16),
                        pltpu.VMEM((tm, D_MODEL), F32)],
        compiler_params=_params(),
        name="conv_ffn",
    )(h, h, h, pre_w.reshape(1, D_MODEL), wu, wg, conv_w, conv_b.reshape(1, D_FF),
      w_out.astype(BF16), post_w.reshape(1, D_MODEL))


def _meta_kernel(meta_ref, _, o_ref):
    o_ref[...] = meta_ref[...].astype(o_ref.dtype)


def _rows_in_kernel(x_ref, _, o_ref):
    o_ref[...] = x_ref[0]


def _rows_out_kernel(h_ref, o_ref):
    o_ref[0] = h_ref[...]


def _copy_block(n):
    return max(b for b in (2048, 1024, 512, 256, 128) if n % b == 0)


def _pack_rows(xs, meta, geom):
    al = lambda v: pl.multiple_of(v, LANE)
    shape = jax.ShapeDtypeStruct((geom.rows, D_MODEL), xs[0].dtype)
    h = jnp.zeros(shape.shape, shape.dtype)
    for x, g in zip(xs, geom.groups):
        pad = g.tp - g.n
        h = pl.pallas_call(
            _meta_kernel, out_shape=shape, grid=(g.nb,),
            in_specs=[_const_spec((N_META, D_MODEL)), pl.BlockSpec(memory_space=pl.ANY)],
            out_specs=pl.BlockSpec((pl.Element(N_META), pl.Element(D_MODEL)),
                                   lambda b, g=g: (pl.multiple_of(g.start + b * g.tp + g.fv, N_META), 0)),
            input_output_aliases={1: 0},
            compiler_params=_params(), name="pack_meta",
        )(meta, h)
        blk = _copy_block(g.n)
        h = pl.pallas_call(
            _rows_in_kernel, out_shape=shape, grid=(g.nb, g.n // blk),
            in_specs=[pl.BlockSpec((1, blk, D_MODEL), lambda b, j: (b, j, 0)), pl.BlockSpec(memory_space=pl.ANY)],
            out_specs=pl.BlockSpec((pl.Element(blk), pl.Element(D_MODEL)),
                                   lambda b, j, g=g, pad=pad, blk=blk: (al(g.start + b * g.tp + pad + j * blk), 0)),
            input_output_aliases={1: 0},
            compiler_params=_params(("parallel", "parallel")), name="pack_rows",
        )(x, h)
    return h


def _unpack_rows(h, geom):
    al = lambda v: pl.multiple_of(v, LANE)
    outs = []
    for g in geom.groups:
        pad = g.tp - g.n
        blk = _copy_block(g.n)
        outs.append(pl.pallas_call(
            _rows_out_kernel, out_shape=jax.ShapeDtypeStruct((g.nb, g.n, D_MODEL), h.dtype),
            grid=(g.nb, g.n // blk),
            in_specs=[pl.BlockSpec((pl.Element(blk), pl.Element(D_MODEL)),
                                   lambda b, j, g=g, pad=pad, blk=blk: (al(g.start + b * g.tp + pad + j * blk), 0))],
            out_specs=pl.BlockSpec((1, blk, D_MODEL), lambda b, j: (b, j, 0)),
            compiler_params=_params(("parallel", "parallel")), name="unpack_rows",
        )(h))
    return tuple(outs)


def _even_layer(h, j, geom, tables, ones, p):
    aw, bw = A_WIDTH, B_WIDTH
    a_cols = 4 * aw + 4 * A_HEADS
    w_in = p["e_w_in"][j]
    lora = B_W_LORA + B_A_LORA + B_G_LORA
    zeros = lambda n: jnp.zeros((D_MODEL, n), F32)
    w_cat = jnp.concatenate([
        w_in[:, :3 * aw],
        w_in[:, a_cols:a_cols + 3 * bw + lora], zeros(B_LORA_PAD - lora),
        w_in[:, 4 * aw:a_cols], zeros(LANE - 4 * A_HEADS),
        w_in[:, 3 * aw:4 * aw]], axis=1).astype(BF16)
    mu = jnp.pad(p["b_shift"][j], ((0, 0), (0, B_LORA_PAD - lora)))
    lora_rows = lambda m, off: jnp.pad(m, ((off, B_LORA_PAD - off - m.shape[0]), (0, 0)))
    w2 = lora_rows(jnp.concatenate([p["b_w2"][j, 0], p["b_w2"][j, 1]], axis=1), 0)
    a2 = lora_rows(p["b_a2"][j], B_W_LORA)
    g2 = lora_rows(p["b_g2"][j], B_W_LORA + B_A_LORA)
    qkv, gts, z, rkv, cf, cb, bonus, gate = _even_prep(
        h, p["mix_pre_norm"][2 * j], w_cat, p["a_conv_w"][j], p["a_log"][j], p["a_dt_bias"][j],
        mu, p["b_w0"][j].reshape(1, 2 * bw), w2, p["b_a0"][j], a2, g2,
        p["b_k_k"][j], p["b_k_a"][j], p["b_r_k"][j], ones, geom)
    oaf, oab = _a_scan(qkv, gts, geom, tables)
    ybf, ybb = _b_scan(rkv, cf, cb, geom, tables)

    return _mixout_even(h, oaf, oab, z, ybf, ybb, bonus, gate, p["e_w_out"][j].astype(BF16),
                        p["a_out_norm"][j], p["b_ln_w"][j], p["b_ln_b"][j], p["mix_post_norm"][2 * j], ones, geom)


def _odd_layer(h, j, geom, rope, p):
    qn, kn, vn = _qkv_proj(h, p["mix_pre_norm"][2 * j + 1], p["o_w_qkv"][j].astype(BF16), rope[0], rope[1],
                           p["o_q_norm"][j], p["o_k_norm"][j], geom)
    o = _attention(qn, kn, vn, geom)
    return _mixout_odd(h, o, p["o_w_out"][j].astype(BF16), p["mix_post_norm"][2 * j + 1], geom)


def kernel(x_prompt, x_sample, meta, mix_pre_norm, mix_post_norm, ffn_pre_norm, ffn_post_norm, e_w_in, a_conv_w, a_log, a_dt_bias, a_out_norm, b_shift, b_w0, b_w2, b_a0, b_a2, b_g2, b_k_k, b_k_a, b_r_k, b_ln_w, b_ln_b, e_w_out, o_w_qkv, o_q_norm, o_k_norm, o_w_out, f_w_in, f_conv_w, f_conv_b, f_w_out):
    p = dict(mix_pre_norm=mix_pre_norm, mix_post_norm=mix_post_norm, e_w_in=e_w_in, a_conv_w=a_conv_w,
             a_log=a_log, a_dt_bias=a_dt_bias, a_out_norm=a_out_norm, b_shift=b_shift, b_w0=b_w0, b_w2=b_w2,
             b_a0=b_a0, b_a2=b_a2, b_g2=b_g2, b_k_k=b_k_k, b_k_a=b_k_a, b_r_k=b_r_k, b_ln_w=b_ln_w,
             b_ln_b=b_ln_b, e_w_out=e_w_out, o_w_qkv=o_w_qkv, o_q_norm=o_q_norm, o_k_norm=o_k_norm,
             o_w_out=o_w_out)
    xs = (x_prompt, x_sample)
    geom = _choose_geom([(x.shape[0], x.shape[1]) for x in xs])
    tables = _scan_tables(geom)
    rope = _rope_tables(geom)
    seg = np.arange(SEG_TILE) // B_HD
    ones = jnp.asarray(seg[:, None] == seg[None, :], BF16)

    h = _pack_rows(xs, meta, geom)
    for i in range(DEPTH):
        h = _even_layer(h, i // 2, geom, tables, ones, p) if i % 2 == 0 else _odd_layer(h, i // 2, geom, rope, p)
        h = _conv_ffn(h, ffn_pre_norm[i], f_w_in[i], f_conv_w[i], f_conv_b[i], f_w_out[i], ffn_post_norm[i], geom)
    return _unpack_rows(h, geom)
```
